```python
import math
import jax, jax.numpy as jnp
from jax import lax
import numpy as np

D_MODEL = 1024
BATCH = 1
SEQ = 16384
DEPTH = 2

GRID_W = 64
CTX_LEN = 256
EPS = 1e-6
ROPE_BASE = 10000.0
Q_BLOCK = 128
M_HEADS = 4
M_DK = 128
M_DV = 128
M_CHUNK = 64
GATE_SOFTCAP = 15.0
A_HEADS = 8
A_NOPE = 64
A_ROPE = 32
A_V = 64
Q_LORA = 256
KV_LORA = 128
DF_HEADS = 4
DF_DK = 64
DF_DV = 128
N_BRANCH = 3
BRANCH_W = 512
N_EXPERTS = 32
TOP_K = 4
D_EXPERT = 1024
SWIGLU_LIMIT = 7.0
SWIGLU_ALPHA = 1.702
MOE_BLOCK = 128

IN_SIZES = (M_HEADS * M_DK, M_HEADS * M_DK, M_HEADS * M_DV, M_HEADS * M_DV, 4 * M_HEADS,
            Q_LORA, KV_LORA, A_ROPE,
            2 * DF_HEADS * DF_DK, 2 * DF_HEADS * DF_DK, DF_HEADS * DF_DV,
            N_BRANCH * D_MODEL)
D_IN = sum(IN_SIZES)

kernel_name = 'hybrid_mlstm_mla_diffattn_moe_dit'

f32 = jnp.float32


def rms_norm(x, g):
    xf = x.astype(f32)
    y = xf * lax.rsqrt(jnp.mean(xf * xf, axis=-1, keepdims=True) + EPS)
    return (y * g.astype(f32)).astype(x.dtype)


def heads(t, n):
    b, l, _ = t.shape
    return t.reshape(b, l, n, -1).transpose(0, 2, 1, 3)


def merge_heads(t):
    b, h, l, d = t.shape
    return t.transpose(0, 2, 1, 3).reshape(b, l, h * d)


def axial_rope(rows, dim):
    row = jnp.repeat(jnp.arange(rows, dtype=f32), GRID_W)
    col = jnp.tile(jnp.arange(GRID_W, dtype=f32), rows)
    quarter = dim // 4
    inv = ROPE_BASE ** (-jnp.arange(quarter, dtype=f32) / quarter)
    ang = jnp.concatenate([row[:, None] * inv, col[:, None] * inv], axis=-1)
    return jnp.cos(ang), jnp.sin(ang)


def apply_rope(x, rope):
    if rope is None:
        return x
    cos, sin = rope
    half = x.shape[-1] // 2
    x1 = x[..., :half].astype(f32)
    x2 = x[..., half:].astype(f32)
    return jnp.concatenate([x1 * cos - x2 * sin, x1 * sin + x2 * cos], axis=-1).astype(x.dtype)


def adaln(cond, w, b):
    m = jnp.einsum('...d,de->...e', jax.nn.silu(cond), w) + b
    return jnp.split(m, 6, axis=-1)


def modulate(x, g, shift, scale):
    return rms_norm(x, g) * (1 + scale) + shift


def in_projection(h, w_in):
    y = jnp.einsum('bld,de->ble', h, w_in)
    idx = [int(i) for i in np.cumsum(IN_SIZES)[:-1]]
    return jnp.split(y, idx, axis=-1)


def map_attention(q, k, v, coef, scale):
    m, b, h, lq, dk = q.shape
    nb = lq // Q_BLOCK
    qb = jnp.moveaxis(q.reshape(m, b, h, nb, Q_BLOCK, dk), 3, 0)

    def one(qblk):
        s = jnp.einsum('mbhqd,mbhkd->mbhqk', qblk, k, preferred_element_type=f32) * scale
        p = jnp.einsum('m,mbhqk->bhqk', coef, jax.nn.softmax(s, axis=-1))
        return jnp.einsum('bhqk,bhkv->bhqv', p.astype(v.dtype), v)

    out = lax.map(one, qb)
    return jnp.moveaxis(out, 0, 2).reshape(b, h, lq, v.shape[-1])


def mlstm_scan(q, k, v, ig, lf, state):
    b, h, l, _ = q.shape
    nc = l // M_CHUNK

    def chunks(a):
        a = a.reshape(a.shape[:2] + (nc, M_CHUNK) + a.shape[3:])
        return jnp.moveaxis(a, 2, 0)

    tri = jnp.tril(jnp.ones((M_CHUNK, M_CHUNK), dtype=bool))

    def step(carry, inp):
        c_st, n_st, m_st = carry
        qc, kc, vc, ic, fc = inp
        bcum = jnp.cumsum(fc, axis=-1)
        log_d = jnp.where(tri, bcum[..., :, None] - bcum[..., None, :] + ic[..., None, :], -jnp.inf)
        inter = bcum + m_st[..., None]
        mj = jnp.maximum(inter, jnp.max(log_d, axis=-1))
        s = jnp.einsum('bhjd,bhsd->bhjs', qc, kc) * jnp.exp(log_d - mj[..., None])
        w_inter = jnp.exp(inter - mj)
        num = jnp.einsum('bhjs,bhsv->bhjv', s, vc) + w_inter[..., None] * jnp.einsum('bhjd,bhdv->bhjv', qc, c_st)
        den = jnp.sum(s, axis=-1) + w_inter * jnp.einsum('bhjd,bhd->bhj', qc, n_st)
        hj = num / jnp.maximum(jnp.abs(den), jnp.exp(-mj))[..., None]
        b_last = bcum[..., -1]
        log_w = b_last[..., None] - bcum + ic
        m_new = jnp.maximum(b_last + m_st, jnp.max(log_w, axis=-1))
        w = jnp.exp(log_w - m_new[..., None])
        decay = jnp.exp(b_last + m_st - m_new)
        c_new = decay[..., None, None] * c_st + jnp.einsum('bhs,bhsd,bhsv->bhdv', w, kc, vc)
        n_new = decay[..., None] * n_st + jnp.einsum('bhs,bhsd->bhd', w, kc)
        return (c_new, n_new, m_new), hj

    final, hs = lax.scan(step, state, (chunks(q), chunks(k), chunks(v), chunks(ig), chunks(lf)))
    return jnp.moveaxis(hs, 0, 2).reshape(b, h, l, v.shape[-1]), final


def mlstm_inputs(mq, mk, mv, mg, gate_b):
    b, l, _ = mq.shape
    q = heads(mq, M_HEADS).astype(f32) * (M_DK ** -0.5)
    k = heads(mk, M_HEADS).astype(f32)
    v = heads(mv, M_HEADS).astype(f32)
    g = mg.astype(f32).reshape(b, l, 4, M_HEADS) + gate_b.astype(f32)
    g = GATE_SOFTCAP * jnp.tanh(g / GATE_SOFTCAP)
    return q, k, v, jnp.transpose(g, (2, 0, 3, 1))


def mlstm_bidir(q, k, v, g, state_f, state_b):
    flip = lambda a: jnp.flip(a, axis=2)
    h_f, fin_f = mlstm_scan(q, k, v, g[0], jax.nn.log_sigmoid(g[1]), state_f)
    h_b, fin_b = mlstm_scan(flip(q), flip(k), flip(v), flip(g[2]), flip(jax.nn.log_sigmoid(g[3])), state_b)
    return h_f + flip(h_b), fin_f, fin_b


def mlstm_output(hsum, o_pre, norm_g):
    _, h, _, dv = hsum.shape
    hn = rms_norm(hsum, norm_g.reshape(h, 1, dv))
    return merge_heads(hn).astype(o_pre.dtype) * jax.nn.sigmoid(o_pre)


def mla_q(cq, q_norm_g, w_uq, rope):
    q = heads(jnp.einsum('blr,re->ble', rms_norm(cq, q_norm_g), w_uq), A_HEADS)
    return jnp.concatenate([q[..., :A_NOPE], apply_rope(q[..., A_NOPE:], rope)], axis=-1)


def mla_kv(ckv, kpe, kv_norm_g, w_ukv, rope):
    kv = heads(jnp.einsum('blr,re->ble', rms_norm(ckv, kv_norm_g), w_ukv), A_HEADS)
    k_pe = apply_rope(kpe[:, None], rope)
    b, a, l, _ = kv.shape
    k = jnp.concatenate([kv[..., :A_NOPE], jnp.broadcast_to(k_pe, (b, a, l, A_ROPE))], axis=-1)
    return k, kv[..., A_NOPE:]


def diff_split(t, rope):
    t = apply_rope(heads(t, 2 * DF_HEADS), rope)
    b, _, l, d = t.shape
    return t.reshape(b, DF_HEADS, 2, l, d).transpose(2, 0, 1, 3, 4)


def diff_out(o, norm_g, lam_init):
    h = o.shape[1]
    return merge_heads(rms_norm(o, norm_g.reshape(h, 1, -1)) * (1.0 - lam_init))


def merge_branches(ym, ya, yd, gate_pre, w_branch, w_out):
    b, l, _ = ym.shape
    yb = jnp.einsum('blnw,nwd->blnd', jnp.stack([ym, ya, yd], axis=2), w_branch)
    g = jax.nn.sigmoid(gate_pre.reshape(b, l, N_BRANCH, D_MODEL))
    return jnp.einsum('bld,de->ble', jnp.sum(g * yb, axis=2), w_out)


def mixer_block(h, hc, w_in, m_gate_b, m_norm_g, q_norm_g, w_uq, kv_norm_g, w_ukv,
                diff_lam, diff_norm_g, w_branch, w_out, rope_a, rope_d, lam_init, need_ctx):
    b = h.shape[0]
    mq, mk, mv, mo, mg, cq, ckv, kpe, dq, dk, dv, gl = in_projection(h, w_in)
    mqc, mkc, mvc, moc, mgc, cqc, ckvc, kpec, dqc, dkc, dvc, glc = in_projection(hc, w_in)
    zero = (jnp.zeros((b, M_HEADS, M_DK, M_DV), f32), jnp.zeros((b, M_HEADS, M_DK), f32),
            jnp.zeros((b, M_HEADS), f32))
    hm_c, st_f, st_b = mlstm_bidir(*mlstm_inputs(mqc, mkc, mvc, mgc, m_gate_b), zero, zero)
    hm, _, _ = mlstm_bidir(*mlstm_inputs(mq, mk, mv, mg, m_gate_b), st_f, st_b)
    scale_a = (A_NOPE + A_ROPE) ** -0.5
    one = jnp.ones((1,), f32)
    ka, va = mla_kv(ckv, kpe, kv_norm_g, w_ukv, rope_a)
    kac, vac = mla_kv(ckvc, kpec, kv_norm_g, w_ukv, None)
    ya = map_attention(mla_q(cq, q_norm_g, w_uq, rope_a)[None], jnp.concatenate([ka, kac], axis=2)[None],
                       jnp.concatenate([va, vac], axis=2), one, scale_a)
    lf = diff_lam.astype(f32)
    lam = jnp.exp(jnp.sum(lf[0] * lf[1])) - jnp.exp(jnp.sum(lf[2] * lf[3])) + lam_init
    coef = jnp.stack([jnp.ones((), f32), -lam])
    scale_d = DF_DK ** -0.5
    kd, kdc = diff_split(dk, rope_d), diff_split(dkc, None)
    vd, vdc = heads(dv, DF_HEADS), heads(dvc, DF_HEADS)
    yd = map_attention(diff_split(dq, rope_d), jnp.concatenate([kd, kdc], axis=3),
                       jnp.concatenate([vd, vdc], axis=2), coef, scale_d)
    y = merge_branches(mlstm_output(hm, mo, m_norm_g), merge_heads(ya), diff_out(yd, diff_norm_g, lam_init),
                       gl, w_branch, w_out)
    if not need_ctx:
        return y, None
    ya_c = map_attention(mla_q(cqc, q_norm_g, w_uq, None)[None], kac[None], vac, one, scale_a)
    yd_c = map_attention(diff_split(dqc, None), kdc, vdc, coef, scale_d)
    yc = merge_branches(mlstm_output(hm_c, moc, m_norm_g), merge_heads(ya_c),
                        diff_out(yd_c, diff_norm_g, lam_init), glc, w_branch, w_out)
    return y, yc


def moe_ffn(h, router_w, router_b, w_gu, b_gu, w_down, b_down):
    n, d = h.shape
    logits = (h @ router_w + router_b).astype(f32)
    top_v, top_i = lax.top_k(logits, TOP_K)
    wts = jax.nn.softmax(top_v, axis=-1)
    nk = n * TOP_K
    flat_e = top_i.reshape(nk)
    flat_t = jnp.repeat(jnp.arange(n, dtype=jnp.int32), TOP_K)
    order = jnp.argsort(flat_e)
    se, st, sw = flat_e[order], flat_t[order], wts.reshape(nk)[order]
    counts = jnp.bincount(flat_e, length=N_EXPERTS)
    padded = (counts + MOE_BLOCK - 1) // MOE_BLOCK * MOE_BLOCK
    pad_end = jnp.cumsum(padded)
    grp_start = jnp.cumsum(counts) - counts
    dest = pad_end[se] - padded[se] + (jnp.arange(nk, dtype=jnp.int32) - grp_start[se])
    cap = -(-(nk + N_EXPERTS * (MOE_BLOCK - 1)) // MOE_BLOCK) * MOE_BLOCK
    n_blocks = cap // MOE_BLOCK
    slot_t = jnp.zeros((cap,), jnp.int32).at[dest].set(st)
    slot_w = jnp.zeros((cap,), f32).at[dest].set(sw)
    blk_e = jnp.minimum(jnp.searchsorted(pad_end, jnp.arange(n_blocks) * MOE_BLOCK, side='right'), N_EXPERTS - 1)
    xs = h[slot_t].reshape(n_blocks, MOE_BLOCK, d)

    def expert_block(args):
        xb, e = args
        gu = xb @ w_gu[e] + b_gu[e]
        gate = jnp.minimum(gu[:, :D_EXPERT], SWIGLU_LIMIT)
        up = jnp.clip(gu[:, D_EXPERT:], -SWIGLU_LIMIT, SWIGLU_LIMIT)
        act = (up + 1) * (gate * jax.nn.sigmoid(SWIGLU_ALPHA * gate))
        return act @ w_down[e] + b_down[e]

    ys = lax.map(expert_block, (xs, blk_e)).reshape(cap, d)
    ys = ys * slot_w[:, None].astype(ys.dtype)
    return jnp.zeros_like(h).at[slot_t].add(ys)


def setup_inputs(seed: int = 0) -> dict:
    key = jax.random.key(seed)
    ks = jax.random.split(key, 24)
    nrm = lambda k, shape, s: jax.random.normal(k, shape, f32) * s
    d = D_MODEL
    gate_off = jnp.array([0.0, 3.0, 0.0, 3.0], f32)[None, :, None]
    return {
        'x': nrm(ks[0], (BATCH, SEQ, d), 1.0),
        'c': nrm(ks[1], (BATCH, d), 1.0),
        'ctx': nrm(ks[2], (BATCH, CTX_LEN, d), 1.0),
        'c_ctx': nrm(ks[3], (d,), 1.0),
        'ada_w': nrm(ks[4], (DEPTH, d, 6 * d), 0.5 * d ** -0.5),
        'ada_b': nrm(ks[5], (DEPTH, 6 * d), 0.02),
        'norm_g': 1.0 + nrm(ks[6], (DEPTH, 4, d), 0.02),
        'w_in': nrm(ks[7], (DEPTH, d, D_IN), d ** -0.5),
        'm_gate_b': gate_off + nrm(ks[8], (DEPTH, 4, M_HEADS), 0.1),
        'm_norm_g': 1.0 + nrm(ks[9], (DEPTH, M_HEADS * M_DV), 0.02),
        'q_norm_g': 1.0 + nrm(ks[10], (DEPTH, Q_LORA), 0.02),
        'w_uq': nrm(ks[11], (DEPTH, Q_LORA, A_HEADS * (A_NOPE + A_ROPE)), Q_LORA ** -0.5),
        'kv_norm_g': 1.0 + nrm(ks[12], (DEPTH, KV_LORA), 0.02),
        'w_ukv': nrm(ks[13], (DEPTH, KV_LORA, A_HEADS * (A_NOPE + A_V)), KV_LORA ** -0.5),
        'diff_lam': nrm(ks[14], (DEPTH, 4, DF_DK), 0.1),
        'diff_norm_g': 1.0 + nrm(ks[15], (DEPTH, DF_HEADS * DF_DV), 0.02),
        'w_branch': nrm(ks[16], (DEPTH, N_BRANCH, BRANCH_W, d), BRANCH_W ** -0.5),
        'w_out': nrm(ks[17], (DEPTH, d, d), d ** -0.5),
        'router_w': nrm(ks[18], (DEPTH, d, N_EXPERTS), d ** -0.5),
        'router_b': nrm(ks[19], (DEPTH, N_EXPERTS), 0.01),
        'w_gu': nrm(ks[20], (DEPTH, N_EXPERTS, d, 2 * D_EXPERT), d ** -0.5),
        'b_gu': nrm(ks[21], (DEPTH, N_EXPERTS, 2 * D_EXPERT), 0.02),
        'w_down': nrm(ks[22], (DEPTH, N_EXPERTS, D_EXPERT, d), D_EXPERT ** -0.5),
        'b_down': nrm(ks[23], (DEPTH, N_EXPERTS, d), 0.02),
    }


def reference(x, c, ctx, c_ctx, ada_w, ada_b, norm_g, w_in, m_gate_b, m_norm_g, q_norm_g, w_uq,
              kv_norm_g, w_ukv, diff_lam, diff_norm_g, w_branch, w_out, router_w, router_b,
              w_gu, b_gu, w_down, b_down):
    n_lat = x.shape[1]
    rows = n_lat // GRID_W
    rope_a = axial_rope(rows, A_ROPE)
    rope_d = axial_rope(rows, DF_DK)
    xc = ctx
    for l in range(DEPTH):
        need_ctx = l < DEPTH - 1
        lam_init = 0.8 - 0.6 * math.exp(-0.3 * l)
        sh1, sc1, ga1, sh2, sc2, ga2 = [p[:, None, :] for p in adaln(c, ada_w[l], ada_b[l])]
        sh1c, sc1c, ga1c, sh2c, sc2c, ga2c = adaln(c_ctx, ada_w[l], ada_b[l])
        y, yc = mixer_block(modulate(x, norm_g[l, 0], sh1, sc1), modulate(xc, norm_g[l, 0], sh1c, sc1c),
                            w_in[l], m_gate_b[l], m_norm_g[l], q_norm_g[l], w_uq[l], kv_norm_g[l], w_ukv[l],
                            diff_lam[l], diff_norm_g[l], w_branch[l], w_out[l], rope_a, rope_d, lam_init, need_ctx)
        x = x + ga1 * rms_norm(y, norm_g[l, 1])
        h2 = modulate(x, norm_g[l, 2], sh2, sc2).reshape(-1, D_MODEL)
        if need_ctx:
            xc = xc + ga1c * rms_norm(yc, norm_g[l, 1])
            h2c = modulate(xc, norm_g[l, 2], sh2c, sc2c).reshape(-1, D_MODEL)
            f = moe_ffn(jnp.concatenate([h2, h2c], axis=0), router_w[l], router_b[l], w_gu[l], b_gu[l],
                        w_down[l], b_down[l])
            f_lat = f[:h2.shape[0]]
            xc = xc + ga2c * rms_norm(f[h2.shape[0]:].reshape(xc.shape), norm_g[l, 3])
        else:
            f_lat = moe_ffn(h2, router_w[l], router_b[l], w_gu[l], b_gu[l], w_down[l], b_down[l])
        x = x + ga2 * rms_norm(f_lat.reshape(x.shape), norm_g[l, 3])
    return x
```

```python
import functools
import math

import jax
import jax.numpy as jnp
from jax import lax
from jax.experimental import pallas as pl
from jax.experimental.pallas import tpu as pltpu

f32 = jnp.float32
bf16 = jnp.bfloat16

D_MODEL = 1024
DEPTH = 2
GRID_W = 64
EPS = 1e-6
ROPE_BASE = 10000.0
M_HEADS = 4
M_DK = 128
M_DV = 128
GATE_SOFTCAP = 15.0
A_HEADS = 8
A_NOPE = 64
A_ROPE = 32
A_V = 64
Q_LORA = 256
KV_LORA = 128
DF_HEADS = 4
DF_DK = 64
DF_DV = 128
N_BRANCH = 3
BRANCH_W = 512
N_EXPERTS = 32
TOP_K = 4
D_EXPERT = 1024
SWIGLU_LIMIT = 7.0
SWIGLU_ALPHA = 1.702

IN_SIZES = (M_HEADS * M_DK, M_HEADS * M_DK, M_HEADS * M_DV, M_HEADS * M_DV, 4 * M_HEADS,
            Q_LORA, KV_LORA, A_ROPE,
            2 * DF_HEADS * DF_DK, 2 * DF_HEADS * DF_DK, DF_HEADS * DF_DV,
            N_BRANCH * D_MODEL)

LOG2E = 1.4426950408889634
LANE = 128
A_DKP = 128
A_DVP = A_V + 16
DF_DVP = DF_DV + 16
VMEM_LIMIT = 56 * 1024 * 1024

ROW_BLOCK = 256
ATTN_TQ = 256
ATTN_TK = 256
MLSTM_T = 256
MOE_TM = 512

NT_DIMS = (((1,), (1,)), ((), ()))


def _cparams(sem):
    return pltpu.CompilerParams(dimension_semantics=sem, vmem_limit_bytes=VMEM_LIMIT)


def _const_spec(shape):
    nd = len(shape)
    return pl.BlockSpec(shape, lambda *_: (0,) * nd)


def _adaln_kernel(cond_ref, w_ref, b_ref, o_ref):
    cnd = cond_ref[...]
    a = cnd * jax.nn.sigmoid(cnd)
    o_ref[...] = jnp.dot(a, w_ref[...], preferred_element_type=f32,
                         precision=lax.Precision.HIGHEST) + b_ref[...]


def _adaln(cond, ada_w, ada_b):
    tn = 1536
    nj = (6 * D_MODEL) // tn
    return pl.pallas_call(
        _adaln_kernel,
        grid=(DEPTH, nj),
        in_specs=[pl.BlockSpec((8, D_MODEL), lambda l, j: (0, 0)),
                  pl.BlockSpec((None, D_MODEL, tn), lambda l, j: (l, 0, j)),
                  pl.BlockSpec((None, 1, tn), lambda l, j: (l, 0, j))],
        out_specs=pl.BlockSpec((None, 8, tn), lambda l, j: (l, 0, j)),
        out_shape=jax.ShapeDtypeStruct((DEPTH, 8, 6 * D_MODEL), f32),
        compiler_params=_cparams(("parallel", "parallel")),
        name="adaln",
    )(cond, ada_w, ada_b.reshape(DEPTH, 1, 6 * D_MODEL))


TOK_W = 3 * 512 + LANE + N_BRANCH * D_MODEL
TR_W = 512 + 16 + Q_LORA + KV_LORA + A_ROPE + 3 * 512


def _ones_row_block(tm):
    r = lax.broadcasted_iota(jnp.int32, (16, tm), 0)
    return jnp.where(r == 0, 1.0, 0.0).astype(bf16)


def _inproj_kernel(x_ref, ab_ref, wtok_ref, wtr_ref, wuq_ref, wukv_ref, gq_ref, gkv_ref,
                   ca_ref, sa_ref, cd_ref, sd_ref,
                   mq_ref, mv_ref, mo_ref, mg_ref, gl_ref, mkt_ref, mgt_ref,
                   qta_ref, kta_ref, vta_ref, qtd_ref, ktd_ref, vtd_ref):
    tm = x_ref.shape[0]
    x = x_ref[...]
    ms = jnp.mean(x * x, axis=-1, keepdims=True)
    h = (x * lax.rsqrt(ms + EPS)) * ab_ref[0:1, :] + ab_ref[1:2, :]
    hb = h.astype(bf16)

    tok = jnp.dot(hb, wtok_ref[...], preferred_element_type=f32)
    mq_ref[...] = (tok[:, 0:512] * (M_DK ** -0.5)).astype(bf16)
    mv_ref[...] = tok[:, 512:1024].astype(bf16)
    mo_ref[...] = tok[:, 1024:1536].astype(bf16)
    mg_ref[...] = tok[:, 1536:1664]
    gl_ref[...] = tok[:, 1664:TOK_W].astype(bf16)

    tr = lax.dot_general(wtr_ref[...], hb, NT_DIMS, preferred_element_type=f32)
    mkt_ref[...] = tr[0:512].astype(bf16)
    mgt_ref[...] = tr[512:528]
    cq = tr[528:784]
    ckv = tr[784:912]
    kpe = tr[912:944]
    dq = tr[944:1456]
    dk = tr[1456:1968]
    dv = tr[1968:2480]

    ca = ca_ref[...]
    sa = sa_ref[...]
    cd = cd_ref[...]
    sd = sd_ref[...]
    ones_blk = _ones_row_block(tm)
    zeros32 = jnp.zeros((32, tm), bf16)

    cqn = (cq * lax.rsqrt(jnp.mean(cq * cq, axis=0, keepdims=True) + EPS) * gq_ref[...]).astype(bf16)
    qscale = ((A_NOPE + A_ROPE) ** -0.5) * LOG2E
    qt = jnp.dot(wuq_ref[...], cqn, preferred_element_type=f32) * qscale
    for hh in range(A_HEADS):
        a = qt[512 + 16 * hh:528 + 16 * hh]
        b = qt[640 + 16 * hh:656 + 16 * hh]
        qta_ref[hh, 0:64, :] = qt[64 * hh:64 * hh + 64].astype(bf16)
        qta_ref[hh, 64:80, :] = (a * ca - b * sa).astype(bf16)
        qta_ref[hh, 80:96, :] = (a * sa + b * ca).astype(bf16)
        qta_ref[hh, 96:128, :] = zeros32

    ckvn = (ckv * lax.rsqrt(jnp.mean(ckv * ckv, axis=0, keepdims=True) + EPS) * gkv_ref[...]).astype(bf16)
    kvt = jnp.dot(wukv_ref[...], ckvn, preferred_element_type=f32)
    ka = kpe[0:16]
    kb = kpe[16:32]
    r1 = (ka * ca - kb * sa).astype(bf16)
    r2 = (ka * sa + kb * ca).astype(bf16)
    for hh in range(A_HEADS):
        kta_ref[hh, 0:64, :] = kvt[64 * hh:64 * hh + 64].astype(bf16)
        kta_ref[hh, 64:80, :] = r1
        kta_ref[hh, 80:96, :] = r2
        kta_ref[hh, 96:128, :] = zeros32
        vta_ref[hh, 0:64, :] = kvt[512 + 64 * hh:576 + 64 * hh].astype(bf16)
        vta_ref[hh, 64:80, :] = ones_blk

    dscale = (DF_DK ** -0.5) * LOG2E
    for hh in range(2 * DF_HEADS):
        a = dq[32 * hh:32 * hh + 32]
        b = dq[256 + 32 * hh:288 + 32 * hh]
        qtd_ref[hh, 0:32, :] = ((a * cd - b * sd) * dscale).astype(bf16)
        qtd_ref[hh, 32:64, :] = ((a * sd + b * cd) * dscale).astype(bf16)
        a = dk[32 * hh:32 * hh + 32]
        b = dk[256 + 32 * hh:288 + 32 * hh]
        ktd_ref[hh, 0:32, :] = (a * cd - b * sd).astype(bf16)
        ktd_ref[hh, 32:64, :] = (a * sd + b * cd).astype(bf16)
    for hh in range(DF_HEADS):
        vtd_ref[hh, 0:128, :] = dv[128 * hh:128 * hh + 128].astype(bf16)
        vtd_ref[hh, 128:144, :] = ones_blk


def _pack_inproj_weights(w, w_uq, w_ukv):
    o = [0]
    for s in IN_SIZES:
        o.append(o[-1] + s)
    mq, mk, mv, mo, mg, cq, ckv, kpe, dq, dk, dv, gl = [w[:, o[i]:o[i + 1]] for i in range(12)]
    mg_pad = jnp.pad(mg, ((0, 0), (0, LANE - mg.shape[1])))
    w_tok = jnp.concatenate([mq, mv, mo, mg_pad, gl], axis=1).astype(bf16)

    def split_halves(t, n_heads, d):
        t3 = t.reshape(t.shape[0], n_heads, d)
        return jnp.concatenate([t3[:, :, :d // 2].reshape(t.shape[0], -1),
                                t3[:, :, d // 2:].reshape(t.shape[0], -1)], axis=1)

    w_tr = jnp.concatenate([mk, mg, cq, ckv, kpe, split_halves(dq, 2 * DF_HEADS, DF_DK),
                            split_halves(dk, 2 * DF_HEADS, DF_DK), dv], axis=1).T.astype(bf16)
    uq = w_uq.reshape(Q_LORA, A_HEADS, A_NOPE + A_ROPE)
    half = A_ROPE // 2
    wt_uq = jnp.concatenate([uq[:, :, :A_NOPE].reshape(Q_LORA, -1),
                             uq[:, :, A_NOPE:A_NOPE + half].reshape(Q_LORA, -1),
                             uq[:, :, A_NOPE + half:].reshape(Q_LORA, -1)], axis=1).T.astype(bf16)
    ukv = w_ukv.reshape(KV_LORA, A_HEADS, A_NOPE + A_V)
    wt_ukv = jnp.concatenate([ukv[:, :, :A_NOPE].reshape(KV_LORA, -1),
                              ukv[:, :, A_NOPE:].reshape(KV_LORA, -1)], axis=1).T.astype(bf16)
    return w_tok, w_tr, wt_uq, wt_ukv


def _inproj(x_all, ab, n_lat_blocks, w_tok, w_tr, wt_uq, wt_ukv, gq, gkv, ca, sa, cd, sd):
    n = x_all.shape[0]
    tm = ROW_BLOCK
    nb = n // tm
    row = lambda i: (i, 0)
    colb = lambda i: (0, i)
    col3 = lambda i: (0, 0, i)
    grp = lambda i: (jnp.where(i >= n_lat_blocks, 1, 0), 0, 0)
    out_shapes = [
        jax.ShapeDtypeStruct((n, 512), bf16), jax.ShapeDtypeStruct((n, 512), bf16),
        jax.ShapeDtypeStruct((n, 512), bf16), jax.ShapeDtypeStruct((n, LANE), f32),
        jax.ShapeDtypeStruct((n, N_BRANCH * D_MODEL), bf16),
        jax.ShapeDtypeStruct((512, n), bf16), jax.ShapeDtypeStruct((16, n), f32),
        jax.ShapeDtypeStruct((A_HEADS, A_DKP, n), bf16), jax.ShapeDtypeStruct((A_HEADS, A_DKP, n), bf16),
        jax.ShapeDtypeStruct((A_HEADS, A_DVP, n), bf16),
        jax.ShapeDtypeStruct((2 * DF_HEADS, DF_DK, n), bf16), jax.ShapeDtypeStruct((2 * DF_HEADS, DF_DK, n), bf16),
        jax.ShapeDtypeStruct((DF_HEADS, DF_DVP, n), bf16),
    ]
    out_specs = [
        pl.BlockSpec((tm, 512), row), pl.BlockSpec((tm, 512), row), pl.BlockSpec((tm, 512), row),
        pl.BlockSpec((tm, LANE), row), pl.BlockSpec((tm, N_BRANCH * D_MODEL), row),
        pl.BlockSpec((512, tm), colb), pl.BlockSpec((16, tm), colb),
        pl.BlockSpec((A_HEADS, A_DKP, tm), col3), pl.BlockSpec((A_HEADS, A_DKP, tm), col3),
        pl.BlockSpec((A_HEADS, A_DVP, tm), col3),
        pl.BlockSpec((2 * DF_HEADS, DF_DK, tm), col3), pl.BlockSpec((2 * DF_HEADS, DF_DK, tm), col3),
        pl.BlockSpec((DF_HEADS, DF_DVP, tm), col3),
    ]
    in_specs = [
        pl.BlockSpec((tm, D_MODEL), row),
        pl.BlockSpec((None, 2, D_MODEL), grp),
        _const_spec(w_tok.shape), _const_spec(w_tr.shape), _const_spec(wt_uq.shape), _const_spec(wt_ukv.shape),
        _const_spec(gq.shape), _const_spec(gkv.shape),
        pl.BlockSpec((A_ROPE // 2, tm), colb), pl.BlockSpec((A_ROPE // 2, tm), colb),
        pl.BlockSpec((DF_DK // 2, tm), colb), pl.BlockSpec((DF_DK // 2, tm), colb),
    ]
    return pl.pallas_call(
        _inproj_kernel, grid=(nb,), in_specs=in_specs, out_specs=out_specs, out_shape=out_shapes,
        compiler_params=_cparams(("parallel",)), name="inproj",
    )(x_all, ab, w_tok, w_tr, wt_uq, wt_ukv, gq, gkv, ca, sa, cd, sd)


def _attn_kernel(*refs, n_maps, dk, dv, tk, n_kt, lam_init):
    if n_maps == 2:
        lam_ref, g_ref, qt_ref, k_ref, vt_ref, o_ref = refs
    else:
        qt_ref, k_ref, vt_ref, o_ref = refs
    tq = qt_ref.shape[-1]
    dvp = vt_ref.shape[-2]
    outs = []
    for m in range(n_maps):
        qt = qt_ref[m * dk:(m + 1) * dk, :]

        def body(j, carry, m=m, qt=qt):
            m_run, acc = carry
            kb = k_ref[m, pl.ds(pl.multiple_of(j * tk, tk), tk), :]
            s = jnp.dot(kb, qt, preferred_element_type=f32)
            m_new = jnp.maximum(m_run, jnp.max(s, axis=0, keepdims=True))
            alpha = jnp.exp2(m_run - m_new)
            p = jnp.exp2(s - m_new).astype(bf16)
            pv = jnp.dot(vt_ref[j], p, preferred_element_type=f32)
            return m_new, alpha * acc + pv

        init = (jnp.full((1, tq), -jnp.inf, f32), jnp.zeros((dvp, tq), f32))
        _, acc = lax.fori_loop(0, n_kt, body, init)
        outs.append(acc[0:dv] / acc[dv:dv + 1])
    if n_maps == 1:
        o_ref[...] = outs[0].astype(o_ref.dtype)
    else:
        lf = lam_ref[...]
        lam = (jnp.exp(jnp.sum(lf[0:1] * lf[1:2], axis=1, keepdims=True))
               - jnp.exp(jnp.sum(lf[2:3] * lf[3:4], axis=1, keepdims=True)) + lam_init)
        o = outs[0] - lam * outs[1]
        o = o * lax.rsqrt(jnp.mean(o * o, axis=0, keepdims=True) + EPS) * g_ref[...]
        o_ref[...] = (o * (1.0 - lam_init)).astype(o_ref.dtype)


def _attention(qt, k, vt, *, n_maps, dk, dv, tq, tk, lam=None, g=None, lam_init=0.0):
    n_heads, _, lq = qt.shape
    lk = k.shape[1]
    dvp = vt.shape[1]
    tq = min(tq, lq)
    tk = min(tk, lk)
    n_kt = lk // tk
    vt4 = vt.reshape(n_heads, dvp, n_kt, tk).transpose(0, 2, 1, 3)
    in_specs = [
        pl.BlockSpec((None, n_maps * dk, tq), lambda h, i: (h, 0, i)),
        pl.BlockSpec((n_maps, lk, dk), lambda h, i: (h, 0, 0)),
        pl.BlockSpec((None, n_kt, dvp, tk), lambda h, i: (h, 0, 0, 0)),
    ]
    args = [qt, k, vt4]
    if n_maps == 2:
        in_specs = [_const_spec(lam.shape), pl.BlockSpec((None, dv, 1), lambda h, i: (h, 0, 0))] + in_specs
        args = [lam, g] + args
    kern = functools.partial(_attn_kernel, n_maps=n_maps, dk=dk, dv=dv, tk=tk, n_kt=n_kt, lam_init=lam_init)
    return pl.pallas_call(
        kern, grid=(n_heads, lq // tq), in_specs=in_specs,
        out_specs=pl.BlockSpec((None, dv, tq), lambda h, i: (h, 0, i)),
        out_shape=jax.ShapeDtypeStruct((n_heads, dv, lq), bf16),
        compiler_params=_cparams(("parallel", "parallel")),
        name="attn_diff" if n_maps == 2 else "attn_mla",
    )(*args)


def _log_sigmoid(x):
    return jnp.minimum(x, 0.0) - jnp.log(1.0 + jnp.exp(-jnp.abs(x)))


def _softcap(x):
    return GATE_SOFTCAP * jnp.tanh(x * (1.0 / GATE_SOFTCAP))


def _mlstm_kernel(gb_ref, gbt_ref,
                  qf_ref, ktf_ref, vf_ref, gf_ref, gtf_ref,
                  qb_ref, ktb_ref, vb_ref, gb2_ref, gtb_ref,
                  hf_ref, hb_ref, c_ref, m_ref):
    t = qf_ref.shape[0]

    @pl.when(pl.program_id(0) == 0)
    def _():
        c_ref[...] = jnp.zeros_like(c_ref)
        m_ref[...] = jnp.zeros_like(m_ref)

    row = lax.broadcasted_iota(jnp.int32, (t, t), 0)
    col = lax.broadcasted_iota(jnp.int32, (t, t), 1)
    lane = lax.broadcasted_iota(jnp.int32, (t, LANE), 1)
    ones_col = jnp.where(lane == 0, 1.0, 0.0).astype(bf16)
    hi = lax.Precision.HIGHEST
    dirs = ((qf_ref, ktf_ref, vf_ref, gf_ref, gtf_ref, hf_ref),
            (qb_ref, ktb_ref, vb_ref, gb2_ref, gtb_ref, hb_ref))
    for d, (q_ref, kt_ref, v_ref, g_ref, gt_ref, o_ref) in enumerate(dirs):
        mask = (col <= row) if d == 0 else (col >= row)
        maskf = mask.astype(f32)
        last = t - 1 if d == 0 else 0
        g_tok = _softcap(g_ref[...] + gb_ref[...])
        g_t = _softcap(gt_ref[...] + gbt_ref[...])
        b_tok = jnp.dot(maskf, _log_sigmoid(g_tok), preferred_element_type=f32, precision=hi)
        b_t = lax.dot_general(_log_sigmoid(g_t), maskf, NT_DIMS, preferred_element_type=f32, precision=hi)
        for hh in range(M_HEADS):
            ri = 4 * (2 * d) + hh
            rf = 4 * (2 * d + 1) + hh
            idx = d * M_HEADS + hh
            b_col = b_tok[:, rf:rf + 1]
            b_row = b_t[rf:rf + 1, :]
            a_row = g_t[ri:ri + 1, :] - b_row
            m_prev = m_ref[idx, 0:1, 0:1]
            ld = jnp.where(mask, b_col + a_row, -jnp.inf)
            inter = b_col + m_prev
            mj = jnp.maximum(inter, jnp.max(ld, axis=1, keepdims=True))
            dmat = jnp.exp(ld - mj)
            q = q_ref[:, hh * M_DK:(hh + 1) * M_DK]
            kt = kt_ref[hh * M_DK:(hh + 1) * M_DK, :]
            vp = jnp.concatenate([v_ref[:, hh * M_DV:(hh + 1) * M_DV], ones_col], axis=1)
            sc = jnp.dot(q, kt, preferred_element_type=f32) * dmat
            cp = c_ref[idx]
            num = (jnp.dot(sc.astype(bf16), vp, preferred_element_type=f32)
                   + jnp.exp(inter - mj) * jnp.dot(q, cp.astype(bf16), preferred_element_type=f32))
            den = num[:, M_DV:M_DV + 1]
            hj = num[:, 0:M_DV] / jnp.maximum(jnp.abs(den), jnp.exp(-mj))
            o_ref[:, hh * M_DV:(hh + 1) * M_DV] = hj.astype(o_ref.dtype)
            b_tot = b_t[rf:rf + 1, last:last + 1]
            lw = b_tot + a_row
            m_new = jnp.maximum(b_tot + m_prev, jnp.max(lw, axis=1, keepdims=True))
            kw = (kt.astype(f32) * jnp.exp(lw - m_new)).astype(bf16)
            c_ref[idx] = jnp.exp(b_tot + m_prev - m_new) * cp + jnp.dot(kw, vp, preferred_element_type=f32)
            m_ref[idx] = jnp.broadcast_to(m_new, (8, LANE))


def _mlstm(mq, mkt, mv, mg, mgt, gate_b):
    n = mq.shape[0]
    t = MLSTM_T
    nb = n // t
    gb = jnp.pad(gate_b.reshape(1, 16), ((0, 0), (0, LANE - 16)))
    gbt = gate_b.reshape(16, 1)
    fwd = lambda s: jnp.where(s == 0, nb - 1, s - 1)
    bwd = lambda s: nb - 1 - s
    specs = []
    for im in (fwd, bwd):
        specs += [pl.BlockSpec((t, 512), lambda s, im=im: (im(s), 0)),
                  pl.BlockSpec((512, t), lambda s, im=im: (0, im(s))),
                  pl.BlockSpec((t, 512), lambda s, im=im: (im(s), 0)),
                  pl.BlockSpec((t, LANE), lambda s, im=im: (im(s), 0)),
                  pl.BlockSpec((16, t), lambda s, im=im: (0, im(s)))]
    return pl.pallas_call(
        _mlstm_kernel, grid=(nb,),
        in_specs=[_const_spec(gb.shape), _const_spec(gbt.shape)] + specs,
        out_specs=[pl.BlockSpec((t, 512), lambda s: (fwd(s), 0)), pl.BlockSpec((t, 512), lambda s: (bwd(s), 0))],
        out_shape=[jax.ShapeDtypeStruct((n, 512), bf16), jax.ShapeDtypeStruct((n, 512), bf16)],
        scratch_shapes=[pltpu.VMEM((2 * M_HEADS, M_DK, 2 * M_DV), f32), pltpu.VMEM((2 * M_HEADS, 8, LANE), f32)],
        compiler_params=_cparams(("arbitrary",)), name="mlstm",
    )(gb, gbt, mq, mkt, mv, mg, mgt, mq, mkt, mv, mg, mgt)


def _rms(x, axis=-1):
    return x * lax.rsqrt(jnp.mean(x * x, axis=axis, keepdims=True) + EPS)


def _merge_kernel(x_ref, hf_ref, hb_ref, mo_ref, ya_ref, yd_ref, gl_ref, mng_ref, wb_ref, wout_ref,
                  mod_ref, rw_ref, rb_ref, xnew_ref, h2_ref, logit_ref):
    hs = hf_ref[...].astype(f32) + hb_ref[...].astype(f32)
    parts = [_rms(hs[:, hh * M_DV:(hh + 1) * M_DV]) for hh in range(M_HEADS)]
    ym = jnp.concatenate(parts, axis=1) * mng_ref[...] * jax.nn.sigmoid(mo_ref[...].astype(f32))
    yb0 = jnp.dot(ym.astype(bf16), wb_ref[0], preferred_element_type=f32)
    yb1 = jnp.dot(ya_ref[...], wb_ref[1], preferred_element_type=f32)
    yb2 = jnp.dot(yd_ref[...], wb_ref[2], preferred_element_type=f32)
    d = D_MODEL
    z = (jax.nn.sigmoid(gl_ref[:, 0:d].astype(f32)) * yb0
         + jax.nn.sigmoid(gl_ref[:, d:2 * d].astype(f32)) * yb1
         + jax.nn.sigmoid(gl_ref[:, 2 * d:3 * d].astype(f32)) * yb2)
    y = jnp.dot(z.astype(bf16), wout_ref[...], preferred_element_type=f32)
    xnew = x_ref[...] + mod_ref[1:2, :] * (_rms(y) * mod_ref[0:1, :])
    h2 = _rms(xnew) * mod_ref[2:3, :] + mod_ref[3:4, :]
    xnew_ref[...] = xnew
    h2_ref[...] = h2.astype(bf16)
    logit_ref[...] = jnp.dot(h2, rw_ref[...], preferred_element_type=f32,
                             precision=lax.Precision.HIGHEST) + rb_ref[...]


def _merge(n_rows, n_lat_blocks, x_all, hf, hb, mo, ya, yd, gl, mng, wb, wout, mod, rw, rb):
    tm = ROW_BLOCK
    nb = n_rows // tm
    row = lambda i: (i, 0)
    grp = lambda i: (jnp.where(i >= n_lat_blocks, 1, 0), 0, 0)
    in_specs = [pl.BlockSpec((tm, D_MODEL), row), pl.BlockSpec((tm, 512), row), pl.BlockSpec((tm, 512), row),
                pl.BlockSpec((tm, 512), row), pl.BlockSpec((tm, 512), row), pl.BlockSpec((tm, 512), row),
                pl.BlockSpec((tm, N_BRANCH * D_MODEL), row), _const_spec(mng.shape), _const_spec(wb.shape),
                _const_spec(wout.shape), pl.BlockSpec((None, 4, D_MODEL), grp),
                _const_spec(rw.shape), _const_spec(rb.shape)]
    return pl.pallas_call(
        _merge_kernel, grid=(nb,), in_specs=in_specs,
        out_specs=[pl.BlockSpec((tm, D_MODEL), row), pl.BlockSpec((tm, D_MODEL), row), pl.BlockSpec((tm, LANE), row)],
        out_shape=[jax.ShapeDtypeStruct((n_rows, D_MODEL), f32), jax.ShapeDtypeStruct((n_rows, D_MODEL), bf16),
                   jax.ShapeDtypeStruct((n_rows, LANE), f32)],
        compiler_params=_cparams(("parallel",)), name="merge",
    )(x_all, hf, hb, mo, ya, yd, gl, mng, wb, wout, mod, rw, rb)


def _moe_kernel(be_ref, nu_ref, xs_ref, sw_ref, wgu_ref, bgu_ref, wd_ref, bd_ref, o_ref, wgu_s, wd_s):
    i = pl.program_id(0)
    prev = be_ref[jnp.maximum(i - 1, 0)]
    used = i < nu_ref[0]

    @pl.when(jnp.logical_and(used, jnp.logical_or(i == 0, be_ref[i] != prev)))
    def _():
        wgu_s[...] = wgu_ref[...].astype(bf16)
        wd_s[...] = wd_ref[...].astype(bf16)

    @pl.when(used)
    def _():
        gu = jnp.dot(xs_ref[...], wgu_s[...], preferred_element_type=f32) + bgu_ref[...]
        gate = jnp.minimum(gu[:, :D_EXPERT], SWIGLU_LIMIT)
        up = jnp.clip(gu[:, D_EXPERT:], -SWIGLU_LIMIT, SWIGLU_LIMIT)
        act = (up + 1.0) * (gate * jax.nn.sigmoid(SWIGLU_ALPHA * gate))
        y = jnp.dot(act.astype(bf16), wd_s[...], preferred_element_type=f32) + bd_ref[...]
        o_ref[...] = y * sw_ref[...]

    @pl.when(jnp.logical_not(used))
    def _():
        o_ref[...] = jnp.zeros_like(o_ref)


def _moe_experts(blk_e, n_used, xs, slot_w, w_gu, b_gu, w_down, b_down):
    cap = xs.shape[0]
    tm = MOE_TM
    nblk = cap // tm
    grid_spec = pltpu.PrefetchScalarGridSpec(
        num_scalar_prefetch=2, grid=(nblk,),
        in_specs=[pl.BlockSpec((tm, D_MODEL), lambda i, be, nu: (i, 0)),
                  pl.BlockSpec((tm, 1), lambda i, be, nu: (i, 0)),
                  pl.BlockSpec((None, D_MODEL, 2 * D_EXPERT), lambda i, be, nu: (be[i], 0, 0)),
                  pl.BlockSpec((None, 1, 2 * D_EXPERT), lambda i, be, nu: (be[i], 0, 0)),
                  pl.BlockSpec((None, D_EXPERT, D_MODEL), lambda i, be, nu: (be[i], 0, 0)),
                  pl.BlockSpec((None, 1, D_MODEL), lambda i, be, nu: (be[i], 0, 0))],
        out_specs=pl.BlockSpec((tm, D_MODEL), lambda i, be, nu: (i, 0)),
        scratch_shapes=[pltpu.VMEM((D_MODEL, 2 * D_EXPERT), bf16), pltpu.VMEM((D_EXPERT, D_MODEL), bf16)])
    return pl.pallas_call(
        _moe_kernel, grid_spec=grid_spec,
        out_shape=jax.ShapeDtypeStruct((cap, D_MODEL), f32),
        compiler_params=_cparams(("arbitrary",)), name="moe_experts",
    )(blk_e, n_used, xs, slot_w, w_gu, b_gu.reshape(N_EXPERTS, 1, -1), w_down, b_down.reshape(N_EXPERTS, 1, -1))


def _moe(h2, logits, w_gu, b_gu, w_down, b_down):
    n = h2.shape[0]
    tm = MOE_TM
    top_v, top_i = lax.top_k(logits[:, :N_EXPERTS], TOP_K)
    wts = jax.nn.softmax(top_v, axis=-1)
    nk = n * TOP_K
    flat_e = top_i.reshape(nk).astype(jnp.int32)
    iota = jnp.arange(nk, dtype=jnp.int32)
    se, order = lax.sort((flat_e, iota), num_keys=1, is_stable=True)
    _, inv = lax.sort((order, iota), num_keys=1)
    counts = jnp.sum((flat_e[:, None] == jnp.arange(N_EXPERTS, dtype=jnp.int32)[None, :]).astype(jnp.int32), axis=0)
    padded = (counts + tm - 1) // tm * tm
    pad_end = jnp.cumsum(padded)
    pad_start = pad_end - padded
    grp_start = jnp.cumsum(counts) - counts
    cap = -(-(nk + N_EXPERTS * (tm - 1)) // tm) * tm
    nblk = cap // tm
    blk_e = jnp.minimum(jnp.searchsorted(pad_end, jnp.arange(nblk, dtype=jnp.int32) * tm, side='right'),
                        N_EXPERTS - 1).astype(jnp.int32)
    n_used = (pad_end[-1] // tm).astype(jnp.int32).reshape(1)
    p = jnp.arange(cap, dtype=jnp.int32)
    e_p = blk_e[p // tm]
    r = p - pad_start[e_p]
    valid = r < counts[e_p]
    src = jnp.clip(grp_start[e_p] + r, 0, nk - 1)
    slot_t = jnp.where(valid, order[src] // TOP_K, 0)
    slot_w = jnp.where(valid, wts.reshape(nk)[order[src]], 0.0).reshape(cap, 1)
    dest_sorted = pad_start[se] + (iota - grp_start[se])
    pos = dest_sorted[inv].reshape(n, TOP_K)
    xs = h2[slot_t]
    ys = _moe_experts(blk_e, n_used, xs, slot_w, w_gu, b_gu, w_down, b_down)
    return jnp.sum(ys[pos], axis=1)


def _resid_kernel(x_ref, f_ref, mod_ref, o_ref):
    o_ref[...] = x_ref[...] + mod_ref[1:2, :] * (_rms(f_ref[...]) * mod_ref[0:1, :])


def _resid(n_lat_blocks, xnew, f, mod):
    n = xnew.shape[0]
    tm = ROW_BLOCK
    row = lambda i: (i, 0)
    grp = lambda i: (jnp.where(i >= n_lat_blocks, 1, 0), 0, 0)
    return pl.pallas_call(
        _resid_kernel, grid=(n // tm,),
        in_specs=[pl.BlockSpec((tm, D_MODEL), row), pl.BlockSpec((tm, D_MODEL), row),
                  pl.BlockSpec((None, 2, D_MODEL), grp)],
        out_specs=pl.BlockSpec((tm, D_MODEL), row),
        out_shape=jax.ShapeDtypeStruct((n, D_MODEL), f32),
        compiler_params=_cparams(("parallel",)), name="resid",
    )(xnew, f, mod)


def _rope_tables(n_lat, n_ctx, dim):
    rows = n_lat // GRID_W
    row = jnp.repeat(jnp.arange(rows, dtype=f32), GRID_W)
    col = jnp.tile(jnp.arange(GRID_W, dtype=f32), rows)
    quarter = dim // 4
    inv = ROPE_BASE ** (-jnp.arange(quarter, dtype=f32) / quarter)
    ang = jnp.concatenate([row[:, None] * inv, col[:, None] * inv], axis=-1)
    cos = jnp.concatenate([jnp.cos(ang), jnp.ones((n_ctx, dim // 2), f32)], axis=0).T
    sin = jnp.concatenate([jnp.sin(ang), jnp.zeros((n_ctx, dim // 2), f32)], axis=0).T
    return cos, sin


def kernel(x, c, ctx, c_ctx, ada_w, ada_b, norm_g, w_in, m_gate_b, m_norm_g, q_norm_g, w_uq, kv_norm_g, w_ukv,
           diff_lam, diff_norm_g, w_branch, w_out, router_w, router_b, w_gu, b_gu, w_down, b_down):
    n_lat = x.shape[1]
    n_ctx = ctx.shape[1]
    n = n_lat + n_ctx
    assert x.shape[0] == 1 and n_lat % ROW_BLOCK == 0 and n_ctx == MLSTM_T and n_lat % GRID_W == 0
    nlb = n_lat // ROW_BLOCK
    x_all = jnp.concatenate([x[0], ctx[0]], axis=0)

    cond = jnp.zeros((8, D_MODEL), f32).at[0].set(c[0]).at[1].set(c_ctx)
    mods = _adaln(cond, ada_w, ada_b)[:, :2].reshape(DEPTH, 2, 6, D_MODEL)

    ca, sa = _rope_tables(n_lat, n_ctx, A_ROPE)
    cd, sd = _rope_tables(n_lat, n_ctx, DF_DK)

    for l in range(DEPTH):
        need_ctx = l < DEPTH - 1
        lam_init = 0.8 - 0.6 * math.exp(-0.3 * l)
        sh1, sc1, ga1, sh2, sc2, ga2 = [mods[l, :, j] for j in range(6)]
        ab1 = jnp.stack([norm_g[l, 0][None] * (1.0 + sc1), sh1], axis=1)
        w_tok, w_tr, wt_uq, wt_ukv = _pack_inproj_weights(w_in[l], w_uq[l], w_ukv[l])
        (mq, mv, mo, mg, gl, mkt, mgt, qta, kta, vta, qtd, ktd, vtd) = _inproj(
            x_all, ab1, nlb, w_tok, w_tr, wt_uq, wt_ukv,
            q_norm_g[l].reshape(Q_LORA, 1), kv_norm_g[l].reshape(KV_LORA, 1), ca, sa, cd, sd)

        hf, hb = _mlstm(mq, mkt, mv, mg, mgt, m_gate_b[l])

        k_a = kta.transpose(0, 2, 1)
        k_d = ktd.transpose(0, 2, 1)
        qtd4 = qtd.reshape(DF_HEADS, 2 * DF_DK, n)
        dg = diff_norm_g[l].reshape(DF_HEADS, DF_DV, 1)
        attn_a = functools.partial(_attention, n_maps=1, dk=A_DKP, dv=A_V, tq=ATTN_TQ, tk=ATTN_TK)
        attn_d = functools.partial(_attention, n_maps=2, dk=DF_DK, dv=DF_DV, tq=ATTN_TQ, tk=ATTN_TK,
                                   lam=diff_lam[l], g=dg, lam_init=lam_init)
        ya_t = attn_a(qta[:, :, :n_lat], k_a, vta)
        yd_t = attn_d(qtd4[:, :, :n_lat], k_d, vtd)
        if need_ctx:
            ya_t = jnp.concatenate([ya_t, attn_a(qta[:, :, n_lat:], k_a[:, n_lat:], vta[:, :, n_lat:])], axis=2)
            yd_t = jnp.concatenate([yd_t, attn_d(qtd4[:, :, n_lat:], k_d[:, n_lat:], vtd[:, :, n_lat:])], axis=2)
        n_rows = n if need_ctx else n_lat
        ya = ya_t.transpose(2, 0, 1).reshape(n_rows, A_HEADS * A_V)
        yd = yd_t.transpose(2, 0, 1).reshape(n_rows, DF_HEADS * DF_DV)

        mod_m = jnp.stack([jnp.broadcast_to(norm_g[l, 1][None], (2, D_MODEL)), ga1,
                           norm_g[l, 2][None] * (1.0 + sc2), sh2], axis=1)
        rw = jnp.pad(router_w[l], ((0, 0), (0, LANE - N_EXPERTS)))
        rb = jnp.pad(router_b[l].reshape(1, N_EXPERTS), ((0, 0), (0, LANE - N_EXPERTS)))
        xnew, h2, logits = _merge(n_rows, nlb, x_all, hf, hb, mo, ya, yd, gl,
                                  m_norm_g[l].reshape(1, -1), w_branch[l].astype(bf16), w_out[l].astype(bf16),
                                  mod_m, rw, rb)
        f = _moe(h2, logits, w_gu[l], b_gu[l], w_down[l], b_down[l])
        mod_r = jnp.stack([jnp.broadcast_to(norm_g[l, 3][None], (2, D_MODEL)), ga2], axis=1)
        x_all = _resid(nlb, xnew, f, mod_r)
    return x_all[:n_lat][None]
```

```python
import functools
import math

import jax
import jax.numpy as jnp
from jax import lax
from jax.experimental import pallas as pl
from jax.experimental.pallas import tpu as pltpu

f32 = jnp.float32
bf16 = jnp.bfloat16

D_MODEL = 1024
DEPTH = 2
GRID_W = 64
EPS = 1e-6
ROPE_BASE = 10000.0
M_HEADS = 4
M_DK = 128
M_DV = 128
GATE_SOFTCAP = 15.0
A_HEADS = 8
A_NOPE = 64
A_ROPE = 32
A_V = 64
Q_LORA = 256
KV_LORA = 128
DF_HEADS = 4
DF_DK = 64
DF_DV = 128
N_BRANCH = 3
BRANCH_W = 512
N_EXPERTS = 32
TOP_K = 4
D_EXPERT = 1024
SWIGLU_LIMIT = 7.0
SWIGLU_ALPHA = 1.702

IN_SIZES = (M_HEADS * M_DK, M_HEADS * M_DK, M_HEADS * M_DV, M_HEADS * M_DV, 4 * M_HEADS,
            Q_LORA, KV_LORA, A_ROPE,
            2 * DF_HEADS * DF_DK, 2 * DF_HEADS * DF_DK, DF_HEADS * DF_DV,
            N_BRANCH * D_MODEL)

LOG2E = 1.4426950408889634
LANE = 128
A_DKP = 128
A_DVP = A_V + 16
DF_DVP = DF_DV + 16
VMEM_LIMIT = 56 * 1024 * 1024

ROW_BLOCK = 256
MXU_DEPTH = 256
ATTN_TQ_MLA = 512
ATTN_TQ_DIFF = 256
ATTN_QSUB = 256
ATTN_TK = 1280
MLSTM_T = 256
MOE_TM = 512

NT_DIMS = (((1,), (1,)), ((), ()))


def _cparams(sem):
    return pltpu.CompilerParams(dimension_semantics=sem, vmem_limit_bytes=VMEM_LIMIT)


def _const_spec(shape):
    nd = len(shape)
    return pl.BlockSpec(shape, lambda *_: (0,) * nd)


def _adaln_kernel(cond_ref, w_ref, b_ref, o_ref):
    cnd = cond_ref[...]
    a = cnd * jax.nn.sigmoid(cnd)
    o_ref[...] = jnp.dot(a, w_ref[...], preferred_element_type=f32,
                         precision=lax.Precision.HIGHEST) + b_ref[...]


def _adaln(cond, ada_w, ada_b):
    tn = 1536
    nj = (6 * D_MODEL) // tn
    return pl.pallas_call(
        _adaln_kernel,
        grid=(DEPTH, nj),
        in_specs=[pl.BlockSpec((8, D_MODEL), lambda l, j: (0, 0)),
                  pl.BlockSpec((None, D_MODEL, tn), lambda l, j: (l, 0, j)),
                  pl.BlockSpec((None, 1, tn), lambda l, j: (l, 0, j))],
        out_specs=pl.BlockSpec((None, 8, tn), lambda l, j: (l, 0, j)),
        out_shape=jax.ShapeDtypeStruct((DEPTH, 8, 6 * D_MODEL), f32),
        compiler_params=_cparams(("parallel", "parallel")),
        name="adaln",
    )(cond, ada_w, ada_b.reshape(DEPTH, 1, 6 * D_MODEL))


TOK_W = 3 * 512 + LANE + N_BRANCH * D_MODEL
TR_W = 512 + 16 + Q_LORA + KV_LORA + A_ROPE + 3 * 512


def _ones_row_block(tm):
    r = lax.broadcasted_iota(jnp.int32, (16, tm), 0)
    return jnp.where(r == 0, 1.0, 0.0).astype(bf16)


def _inproj_kernel(x_ref, ab_ref, wtok_ref, wtr_ref, wuq_ref, wukv_ref, gq_ref, gkv_ref,
                   ca_ref, sa_ref, cd_ref, sd_ref,
                   mq_ref, mv_ref, mo_ref, mg_ref, gl_ref, mkt_ref, mgt_ref,
                   qta_ref, kta_ref, vta_ref, qtd_ref, ktd_ref, vtd_ref):
    tm = x_ref.shape[0]
    x = x_ref[...]
    ms = jnp.mean(x * x, axis=-1, keepdims=True)
    h = (x * lax.rsqrt(ms + EPS)) * ab_ref[0:1, :] + ab_ref[1:2, :]
    hb = h.astype(bf16)

    tok = jnp.dot(hb, wtok_ref[...], preferred_element_type=f32)
    mq_ref[...] = (tok[:, 0:512] * (M_DK ** -0.5)).astype(bf16)
    mv_ref[...] = tok[:, 512:1024].astype(bf16)
    mo_ref[...] = tok[:, 1024:1536].astype(bf16)
    mg_ref[...] = tok[:, 1536:1664]
    gl_ref[...] = tok[:, 1664:TOK_W].astype(bf16)

    tr = lax.dot_general(wtr_ref[...], hb, NT_DIMS, preferred_element_type=f32)
    mkt_ref[...] = tr[0:512].astype(bf16)
    mgt_ref[...] = tr[512:528]
    cq = tr[528:784]
    ckv = tr[784:912]
    kpe = tr[912:944]
    dq = tr[944:1456]
    dk = tr[1456:1968]
    dv = tr[1968:2480]

    ca = ca_ref[...]
    sa = sa_ref[...]
    cd = cd_ref[...]
    sd = sd_ref[...]
    ones_blk = _ones_row_block(tm)
    zeros32 = jnp.zeros((32, tm), bf16)

    cqn = (cq * lax.rsqrt(jnp.mean(cq * cq, axis=0, keepdims=True) + EPS) * gq_ref[...]).astype(bf16)
    qscale = ((A_NOPE + A_ROPE) ** -0.5) * LOG2E
    qt = jnp.dot(wuq_ref[...], cqn, preferred_element_type=f32) * qscale
    for hh in range(A_HEADS):
        a = qt[512 + 16 * hh:528 + 16 * hh]
        b = qt[640 + 16 * hh:656 + 16 * hh]
        qta_ref[hh, 0:64, :] = qt[64 * hh:64 * hh + 64].astype(bf16)
        qta_ref[hh, 64:80, :] = (a * ca - b * sa).astype(bf16)
        qta_ref[hh, 80:96, :] = (a * sa + b * ca).astype(bf16)
        qta_ref[hh, 96:128, :] = zeros32

    ckvn = (ckv * lax.rsqrt(jnp.mean(ckv * ckv, axis=0, keepdims=True) + EPS) * gkv_ref[...]).astype(bf16)
    kvt = jnp.dot(wukv_ref[...], ckvn, preferred_element_type=f32)
    ka = kpe[0:16]
    kb = kpe[16:32]
    r1 = (ka * ca - kb * sa).astype(bf16)
    r2 = (ka * sa + kb * ca).astype(bf16)
    for hh in range(A_HEADS):
        kta_ref[hh, 0:64, :] = kvt[64 * hh:64 * hh + 64].astype(bf16)
        kta_ref[hh, 64:80, :] = r1
        kta_ref[hh, 80:96, :] = r2
        kta_ref[hh, 96:128, :] = zeros32
        vta_ref[hh, 0:64, :] = kvt[512 + 64 * hh:576 + 64 * hh].astype(bf16)
        vta_ref[hh, 64:80, :] = ones_blk

    dscale = (DF_DK ** -0.5) * LOG2E
    for hh in range(2 * DF_HEADS):
        a = dq[32 * hh:32 * hh + 32]
        b = dq[256 + 32 * hh:288 + 32 * hh]
        qtd_ref[hh, 0:32, :] = ((a * cd - b * sd) * dscale).astype(bf16)
        qtd_ref[hh, 32:64, :] = ((a * sd + b * cd) * dscale).astype(bf16)
        a = dk[32 * hh:32 * hh + 32]
        b = dk[256 + 32 * hh:288 + 32 * hh]
        ktd_ref[hh, 0:32, :] = (a * cd - b * sd).astype(bf16)
        ktd_ref[hh, 32:64, :] = (a * sd + b * cd).astype(bf16)
    for hh in range(DF_HEADS):
        vtd_ref[hh, 0:128, :] = dv[128 * hh:128 * hh + 128].astype(bf16)
        vtd_ref[hh, 128:144, :] = ones_blk


def _pack_inproj_weights(w, w_uq, w_ukv):
    o = [0]
    for s in IN_SIZES:
        o.append(o[-1] + s)
    mq, mk, mv, mo, mg, cq, ckv, kpe, dq, dk, dv, gl = [w[:, o[i]:o[i + 1]] for i in range(12)]
    mg_pad = jnp.pad(mg, ((0, 0), (0, LANE - mg.shape[1])))
    w_tok = jnp.concatenate([mq, mv, mo, mg_pad, gl], axis=1).astype(bf16)

    def split_halves(t, n_heads, d):
        t3 = t.reshape(t.shape[0], n_heads, d)
        return jnp.concatenate([t3[:, :, :d // 2].reshape(t.shape[0], -1),
                                t3[:, :, d // 2:].reshape(t.shape[0], -1)], axis=1)

    w_tr = jnp.concatenate([mk, mg, cq, ckv, kpe, split_halves(dq, 2 * DF_HEADS, DF_DK),
                            split_halves(dk, 2 * DF_HEADS, DF_DK), dv], axis=1).T.astype(bf16)
    uq = w_uq.reshape(Q_LORA, A_HEADS, A_NOPE + A_ROPE)
    half = A_ROPE // 2
    wt_uq = jnp.concatenate([uq[:, :, :A_NOPE].reshape(Q_LORA, -1),
                             uq[:, :, A_NOPE:A_NOPE + half].reshape(Q_LORA, -1),
                             uq[:, :, A_NOPE + half:].reshape(Q_LORA, -1)], axis=1).T.astype(bf16)
    ukv = w_ukv.reshape(KV_LORA, A_HEADS, A_NOPE + A_V)
    wt_ukv = jnp.concatenate([ukv[:, :, :A_NOPE].reshape(KV_LORA, -1),
                              ukv[:, :, A_NOPE:].reshape(KV_LORA, -1)], axis=1).T.astype(bf16)
    return w_tok, w_tr, wt_uq, wt_ukv


def _inproj(x_all, ab, n_lat_blocks, w_tok, w_tr, wt_uq, wt_ukv, gq, gkv, ca, sa, cd, sd):
    n = x_all.shape[0]
    tm = ROW_BLOCK
    nb = n // tm
    row = lambda i: (i, 0)
    colb = lambda i: (0, i)
    col3 = lambda i: (0, 0, i)
    grp = lambda i: (jnp.where(i >= n_lat_blocks, 1, 0), 0, 0)
    out_shapes = [
        jax.ShapeDtypeStruct((n, 512), bf16), jax.ShapeDtypeStruct((n, 512), bf16),
        jax.ShapeDtypeStruct((n, 512), bf16), jax.ShapeDtypeStruct((n, LANE), f32),
        jax.ShapeDtypeStruct((n, N_BRANCH * D_MODEL), bf16),
        jax.ShapeDtypeStruct((512, n), bf16), jax.ShapeDtypeStruct((16, n), f32),
        jax.ShapeDtypeStruct((A_HEADS, A_DKP, n), bf16), jax.ShapeDtypeStruct((A_HEADS, A_DKP, n), bf16),
        jax.ShapeDtypeStruct((A_HEADS, A_DVP, n), bf16),
        jax.ShapeDtypeStruct((2 * DF_HEADS, DF_DK, n), bf16), jax.ShapeDtypeStruct((2 * DF_HEADS, DF_DK, n), bf16),
        jax.ShapeDtypeStruct((DF_HEADS, DF_DVP, n), bf16),
    ]
    out_specs = [
        pl.BlockSpec((tm, 512), row), pl.BlockSpec((tm, 512), row), pl.BlockSpec((tm, 512), row),
        pl.BlockSpec((tm, LANE), row), pl.BlockSpec((tm, N_BRANCH * D_MODEL), row),
        pl.BlockSpec((512, tm), colb), pl.BlockSpec((16, tm), colb),
        pl.BlockSpec((A_HEADS, A_DKP, tm), col3), pl.BlockSpec((A_HEADS, A_DKP, tm), col3),
        pl.BlockSpec((A_HEADS, A_DVP, tm), col3),
        pl.BlockSpec((2 * DF_HEADS, DF_DK, tm), col3), pl.BlockSpec((2 * DF_HEADS, DF_DK, tm), col3),
        pl.BlockSpec((DF_HEADS, DF_DVP, tm), col3),
    ]
    in_specs = [
        pl.BlockSpec((tm, D_MODEL), row),
        pl.BlockSpec((None, 2, D_MODEL), grp),
        _const_spec(w_tok.shape), _const_spec(w_tr.shape), _const_spec(wt_uq.shape), _const_spec(wt_ukv.shape),
        _const_spec(gq.shape), _const_spec(gkv.shape),
        pl.BlockSpec((A_ROPE // 2, tm), colb), pl.BlockSpec((A_ROPE // 2, tm), colb),
        pl.BlockSpec((DF_DK // 2, tm), colb), pl.BlockSpec((DF_DK // 2, tm), colb),
    ]
    return pl.pallas_call(
        _inproj_kernel, grid=(nb,), in_specs=in_specs, out_specs=out_specs, out_shape=out_shapes,
        compiler_params=_cparams(("parallel",)), name="inproj",
    )(x_all, ab, w_tok, w_tr, wt_uq, wt_ukv, gq, gkv, ca, sa, cd, sd)


def _attn_kernel(*refs, n_maps, dk, dv, tk, n_kt, qsub, lam_init):
    if n_maps == 2:
        lam_ref, g_ref, qt_ref, k_ref, vt_ref, o_ref = refs
    else:
        qt_ref, k_ref, vt_ref, o_ref = refs
    tq = qt_ref.shape[-1]
    dvp = vt_ref.shape[-2]
    chains = [(m, c) for m in range(n_maps) for c in range(tq // qsub)]

    def qk(j, m, c):
        qt = qt_ref[m * dk:(m + 1) * dk, c * qsub:(c + 1) * qsub]
        return jnp.dot(k_ref[m, j * tk:(j + 1) * tk, :], qt, preferred_element_type=f32)

    state = [(jnp.full((1, qsub), -jnp.inf, f32), jnp.zeros((dvp, qsub), f32)) for _ in chains]
    s_next = [qk(0, m, c) for (m, c) in chains]
    for j in range(n_kt):
        s_cur = s_next
        if j + 1 < n_kt:
            s_next = [qk(j + 1, m, c) for (m, c) in chains]
        vtj = vt_ref[j]
        for i in range(len(chains)):
            m_run, acc = state[i]
            m_new = jnp.maximum(m_run, jnp.max(s_cur[i], axis=0, keepdims=True))
            alpha = jnp.exp2(m_run - m_new)
            p = jnp.exp2(s_cur[i] - m_new).astype(bf16)
            pv = jnp.dot(vtj, p, preferred_element_type=f32)
            state[i] = (m_new, alpha * acc + pv)
    outs = []
    for m in range(n_maps):
        accs = [state[i][1] for i, (mm, _) in enumerate(chains) if mm == m]
        acc = accs[0] if len(accs) == 1 else jnp.concatenate(accs, axis=1)
        outs.append(acc[0:dv] / acc[dv:dv + 1])
    if n_maps == 1:
        o_ref[...] = outs[0].astype(o_ref.dtype)
    else:
        lf = lam_ref[...]
        lam = (jnp.exp(jnp.sum(lf[0:1] * lf[1:2], axis=1, keepdims=True))
               - jnp.exp(jnp.sum(lf[2:3] * lf[3:4], axis=1, keepdims=True)) + lam_init)
        o = outs[0] - lam * outs[1]
        o = o * lax.rsqrt(jnp.mean(o * o, axis=0, keepdims=True) + EPS) * g_ref[...]
        o_ref[...] = (o * (1.0 - lam_init)).astype(o_ref.dtype)


def _key_tile(lk):
    return max(t for t in range(MXU_DEPTH, min(ATTN_TK, lk) + 1, MXU_DEPTH) if lk % t == 0)


def _attention(qt, k, vt, *, n_maps, dk, dv, tq, lam=None, g=None, lam_init=0.0):
    n_heads, _, lq = qt.shape
    lk = k.shape[1]
    dvp = vt.shape[1]
    tq = min(tq, lq)
    qsub = min(ATTN_QSUB, tq)
    tk = _key_tile(lk)
    n_kt = lk // tk
    vt4 = vt.reshape(n_heads, dvp, n_kt, tk).transpose(0, 2, 1, 3)
    in_specs = [
        pl.BlockSpec((None, n_maps * dk, tq), lambda h, i: (h, 0, i)),
        pl.BlockSpec((n_maps, lk, dk), lambda h, i: (h, 0, 0)),
        pl.BlockSpec((None, n_kt, dvp, tk), lambda h, i: (h, 0, 0, 0)),
    ]
    args = [qt, k, vt4]
    if n_maps == 2:
        in_specs = [_const_spec(lam.shape), pl.BlockSpec((None, dv, 1), lambda h, i: (h, 0, 0))] + in_specs
        args = [lam, g] + args
    kern = functools.partial(_attn_kernel, n_maps=n_maps, dk=dk, dv=dv, tk=tk, n_kt=n_kt, qsub=qsub,
                             lam_init=lam_init)
    return pl.pallas_call(
        kern, grid=(n_heads, lq // tq), in_specs=in_specs,
        out_specs=pl.BlockSpec((None, dv, tq), lambda h, i: (h, 0, i)),
        out_shape=jax.ShapeDtypeStruct((n_heads, dv, lq), bf16),
        compiler_params=_cparams(("parallel", "parallel")),
        name="attn_diff" if n_maps == 2 else "attn_mla",
    )(*args)


def _log_sigmoid(x):
    return jnp.minimum(x, 0.0) - jnp.log(1.0 + jnp.exp(-jnp.abs(x)))


def _softcap(x):
    return GATE_SOFTCAP * jnp.tanh(x * (1.0 / GATE_SOFTCAP))


def _mlstm_kernel(gb_ref, gbt_ref,
                  qf_ref, ktf_ref, vf_ref, gf_ref, gtf_ref,
                  qb_ref, ktb_ref, vb_ref, gb2_ref, gtb_ref,
                  hf_ref, hb_ref, c_ref, m_ref):
    t = qf_ref.shape[0]

    @pl.when(pl.program_id(0) == 0)
    def _():
        c_ref[...] = jnp.zeros_like(c_ref)
        m_ref[...] = jnp.zeros_like(m_ref)

    row = lax.broadcasted_iota(jnp.int32, (t, t), 0)
    col = lax.broadcasted_iota(jnp.int32, (t, t), 1)
    lane = lax.broadcasted_iota(jnp.int32, (t, LANE), 1)
    ones_col = jnp.where(lane == 0, 1.0, 0.0).astype(bf16)
    hi = lax.Precision.HIGHEST
    dirs = ((qf_ref, ktf_ref, vf_ref, gf_ref, gtf_ref, hf_ref),
            (qb_ref, ktb_ref, vb_ref, gb2_ref, gtb_ref, hb_ref))
    for d, (q_ref, kt_ref, v_ref, g_ref, gt_ref, o_ref) in enumerate(dirs):
        mask = (col <= row) if d == 0 else (col >= row)
        maskf = mask.astype(f32)
        last = t - 1 if d == 0 else 0
        g_tok = _softcap(g_ref[...] + gb_ref[...])
        g_t = _softcap(gt_ref[...] + gbt_ref[...])
        b_tok = jnp.dot(maskf, _log_sigmoid(g_tok), preferred_element_type=f32, precision=hi)
        b_t = lax.dot_general(_log_sigmoid(g_t), maskf, NT_DIMS, preferred_element_type=f32, precision=hi)
        for hh in range(M_HEADS):
            ri = 4 * (2 * d) + hh
            rf = 4 * (2 * d + 1) + hh
            idx = d * M_HEADS + hh
            b_col = b_tok[:, rf:rf + 1]
            b_row = b_t[rf:rf + 1, :]
            a_row = g_t[ri:ri + 1, :] - b_row
            m_prev = m_ref[idx, 0:1, 0:1]
            ld = jnp.where(mask, b_col + a_row, -jnp.inf)
            inter = b_col + m_prev
            mj = jnp.maximum(inter, jnp.max(ld, axis=1, keepdims=True))
            dmat = jnp.exp(ld - mj)
            q = q_ref[:, hh * M_DK:(hh + 1) * M_DK]
            kt = kt_ref[hh * M_DK:(hh + 1) * M_DK, :]
            vp = jnp.concatenate([v_ref[:, hh * M_DV:(hh + 1) * M_DV], ones_col], axis=1)
            sc = jnp.dot(q, kt, preferred_element_type=f32) * dmat
            cp = c_ref[idx]
            num = (jnp.dot(sc.astype(bf16), vp, preferred_element_type=f32)
                   + jnp.exp(inter - mj) * jnp.dot(q, cp.astype(bf16), preferred_element_type=f32))
            den = num[:, M_DV:M_DV + 1]
            hj = num[:, 0:M_DV] / jnp.maximum(jnp.abs(den), jnp.exp(-mj))
            o_ref[:, hh * M_DV:(hh + 1) * M_DV] = hj.astype(o_ref.dtype)
            b_tot = b_t[rf:rf + 1, last:last + 1]
            lw = b_tot + a_row
            m_new = jnp.maximum(b_tot + m_prev, jnp.max(lw, axis=1, keepdims=True))
            kw = (kt.astype(f32) * jnp.exp(lw - m_new)).astype(bf16)
            c_ref[idx] = jnp.exp(b_tot + m_prev - m_new) * cp + jnp.dot(kw, vp, preferred_element_type=f32)
            m_ref[idx] = jnp.broadcast_to(m_new, (8, LANE))


def _mlstm(mq, mkt, mv, mg, mgt, gate_b):
    n = mq.shape[0]
    t = MLSTM_T
    nb = n // t
    gb = jnp.pad(gate_b.reshape(1, 16), ((0, 0), (0, LANE - 16)))
    gbt = gate_b.reshape(16, 1)
    fwd = lambda s: jnp.where(s == 0, nb - 1, s - 1)
    bwd = lambda s: nb - 1 - s
    specs = []
    for im in (fwd, bwd):
        specs += [pl.BlockSpec((t, 512), lambda s, im=im: (im(s), 0)),
                  pl.BlockSpec((512, t), lambda s, im=im: (0, im(s))),
                  pl.BlockSpec((t, 512), lambda s, im=im: (im(s), 0)),
                  pl.BlockSpec((t, LANE), lambda s, im=im: (im(s), 0)),
                  pl.BlockSpec((16, t), lambda s, im=im: (0, im(s)))]
    return pl.pallas_call(
        _mlstm_kernel, grid=(nb,),
        in_specs=[_const_spec(gb.shape), _const_spec(gbt.shape)] + specs,
        out_specs=[pl.BlockSpec((t, 512), lambda s: (fwd(s), 0)), pl.BlockSpec((t, 512), lambda s: (bwd(s), 0))],
        out_shape=[jax.ShapeDtypeStruct((n, 512), bf16), jax.ShapeDtypeStruct((n, 512), bf16)],
        scratch_shapes=[pltpu.VMEM((2 * M_HEADS, M_DK, 2 * M_DV), f32), pltpu.VMEM((2 * M_HEADS, 8, LANE), f32)],
        compiler_params=_cparams(("arbitrary",)), name="mlstm",
    )(gb, gbt, mq, mkt, mv, mg, mgt, mq, mkt, mv, mg, mgt)


def _rms(x, axis=-1):
    return x * lax.rsqrt(jnp.mean(x * x, axis=axis, keepdims=True) + EPS)


def _merge_kernel(x_ref, hf_ref, hb_ref, mo_ref, ya_ref, yd_ref, gl_ref, mng_ref, wb_ref, wout_ref,
                  mod_ref, rw_ref, rb_ref, xnew_ref, h2_ref, logit_ref):
    hs = hf_ref[...].astype(f32) + hb_ref[...].astype(f32)
    parts = [_rms(hs[:, hh * M_DV:(hh + 1) * M_DV]) for hh in range(M_HEADS)]
    ym = jnp.concatenate(parts, axis=1) * mng_ref[...] * jax.nn.sigmoid(mo_ref[...].astype(f32))
    yb0 = jnp.dot(ym.astype(bf16), wb_ref[0], preferred_element_type=f32)
    yb1 = jnp.dot(ya_ref[...], wb_ref[1], preferred_element_type=f32)
    yb2 = jnp.dot(yd_ref[...], wb_ref[2], preferred_element_type=f32)
    d = D_MODEL
    z = (jax.nn.sigmoid(gl_ref[:, 0:d].astype(f32)) * yb0
         + jax.nn.sigmoid(gl_ref[:, d:2 * d].astype(f32)) * yb1
         + jax.nn.sigmoid(gl_ref[:, 2 * d:3 * d].astype(f32)) * yb2)
    y = jnp.dot(z.astype(bf16), wout_ref[...], preferred_element_type=f32)
    xnew = x_ref[...] + mod_ref[1:2, :] * (_rms(y) * mod_ref[0:1, :])
    h2 = _rms(xnew) * mod_ref[2:3, :] + mod_ref[3:4, :]
    xnew_ref[...] = xnew
    h2_ref[...] = h2.astype(bf16)
    logit_ref[...] = jnp.dot(h2, rw_ref[...], preferred_element_type=f32,
                             precision=lax.Precision.HIGHEST) + rb_ref[...]


def _merge(n_rows, n_lat_blocks, x_all, hf, hb, mo, ya, yd, gl, mng, wb, wout, mod, rw, rb):
    tm = ROW_BLOCK
    nb = n_rows // tm
    row = lambda i: (i, 0)
    grp = lambda i: (jnp.where(i >= n_lat_blocks, 1, 0), 0, 0)
    in_specs = [pl.BlockSpec((tm, D_MODEL), row), pl.BlockSpec((tm, 512), row), pl.BlockSpec((tm, 512), row),
                pl.BlockSpec((tm, 512), row), pl.BlockSpec((tm, 512), row), pl.BlockSpec((tm, 512), row),
                pl.BlockSpec((tm, N_BRANCH * D_MODEL), row), _const_spec(mng.shape), _const_spec(wb.shape),
                _const_spec(wout.shape), pl.BlockSpec((None, 4, D_MODEL), grp),
                _const_spec(rw.shape), _const_spec(rb.shape)]
    return pl.pallas_call(
        _merge_kernel, grid=(nb,), in_specs=in_specs,
        out_specs=[pl.BlockSpec((tm, D_MODEL), row), pl.BlockSpec((tm, D_MODEL), row), pl.BlockSpec((tm, LANE), row)],
        out_shape=[jax.ShapeDtypeStruct((n_rows, D_MODEL), f32), jax.ShapeDtypeStruct((n_rows, D_MODEL), bf16),
                   jax.ShapeDtypeStruct((n_rows, LANE), f32)],
        compiler_params=_cparams(("parallel",)), name="merge",
    )(x_all, hf, hb, mo, ya, yd, gl, mng, wb, wout, mod, rw, rb)


def _moe_kernel(be_ref, nu_ref, xs_ref, sw_ref, wgu_ref, bgu_ref, wd_ref, bd_ref, o_ref, wgu_s, wd_s):
    i = pl.program_id(0)
    prev = be_ref[jnp.maximum(i - 1, 0)]
    used = i < nu_ref[0]

    @pl.when(jnp.logical_and(used, jnp.logical_or(i == 0, be_ref[i] != prev)))
    def _():
        wgu_s[...] = wgu_ref[...].astype(bf16)
        wd_s[...] = wd_ref[...].astype(bf16)

    @pl.when(used)
    def _():
        gu = jnp.dot(xs_ref[...], wgu_s[...], preferred_element_type=f32) + bgu_ref[...]
        gate = jnp.minimum(gu[:, :D_EXPERT], SWIGLU_LIMIT)
        up = jnp.clip(gu[:, D_EXPERT:], -SWIGLU_LIMIT, SWIGLU_LIMIT)
        act = (up + 1.0) * (gate * jax.nn.sigmoid(SWIGLU_ALPHA * gate))
        y = jnp.dot(act.astype(bf16), wd_s[...], preferred_element_type=f32) + bd_ref[...]
        o_ref[...] = (y * sw_ref[...]).astype(o_ref.dtype)

    @pl.when(jnp.logical_not(used))
    def _():
        o_ref[...] = jnp.zeros_like(o_ref)


def _moe_experts(blk_e, n_used, xs, slot_w, w_gu, b_gu, w_down, b_down):
    cap = xs.shape[0]
    tm = MOE_TM
    nblk = cap // tm
    grid_spec = pltpu.PrefetchScalarGridSpec(
        num_scalar_prefetch=2, grid=(nblk,),
        in_specs=[pl.BlockSpec((tm, D_MODEL), lambda i, be, nu: (i, 0)),
                  pl.BlockSpec((tm, 1), lambda i, be, nu: (i, 0)),
                  pl.BlockSpec((None, D_MODEL, 2 * D_EXPERT), lambda i, be, nu: (be[i], 0, 0)),
                  pl.BlockSpec((None, 1, 2 * D_EXPERT), lambda i, be, nu: (be[i], 0, 0)),
                  pl.BlockSpec((None, D_EXPERT, D_MODEL), lambda i, be, nu: (be[i], 0, 0)),
                  pl.BlockSpec((None, 1, D_MODEL), lambda i, be, nu: (be[i], 0, 0))],
        out_specs=pl.BlockSpec((tm, D_MODEL), lambda i, be, nu: (i, 0)),
        scratch_shapes=[pltpu.VMEM((D_MODEL, 2 * D_EXPERT), bf16), pltpu.VMEM((D_EXPERT, D_MODEL), bf16)])
    return pl.pallas_call(
        _moe_kernel, grid_spec=grid_spec,
        out_shape=jax.ShapeDtypeStruct((cap, D_MODEL), bf16),
        compiler_params=_cparams(("arbitrary",)), name="moe_experts",
    )(blk_e, n_used, xs, slot_w, w_gu, b_gu.reshape(N_EXPERTS, 1, -1), w_down, b_down.reshape(N_EXPERTS, 1, -1))


def _moe(h2, logits, w_gu, b_gu, w_down, b_down):
    n = h2.shape[0]
    tm = MOE_TM
    top_v, top_i = lax.top_k(logits[:, :N_EXPERTS], TOP_K)
    wts = jax.nn.softmax(top_v, axis=-1)
    nk = n * TOP_K
    flat_e = top_i.reshape(nk).astype(jnp.int32)
    iota = jnp.arange(nk, dtype=jnp.int32)
    se, order = lax.sort((flat_e, iota), num_keys=1, is_stable=True)
    _, inv = lax.sort((order, iota), num_keys=1)
    bounds = jnp.searchsorted(se, jnp.arange(N_EXPERTS + 1, dtype=jnp.int32), side='left').astype(jnp.int32)
    grp_start = bounds[:-1]
    counts = bounds[1:] - grp_start
    padded = (counts + tm - 1) // tm * tm
    pad_end = jnp.cumsum(padded)
    pad_start = pad_end - padded
    cap = -(-(nk + N_EXPERTS * (tm - 1)) // tm) * tm
    nblk = cap // tm
    blk_e = jnp.minimum(jnp.searchsorted(pad_end, jnp.arange(nblk, dtype=jnp.int32) * tm, side='right'),
                        N_EXPERTS - 1).astype(jnp.int32)
    n_used = (pad_end[-1] // tm).astype(jnp.int32).reshape(1)
    p = jnp.arange(cap, dtype=jnp.int32)
    e_p = blk_e[p // tm]
    r = p - pad_start[e_p]
    valid = r < counts[e_p]
    src = jnp.clip(grp_start[e_p] + r, 0, nk - 1)
    slot_t = jnp.where(valid, order[src] // TOP_K, 0)
    slot_w = jnp.where(valid, wts.reshape(nk)[order[src]], 0.0).reshape(cap, 1)
    dest_sorted = pad_start[se] + (iota - grp_start[se])
    pos = dest_sorted[inv].reshape(n, TOP_K)
    xs = h2[slot_t]
    ys = _moe_experts(blk_e, n_used, xs, slot_w, w_gu, b_gu, w_down, b_down)
    return jnp.sum(ys[pos].astype(f32), axis=1)


def _resid_kernel(x_ref, f_ref, mod_ref, o_ref):
    o_ref[...] = x_ref[...] + mod_ref[1:2, :] * (_rms(f_ref[...]) * mod_ref[0:1, :])


def _resid(n_lat_blocks, xnew, f, mod):
    n = xnew.shape[0]
    tm = ROW_BLOCK
    row = lambda i: (i, 0)
    grp = lambda i: (jnp.where(i >= n_lat_blocks, 1, 0), 0, 0)
    return pl.pallas_call(
        _resid_kernel, grid=(n // tm,),
        in_specs=[pl.BlockSpec((tm, D_MODEL), row), pl.BlockSpec((tm, D_MODEL), row),
                  pl.BlockSpec((None, 2, D_MODEL), grp)],
        out_specs=pl.BlockSpec((tm, D_MODEL), row),
        out_shape=jax.ShapeDtypeStruct((n, D_MODEL), f32),
        compiler_params=_cparams(("parallel",)), name="resid",
    )(xnew, f, mod)


def _rope_tables(n_lat, n_ctx, dim):
    rows = n_lat // GRID_W
    row = jnp.repeat(jnp.arange(rows, dtype=f32), GRID_W)
    col = jnp.tile(jnp.arange(GRID_W, dtype=f32), rows)
    quarter = dim // 4
    inv = ROPE_BASE ** (-jnp.arange(quarter, dtype=f32) / quarter)
    ang = jnp.concatenate([row[:, None] * inv, col[:, None] * inv], axis=-1)
    cos = jnp.concatenate([jnp.cos(ang), jnp.ones((n_ctx, dim // 2), f32)], axis=0).T
    sin = jnp.concatenate([jnp.sin(ang), jnp.zeros((n_ctx, dim // 2), f32)], axis=0).T
    return cos, sin


def kernel(x, c, ctx, c_ctx, ada_w, ada_b, norm_g, w_in, m_gate_b, m_norm_g, q_norm_g, w_uq, kv_norm_g, w_ukv,
           diff_lam, diff_norm_g, w_branch, w_out, router_w, router_b, w_gu, b_gu, w_down, b_down):
    n_lat = x.shape[1]
    n_ctx = ctx.shape[1]
    n = n_lat + n_ctx
    assert x.shape[0] == 1 and n_lat % ROW_BLOCK == 0 and n_ctx == MLSTM_T and n_lat % GRID_W == 0
    nlb = n_lat // ROW_BLOCK
    x_all = jnp.concatenate([x[0], ctx[0]], axis=0)

    cond = jnp.zeros((8, D_MODEL), f32).at[0].set(c[0]).at[1].set(c_ctx)
    mods = _adaln(cond, ada_w, ada_b)[:, :2].reshape(DEPTH, 2, 6, D_MODEL)

    ca, sa = _rope_tables(n_lat, n_ctx, A_ROPE)
    cd, sd = _rope_tables(n_lat, n_ctx, DF_DK)

    for l in range(DEPTH):
        need_ctx = l < DEPTH - 1
        lam_init = 0.8 - 0.6 * math.exp(-0.3 * l)
        sh1, sc1, ga1, sh2, sc2, ga2 = [mods[l, :, j] for j in range(6)]
        ab1 = jnp.stack([norm_g[l, 0][None] * (1.0 + sc1), sh1], axis=1)
        w_tok, w_tr, wt_uq, wt_ukv = _pack_inproj_weights(w_in[l], w_uq[l], w_ukv[l])
        (mq, mv, mo, mg, gl, mkt, mgt, qta, kta, vta, qtd, ktd, vtd) = _inproj(
            x_all, ab1, nlb, w_tok, w_tr, wt_uq, wt_ukv,
            q_norm_g[l].reshape(Q_LORA, 1), kv_norm_g[l].reshape(KV_LORA, 1), ca, sa, cd, sd)

        hf, hb = _mlstm(mq, mkt, mv, mg, mgt, m_gate_b[l])

        k_a = kta.transpose(0, 2, 1)
        k_d = ktd.transpose(0, 2, 1)
        qtd4 = qtd.reshape(DF_HEADS, 2 * DF_DK, n)
        dg = diff_norm_g[l].reshape(DF_HEADS, DF_DV, 1)
        attn_a = functools.partial(_attention, n_maps=1, dk=A_DKP, dv=A_V, tq=ATTN_TQ_MLA)
        attn_d = functools.partial(_attention, n_maps=2, dk=DF_DK, dv=DF_DV, tq=ATTN_TQ_DIFF,
                                   lam=diff_lam[l], g=dg, lam_init=lam_init)
        ya_t = attn_a(qta[:, :, :n_lat], k_a, vta)
        yd_t = attn_d(qtd4[:, :, :n_lat], k_d, vtd)
        if need_ctx:
            ya_t = jnp.concatenate([ya_t, attn_a(qta[:, :, n_lat:], k_a[:, n_lat:], vta[:, :, n_lat:])], axis=2)
            yd_t = jnp.concatenate([yd_t, attn_d(qtd4[:, :, n_lat:], k_d[:, n_lat:], vtd[:, :, n_lat:])], axis=2)
        n_rows = n if need_ctx else n_lat
        ya = ya_t.transpose(2, 0, 1).reshape(n_rows, A_HEADS * A_V)
        yd = yd_t.transpose(2, 0, 1).reshape(n_rows, DF_HEADS * DF_DV)

        mod_m = jnp.stack([jnp.broadcast_to(norm_g[l, 1][None], (2, D_MODEL)), ga1,
                           norm_g[l, 2][None] * (1.0 + sc2), sh2], axis=1)
        rw = jnp.pad(router_w[l], ((0, 0), (0, LANE - N_EXPERTS)))
        rb = jnp.pad(router_b[l].reshape(1, N_EXPERTS), ((0, 0), (0, LANE - N_EXPERTS)))
        xnew, h2, logits = _merge(n_rows, nlb, x_all, hf, hb, mo, ya, yd, gl,
                                  m_norm_g[l].reshape(1, -1), w_branch[l].astype(bf16), w_out[l].astype(bf16),
                                  mod_m, rw, rb)
        f = _moe(h2, logits, w_gu[l], b_gu[l], w_down[l], b_down[l])
        mod_r = jnp.stack([jnp.broadcast_to(norm_g[l, 3][None], (2, D_MODEL)), ga2], axis=1)
        x_all = _resid(nlb, xnew, f, mod_r)
    return x_all[:n_lat][None]
```

```python
import functools
import math

import jax
import jax.numpy as jnp
from jax import lax
from jax.experimental import pallas as pl
from jax.experimental.pallas import tpu as pltpu

f32 = jnp.float32
bf16 = jnp.bfloat16

D_MODEL = 1024
DEPTH = 2
GRID_W = 64
EPS = 1e-6
ROPE_BASE = 10000.0
M_HEADS = 4
M_DK = 128
M_DV = 128
GATE_SOFTCAP = 15.0
A_HEADS = 8
A_NOPE = 64
A_ROPE = 32
A_V = 64
Q_LORA = 256
KV_LORA = 128
DF_HEADS = 4
DF_DK = 64
DF_DV = 128
N_BRANCH = 3
BRANCH_W = 512
N_EXPERTS = 32
TOP_K = 4
D_EXPERT = 1024
SWIGLU_LIMIT = 7.0
SWIGLU_ALPHA = 1.702

IN_SIZES = (M_HEADS * M_DK, M_HEADS * M_DK, M_HEADS * M_DV, M_HEADS * M_DV, 4 * M_HEADS,
            Q_LORA, KV_LORA, A_ROPE,
            2 * DF_HEADS * DF_DK, 2 * DF_HEADS * DF_DK, DF_HEADS * DF_DV,
            N_BRANCH * D_MODEL)

LOG2E = 1.4426950408889634
LANE = 128
A_DKP = 128
A_DVP = A_V + 16
DF_DVP = DF_DV + 16
VMEM_LIMIT = 56 * 1024 * 1024

ROW_BLOCK = 256
MXU_DEPTH = 256
ATTN_TQ_MLA = 512
ATTN_TQ_DIFF = 256
ATTN_QSUB = 256
ATTN_TK = 1280
MLSTM_T = 256
MOE_TM = 512
H2_ROWS = 32768

NT_DIMS = (((1,), (1,)), ((), ()))


def _cparams(sem):
    return pltpu.CompilerParams(dimension_semantics=sem, vmem_limit_bytes=VMEM_LIMIT)


def _const_spec(shape):
    nd = len(shape)
    return pl.BlockSpec(shape, lambda *_: (0,) * nd)


def _adaln_kernel(cond_ref, w_ref, b_ref, o_ref):
    cnd = cond_ref[...]
    a = cnd * jax.nn.sigmoid(cnd)
    o_ref[...] = jnp.dot(a, w_ref[...], preferred_element_type=f32,
                         precision=lax.Precision.HIGHEST) + b_ref[...]


def _adaln(cond, ada_w, ada_b):
    tn = 1536
    nj = (6 * D_MODEL) // tn
    return pl.pallas_call(
        _adaln_kernel,
        grid=(DEPTH, nj),
        in_specs=[pl.BlockSpec((8, D_MODEL), lambda l, j: (0, 0)),
                  pl.BlockSpec((None, D_MODEL, tn), lambda l, j: (l, 0, j)),
                  pl.BlockSpec((None, 1, tn), lambda l, j: (l, 0, j))],
        out_specs=pl.BlockSpec((None, 8, tn), lambda l, j: (l, 0, j)),
        out_shape=jax.ShapeDtypeStruct((DEPTH, 8, 6 * D_MODEL), f32),
        compiler_params=_cparams(("parallel", "parallel")),
        name="adaln",
    )(cond, ada_w, ada_b.reshape(DEPTH, 1, 6 * D_MODEL))


TOK_W = 3 * 512 + LANE + N_BRANCH * D_MODEL
TR_W = 512 + 16 + Q_LORA + KV_LORA + A_ROPE + 3 * 512


def _ones_row_block(tm):
    r = lax.broadcasted_iota(jnp.int32, (16, tm), 0)
    return jnp.where(r == 0, 1.0, 0.0).astype(bf16)


def _inproj_kernel(x_ref, ab_ref, wtok_ref, wtr_ref, wuq_ref, wukv_ref, gq_ref, gkv_ref,
                   ca_ref, sa_ref, cd_ref, sd_ref,
                   mq_ref, mv_ref, mo_ref, mg_ref, gl_ref, mkt_ref, mgt_ref,
                   qta_ref, kta_ref, vta_ref, qtd_ref, ktd_ref, vtd_ref):
    tm = x_ref.shape[0]
    x = x_ref[...]
    ms = jnp.mean(x * x, axis=-1, keepdims=True)
    h = (x * lax.rsqrt(ms + EPS)) * ab_ref[0:1, :] + ab_ref[1:2, :]
    hb = h.astype(bf16)

    tok = jnp.dot(hb, wtok_ref[...], preferred_element_type=f32)
    mq_ref[...] = (tok[:, 0:512] * (M_DK ** -0.5)).astype(bf16)
    mv_ref[...] = tok[:, 512:1024].astype(bf16)
    mo_ref[...] = tok[:, 1024:1536].astype(bf16)
    mg_ref[...] = tok[:, 1536:1664]
    gl_ref[...] = tok[:, 1664:TOK_W].astype(bf16)

    tr = lax.dot_general(wtr_ref[...], hb, NT_DIMS, preferred_element_type=f32)
    mkt_ref[...] = tr[0:512].astype(bf16)
    mgt_ref[...] = tr[512:528]
    cq = tr[528:784]
    ckv = tr[784:912]
    kpe = tr[912:944]
    dq = tr[944:1456]
    dk = tr[1456:1968]
    dv = tr[1968:2480]

    ca = ca_ref[...]
    sa = sa_ref[...]
    cd = cd_ref[...]
    sd = sd_ref[...]
    ones_blk = _ones_row_block(tm)
    zeros32 = jnp.zeros((32, tm), bf16)

    cqn = (cq * lax.rsqrt(jnp.mean(cq * cq, axis=0, keepdims=True) + EPS) * gq_ref[...]).astype(bf16)
    qscale = ((A_NOPE + A_ROPE) ** -0.5) * LOG2E
    qt = jnp.dot(wuq_ref[...], cqn, preferred_element_type=f32) * qscale
    for hh in range(A_HEADS):
        a = qt[512 + 16 * hh:528 + 16 * hh]
        b = qt[640 + 16 * hh:656 + 16 * hh]
        qta_ref[hh, 0:64, :] = qt[64 * hh:64 * hh + 64].astype(bf16)
        qta_ref[hh, 64:80, :] = (a * ca - b * sa).astype(bf16)
        qta_ref[hh, 80:96, :] = (a * sa + b * ca).astype(bf16)
        qta_ref[hh, 96:128, :] = zeros32

    ckvn = (ckv * lax.rsqrt(jnp.mean(ckv * ckv, axis=0, keepdims=True) + EPS) * gkv_ref[...]).astype(bf16)
    kvt = jnp.dot(wukv_ref[...], ckvn, preferred_element_type=f32)
    ka = kpe[0:16]
    kb = kpe[16:32]
    kpe_rot = jnp.concatenate([ka * ca - kb * sa, ka * sa + kb * ca, jnp.zeros((32, tm), f32)], axis=0)
    for hh in range(A_HEADS):
        kt_h = jnp.concatenate([kvt[64 * hh:64 * hh + 64], kpe_rot], axis=0)
        kta_ref[hh] = kt_h.T.astype(bf16)
        vta_ref[hh, 0:64, :] = kvt[512 + 64 * hh:576 + 64 * hh].astype(bf16)
        vta_ref[hh, 64:80, :] = ones_blk

    dscale = (DF_DK ** -0.5) * LOG2E
    for hh in range(2 * DF_HEADS):
        a = dq[32 * hh:32 * hh + 32]
        b = dq[256 + 32 * hh:288 + 32 * hh]
        qtd_ref[hh, 0:32, :] = ((a * cd - b * sd) * dscale).astype(bf16)
        qtd_ref[hh, 32:64, :] = ((a * sd + b * cd) * dscale).astype(bf16)
        a = dk[32 * hh:32 * hh + 32]
        b = dk[256 + 32 * hh:288 + 32 * hh]
        ktd_ref[hh] = jnp.concatenate([a * cd - b * sd, a * sd + b * cd], axis=0).T.astype(bf16)
    for hh in range(DF_HEADS):
        vtd_ref[hh, 0:128, :] = dv[128 * hh:128 * hh + 128].astype(bf16)
        vtd_ref[hh, 128:144, :] = ones_blk


def _pack_inproj_weights(w, w_uq, w_ukv):
    o = [0]
    for s in IN_SIZES:
        o.append(o[-1] + s)
    mq, mk, mv, mo, mg, cq, ckv, kpe, dq, dk, dv, gl = [w[:, o[i]:o[i + 1]] for i in range(12)]
    mg_pad = jnp.pad(mg, ((0, 0), (0, LANE - mg.shape[1])))
    w_tok = jnp.concatenate([mq, mv, mo, mg_pad, gl], axis=1).astype(bf16)

    def split_halves(t, n_heads, d):
        t3 = t.reshape(t.shape[0], n_heads, d)
        return jnp.concatenate([t3[:, :, :d // 2].reshape(t.shape[0], -1),
                                t3[:, :, d // 2:].reshape(t.shape[0], -1)], axis=1)

    w_tr = jnp.concatenate([mk, mg, cq, ckv, kpe, split_halves(dq, 2 * DF_HEADS, DF_DK),
                            split_halves(dk, 2 * DF_HEADS, DF_DK), dv], axis=1).T.astype(bf16)
    uq = w_uq.reshape(Q_LORA, A_HEADS, A_NOPE + A_ROPE)
    half = A_ROPE // 2
    wt_uq = jnp.concatenate([uq[:, :, :A_NOPE].reshape(Q_LORA, -1),
                             uq[:, :, A_NOPE:A_NOPE + half].reshape(Q_LORA, -1),
                             uq[:, :, A_NOPE + half:].reshape(Q_LORA, -1)], axis=1).T.astype(bf16)
    ukv = w_ukv.reshape(KV_LORA, A_HEADS, A_NOPE + A_V)
    wt_ukv = jnp.concatenate([ukv[:, :, :A_NOPE].reshape(KV_LORA, -1),
                              ukv[:, :, A_NOPE:].reshape(KV_LORA, -1)], axis=1).T.astype(bf16)
    return w_tok, w_tr, wt_uq, wt_ukv


def _inproj(x_all, ab, n_lat_blocks, w_tok, w_tr, wt_uq, wt_ukv, gq, gkv, ca, sa, cd, sd):
    n = x_all.shape[0]
    tm = ROW_BLOCK
    nb = n // tm
    row = lambda i: (i, 0)
    colb = lambda i: (0, i)
    col3 = lambda i: (0, 0, i)
    grp = lambda i: (jnp.where(i >= n_lat_blocks, 1, 0), 0, 0)
    out_shapes = [
        jax.ShapeDtypeStruct((n, 512), bf16), jax.ShapeDtypeStruct((n, 512), bf16),
        jax.ShapeDtypeStruct((n, 512), bf16), jax.ShapeDtypeStruct((n, LANE), f32),
        jax.ShapeDtypeStruct((n, N_BRANCH * D_MODEL), bf16),
        jax.ShapeDtypeStruct((512, n), bf16), jax.ShapeDtypeStruct((16, n), f32),
        jax.ShapeDtypeStruct((A_HEADS, A_DKP, n), bf16), jax.ShapeDtypeStruct((A_HEADS, n, A_DKP), bf16),
        jax.ShapeDtypeStruct((A_HEADS, A_DVP, n), bf16),
        jax.ShapeDtypeStruct((2 * DF_HEADS, DF_DK, n), bf16), jax.ShapeDtypeStruct((2 * DF_HEADS, n, DF_DK), bf16),
        jax.ShapeDtypeStruct((DF_HEADS, DF_DVP, n), bf16),
    ]
    out_specs = [
        pl.BlockSpec((tm, 512), row), pl.BlockSpec((tm, 512), row), pl.BlockSpec((tm, 512), row),
        pl.BlockSpec((tm, LANE), row), pl.BlockSpec((tm, N_BRANCH * D_MODEL), row),
        pl.BlockSpec((512, tm), colb), pl.BlockSpec((16, tm), colb),
        pl.BlockSpec((A_HEADS, A_DKP, tm), col3), pl.BlockSpec((A_HEADS, tm, A_DKP), lambda i: (0, i, 0)),
        pl.BlockSpec((A_HEADS, A_DVP, tm), col3),
        pl.BlockSpec((2 * DF_HEADS, DF_DK, tm), col3), pl.BlockSpec((2 * DF_HEADS, tm, DF_DK), lambda i: (0, i, 0)),
        pl.BlockSpec((DF_HEADS, DF_DVP, tm), col3),
    ]
    in_specs = [
        pl.BlockSpec((tm, D_MODEL), row),
        pl.BlockSpec((None, 2, D_MODEL), grp),
        _const_spec(w_tok.shape), _const_spec(w_tr.shape), _const_spec(wt_uq.shape), _const_spec(wt_ukv.shape),
        _const_spec(gq.shape), _const_spec(gkv.shape),
        pl.BlockSpec((A_ROPE // 2, tm), colb), pl.BlockSpec((A_ROPE // 2, tm), colb),
        pl.BlockSpec((DF_DK // 2, tm), colb), pl.BlockSpec((DF_DK // 2, tm), colb),
    ]
    return pl.pallas_call(
        _inproj_kernel, grid=(nb,), in_specs=in_specs, out_specs=out_specs, out_shape=out_shapes,
        compiler_params=_cparams(("parallel",)), name="inproj",
    )(x_all, ab, w_tok, w_tr, wt_uq, wt_ukv, gq, gkv, ca, sa, cd, sd)


def _attn_kernel(*refs, n_maps, dk, dv, tk, n_kt, qsub, lam_init, n_in):
    o_ref = refs[n_in]
    if n_maps == 2:
        lam_ref, g_ref, qt_ref, k_ref, vt_ref = refs[:5]
    else:
        qt_ref, k_ref, vt_ref = refs[:3]
    tq = qt_ref.shape[-1]
    dvp = vt_ref.shape[-2]
    chains = [(m, c) for m in range(n_maps) for c in range(tq // qsub)]

    def qk(j, m, c):
        qt = qt_ref[m * dk:(m + 1) * dk, c * qsub:(c + 1) * qsub]
        return jnp.dot(k_ref[m, j * tk:(j + 1) * tk, :], qt, preferred_element_type=f32)

    state = [(jnp.full((1, qsub), -jnp.inf, f32), jnp.zeros((dvp, qsub), f32)) for _ in chains]
    s_next = [qk(0, m, c) for (m, c) in chains]
    for j in range(n_kt):
        s_cur = s_next
        if j + 1 < n_kt:
            s_next = [qk(j + 1, m, c) for (m, c) in chains]
        vtj = vt_ref[j]
        for i in range(len(chains)):
            m_run, acc = state[i]
            m_new = jnp.maximum(m_run, jnp.max(s_cur[i], axis=0, keepdims=True))
            alpha = jnp.exp2(m_run - m_new)
            p = jnp.exp2(s_cur[i] - m_new).astype(bf16)
            pv = jnp.dot(vtj, p, preferred_element_type=f32)
            state[i] = (m_new, alpha * acc + pv)
    outs = []
    for m in range(n_maps):
        accs = [state[i][1] for i, (mm, _) in enumerate(chains) if mm == m]
        acc = accs[0] if len(accs) == 1 else jnp.concatenate(accs, axis=1)
        outs.append(acc[0:dv] / acc[dv:dv + 1])
    if n_maps == 1:
        o_ref[...] = outs[0].astype(o_ref.dtype)
    else:
        lf = lam_ref[...]
        lam = (jnp.exp(jnp.sum(lf[0:1] * lf[1:2], axis=1, keepdims=True))
               - jnp.exp(jnp.sum(lf[2:3] * lf[3:4], axis=1, keepdims=True)) + lam_init)
        o = outs[0] - lam * outs[1]
        o = o * lax.rsqrt(jnp.mean(o * o, axis=0, keepdims=True) + EPS) * g_ref[...]
        o_ref[...] = (o * (1.0 - lam_init)).astype(o_ref.dtype)


def _key_tile(lk):
    return max(t for t in range(MXU_DEPTH, min(ATTN_TK, lk) + 1, MXU_DEPTH) if lk % t == 0)


def _attention(qt, k, vt, *, n_maps, dk, dv, tq, lq, n_out, q_start=0, out=None, lam=None, g=None, lam_init=0.0):
    n_heads = qt.shape[0]
    lk = k.shape[1]
    dvp = vt.shape[1]
    tq = min(tq, lq)
    qsub = min(ATTN_QSUB, tq)
    tk = _key_tile(lk)
    n_kt = lk // tk
    qb0 = q_start // tq
    vt4 = vt.reshape(n_heads, dvp, n_kt, tk).transpose(0, 2, 1, 3)
    in_specs = [
        pl.BlockSpec((None, n_maps * dk, tq), lambda h, i: (h, 0, qb0 + i)),
        pl.BlockSpec((n_maps, lk, dk), lambda h, i: (h, 0, 0)),
        pl.BlockSpec((None, n_kt, dvp, tk), lambda h, i: (h, 0, 0, 0)),
    ]
    args = [qt, k, vt4]
    if n_maps == 2:
        in_specs = [_const_spec(lam.shape), pl.BlockSpec((None, dv, 1), lambda h, i: (h, 0, 0))] + in_specs
        args = [lam, g] + args
    aliases = {}
    if out is not None:
        aliases = {len(args): 0}
        in_specs.append(pl.BlockSpec(memory_space=pl.ANY))
        args.append(out)
    kern = functools.partial(_attn_kernel, n_maps=n_maps, dk=dk, dv=dv, tk=tk, n_kt=n_kt, qsub=qsub,
                             lam_init=lam_init, n_in=len(args))
    return pl.pallas_call(
        kern, grid=(n_heads, lq // tq), in_specs=in_specs,
        out_specs=pl.BlockSpec((None, dv, tq), lambda h, i: (h, 0, qb0 + i)),
        out_shape=jax.ShapeDtypeStruct((n_heads, dv, n_out), bf16),
        input_output_aliases=aliases,
        compiler_params=_cparams(("parallel", "parallel")),
        name="attn_diff" if n_maps == 2 else "attn_mla",
    )(*args)


def _log_sigmoid(x):
    return jnp.minimum(x, 0.0) - jnp.log(1.0 + jnp.exp(-jnp.abs(x)))


def _softcap(x):
    return GATE_SOFTCAP * jnp.tanh(x * (1.0 / GATE_SOFTCAP))


def _mlstm_kernel(gb_ref, gbt_ref,
                  qf_ref, ktf_ref, vf_ref, gf_ref, gtf_ref,
                  qb_ref, ktb_ref, vb_ref, gb2_ref, gtb_ref,
                  hf_ref, hb_ref, c_ref, m_ref):
    t = qf_ref.shape[0]

    @pl.when(pl.program_id(0) == 0)
    def _():
        c_ref[...] = jnp.zeros_like(c_ref)
        m_ref[...] = jnp.zeros_like(m_ref)

    row = lax.broadcasted_iota(jnp.int32, (t, t), 0)
    col = lax.broadcasted_iota(jnp.int32, (t, t), 1)
    lane = lax.broadcasted_iota(jnp.int32, (t, LANE), 1)
    ones_col = jnp.where(lane == 0, 1.0, 0.0).astype(bf16)
    hi = lax.Precision.HIGHEST
    dirs = ((qf_ref, ktf_ref, vf_ref, gf_ref, gtf_ref, hf_ref),
            (qb_ref, ktb_ref, vb_ref, gb2_ref, gtb_ref, hb_ref))
    for d, (q_ref, kt_ref, v_ref, g_ref, gt_ref, o_ref) in enumerate(dirs):
        mask = (col <= row) if d == 0 else (col >= row)
        maskf = mask.astype(f32)
        last = t - 1 if d == 0 else 0
        g_tok = _softcap(g_ref[...] + gb_ref[...])
        g_t = _softcap(gt_ref[...] + gbt_ref[...])
        b_tok = jnp.dot(maskf, _log_sigmoid(g_tok), preferred_element_type=f32, precision=hi)
        b_t = lax.dot_general(_log_sigmoid(g_t), maskf, NT_DIMS, preferred_element_type=f32, precision=hi)
        for hh in range(M_HEADS):
            ri = 4 * (2 * d) + hh
            rf = 4 * (2 * d + 1) + hh
            idx = d * M_HEADS + hh
            b_col = b_tok[:, rf:rf + 1]
            b_row = b_t[rf:rf + 1, :]
            a_row = g_t[ri:ri + 1, :] - b_row
            m_prev = m_ref[idx, 0:1, 0:1]
            ld = jnp.where(mask, b_col + a_row, -jnp.inf)
            inter = b_col + m_prev
            mj = jnp.maximum(inter, jnp.max(ld, axis=1, keepdims=True))
            dmat = jnp.exp(ld - mj)
            q = q_ref[:, hh * M_DK:(hh + 1) * M_DK]
            kt = kt_ref[hh * M_DK:(hh + 1) * M_DK, :]
            vp = jnp.concatenate([v_ref[:, hh * M_DV:(hh + 1) * M_DV], ones_col], axis=1)
            sc = jnp.dot(q, kt, preferred_element_type=f32) * dmat
            cp = c_ref[idx]
            num = (jnp.dot(sc.astype(bf16), vp, preferred_element_type=f32)
                   + jnp.exp(inter - mj) * jnp.dot(q, cp.astype(bf16), preferred_element_type=f32))
            den = num[:, M_DV:M_DV + 1]
            hj = num[:, 0:M_DV] / jnp.maximum(jnp.abs(den), jnp.exp(-mj))
            o_ref[:, hh * M_DV:(hh + 1) * M_DV] = hj.astype(o_ref.dtype)
            b_tot = b_t[rf:rf + 1, last:last + 1]
            lw = b_tot + a_row
            m_new = jnp.maximum(b_tot + m_prev, jnp.max(lw, axis=1, keepdims=True))
            kw = (kt.astype(f32) * jnp.exp(lw - m_new)).astype(bf16)
            c_ref[idx] = jnp.exp(b_tot + m_prev - m_new) * cp + jnp.dot(kw, vp, preferred_element_type=f32)
            m_ref[idx] = jnp.broadcast_to(m_new, (8, LANE))


def _mlstm(mq, mkt, mv, mg, mgt, gate_b):
    n = mq.shape[0]
    t = MLSTM_T
    nb = n // t
    gb = jnp.pad(gate_b.reshape(1, 16), ((0, 0), (0, LANE - 16)))
    gbt = gate_b.reshape(16, 1)
    fwd = lambda s: jnp.where(s == 0, nb - 1, s - 1)
    bwd = lambda s: nb - 1 - s
    specs = []
    for im in (fwd, bwd):
        specs += [pl.BlockSpec((t, 512), lambda s, im=im: (im(s), 0)),
                  pl.BlockSpec((512, t), lambda s, im=im: (0, im(s))),
                  pl.BlockSpec((t, 512), lambda s, im=im: (im(s), 0)),
                  pl.BlockSpec((t, LANE), lambda s, im=im: (im(s), 0)),
                  pl.BlockSpec((16, t), lambda s, im=im: (0, im(s)))]
    return pl.pallas_call(
        _mlstm_kernel, grid=(nb,),
        in_specs=[_const_spec(gb.shape), _const_spec(gbt.shape)] + specs,
        out_specs=[pl.BlockSpec((t, 512), lambda s: (fwd(s), 0)), pl.BlockSpec((t, 512), lambda s: (bwd(s), 0))],
        out_shape=[jax.ShapeDtypeStruct((n, 512), bf16), jax.ShapeDtypeStruct((n, 512), bf16)],
        scratch_shapes=[pltpu.VMEM((2 * M_HEADS, M_DK, 2 * M_DV), f32), pltpu.VMEM((2 * M_HEADS, 8, LANE), f32)],
        compiler_params=_cparams(("arbitrary",)), name="mlstm",
    )(gb, gbt, mq, mkt, mv, mg, mgt, mq, mkt, mv, mg, mgt)


def _rms(x, axis=-1):
    return x * lax.rsqrt(jnp.mean(x * x, axis=axis, keepdims=True) + EPS)


def _merge_kernel(x_ref, hf_ref, hb_ref, mo_ref, ya_ref, yd_ref, gl_ref, mng_ref, wb_ref, wout_ref,
                  mod_ref, rw_ref, rb_ref, xnew_ref, h2_ref, logit_ref):
    hs = hf_ref[...].astype(f32) + hb_ref[...].astype(f32)
    parts = [_rms(hs[:, hh * M_DV:(hh + 1) * M_DV]) for hh in range(M_HEADS)]
    ym = jnp.concatenate(parts, axis=1) * mng_ref[...] * jax.nn.sigmoid(mo_ref[...].astype(f32))
    yb0 = jnp.dot(ym.astype(bf16), wb_ref[0], preferred_element_type=f32)
    tm = x_ref.shape[0]
    tn_dims = (((0,), (0,)), ((), ()))
    yb1 = lax.dot_general(ya_ref[...].reshape(BRANCH_W, tm), wb_ref[1], tn_dims, preferred_element_type=f32)
    yb2 = lax.dot_general(yd_ref[...].reshape(BRANCH_W, tm), wb_ref[2], tn_dims, preferred_element_type=f32)
    d = D_MODEL
    z = (jax.nn.sigmoid(gl_ref[:, 0:d].astype(f32)) * yb0
         + jax.nn.sigmoid(gl_ref[:, d:2 * d].astype(f32)) * yb1
         + jax.nn.sigmoid(gl_ref[:, 2 * d:3 * d].astype(f32)) * yb2)
    y = jnp.dot(z.astype(bf16), wout_ref[...], preferred_element_type=f32)
    xnew = x_ref[...] + mod_ref[1:2, :] * (_rms(y) * mod_ref[0:1, :])
    h2 = _rms(xnew) * mod_ref[2:3, :] + mod_ref[3:4, :]
    xnew_ref[...] = xnew
    h2_ref[...] = h2.astype(bf16)
    logit_ref[...] = jnp.dot(h2, rw_ref[...], preferred_element_type=f32,
                             precision=lax.Precision.HIGHEST) + rb_ref[...]


def _merge(n_rows, n_lat_blocks, x_all, hf, hb, mo, ya, yd, gl, mng, wb, wout, mod, rw, rb):
    tm = ROW_BLOCK
    nb = n_rows // tm
    row = lambda i: (i, 0)
    grp = lambda i: (jnp.where(i >= n_lat_blocks, 1, 0), 0, 0)
    in_specs = [pl.BlockSpec((tm, D_MODEL), row), pl.BlockSpec((tm, 512), row), pl.BlockSpec((tm, 512), row),
                pl.BlockSpec((tm, 512), row),
                pl.BlockSpec((A_HEADS, A_V, tm), lambda i: (0, 0, i)),
                pl.BlockSpec((DF_HEADS, DF_DV, tm), lambda i: (0, 0, i)),
                pl.BlockSpec((tm, N_BRANCH * D_MODEL), row), _const_spec(mng.shape), _const_spec(wb.shape),
                _const_spec(wout.shape), pl.BlockSpec((None, 4, D_MODEL), grp),
                _const_spec(rw.shape), _const_spec(rb.shape)]
    return pl.pallas_call(
        _merge_kernel, grid=(nb,), in_specs=in_specs,
        out_specs=[pl.BlockSpec((tm, D_MODEL), row), pl.BlockSpec((tm, D_MODEL), row), pl.BlockSpec((tm, LANE), row)],
        out_shape=[jax.ShapeDtypeStruct((n_rows, D_MODEL), f32),
                   jax.ShapeDtypeStruct((max(n_rows, H2_ROWS), D_MODEL), bf16),
                   jax.ShapeDtypeStruct((n_rows, LANE), f32)],
        compiler_params=_cparams(("parallel",)), name="merge",
    )(x_all, hf, hb, mo, ya, yd, gl, mng, wb, wout, mod, rw, rb)


def _moe_kernel(be_ref, nu_ref, xs_ref, wgu_ref, bgu_ref, wd_ref, bd_ref, o_ref, wgu_s, wd_s):
    i = pl.program_id(0)
    prev = be_ref[jnp.maximum(i - 1, 0)]
    used = i < nu_ref[0]

    @pl.when(jnp.logical_and(used, jnp.logical_or(i == 0, be_ref[i] != prev)))
    def _():
        wgu_s[...] = wgu_ref[...].astype(bf16)
        wd_s[...] = wd_ref[...].astype(bf16)

    @pl.when(used)
    def _():
        gu = jnp.dot(xs_ref[...], wgu_s[...], preferred_element_type=f32) + bgu_ref[...]
        gate = jnp.minimum(gu[:, :D_EXPERT], SWIGLU_LIMIT)
        up = jnp.clip(gu[:, D_EXPERT:], -SWIGLU_LIMIT, SWIGLU_LIMIT)
        act = (up + 1.0) * (gate * jax.nn.sigmoid(SWIGLU_ALPHA * gate))
        y = jnp.dot(act.astype(bf16), wd_s[...], preferred_element_type=f32) + bd_ref[...]
        o_ref[...] = y.astype(o_ref.dtype)

    @pl.when(jnp.logical_not(used))
    def _():
        o_ref[...] = jnp.zeros_like(o_ref)


def _moe_experts(blk_e, n_used, xs, w_gu, b_gu, w_down, b_down):
    cap = xs.shape[0]
    tm = MOE_TM
    nblk = cap // tm
    grid_spec = pltpu.PrefetchScalarGridSpec(
        num_scalar_prefetch=2, grid=(nblk,),
        in_specs=[pl.BlockSpec((tm, D_MODEL), lambda i, be, nu: (i, 0)),
                  pl.BlockSpec((None, D_MODEL, 2 * D_EXPERT), lambda i, be, nu: (be[i], 0, 0)),
                  pl.BlockSpec((None, 1, 2 * D_EXPERT), lambda i, be, nu: (be[i], 0, 0)),
                  pl.BlockSpec((None, D_EXPERT, D_MODEL), lambda i, be, nu: (be[i], 0, 0)),
                  pl.BlockSpec((None, 1, D_MODEL), lambda i, be, nu: (be[i], 0, 0))],
        out_specs=pl.BlockSpec((tm, D_MODEL), lambda i, be, nu: (i, 0)),
        scratch_shapes=[pltpu.VMEM((D_MODEL, 2 * D_EXPERT), bf16), pltpu.VMEM((D_EXPERT, D_MODEL), bf16)])
    return pl.pallas_call(
        _moe_kernel, grid_spec=grid_spec,
        out_shape=jax.ShapeDtypeStruct((cap, D_MODEL), bf16),
        compiler_params=_cparams(("arbitrary",)), name="moe_experts",
    )(blk_e, n_used, xs, w_gu, b_gu.reshape(N_EXPERTS, 1, -1), w_down, b_down.reshape(N_EXPERTS, 1, -1))


def _moe(h2, logits, w_gu, b_gu, w_down, b_down):
    n = logits.shape[0]
    tm = MOE_TM
    i32 = jnp.int32
    top_v, top_i = lax.top_k(logits[:, :N_EXPERTS], TOP_K)
    wts = jax.nn.softmax(top_v, axis=-1)
    nk = n * TOP_K
    flat_e = top_i.reshape(nk).astype(i32)
    iota = jnp.arange(nk, dtype=i32)
    se, order = lax.sort((flat_e, iota), num_keys=1, is_stable=True)
    e_ids = jnp.arange(N_EXPERTS, dtype=i32)
    grp_start = jnp.sum((se[:, None] < e_ids[None, :]).astype(i32), axis=0)
    counts = jnp.sum((se[:, None] == e_ids[None, :]).astype(i32), axis=0)
    padded = (counts + tm - 1) // tm * tm
    pad_end = jnp.cumsum(padded)
    pad_start = pad_end - padded
    off = pad_start - grp_start
    step = off - jnp.concatenate([jnp.zeros((1,), i32), off[:-1]])
    dest_sorted = iota + jnp.sum(jnp.where(iota[:, None] >= grp_start[None, :], step[None, :], 0), axis=1)
    _, pos_flat = lax.sort((order, dest_sorted), num_keys=1)
    cap = -(-(nk + N_EXPERTS * (tm - 1)) // tm) * tm
    nblk = cap // tm
    bstart = jnp.arange(nblk, dtype=i32) * tm
    blk_e = jnp.minimum(jnp.sum((pad_end[None, :] <= bstart[:, None]).astype(i32), axis=1), N_EXPERTS - 1)
    n_used = (pad_end[-1] // tm).astype(i32).reshape(1)
    sel = blk_e[:, None] == e_ids[None, :]
    off_b = jnp.sum(jnp.where(sel, off[None, :], 0), axis=1)
    end_b = jnp.sum(jnp.where(sel, (pad_start + counts)[None, :], 0), axis=1)
    p2 = bstart[:, None] + jnp.arange(tm, dtype=i32)[None, :]
    src = jnp.clip(p2 - off_b[:, None], 0, nk - 1)
    slot_t = jnp.where(p2 < end_b[:, None], (order // TOP_K)[src], 0).reshape(cap)
    xs = h2[slot_t]
    ys = _moe_experts(blk_e, n_used, xs, w_gu, b_gu, w_down, b_down)
    picked = ys[pos_flat].astype(f32).reshape(n, TOP_K, D_MODEL)
    return jnp.sum(picked * wts[:, :, None], axis=1)


def _resid_kernel(x_ref, f_ref, mod_ref, o_ref):
    o_ref[...] = x_ref[...] + mod_ref[1:2, :] * (_rms(f_ref[...]) * mod_ref[0:1, :])


def _resid(n_lat_blocks, xnew, f, mod):
    n = xnew.shape[0]
    tm = ROW_BLOCK
    row = lambda i: (i, 0)
    grp = lambda i: (jnp.where(i >= n_lat_blocks, 1, 0), 0, 0)
    return pl.pallas_call(
        _resid_kernel, grid=(n // tm,),
        in_specs=[pl.BlockSpec((tm, D_MODEL), row), pl.BlockSpec((tm, D_MODEL), row),
                  pl.BlockSpec((None, 2, D_MODEL), grp)],
        out_specs=pl.BlockSpec((tm, D_MODEL), row),
        out_shape=jax.ShapeDtypeStruct((n, D_MODEL), f32),
        compiler_params=_cparams(("parallel",)), name="resid",
    )(xnew, f, mod)


def _rope_tables(n_lat, n_ctx, dim):
    rows = n_lat // GRID_W
    row = jnp.repeat(jnp.arange(rows, dtype=f32), GRID_W)
    col = jnp.tile(jnp.arange(GRID_W, dtype=f32), rows)
    quarter = dim // 4
    inv = ROPE_BASE ** (-jnp.arange(quarter, dtype=f32) / quarter)
    ang = jnp.concatenate([row[:, None] * inv, col[:, None] * inv], axis=-1)
    cos = jnp.concatenate([jnp.cos(ang), jnp.ones((n_ctx, dim // 2), f32)], axis=0).T
    sin = jnp.concatenate([jnp.sin(ang), jnp.zeros((n_ctx, dim // 2), f32)], axis=0).T
    return cos, sin


def kernel(x, c, ctx, c_ctx, ada_w, ada_b, norm_g, w_in, m_gate_b, m_norm_g, q_norm_g, w_uq, kv_norm_g, w_ukv,
           diff_lam, diff_norm_g, w_branch, w_out, router_w, router_b, w_gu, b_gu, w_down, b_down):
    n_lat = x.shape[1]
    n_ctx = ctx.shape[1]
    n = n_lat + n_ctx
    assert x.shape[0] == 1 and n_lat % ROW_BLOCK == 0 and n_ctx == MLSTM_T and n_lat % GRID_W == 0
    nlb = n_lat // ROW_BLOCK
    x_all = jnp.concatenate([x[0], ctx[0]], axis=0)

    cond = jnp.zeros((8, D_MODEL), f32).at[0].set(c[0]).at[1].set(c_ctx)
    mods = _adaln(cond, ada_w, ada_b)[:, :2].reshape(DEPTH, 2, 6, D_MODEL)

    ca, sa = _rope_tables(n_lat, n_ctx, A_ROPE)
    cd, sd = _rope_tables(n_lat, n_ctx, DF_DK)

    for l in range(DEPTH):
        need_ctx = l < DEPTH - 1
        lam_init = 0.8 - 0.6 * math.exp(-0.3 * l)
        sh1, sc1, ga1, sh2, sc2, ga2 = [mods[l, :, j] for j in range(6)]
        ab1 = jnp.stack([norm_g[l, 0][None] * (1.0 + sc1), sh1], axis=1)
        w_tok, w_tr, wt_uq, wt_ukv = _pack_inproj_weights(w_in[l], w_uq[l], w_ukv[l])
        (mq, mv, mo, mg, gl, mkt, mgt, qta, k_a, vta, qtd, k_d, vtd) = _inproj(
            x_all, ab1, nlb, w_tok, w_tr, wt_uq, wt_ukv,
            q_norm_g[l].reshape(Q_LORA, 1), kv_norm_g[l].reshape(KV_LORA, 1), ca, sa, cd, sd)

        hf, hb = _mlstm(mq, mkt, mv, mg, mgt, m_gate_b[l])

        qtd4 = qtd.reshape(DF_HEADS, 2 * DF_DK, n)
        dg = diff_norm_g[l].reshape(DF_HEADS, DF_DV, 1)
        n_rows = n if need_ctx else n_lat
        attn_a = functools.partial(_attention, n_maps=1, dk=A_DKP, dv=A_V, n_out=n_rows)
        attn_d = functools.partial(_attention, n_maps=2, dk=DF_DK, dv=DF_DV, n_out=n_rows,
                                   lam=diff_lam[l], g=dg, lam_init=lam_init)
        ya = attn_a(qta, k_a, vta, tq=ATTN_TQ_MLA, lq=n_lat)
        yd = attn_d(qtd4, k_d, vtd, tq=ATTN_TQ_DIFF, lq=n_lat)
        if need_ctx:
            ya = attn_a(qta, k_a[:, n_lat:], vta[:, :, n_lat:], tq=n_ctx, lq=n_ctx, q_start=n_lat, out=ya)
            yd = attn_d(qtd4, k_d[:, n_lat:], vtd[:, :, n_lat:], tq=n_ctx, lq=n_ctx, q_start=n_lat, out=yd)

        mod_m = jnp.stack([jnp.broadcast_to(norm_g[l, 1][None], (2, D_MODEL)), ga1,
                           norm_g[l, 2][None] * (1.0 + sc2), sh2], axis=1)
        rw = jnp.pad(router_w[l], ((0, 0), (0, LANE - N_EXPERTS)))
        rb = jnp.pad(router_b[l].reshape(1, N_EXPERTS), ((0, 0), (0, LANE - N_EXPERTS)))
        xnew, h2, logits = _merge(n_rows, nlb, x_all, hf, hb, mo, ya, yd, gl,
                                  m_norm_g[l].reshape(1, -1), w_branch[l].astype(bf16), w_out[l].astype(bf16),
                                  mod_m, rw, rb)
        f = _moe(h2, logits, w_gu[l], b_gu[l], w_down[l], b_down[l])
        mod_r = jnp.stack([jnp.broadcast_to(norm_g[l, 3][None], (2, D_MODEL)), ga2], axis=1)
        x_all = _resid(nlb, xnew, f, mod_r)
    return x_all[:n_lat][None]
```

```python
import functools
import math

import jax
import jax.numpy as jnp
from jax import lax
from jax.experimental import pallas as pl
from jax.experimental.pallas import tpu as pltpu

f32 = jnp.float32
bf16 = jnp.bfloat16

D_MODEL = 1024
DEPTH = 2
GRID_W = 64
EPS = 1e-6
ROPE_BASE = 10000.0
M_HEADS = 4
M_DK = 128
M_DV = 128
GATE_SOFTCAP = 15.0
A_HEADS = 8
A_NOPE = 64
A_ROPE = 32
A_V = 64
Q_LORA = 256
KV_LORA = 128
DF_HEADS = 4
DF_DK = 64
DF_DV = 128
N_BRANCH = 3
BRANCH_W = 512
N_EXPERTS = 32
TOP_K = 4
D_EXPERT = 1024
SWIGLU_LIMIT = 7.0
SWIGLU_ALPHA = 1.702

IN_SIZES = (M_HEADS * M_DK, M_HEADS * M_DK, M_HEADS * M_DV, M_HEADS * M_DV, 4 * M_HEADS,
            Q_LORA, KV_LORA, A_ROPE,
            2 * DF_HEADS * DF_DK, 2 * DF_HEADS * DF_DK, DF_HEADS * DF_DV,
            N_BRANCH * D_MODEL)

LOG2E = 1.4426950408889634
LANE = 128
A_DKP = 128
A_DVP = A_V + 16
DF_DVP = DF_DV + 16
VMEM_LIMIT = 56 * 1024 * 1024

ROW_BLOCK = 256
MXU_DEPTH = 256
ATTN_TQ_MLA = 512
ATTN_TQ_DIFF = 256
ATTN_QSUB = 256
ATTN_TK = 1280
MLSTM_T = 256
MOE_TM = 512
H2_ROWS = 32768

NT_DIMS = (((1,), (1,)), ((), ()))


def _cparams(sem):
    return pltpu.CompilerParams(dimension_semantics=sem, vmem_limit_bytes=VMEM_LIMIT)


def _const_spec(shape):
    nd = len(shape)
    return pl.BlockSpec(shape, lambda *_: (0,) * nd)


def _adaln_kernel(cond_ref, w_ref, b_ref, o_ref):
    cnd = cond_ref[...]
    a = cnd * jax.nn.sigmoid(cnd)
    o_ref[...] = jnp.dot(a, w_ref[...], preferred_element_type=f32,
                         precision=lax.Precision.HIGHEST) + b_ref[...]


def _adaln(cond, ada_w, ada_b):
    tn = 1536
    nj = (6 * D_MODEL) // tn
    return pl.pallas_call(
        _adaln_kernel,
        grid=(DEPTH, nj),
        in_specs=[pl.BlockSpec((8, D_MODEL), lambda l, j: (0, 0)),
                  pl.BlockSpec((None, D_MODEL, tn), lambda l, j: (l, 0, j)),
                  pl.BlockSpec((None, 1, tn), lambda l, j: (l, 0, j))],
        out_specs=pl.BlockSpec((None, 8, tn), lambda l, j: (l, 0, j)),
        out_shape=jax.ShapeDtypeStruct((DEPTH, 8, 6 * D_MODEL), f32),
        compiler_params=_cparams(("parallel", "parallel")),
        name="adaln",
    )(cond, ada_w, ada_b.reshape(DEPTH, 1, 6 * D_MODEL))


TOK_W = 3 * 512 + LANE + N_BRANCH * D_MODEL
TR_W = 512 + 16 + Q_LORA + KV_LORA + A_ROPE + 3 * 512


def _ones_row_block(tm):
    r = lax.broadcasted_iota(jnp.int32, (16, tm), 0)
    return jnp.where(r == 0, 1.0, 0.0).astype(bf16)


def _inproj_kernel(x_ref, ab_ref, wtok_ref, wtr_ref, wuq_ref, wukv_ref, gq_ref, gkv_ref,
                   ca_ref, sa_ref, cd_ref, sd_ref,
                   mq_ref, mv_ref, mo_ref, mg_ref, gl_ref, mkt_ref, mgt_ref,
                   qta_ref, kta_ref, vta_ref, qtd_ref, ktd_ref, vtd_ref):
    tm = x_ref.shape[0]
    x = x_ref[...]
    ms = jnp.mean(x * x, axis=-1, keepdims=True)
    h = (x * lax.rsqrt(ms + EPS)) * ab_ref[0:1, :] + ab_ref[1:2, :]
    hb = h.astype(bf16)

    tok = jnp.dot(hb, wtok_ref[...], preferred_element_type=f32)
    mq_ref[...] = (tok[:, 0:512] * (M_DK ** -0.5)).astype(bf16)
    mv_ref[...] = tok[:, 512:1024].astype(bf16)
    mo_ref[...] = tok[:, 1024:1536].astype(bf16)
    mg_ref[...] = tok[:, 1536:1664]
    gl_ref[...] = tok[:, 1664:TOK_W].astype(bf16)

    tr = lax.dot_general(wtr_ref[...], hb, NT_DIMS, preferred_element_type=f32)
    mkt_ref[...] = tr[0:512].astype(bf16)
    mgt_ref[...] = tr[512:528]
    cq = tr[528:784]
    ckv = tr[784:912]
    kpe = tr[912:944]
    dq = tr[944:1456]
    dk = tr[1456:1968]
    dv = tr[1968:2480]

    ca = ca_ref[...]
    sa = sa_ref[...]
    cd = cd_ref[...]
    sd = sd_ref[...]
    ones_blk = _ones_row_block(tm)
    zeros32 = jnp.zeros((32, tm), bf16)

    cqn = (cq * lax.rsqrt(jnp.mean(cq * cq, axis=0, keepdims=True) + EPS) * gq_ref[...]).astype(bf16)
    qscale = ((A_NOPE + A_ROPE) ** -0.5) * LOG2E
    qt = jnp.dot(wuq_ref[...], cqn, preferred_element_type=f32) * qscale
    for hh in range(A_HEADS):
        a = qt[512 + 16 * hh:528 + 16 * hh]
        b = qt[640 + 16 * hh:656 + 16 * hh]
        qta_ref[hh, 0:64, :] = qt[64 * hh:64 * hh + 64].astype(bf16)
        qta_ref[hh, 64:80, :] = (a * ca - b * sa).astype(bf16)
        qta_ref[hh, 80:96, :] = (a * sa + b * ca).astype(bf16)
        qta_ref[hh, 96:128, :] = zeros32

    ckvn = (ckv * lax.rsqrt(jnp.mean(ckv * ckv, axis=0, keepdims=True) + EPS) * gkv_ref[...]).astype(bf16)
    kvt = jnp.dot(wukv_ref[...], ckvn, preferred_element_type=f32)
    ka = kpe[0:16]
    kb = kpe[16:32]
    kpe_rot = jnp.concatenate([ka * ca - kb * sa, ka * sa + kb * ca, jnp.zeros((32, tm), f32)], axis=0)
    for hh in range(A_HEADS):
        kt_h = jnp.concatenate([kvt[64 * hh:64 * hh + 64], kpe_rot], axis=0)
        kta_ref[hh] = kt_h.T.astype(bf16)
        vta_ref[hh, 0:64, :] = kvt[512 + 64 * hh:576 + 64 * hh].astype(bf16)
        vta_ref[hh, 64:80, :] = ones_blk

    dscale = (DF_DK ** -0.5) * LOG2E
    for hh in range(2 * DF_HEADS):
        a = dq[32 * hh:32 * hh + 32]
        b = dq[256 + 32 * hh:288 + 32 * hh]
        qtd_ref[hh, 0:32, :] = ((a * cd - b * sd) * dscale).astype(bf16)
        qtd_ref[hh, 32:64, :] = ((a * sd + b * cd) * dscale).astype(bf16)
        a = dk[32 * hh:32 * hh + 32]
        b = dk[256 + 32 * hh:288 + 32 * hh]
        ktd_ref[hh] = jnp.concatenate([a * cd - b * sd, a * sd + b * cd], axis=0).T.astype(bf16)
    for hh in range(DF_HEADS):
        vtd_ref[hh, 0:128, :] = dv[128 * hh:128 * hh + 128].astype(bf16)
        vtd_ref[hh, 128:144, :] = ones_blk


def _pack_inproj_weights(w, w_uq, w_ukv):
    o = [0]
    for s in IN_SIZES:
        o.append(o[-1] + s)
    mq, mk, mv, mo, mg, cq, ckv, kpe, dq, dk, dv, gl = [w[:, o[i]:o[i + 1]] for i in range(12)]
    mg_pad = jnp.pad(mg, ((0, 0), (0, LANE - mg.shape[1])))
    w_tok = jnp.concatenate([mq, mv, mo, mg_pad, gl], axis=1).astype(bf16)

    def split_halves(t, n_heads, d):
        t3 = t.reshape(t.shape[0], n_heads, d)
        return jnp.concatenate([t3[:, :, :d // 2].reshape(t.shape[0], -1),
                                t3[:, :, d // 2:].reshape(t.shape[0], -1)], axis=1)

    w_tr = jnp.concatenate([mk, mg, cq, ckv, kpe, split_halves(dq, 2 * DF_HEADS, DF_DK),
                            split_halves(dk, 2 * DF_HEADS, DF_DK), dv], axis=1).T.astype(bf16)
    uq = w_uq.reshape(Q_LORA, A_HEADS, A_NOPE + A_ROPE)
    half = A_ROPE // 2
    wt_uq = jnp.concatenate([uq[:, :, :A_NOPE].reshape(Q_LORA, -1),
                             uq[:, :, A_NOPE:A_NOPE + half].reshape(Q_LORA, -1),
                             uq[:, :, A_NOPE + half:].reshape(Q_LORA, -1)], axis=1).T.astype(bf16)
    ukv = w_ukv.reshape(KV_LORA, A_HEADS, A_NOPE + A_V)
    wt_ukv = jnp.concatenate([ukv[:, :, :A_NOPE].reshape(KV_LORA, -1),
                              ukv[:, :, A_NOPE:].reshape(KV_LORA, -1)], axis=1).T.astype(bf16)
    return w_tok, w_tr, wt_uq, wt_ukv


def _inproj(x_all, ab, n_lat_blocks, w_tok, w_tr, wt_uq, wt_ukv, gq, gkv, ca, sa, cd, sd):
    n = x_all.shape[0]
    tm = ROW_BLOCK
    tk = _key_tile(n)
    per = tk // tm
    vtile = lambda i: (0, i // per, 0, i % per)
    nb = n // tm
    row = lambda i: (i, 0)
    colb = lambda i: (0, i)
    col3 = lambda i: (0, 0, i)
    grp = lambda i: (jnp.where(i >= n_lat_blocks, 1, 0), 0, 0)
    out_shapes = [
        jax.ShapeDtypeStruct((n, 512), bf16), jax.ShapeDtypeStruct((n, 512), bf16),
        jax.ShapeDtypeStruct((n, 512), bf16), jax.ShapeDtypeStruct((n, LANE), f32),
        jax.ShapeDtypeStruct((n, N_BRANCH * D_MODEL), bf16),
        jax.ShapeDtypeStruct((512, n), bf16), jax.ShapeDtypeStruct((16, n), f32),
        jax.ShapeDtypeStruct((A_HEADS, A_DKP, n), bf16), jax.ShapeDtypeStruct((A_HEADS, n, A_DKP), bf16),
        jax.ShapeDtypeStruct((A_HEADS, n // tk, A_DVP, tk), bf16),
        jax.ShapeDtypeStruct((2 * DF_HEADS, DF_DK, n), bf16), jax.ShapeDtypeStruct((2 * DF_HEADS, n, DF_DK), bf16),
        jax.ShapeDtypeStruct((DF_HEADS, n // tk, DF_DVP, tk), bf16),
    ]
    out_specs = [
        pl.BlockSpec((tm, 512), row), pl.BlockSpec((tm, 512), row), pl.BlockSpec((tm, 512), row),
        pl.BlockSpec((tm, LANE), row), pl.BlockSpec((tm, N_BRANCH * D_MODEL), row),
        pl.BlockSpec((512, tm), colb), pl.BlockSpec((16, tm), colb),
        pl.BlockSpec((A_HEADS, A_DKP, tm), col3), pl.BlockSpec((A_HEADS, tm, A_DKP), lambda i: (0, i, 0)),
        pl.BlockSpec((A_HEADS, None, A_DVP, tm), vtile),
        pl.BlockSpec((2 * DF_HEADS, DF_DK, tm), col3), pl.BlockSpec((2 * DF_HEADS, tm, DF_DK), lambda i: (0, i, 0)),
        pl.BlockSpec((DF_HEADS, None, DF_DVP, tm), vtile),
    ]
    in_specs = [
        pl.BlockSpec((tm, D_MODEL), row),
        pl.BlockSpec((None, 2, D_MODEL), grp),
        _const_spec(w_tok.shape), _const_spec(w_tr.shape), _const_spec(wt_uq.shape), _const_spec(wt_ukv.shape),
        _const_spec(gq.shape), _const_spec(gkv.shape),
        pl.BlockSpec((A_ROPE // 2, tm), colb), pl.BlockSpec((A_ROPE // 2, tm), colb),
        pl.BlockSpec((DF_DK // 2, tm), colb), pl.BlockSpec((DF_DK // 2, tm), colb),
    ]
    return pl.pallas_call(
        _inproj_kernel, grid=(nb,), in_specs=in_specs, out_specs=out_specs, out_shape=out_shapes,
        compiler_params=_cparams(("parallel",)), name="inproj",
    )(x_all, ab, w_tok, w_tr, wt_uq, wt_ukv, gq, gkv, ca, sa, cd, sd)


def _attn_kernel(*refs, n_maps, dk, dv, tk, n_kt, qsub, lam_init, n_in):
    o_ref = refs[n_in]
    if n_maps == 2:
        lam_ref, g_ref, qt_ref, k_ref, vt_ref = refs[:5]
    else:
        qt_ref, k_ref, vt_ref = refs[:3]
    tq = qt_ref.shape[-1]
    dvp = vt_ref.shape[-2]
    chains = [(m, c) for m in range(n_maps) for c in range(tq // qsub)]

    def qk(j, m, c):
        qt = qt_ref[m * dk:(m + 1) * dk, c * qsub:(c + 1) * qsub]
        return jnp.dot(k_ref[m, j * tk:(j + 1) * tk, :], qt, preferred_element_type=f32)

    state = [(jnp.full((1, qsub), -jnp.inf, f32), jnp.zeros((dvp, qsub), f32)) for _ in chains]
    s_cur = [qk(0, m, c) for (m, c) in chains]
    for j in range(n_kt):
        vtj = vt_ref[j]
        for i, (m, c) in enumerate(chains):
            s_nxt = qk(j + 1, m, c) if j + 1 < n_kt else None
            m_run, acc = state[i]
            m_new = jnp.maximum(m_run, jnp.max(s_cur[i], axis=0, keepdims=True))
            alpha = jnp.exp2(m_run - m_new)
            p = jnp.exp2(s_cur[i] - m_new).astype(bf16)
            pv = jnp.dot(vtj, p, preferred_element_type=f32)
            state[i] = (m_new, alpha * acc + pv)
            s_cur[i] = s_nxt
    outs = []
    for m in range(n_maps):
        accs = [state[i][1] for i, (mm, _) in enumerate(chains) if mm == m]
        acc = accs[0] if len(accs) == 1 else jnp.concatenate(accs, axis=1)
        outs.append(acc[0:dv] / acc[dv:dv + 1])
    if n_maps == 1:
        o_ref[...] = outs[0].astype(o_ref.dtype)
    else:
        lf = lam_ref[...]
        lam = (jnp.exp(jnp.sum(lf[0:1] * lf[1:2], axis=1, keepdims=True))
               - jnp.exp(jnp.sum(lf[2:3] * lf[3:4], axis=1, keepdims=True)) + lam_init)
        o = outs[0] - lam * outs[1]
        o = o * lax.rsqrt(jnp.mean(o * o, axis=0, keepdims=True) + EPS) * g_ref[...]
        o_ref[...] = (o * (1.0 - lam_init)).astype(o_ref.dtype)


def _key_tile(lk):
    return max(t for t in range(MXU_DEPTH, min(ATTN_TK, lk) + 1, MXU_DEPTH) if lk % t == 0)


def _attention(qt, k, vt4, *, n_maps, dk, dv, tq, lq, n_out, q_start=0, out=None, lam=None, g=None, lam_init=0.0):
    n_heads = qt.shape[0]
    lk = k.shape[1]
    _, n_kt, dvp, tk = vt4.shape
    assert n_kt * tk == lk
    tq = min(tq, lq)
    qsub = min(ATTN_QSUB, tq)
    qb0 = q_start // tq
    in_specs = [
        pl.BlockSpec((None, n_maps * dk, tq), lambda h, i: (h, 0, qb0 + i)),
        pl.BlockSpec((n_maps, lk, dk), lambda h, i: (h, 0, 0)),
        pl.BlockSpec((None, n_kt, dvp, tk), lambda h, i: (h, 0, 0, 0)),
    ]
    args = [qt, k, vt4]
    if n_maps == 2:
        in_specs = [_const_spec(lam.shape), pl.BlockSpec((None, dv, 1), lambda h, i: (h, 0, 0))] + in_specs
        args = [lam, g] + args
    aliases = {}
    if out is not None:
        aliases = {len(args): 0}
        in_specs.append(pl.BlockSpec(memory_space=pl.ANY))
        args.append(out)
    kern = functools.partial(_attn_kernel, n_maps=n_maps, dk=dk, dv=dv, tk=tk, n_kt=n_kt, qsub=qsub,
                             lam_init=lam_init, n_in=len(args))
    return pl.pallas_call(
        kern, grid=(n_heads, lq // tq), in_specs=in_specs,
        out_specs=pl.BlockSpec((None, dv, tq), lambda h, i: (h, 0, qb0 + i)),
        out_shape=jax.ShapeDtypeStruct((n_heads, dv, n_out), bf16),
        input_output_aliases=aliases,
        compiler_params=_cparams(("parallel", "parallel")),
        name="attn_diff" if n_maps == 2 else "attn_mla",
    )(*args)


def _log_sigmoid(x):
    return jnp.minimum(x, 0.0) - jnp.log(1.0 + jnp.exp(-jnp.abs(x)))


def _softcap(x):
    return GATE_SOFTCAP * jnp.tanh(x * (1.0 / GATE_SOFTCAP))


def _mlstm_kernel(gb_ref, gbt_ref,
                  qf_ref, ktf_ref, vf_ref, gf_ref, gtf_ref,
                  qb_ref, ktb_ref, vb_ref, gb2_ref, gtb_ref,
                  hf_ref, hb_ref, c_ref, m_ref):
    t = qf_ref.shape[0]

    @pl.when(pl.program_id(0) == 0)
    def _():
        c_ref[...] = jnp.zeros_like(c_ref)
        m_ref[...] = jnp.zeros_like(m_ref)

    row = lax.broadcasted_iota(jnp.int32, (t, t), 0)
    col = lax.broadcasted_iota(jnp.int32, (t, t), 1)
    lane = lax.broadcasted_iota(jnp.int32, (t, LANE), 1)
    ones_col = jnp.where(lane == 0, 1.0, 0.0).astype(bf16)
    hi = lax.Precision.HIGHEST
    dirs = ((qf_ref, ktf_ref, vf_ref, gf_ref, gtf_ref, hf_ref),
            (qb_ref, ktb_ref, vb_ref, gb2_ref, gtb_ref, hb_ref))
    for d, (q_ref, kt_ref, v_ref, g_ref, gt_ref, o_ref) in enumerate(dirs):
        mask = (col <= row) if d == 0 else (col >= row)
        maskf = mask.astype(f32)
        last = t - 1 if d == 0 else 0
        g_tok = _softcap(g_ref[...] + gb_ref[...])
        g_t = _softcap(gt_ref[...] + gbt_ref[...])
        b_tok = jnp.dot(maskf, _log_sigmoid(g_tok), preferred_element_type=f32, precision=hi)
        b_t = lax.dot_general(_log_sigmoid(g_t), maskf, NT_DIMS, preferred_element_type=f32, precision=hi)
        for hh in range(M_HEADS):
            ri = 4 * (2 * d) + hh
            rf = 4 * (2 * d + 1) + hh
            idx = d * M_HEADS + hh
            b_col = b_tok[:, rf:rf + 1]
            b_row = b_t[rf:rf + 1, :]
            a_row = g_t[ri:ri + 1, :] - b_row
            m_prev = m_ref[idx, 0:1, 0:1]
            ld = jnp.where(mask, b_col + a_row, -jnp.inf)
            inter = b_col + m_prev
            mj = jnp.maximum(inter, jnp.max(ld, axis=1, keepdims=True))
            dmat = jnp.exp(ld - mj)
            q = q_ref[:, hh * M_DK:(hh + 1) * M_DK]
            kt = kt_ref[hh * M_DK:(hh + 1) * M_DK, :]
            vp = jnp.concatenate([v_ref[:, hh * M_DV:(hh + 1) * M_DV], ones_col], axis=1)
            sc = jnp.dot(q, kt, preferred_element_type=f32) * dmat
            cp = c_ref[idx]
            num = (jnp.dot(sc.astype(bf16), vp, preferred_element_type=f32)
                   + jnp.exp(inter - mj) * jnp.dot(q, cp.astype(bf16), preferred_element_type=f32))
            den = num[:, M_DV:M_DV + 1]
            hj = num[:, 0:M_DV] / jnp.maximum(jnp.abs(den), jnp.exp(-mj))
            o_ref[:, hh * M_DV:(hh + 1) * M_DV] = hj.astype(o_ref.dtype)
            b_tot = b_t[rf:rf + 1, last:last + 1]
            lw = b_tot + a_row
            m_new = jnp.maximum(b_tot + m_prev, jnp.max(lw, axis=1, keepdims=True))
            kw = (kt.astype(f32) * jnp.exp(lw - m_new)).astype(bf16)
            c_ref[idx] = jnp.exp(b_tot + m_prev - m_new) * cp + jnp.dot(kw, vp, preferred_element_type=f32)
            m_ref[idx] = jnp.broadcast_to(m_new, (8, LANE))


def _mlstm(mq, mkt, mv, mg, mgt, gate_b):
    n = mq.shape[0]
    t = MLSTM_T
    nb = n // t
    gb = jnp.pad(gate_b.reshape(1, 16), ((0, 0), (0, LANE - 16)))
    gbt = gate_b.reshape(16, 1)
    fwd = lambda s: jnp.where(s == 0, nb - 1, s - 1)
    bwd = lambda s: nb - 1 - s
    specs = []
    for im in (fwd, bwd):
        specs += [pl.BlockSpec((t, 512), lambda s, im=im: (im(s), 0)),
                  pl.BlockSpec((512, t), lambda s, im=im: (0, im(s))),
                  pl.BlockSpec((t, 512), lambda s, im=im: (im(s), 0)),
                  pl.BlockSpec((t, LANE), lambda s, im=im: (im(s), 0)),
                  pl.BlockSpec((16, t), lambda s, im=im: (0, im(s)))]
    return pl.pallas_call(
        _mlstm_kernel, grid=(nb,),
        in_specs=[_const_spec(gb.shape), _const_spec(gbt.shape)] + specs,
        out_specs=[pl.BlockSpec((t, 512), lambda s: (fwd(s), 0)), pl.BlockSpec((t, 512), lambda s: (bwd(s), 0))],
        out_shape=[jax.ShapeDtypeStruct((n, 512), bf16), jax.ShapeDtypeStruct((n, 512), bf16)],
        scratch_shapes=[pltpu.VMEM((2 * M_HEADS, M_DK, 2 * M_DV), f32), pltpu.VMEM((2 * M_HEADS, 8, LANE), f32)],
        compiler_params=_cparams(("arbitrary",)), name="mlstm",
    )(gb, gbt, mq, mkt, mv, mg, mgt, mq, mkt, mv, mg, mgt)


def _rms(x, axis=-1):
    return x * lax.rsqrt(jnp.mean(x * x, axis=axis, keepdims=True) + EPS)


def _merge_kernel(x_ref, hf_ref, hb_ref, mo_ref, ya_ref, yd_ref, gl_ref, mng_ref, wb_ref, wout_ref,
                  mod_ref, rw_ref, rb_ref, xnew_ref, h2_ref, logit_ref):
    hs = hf_ref[...].astype(f32) + hb_ref[...].astype(f32)
    parts = [_rms(hs[:, hh * M_DV:(hh + 1) * M_DV]) for hh in range(M_HEADS)]
    ym = jnp.concatenate(parts, axis=1) * mng_ref[...] * jax.nn.sigmoid(mo_ref[...].astype(f32))
    yb0 = jnp.dot(ym.astype(bf16), wb_ref[0], preferred_element_type=f32)
    tm = x_ref.shape[0]
    tn_dims = (((0,), (0,)), ((), ()))
    yb1 = lax.dot_general(ya_ref[...].reshape(BRANCH_W, tm), wb_ref[1], tn_dims, preferred_element_type=f32)
    yb2 = lax.dot_general(yd_ref[...].reshape(BRANCH_W, tm), wb_ref[2], tn_dims, preferred_element_type=f32)
    d = D_MODEL
    z = (jax.nn.sigmoid(gl_ref[:, 0:d].astype(f32)) * yb0
         + jax.nn.sigmoid(gl_ref[:, d:2 * d].astype(f32)) * yb1
         + jax.nn.sigmoid(gl_ref[:, 2 * d:3 * d].astype(f32)) * yb2)
    y = jnp.dot(z.astype(bf16), wout_ref[...], preferred_element_type=f32)
    xnew = x_ref[...] + mod_ref[1:2, :] * (_rms(y) * mod_ref[0:1, :])
    h2 = _rms(xnew) * mod_ref[2:3, :] + mod_ref[3:4, :]
    xnew_ref[...] = xnew
    h2_ref[...] = h2.astype(bf16)
    logit_ref[...] = jnp.dot(h2, rw_ref[...], preferred_element_type=f32,
                             precision=lax.Precision.HIGHEST) + rb_ref[...]


def _merge(n_rows, n_lat_blocks, x_all, hf, hb, mo, ya, yd, gl, mng, wb, wout, mod, rw, rb):
    tm = ROW_BLOCK
    nb = n_rows // tm
    row = lambda i: (i, 0)
    grp = lambda i: (jnp.where(i >= n_lat_blocks, 1, 0), 0, 0)
    in_specs = [pl.BlockSpec((tm, D_MODEL), row), pl.BlockSpec((tm, 512), row), pl.BlockSpec((tm, 512), row),
                pl.BlockSpec((tm, 512), row),
                pl.BlockSpec((A_HEADS, A_V, tm), lambda i: (0, 0, i)),
                pl.BlockSpec((DF_HEADS, DF_DV, tm), lambda i: (0, 0, i)),
                pl.BlockSpec((tm, N_BRANCH * D_MODEL), row), _const_spec(mng.shape), _const_spec(wb.shape),
                _const_spec(wout.shape), pl.BlockSpec((None, 4, D_MODEL), grp),
                _const_spec(rw.shape), _const_spec(rb.shape)]
    return pl.pallas_call(
        _merge_kernel, grid=(nb,), in_specs=in_specs,
        out_specs=[pl.BlockSpec((tm, D_MODEL), row), pl.BlockSpec((tm, D_MODEL), row), pl.BlockSpec((tm, LANE), row)],
        out_shape=[jax.ShapeDtypeStruct((n_rows, D_MODEL), f32),
                   jax.ShapeDtypeStruct((max(n_rows, H2_ROWS), D_MODEL), bf16),
                   jax.ShapeDtypeStruct((n_rows, LANE), f32)],
        compiler_params=_cparams(("parallel",)), name="merge",
    )(x_all, hf, hb, mo, ya, yd, gl, mng, wb, wout, mod, rw, rb)


def _moe_kernel(be_ref, nu_ref, xs_ref, wgu_ref, bgu_ref, wd_ref, bd_ref, o_ref, wgu_s, wd_s):
    i = pl.program_id(0)
    prev = be_ref[jnp.maximum(i - 1, 0)]
    used = i < nu_ref[0]

    @pl.when(jnp.logical_and(used, jnp.logical_or(i == 0, be_ref[i] != prev)))
    def _():
        wgu_s[...] = wgu_ref[...].astype(bf16)
        wd_s[...] = wd_ref[...].astype(bf16)

    @pl.when(used)
    def _():
        gu = jnp.dot(xs_ref[...], wgu_s[...], preferred_element_type=f32) + bgu_ref[...]
        gate = jnp.minimum(gu[:, :D_EXPERT], SWIGLU_LIMIT)
        up = jnp.clip(gu[:, D_EXPERT:], -SWIGLU_LIMIT, SWIGLU_LIMIT)
        act = (up + 1.0) * (gate * jax.nn.sigmoid(SWIGLU_ALPHA * gate))
        y = jnp.dot(act.astype(bf16), wd_s[...], preferred_element_type=f32) + bd_ref[...]
        o_ref[...] = y.astype(o_ref.dtype)

    @pl.when(jnp.logical_not(used))
    def _():
        o_ref[...] = jnp.zeros_like(o_ref)


def _moe_experts(blk_e, n_used, xs, w_gu, b_gu, w_down, b_down):
    cap = xs.shape[0]
    tm = MOE_TM
    nblk = cap // tm
    grid_spec = pltpu.PrefetchScalarGridSpec(
        num_scalar_prefetch=2, grid=(nblk,),
        in_specs=[pl.BlockSpec((tm, D_MODEL), lambda i, be, nu: (i, 0)),
                  pl.BlockSpec((None, D_MODEL, 2 * D_EXPERT), lambda i, be, nu: (be[i], 0, 0)),
                  pl.BlockSpec((None, 1, 2 * D_EXPERT), lambda i, be, nu: (be[i], 0, 0)),
                  pl.BlockSpec((None, D_EXPERT, D_MODEL), lambda i, be, nu: (be[i], 0, 0)),
                  pl.BlockSpec((None, 1, D_MODEL), lambda i, be, nu: (be[i], 0, 0))],
        out_specs=pl.BlockSpec((tm, D_MODEL), lambda i, be, nu: (i, 0)),
        scratch_shapes=[pltpu.VMEM((D_MODEL, 2 * D_EXPERT), bf16), pltpu.VMEM((D_EXPERT, D_MODEL), bf16)])
    return pl.pallas_call(
        _moe_kernel, grid_spec=grid_spec,
        out_shape=jax.ShapeDtypeStruct((cap, D_MODEL), bf16),
        compiler_params=_cparams(("arbitrary",)), name="moe_experts",
    )(blk_e, n_used, xs, w_gu, b_gu.reshape(N_EXPERTS, 1, -1), w_down, b_down.reshape(N_EXPERTS, 1, -1))


def _moe(h2, logits, w_gu, b_gu, w_down, b_down):
    n = logits.shape[0]
    tm = MOE_TM
    i32 = jnp.int32
    top_v, top_i = lax.top_k(logits[:, :N_EXPERTS], TOP_K)
    wts = jax.nn.softmax(top_v, axis=-1)
    nk = n * TOP_K
    flat_e = top_i.reshape(nk).astype(i32)
    iota = jnp.arange(nk, dtype=i32)
    se, order = lax.sort((flat_e, iota), num_keys=1, is_stable=True)
    e_ids = jnp.arange(N_EXPERTS, dtype=i32)
    grp_start = jnp.sum((se[:, None] < e_ids[None, :]).astype(i32), axis=0)
    counts = jnp.sum((se[:, None] == e_ids[None, :]).astype(i32), axis=0)
    padded = (counts + tm - 1) // tm * tm
    pad_end = jnp.cumsum(padded)
    pad_start = pad_end - padded
    off = pad_start - grp_start
    step = off - jnp.concatenate([jnp.zeros((1,), i32), off[:-1]])
    dest_sorted = iota + jnp.sum(jnp.where(iota[:, None] >= grp_start[None, :], step[None, :], 0), axis=1)
    _, pos_km = lax.sort(((order % TOP_K) * n + order // TOP_K, dest_sorted), num_keys=1)
    cap = -(-(nk + N_EXPERTS * (tm - 1)) // tm) * tm
    nblk = cap // tm
    bstart = jnp.arange(nblk, dtype=i32) * tm
    blk_e = jnp.minimum(jnp.sum((pad_end[None, :] <= bstart[:, None]).astype(i32), axis=1), N_EXPERTS - 1)
    n_used = (pad_end[-1] // tm).astype(i32).reshape(1)
    sel = blk_e[:, None] == e_ids[None, :]
    off_b = jnp.sum(jnp.where(sel, off[None, :], 0), axis=1)
    end_b = jnp.sum(jnp.where(sel, (pad_start + counts)[None, :], 0), axis=1)
    p2 = bstart[:, None] + jnp.arange(tm, dtype=i32)[None, :]
    src = jnp.clip(p2 - off_b[:, None], 0, nk - 1)
    slot_t = jnp.where(p2 < end_b[:, None], (order // TOP_K)[src], p2 % n).reshape(cap)
    xs = h2[slot_t]
    ys = _moe_experts(blk_e, n_used, xs, w_gu, b_gu, w_down, b_down)
    return ys[pos_km].reshape(TOP_K, n, D_MODEL), wts


def _resid_kernel(x_ref, y_ref, w_ref, mod_ref, o_ref):
    w = w_ref[...]
    f = y_ref[0].astype(f32) * w[:, 0:1]
    for kk in range(1, TOP_K):
        f = f + y_ref[kk].astype(f32) * w[:, kk:kk + 1]
    o_ref[...] = x_ref[...] + mod_ref[1:2, :] * (_rms(f) * mod_ref[0:1, :])


def _resid(n_lat_blocks, xnew, picked, wts, mod):
    n = xnew.shape[0]
    tm = ROW_BLOCK
    row = lambda i: (i, 0)
    grp = lambda i: (jnp.where(i >= n_lat_blocks, 1, 0), 0, 0)
    return pl.pallas_call(
        _resid_kernel, grid=(n // tm,),
        in_specs=[pl.BlockSpec((tm, D_MODEL), row), pl.BlockSpec((TOP_K, tm, D_MODEL), lambda i: (0, i, 0)),
                  pl.BlockSpec((tm, TOP_K), row), pl.BlockSpec((None, 2, D_MODEL), grp)],
        out_specs=pl.BlockSpec((tm, D_MODEL), row),
        out_shape=jax.ShapeDtypeStruct((n, D_MODEL), f32),
        compiler_params=_cparams(("parallel",)), name="resid",
    )(xnew, picked, wts, mod)


def _rope_tables(n_lat, n_ctx, dim):
    rows = n_lat // GRID_W
    row = jnp.repeat(jnp.arange(rows, dtype=f32), GRID_W)
    col = jnp.tile(jnp.arange(GRID_W, dtype=f32), rows)
    quarter = dim // 4
    inv = ROPE_BASE ** (-jnp.arange(quarter, dtype=f32) / quarter)
    ang = jnp.concatenate([row[:, None] * inv, col[:, None] * inv], axis=-1)
    cos = jnp.concatenate([jnp.cos(ang), jnp.ones((n_ctx, dim // 2), f32)], axis=0).T
    sin = jnp.concatenate([jnp.sin(ang), jnp.zeros((n_ctx, dim // 2), f32)], axis=0).T
    return cos, sin


def kernel(x, c, ctx, c_ctx, ada_w, ada_b, norm_g, w_in, m_gate_b, m_norm_g, q_norm_g, w_uq, kv_norm_g, w_ukv,
           diff_lam, diff_norm_g, w_branch, w_out, router_w, router_b, w_gu, b_gu, w_down, b_down):
    n_lat = x.shape[1]
    n_ctx = ctx.shape[1]
    n = n_lat + n_ctx
    assert x.shape[0] == 1 and n_lat % ROW_BLOCK == 0 and n_ctx == MLSTM_T and n_lat % GRID_W == 0
    nlb = n_lat // ROW_BLOCK
    x_all = jnp.concatenate([x[0], ctx[0]], axis=0)

    cond = jnp.zeros((8, D_MODEL), f32).at[0].set(c[0]).at[1].set(c_ctx)
    mods = _adaln(cond, ada_w, ada_b)[:, :2].reshape(DEPTH, 2, 6, D_MODEL)

    ca, sa = _rope_tables(n_lat, n_ctx, A_ROPE)
    cd, sd = _rope_tables(n_lat, n_ctx, DF_DK)

    for l in range(DEPTH):
        need_ctx = l < DEPTH - 1
        lam_init = 0.8 - 0.6 * math.exp(-0.3 * l)
        sh1, sc1, ga1, sh2, sc2, ga2 = [mods[l, :, j] for j in range(6)]
        ab1 = jnp.stack([norm_g[l, 0][None] * (1.0 + sc1), sh1], axis=1)
        w_tok, w_tr, wt_uq, wt_ukv = _pack_inproj_weights(w_in[l], w_uq[l], w_ukv[l])
        (mq, mv, mo, mg, gl, mkt, mgt, qta, k_a, vta, qtd, k_d, vtd) = _inproj(
            x_all, ab1, nlb, w_tok, w_tr, wt_uq, wt_ukv,
            q_norm_g[l].reshape(Q_LORA, 1), kv_norm_g[l].reshape(KV_LORA, 1), ca, sa, cd, sd)

        hf, hb = _mlstm(mq, mkt, mv, mg, mgt, m_gate_b[l])

        qtd4 = qtd.reshape(DF_HEADS, 2 * DF_DK, n)
        dg = diff_norm_g[l].reshape(DF_HEADS, DF_DV, 1)
        n_rows = n if need_ctx else n_lat
        attn_a = functools.partial(_attention, n_maps=1, dk=A_DKP, dv=A_V, n_out=n_rows)
        attn_d = functools.partial(_attention, n_maps=2, dk=DF_DK, dv=DF_DV, n_out=n_rows,
                                   lam=diff_lam[l], g=dg, lam_init=lam_init)
        ya = attn_a(qta, k_a, vta, tq=ATTN_TQ_MLA, lq=n_lat)
        yd = attn_d(qtd4, k_d, vtd, tq=ATTN_TQ_DIFF, lq=n_lat)
        if need_ctx:
            ctx_tile = lambda v: v[:, -1:, :, v.shape[3] - n_ctx:]
            ya = attn_a(qta, k_a[:, n_lat:], ctx_tile(vta), tq=n_ctx, lq=n_ctx, q_start=n_lat, out=ya)
            yd = attn_d(qtd4, k_d[:, n_lat:], ctx_tile(vtd), tq=n_ctx, lq=n_ctx, q_start=n_lat, out=yd)

        mod_m = jnp.stack([jnp.broadcast_to(norm_g[l, 1][None], (2, D_MODEL)), ga1,
                           norm_g[l, 2][None] * (1.0 + sc2), sh2], axis=1)
        rw = jnp.pad(router_w[l], ((0, 0), (0, LANE - N_EXPERTS)))
        rb = jnp.pad(router_b[l].reshape(1, N_EXPERTS), ((0, 0), (0, LANE - N_EXPERTS)))
        xnew, h2, logits = _merge(n_rows, nlb, x_all, hf, hb, mo, ya, yd, gl,
                                  m_norm_g[l].reshape(1, -1), w_branch[l].astype(bf16), w_out[l].astype(bf16),
                                  mod_m, rw, rb)
        picked, wts = _moe(h2, logits, w_gu[l], b_gu[l], w_down[l], b_down[l])
        mod_r = jnp.stack([jnp.broadcast_to(norm_g[l, 3][None], (2, D_MODEL)), ga2], axis=1)
        x_all = _resid(nlb, xnew, picked, wts, mod_r)
    return x_all[:n_lat][None]
```

```python
import functools
import math

import jax
import jax.numpy as jnp
from jax import lax
from jax.experimental import pallas as pl
from jax.experimental.pallas import tpu as pltpu

f32 = jnp.float32
bf16 = jnp.bfloat16

D_MODEL = 1024
DEPTH = 2
GRID_W = 64
EPS = 1e-6
ROPE_BASE = 10000.0
M_HEADS = 4
M_DK = 128
M_DV = 128
GATE_SOFTCAP = 15.0
A_HEADS = 8
A_NOPE = 64
A_ROPE = 32
A_V = 64
Q_LORA = 256
KV_LORA = 128
DF_HEADS = 4
DF_DK = 64
DF_DV = 128
N_BRANCH = 3
BRANCH_W = 512
N_EXPERTS = 32
TOP_K = 4
D_EXPERT = 1024
SWIGLU_LIMIT = 7.0
SWIGLU_ALPHA = 1.702

IN_SIZES = (M_HEADS * M_DK, M_HEADS * M_DK, M_HEADS * M_DV, M_HEADS * M_DV, 4 * M_HEADS,
            Q_LORA, KV_LORA, A_ROPE,
            2 * DF_HEADS * DF_DK, 2 * DF_HEADS * DF_DK, DF_HEADS * DF_DV,
            N_BRANCH * D_MODEL)

LOG2E = 1.4426950408889634
LANE = 128
A_DKP = 128
A_DVP = A_V + 16
DF_DVP = DF_DV + 16
VMEM_LIMIT = 56 * 1024 * 1024

ROW_BLOCK = 256
MXU_DEPTH = 256
ATTN_TQ_MLA = 512
ATTN_TQ_DIFF = 256
ATTN_QSUB = 256
ATTN_TK = 1280
MLSTM_T = 256
MOE_TM = 512
H2_ROWS = 32768

NT_DIMS = (((1,), (1,)), ((), ()))


def _cparams(sem):
    return pltpu.CompilerParams(dimension_semantics=sem, vmem_limit_bytes=VMEM_LIMIT)


def _const_spec(shape):
    nd = len(shape)
    return pl.BlockSpec(shape, lambda *_: (0,) * nd)


def _adaln_kernel(cond_ref, w_ref, b_ref, o_ref):
    cnd = cond_ref[...]
    a = cnd * jax.nn.sigmoid(cnd)
    o_ref[...] = jnp.dot(a, w_ref[...], preferred_element_type=f32,
                         precision=lax.Precision.HIGHEST) + b_ref[...]


def _adaln(cond, ada_w, ada_b):
    tn = 1536
    nj = (6 * D_MODEL) // tn
    return pl.pallas_call(
        _adaln_kernel,
        grid=(DEPTH, nj),
        in_specs=[pl.BlockSpec((8, D_MODEL), lambda l, j: (0, 0)),
                  pl.BlockSpec((None, D_MODEL, tn), lambda l, j: (l, 0, j)),
                  pl.BlockSpec((None, 1, tn), lambda l, j: (l, 0, j))],
        out_specs=pl.BlockSpec((None, 8, tn), lambda l, j: (l, 0, j)),
        out_shape=jax.ShapeDtypeStruct((DEPTH, 8, 6 * D_MODEL), f32),
        compiler_params=_cparams(("parallel", "parallel")),
        name="adaln",
    )(cond, ada_w, ada_b.reshape(DEPTH, 1, 6 * D_MODEL))


TOK_W = 3 * 512 + LANE + N_BRANCH * D_MODEL
TR_W = 512 + 16 + Q_LORA + KV_LORA + A_ROPE + 3 * 512


def _ones_row_block(tm):
    r = lax.broadcasted_iota(jnp.int32, (16, tm), 0)
    return jnp.where(r == 0, 1.0, 0.0).astype(bf16)


def _inproj_kernel(x_ref, ab_ref, wtok_ref, wtr_ref, wuq_ref, wukv_ref, gq_ref, gkv_ref,
                   ca_ref, sa_ref, cd_ref, sd_ref,
                   mq_ref, mv_ref, mo_ref, mg_ref, gl_ref, mkt_ref, mgt_ref,
                   qta_ref, kta_ref, vta_ref, qtd_ref, ktd_ref, vtd_ref):
    tm = x_ref.shape[0]
    x = x_ref[...]
    ms = jnp.mean(x * x, axis=-1, keepdims=True)
    h = (x * lax.rsqrt(ms + EPS)) * ab_ref[0:1, :] + ab_ref[1:2, :]
    hb = h.astype(bf16)

    tok = jnp.dot(hb, wtok_ref[...], preferred_element_type=f32)
    mq_ref[...] = (tok[:, 0:512] * (M_DK ** -0.5)).astype(bf16)
    mv_ref[...] = tok[:, 512:1024].astype(bf16)
    mo_ref[...] = tok[:, 1024:1536].astype(bf16)
    mg_ref[...] = tok[:, 1536:1664]
    gl_ref[...] = tok[:, 1664:TOK_W].astype(bf16)

    tr = lax.dot_general(wtr_ref[...], hb, NT_DIMS, preferred_element_type=f32)
    mkt_ref[...] = tr[0:512].astype(bf16)
    mgt_ref[...] = tr[512:528]
    cq = tr[528:784]
    ckv = tr[784:912]
    kpe = tr[912:944]
    dq = tr[944:1456]
    dk = tr[1456:1968]
    dv = tr[1968:2480]

    ca = ca_ref[...]
    sa = sa_ref[...]
    cd = cd_ref[...]
    sd = sd_ref[...]
    ones_blk = _ones_row_block(tm)
    zeros32 = jnp.zeros((32, tm), bf16)

    cqn = (cq * lax.rsqrt(jnp.mean(cq * cq, axis=0, keepdims=True) + EPS) * gq_ref[...]).astype(bf16)
    qscale = ((A_NOPE + A_ROPE) ** -0.5) * LOG2E
    qt = jnp.dot(wuq_ref[...], cqn, preferred_element_type=f32) * qscale
    for hh in range(A_HEADS):
        a = qt[512 + 16 * hh:528 + 16 * hh]
        b = qt[640 + 16 * hh:656 + 16 * hh]
        qta_ref[hh, 0:64, :] = qt[64 * hh:64 * hh + 64].astype(bf16)
        qta_ref[hh, 64:80, :] = (a * ca - b * sa).astype(bf16)
        qta_ref[hh, 80:96, :] = (a * sa + b * ca).astype(bf16)
        qta_ref[hh, 96:128, :] = zeros32

    ckvn = (ckv * lax.rsqrt(jnp.mean(ckv * ckv, axis=0, keepdims=True) + EPS) * gkv_ref[...]).astype(bf16)
    kvt = jnp.dot(wukv_ref[...], ckvn, preferred_element_type=f32)
    ka = kpe[0:16]
    kb = kpe[16:32]
    kpe_rot = jnp.concatenate([ka * ca - kb * sa, ka * sa + kb * ca, jnp.zeros((32, tm), f32)], axis=0)
    for hh in range(A_HEADS):
        kt_h = jnp.concatenate([kvt[64 * hh:64 * hh + 64], kpe_rot], axis=0)
        kta_ref[hh] = kt_h.T.astype(bf16)
        vta_ref[hh, 0:64, :] = kvt[512 + 64 * hh:576 + 64 * hh].astype(bf16)
        vta_ref[hh, 64:80, :] = ones_blk

    dscale = (DF_DK ** -0.5) * LOG2E
    for hh in range(2 * DF_HEADS):
        a = dq[32 * hh:32 * hh + 32]
        b = dq[256 + 32 * hh:288 + 32 * hh]
        qtd_ref[hh, 0:32, :] = ((a * cd - b * sd) * dscale).astype(bf16)
        qtd_ref[hh, 32:64, :] = ((a * sd + b * cd) * dscale).astype(bf16)
        a = dk[32 * hh:32 * hh + 32]
        b = dk[256 + 32 * hh:288 + 32 * hh]
        ktd_ref[hh] = jnp.concatenate([a * cd - b * sd, a * sd + b * cd], axis=0).T.astype(bf16)
    for hh in range(DF_HEADS):
        vtd_ref[hh, 0:128, :] = dv[128 * hh:128 * hh + 128].astype(bf16)
        vtd_ref[hh, 128:144, :] = ones_blk


def _pack_inproj_weights(w, w_uq, w_ukv):
    o = [0]
    for s in IN_SIZES:
        o.append(o[-1] + s)
    mq, mk, mv, mo, mg, cq, ckv, kpe, dq, dk, dv, gl = [w[:, o[i]:o[i + 1]] for i in range(12)]
    mg_pad = jnp.pad(mg, ((0, 0), (0, LANE - mg.shape[1])))
    w_tok = jnp.concatenate([mq, mv, mo, mg_pad, gl], axis=1).astype(bf16)

    def split_halves(t, n_heads, d):
        t3 = t.reshape(t.shape[0], n_heads, d)
        return jnp.concatenate([t3[:, :, :d // 2].reshape(t.shape[0], -1),
                                t3[:, :, d // 2:].reshape(t.shape[0], -1)], axis=1)

    w_tr = jnp.concatenate([mk, mg, cq, ckv, kpe, split_halves(dq, 2 * DF_HEADS, DF_DK),
                            split_halves(dk, 2 * DF_HEADS, DF_DK), dv], axis=1).T.astype(bf16)
    uq = w_uq.reshape(Q_LORA, A_HEADS, A_NOPE + A_ROPE)
    half = A_ROPE // 2
    wt_uq = jnp.concatenate([uq[:, :, :A_NOPE].reshape(Q_LORA, -1),
                             uq[:, :, A_NOPE:A_NOPE + half].reshape(Q_LORA, -1),
                             uq[:, :, A_NOPE + half:].reshape(Q_LORA, -1)], axis=1).T.astype(bf16)
    ukv = w_ukv.reshape(KV_LORA, A_HEADS, A_NOPE + A_V)
    wt_ukv = jnp.concatenate([ukv[:, :, :A_NOPE].reshape(KV_LORA, -1),
                              ukv[:, :, A_NOPE:].reshape(KV_LORA, -1)], axis=1).T.astype(bf16)
    return w_tok, w_tr, wt_uq, wt_ukv


def _inproj(x_all, ab, n_lat_blocks, w_tok, w_tr, wt_uq, wt_ukv, gq, gkv, ca, sa, cd, sd):
    n = x_all.shape[0]
    tm = ROW_BLOCK
    tk = _key_tile(n)
    per = tk // tm
    vtile = lambda i: (0, i // per, 0, i % per)
    nb = n // tm
    row = lambda i: (i, 0)
    colb = lambda i: (0, i)
    col3 = lambda i: (0, 0, i)
    grp = lambda i: (jnp.where(i >= n_lat_blocks, 1, 0), 0, 0)
    out_shapes = [
        jax.ShapeDtypeStruct((n, 512), bf16), jax.ShapeDtypeStruct((n, 512), bf16),
        jax.ShapeDtypeStruct((n, 512), bf16), jax.ShapeDtypeStruct((n, LANE), f32),
        jax.ShapeDtypeStruct((n, N_BRANCH * D_MODEL), bf16),
        jax.ShapeDtypeStruct((512, n), bf16), jax.ShapeDtypeStruct((16, n), f32),
        jax.ShapeDtypeStruct((A_HEADS, A_DKP, n), bf16), jax.ShapeDtypeStruct((A_HEADS, n, A_DKP), bf16),
        jax.ShapeDtypeStruct((A_HEADS, n // tk, A_DVP, tk), bf16),
        jax.ShapeDtypeStruct((2 * DF_HEADS, DF_DK, n), bf16), jax.ShapeDtypeStruct((2 * DF_HEADS, n, DF_DK), bf16),
        jax.ShapeDtypeStruct((DF_HEADS, n // tk, DF_DVP, tk), bf16),
    ]
    out_specs = [
        pl.BlockSpec((tm, 512), row), pl.BlockSpec((tm, 512), row), pl.BlockSpec((tm, 512), row),
        pl.BlockSpec((tm, LANE), row), pl.BlockSpec((tm, N_BRANCH * D_MODEL), row),
        pl.BlockSpec((512, tm), colb), pl.BlockSpec((16, tm), colb),
        pl.BlockSpec((A_HEADS, A_DKP, tm), col3), pl.BlockSpec((A_HEADS, tm, A_DKP), lambda i: (0, i, 0)),
        pl.BlockSpec((A_HEADS, None, A_DVP, tm), vtile),
        pl.BlockSpec((2 * DF_HEADS, DF_DK, tm), col3), pl.BlockSpec((2 * DF_HEADS, tm, DF_DK), lambda i: (0, i, 0)),
        pl.BlockSpec((DF_HEADS, None, DF_DVP, tm), vtile),
    ]
    in_specs = [
        pl.BlockSpec((tm, D_MODEL), row),
        pl.BlockSpec((None, 2, D_MODEL), grp),
        _const_spec(w_tok.shape), _const_spec(w_tr.shape), _const_spec(wt_uq.shape), _const_spec(wt_ukv.shape),
        _const_spec(gq.shape), _const_spec(gkv.shape),
        pl.BlockSpec((A_ROPE // 2, tm), colb), pl.BlockSpec((A_ROPE // 2, tm), colb),
        pl.BlockSpec((DF_DK // 2, tm), colb), pl.BlockSpec((DF_DK // 2, tm), colb),
    ]
    return pl.pallas_call(
        _inproj_kernel, grid=(nb,), in_specs=in_specs, out_specs=out_specs, out_shape=out_shapes,
        compiler_params=_cparams(("parallel",)), name="inproj",
    )(x_all, ab, w_tok, w_tr, wt_uq, wt_ukv, gq, gkv, ca, sa, cd, sd)


def _attn_kernel(*refs, n_maps, dk, dv, tk, n_kt, qsub, lam_init, n_in):
    o_ref = refs[n_in]
    if n_maps == 2:
        lam_ref, g_ref, qt_ref, k_ref, vt_ref = refs[:5]
    else:
        qt_ref, k_ref, vt_ref = refs[:3]
    tq = qt_ref.shape[-1]
    dvp = vt_ref.shape[-2]
    chains = [(m, c) for m in range(n_maps) for c in range(tq // qsub)]

    def qk(j, m, c):
        qt = qt_ref[m * dk:(m + 1) * dk, c * qsub:(c + 1) * qsub]
        return jnp.dot(k_ref[m, j * tk:(j + 1) * tk, :], qt, preferred_element_type=f32)

    state = [(jnp.full((1, qsub), -jnp.inf, f32), jnp.zeros((dvp, qsub), f32)) for _ in chains]
    s_cur = [qk(0, m, c) for (m, c) in chains]
    for j in range(n_kt):
        vtj = vt_ref[j]
        for i, (m, c) in enumerate(chains):
            s_nxt = qk(j + 1, m, c) if j + 1 < n_kt else None
            m_run, acc = state[i]
            m_new = jnp.maximum(m_run, jnp.max(s_cur[i], axis=0, keepdims=True))
            alpha = jnp.exp2(m_run - m_new)
            p = jnp.exp2(s_cur[i] - m_new).astype(bf16)
            pv = jnp.dot(vtj, p, preferred_element_type=f32)
            state[i] = (m_new, alpha * acc + pv)
            s_cur[i] = s_nxt
    outs = []
    for m in range(n_maps):
        accs = [state[i][1] for i, (mm, _) in enumerate(chains) if mm == m]
        acc = accs[0] if len(accs) == 1 else jnp.concatenate(accs, axis=1)
        outs.append(acc[0:dv] / acc[dv:dv + 1])
    if n_maps == 1:
        o_ref[...] = outs[0].astype(o_ref.dtype)
    else:
        lf = lam_ref[...]
        lam = (jnp.exp(jnp.sum(lf[0:1] * lf[1:2], axis=1, keepdims=True))
               - jnp.exp(jnp.sum(lf[2:3] * lf[3:4], axis=1, keepdims=True)) + lam_init)
        o = outs[0] - lam * outs[1]
        o = o * lax.rsqrt(jnp.mean(o * o, axis=0, keepdims=True) + EPS) * g_ref[...]
        o_ref[...] = (o * (1.0 - lam_init)).astype(o_ref.dtype)


def _key_tile(lk):
    return max(t for t in range(MXU_DEPTH, min(ATTN_TK, lk) + 1, MXU_DEPTH) if lk % t == 0)


def _attention(qt, k, vt4, *, n_maps, dk, dv, tq, lq, n_out, q_start=0, out=None, lam=None, g=None, lam_init=0.0):
    n_heads = qt.shape[0]
    lk = k.shape[1]
    _, n_kt, dvp, tk = vt4.shape
    assert n_kt * tk == lk
    tq = min(tq, lq)
    qsub = min(ATTN_QSUB, tq)
    qb0 = q_start // tq
    in_specs = [
        pl.BlockSpec((None, n_maps * dk, tq), lambda h, i: (h, 0, qb0 + i)),
        pl.BlockSpec((n_maps, lk, dk), lambda h, i: (h, 0, 0)),
        pl.BlockSpec((None, n_kt, dvp, tk), lambda h, i: (h, 0, 0, 0)),
    ]
    args = [qt, k, vt4]
    if n_maps == 2:
        in_specs = [_const_spec(lam.shape), pl.BlockSpec((None, dv, 1), lambda h, i: (h, 0, 0))] + in_specs
        args = [lam, g] + args
    aliases = {}
    if out is not None:
        aliases = {len(args): 0}
        in_specs.append(pl.BlockSpec(memory_space=pl.ANY))
        args.append(out)
    kern = functools.partial(_attn_kernel, n_maps=n_maps, dk=dk, dv=dv, tk=tk, n_kt=n_kt, qsub=qsub,
                             lam_init=lam_init, n_in=len(args))
    return pl.pallas_call(
        kern, grid=(n_heads, lq // tq), in_specs=in_specs,
        out_specs=pl.BlockSpec((None, dv, tq), lambda h, i: (h, 0, qb0 + i)),
        out_shape=jax.ShapeDtypeStruct((n_heads, dv, n_out), bf16),
        input_output_aliases=aliases,
        compiler_params=_cparams(("parallel", "parallel")),
        name="attn_diff" if n_maps == 2 else "attn_mla",
    )(*args)


def _split3(x):
    hi = x.astype(bf16)
    r = x - hi.astype(f32)
    mid = r.astype(bf16)
    lo = (r - mid.astype(f32)).astype(bf16)
    return hi, mid, lo


def _log_sigmoid(x):
    return jnp.minimum(x, 0.0) - jnp.log(1.0 + jnp.exp(-jnp.abs(x)))


def _softcap(x):
    return GATE_SOFTCAP * jnp.tanh(x * (1.0 / GATE_SOFTCAP))


def _mlstm_kernel(gb_ref, gbt_ref,
                  qf_ref, ktf_ref, vf_ref, gf_ref, gtf_ref,
                  qb_ref, ktb_ref, vb_ref, gb2_ref, gtb_ref,
                  hf_ref, hb_ref, c_ref, m_ref):
    t = qf_ref.shape[0]

    @pl.when(pl.program_id(0) == 0)
    def _():
        c_ref[...] = jnp.zeros_like(c_ref)
        m_ref[...] = jnp.zeros_like(m_ref)

    row = lax.broadcasted_iota(jnp.int32, (t, t), 0)
    col = lax.broadcasted_iota(jnp.int32, (t, t), 1)
    lane = lax.broadcasted_iota(jnp.int32, (t, LANE), 1)
    ones_col = jnp.where(lane == 0, 1.0, 0.0).astype(bf16)
    dirs = ((qf_ref, ktf_ref, vf_ref, gf_ref, gtf_ref, hf_ref),
            (qb_ref, ktb_ref, vb_ref, gb2_ref, gtb_ref, hb_ref))
    for d, (q_ref, kt_ref, v_ref, g_ref, gt_ref, o_ref) in enumerate(dirs):
        mask = (col <= row) if d == 0 else (col >= row)
        maskb = mask.astype(bf16)
        last = t - 1 if d == 0 else 0
        g_tok = _softcap(g_ref[...] + gb_ref[...])
        g_t = _softcap(gt_ref[...] + gbt_ref[...])
        b_tok = sum(jnp.dot(maskb, part, preferred_element_type=f32) for part in _split3(_log_sigmoid(g_tok)))
        b_t = sum(lax.dot_general(part, maskb, NT_DIMS, preferred_element_type=f32)
                  for part in _split3(_log_sigmoid(g_t)))
        for hh in range(M_HEADS):
            ri = 4 * (2 * d) + hh
            rf = 4 * (2 * d + 1) + hh
            idx = d * M_HEADS + hh
            b_col = b_tok[:, rf:rf + 1]
            b_row = b_t[rf:rf + 1, :]
            a_row = g_t[ri:ri + 1, :] - b_row
            m_prev = m_ref[idx, 0:1, 0:1]
            ld = jnp.where(mask, b_col + a_row, -jnp.inf)
            inter = b_col + m_prev
            mj = jnp.maximum(inter, jnp.max(ld, axis=1, keepdims=True))
            dmat = jnp.exp(ld - mj)
            q = q_ref[:, hh * M_DK:(hh + 1) * M_DK]
            kt = kt_ref[hh * M_DK:(hh + 1) * M_DK, :]
            vp = jnp.concatenate([v_ref[:, hh * M_DV:(hh + 1) * M_DV], ones_col], axis=1)
            sc = jnp.dot(q, kt, preferred_element_type=f32) * dmat
            cp = c_ref[idx]
            num = (jnp.dot(sc.astype(bf16), vp, preferred_element_type=f32)
                   + jnp.exp(inter - mj) * jnp.dot(q, cp.astype(bf16), preferred_element_type=f32))
            den = num[:, M_DV:M_DV + 1]
            hj = num[:, 0:M_DV] / jnp.maximum(jnp.abs(den), jnp.exp(-mj))
            o_ref[:, hh * M_DV:(hh + 1) * M_DV] = hj.astype(o_ref.dtype)
            b_tot = b_t[rf:rf + 1, last:last + 1]
            lw = b_tot + a_row
            m_new = jnp.maximum(b_tot + m_prev, jnp.max(lw, axis=1, keepdims=True))
            kw = (kt.astype(f32) * jnp.exp(lw - m_new)).astype(bf16)
            c_ref[idx] = jnp.exp(b_tot + m_prev - m_new) * cp + jnp.dot(kw, vp, preferred_element_type=f32)
            m_ref[idx] = jnp.broadcast_to(m_new, (8, LANE))


def _mlstm(mq, mkt, mv, mg, mgt, gate_b):
    n = mq.shape[0]
    t = MLSTM_T
    nb = n // t
    gb = jnp.pad(gate_b.reshape(1, 16), ((0, 0), (0, LANE - 16)))
    gbt = gate_b.reshape(16, 1)
    fwd = lambda s: jnp.where(s == 0, nb - 1, s - 1)
    bwd = lambda s: nb - 1 - s
    specs = []
    for im in (fwd, bwd):
        specs += [pl.BlockSpec((t, 512), lambda s, im=im: (im(s), 0)),
                  pl.BlockSpec((512, t), lambda s, im=im: (0, im(s))),
                  pl.BlockSpec((t, 512), lambda s, im=im: (im(s), 0)),
                  pl.BlockSpec((t, LANE), lambda s, im=im: (im(s), 0)),
                  pl.BlockSpec((16, t), lambda s, im=im: (0, im(s)))]
    return pl.pallas_call(
        _mlstm_kernel, grid=(nb,),
        in_specs=[_const_spec(gb.shape), _const_spec(gbt.shape)] + specs,
        out_specs=[pl.BlockSpec((t, 512), lambda s: (fwd(s), 0)), pl.BlockSpec((t, 512), lambda s: (bwd(s), 0))],
        out_shape=[jax.ShapeDtypeStruct((n, 512), bf16), jax.ShapeDtypeStruct((n, 512), bf16)],
        scratch_shapes=[pltpu.VMEM((2 * M_HEADS, M_DK, 2 * M_DV), f32), pltpu.VMEM((2 * M_HEADS, 8, LANE), f32)],
        compiler_params=_cparams(("arbitrary",)), name="mlstm",
    )(gb, gbt, mq, mkt, mv, mg, mgt, mq, mkt, mv, mg, mgt)


def _rms(x, axis=-1):
    return x * lax.rsqrt(jnp.mean(x * x, axis=axis, keepdims=True) + EPS)


def _merge_kernel(x_ref, hf_ref, hb_ref, mo_ref, ya_ref, yd_ref, gl_ref, mng_ref, wb_ref, wout_ref,
                  mod_ref, rw_ref, rb_ref, xnew_ref, h2_ref, logit_ref):
    hs = hf_ref[...].astype(f32) + hb_ref[...].astype(f32)
    parts = [_rms(hs[:, hh * M_DV:(hh + 1) * M_DV]) for hh in range(M_HEADS)]
    ym = jnp.concatenate(parts, axis=1) * mng_ref[...] * jax.nn.sigmoid(mo_ref[...].astype(f32))
    yb0 = jnp.dot(ym.astype(bf16), wb_ref[0], preferred_element_type=f32)
    tm = x_ref.shape[0]
    tn_dims = (((0,), (0,)), ((), ()))
    yb1 = lax.dot_general(ya_ref[...].reshape(BRANCH_W, tm), wb_ref[1], tn_dims, preferred_element_type=f32)
    yb2 = lax.dot_general(yd_ref[...].reshape(BRANCH_W, tm), wb_ref[2], tn_dims, preferred_element_type=f32)
    d = D_MODEL
    z = (jax.nn.sigmoid(gl_ref[:, 0:d].astype(f32)) * yb0
         + jax.nn.sigmoid(gl_ref[:, d:2 * d].astype(f32)) * yb1
         + jax.nn.sigmoid(gl_ref[:, 2 * d:3 * d].astype(f32)) * yb2)
    y = jnp.dot(z.astype(bf16), wout_ref[...], preferred_element_type=f32)
    xnew = x_ref[...] + mod_ref[1:2, :] * (_rms(y) * mod_ref[0:1, :])
    h2 = _rms(xnew) * mod_ref[2:3, :] + mod_ref[3:4, :]
    xnew_ref[...] = xnew
    h2_ref[...] = h2.astype(bf16)
    h2_hi = h2.astype(bf16)
    h2_lo = (h2 - h2_hi.astype(f32)).astype(bf16)
    rw = rw_ref[...]
    rw_hi = rw.astype(bf16)
    rw_lo = (rw - rw_hi.astype(f32)).astype(bf16)
    logit_ref[...] = (jnp.dot(h2_hi, rw_hi, preferred_element_type=f32)
                      + jnp.dot(h2_lo, rw_hi, preferred_element_type=f32)
                      + jnp.dot(h2_hi, rw_lo, preferred_element_type=f32) + rb_ref[...])


def _merge(n_rows, n_lat_blocks, x_all, hf, hb, mo, ya, yd, gl, mng, wb, wout, mod, rw, rb):
    tm = ROW_BLOCK
    nb = n_rows // tm
    row = lambda i: (i, 0)
    grp = lambda i: (jnp.where(i >= n_lat_blocks, 1, 0), 0, 0)
    in_specs = [pl.BlockSpec((tm, D_MODEL), row), pl.BlockSpec((tm, 512), row), pl.BlockSpec((tm, 512), row),
                pl.BlockSpec((tm, 512), row),
                pl.BlockSpec((A_HEADS, A_V, tm), lambda i: (0, 0, i)),
                pl.BlockSpec((DF_HEADS, DF_DV, tm), lambda i: (0, 0, i)),
                pl.BlockSpec((tm, N_BRANCH * D_MODEL), row), _const_spec(mng.shape), _const_spec(wb.shape),
                _const_spec(wout.shape), pl.BlockSpec((None, 4, D_MODEL), grp),
                _const_spec(rw.shape), _const_spec(rb.shape)]
    return pl.pallas_call(
        _merge_kernel, grid=(nb,), in_specs=in_specs,
        out_specs=[pl.BlockSpec((tm, D_MODEL), row), pl.BlockSpec((tm, D_MODEL), row), pl.BlockSpec((tm, LANE), row)],
        out_shape=[jax.ShapeDtypeStruct((n_rows, D_MODEL), f32),
                   jax.ShapeDtypeStruct((max(n_rows, H2_ROWS), D_MODEL), bf16),
                   jax.ShapeDtypeStruct((n_rows, LANE), f32)],
        compiler_params=_cparams(("parallel",)), name="merge",
    )(x_all, hf, hb, mo, ya, yd, gl, mng, wb, wout, mod, rw, rb)


def _moe_kernel(be_ref, nu_ref, xs_ref, wgu_ref, bgu_ref, wd_ref, bd_ref, o_ref, wgu_s, wd_s):
    i = pl.program_id(0)
    prev = be_ref[jnp.maximum(i - 1, 0)]
    used = i < nu_ref[0]

    @pl.when(jnp.logical_and(used, jnp.logical_or(i == 0, be_ref[i] != prev)))
    def _():
        wgu_s[...] = wgu_ref[...].astype(bf16)
        wd_s[...] = wd_ref[...].astype(bf16)

    @pl.when(used)
    def _():
        gu = jnp.dot(xs_ref[...], wgu_s[...], preferred_element_type=f32) + bgu_ref[...]
        gate = jnp.minimum(gu[:, :D_EXPERT], SWIGLU_LIMIT)
        up = jnp.clip(gu[:, D_EXPERT:], -SWIGLU_LIMIT, SWIGLU_LIMIT)
        act = (up + 1.0) * (gate * jax.nn.sigmoid(SWIGLU_ALPHA * gate))
        y = jnp.dot(act.astype(bf16), wd_s[...], preferred_element_type=f32) + bd_ref[...]
        o_ref[...] = y.astype(o_ref.dtype)

    @pl.when(jnp.logical_not(used))
    def _():
        o_ref[...] = jnp.zeros_like(o_ref)


def _moe_experts(blk_e, n_used, xs, layer, w_gu, b_gu, w_down, b_down):
    cap = xs.shape[0]
    tm = MOE_TM
    nblk = cap // tm
    grid_spec = pltpu.PrefetchScalarGridSpec(
        num_scalar_prefetch=2, grid=(nblk,),
        in_specs=[pl.BlockSpec((tm, D_MODEL), lambda i, be, nu: (i, 0)),
                  pl.BlockSpec((None, None, D_MODEL, 2 * D_EXPERT), lambda i, be, nu: (layer, be[i], 0, 0)),
                  pl.BlockSpec((None, None, 1, 2 * D_EXPERT), lambda i, be, nu: (layer, be[i], 0, 0)),
                  pl.BlockSpec((None, None, D_EXPERT, D_MODEL), lambda i, be, nu: (layer, be[i], 0, 0)),
                  pl.BlockSpec((None, None, 1, D_MODEL), lambda i, be, nu: (layer, be[i], 0, 0))],
        out_specs=pl.BlockSpec((tm, D_MODEL), lambda i, be, nu: (i, 0)),
        scratch_shapes=[pltpu.VMEM((D_MODEL, 2 * D_EXPERT), bf16), pltpu.VMEM((D_EXPERT, D_MODEL), bf16)])
    return pl.pallas_call(
        _moe_kernel, grid_spec=grid_spec,
        out_shape=jax.ShapeDtypeStruct((cap, D_MODEL), bf16),
        compiler_params=_cparams(("arbitrary",)), name="moe_experts",
    )(blk_e, n_used, xs, w_gu, b_gu.reshape(DEPTH, N_EXPERTS, 1, -1), w_down, b_down.reshape(DEPTH, N_EXPERTS, 1, -1))


def _moe(h2, logits, layer, w_gu, b_gu, w_down, b_down):
    n = logits.shape[0]
    tm = MOE_TM
    i32 = jnp.int32
    top_v, top_i = lax.top_k(logits[:, :N_EXPERTS], TOP_K)
    wts = jax.nn.softmax(top_v, axis=-1)
    nk = n * TOP_K
    flat_e = top_i.reshape(nk).astype(i32)
    iota = jnp.arange(nk, dtype=i32)
    se, order = lax.sort((flat_e, iota), num_keys=1, is_stable=True)
    e_ids = jnp.arange(N_EXPERTS, dtype=i32)
    grp_start = jnp.sum((se[:, None] < e_ids[None, :]).astype(i32), axis=0)
    counts = jnp.sum((se[:, None] == e_ids[None, :]).astype(i32), axis=0)
    padded = (counts + tm - 1) // tm * tm
    pad_end = jnp.cumsum(padded)
    pad_start = pad_end - padded
    off = pad_start - grp_start
    step = off - jnp.concatenate([jnp.zeros((1,), i32), off[:-1]])
    dest_sorted = iota + jnp.sum(jnp.where(iota[:, None] >= grp_start[None, :], step[None, :], 0), axis=1)
    _, pos_km = lax.sort(((order % TOP_K) * n + order // TOP_K, dest_sorted), num_keys=1)
    cap = -(-(nk + N_EXPERTS * (tm - 1)) // tm) * tm
    nblk = cap // tm
    bstart = jnp.arange(nblk, dtype=i32) * tm
    blk_e = jnp.minimum(jnp.sum((pad_end[None, :] <= bstart[:, None]).astype(i32), axis=1), N_EXPERTS - 1)
    n_used = (pad_end[-1] // tm).astype(i32).reshape(1)
    sel = blk_e[:, None] == e_ids[None, :]
    off_b = jnp.sum(jnp.where(sel, off[None, :], 0), axis=1)
    end_b = jnp.sum(jnp.where(sel, (pad_start + counts)[None, :], 0), axis=1)
    p2 = bstart[:, None] + jnp.arange(tm, dtype=i32)[None, :]
    src = jnp.clip(p2 - off_b[:, None], 0, nk - 1)
    slot_t = jnp.where(p2 < end_b[:, None], (order // TOP_K)[src], p2 % n).reshape(cap)
    xs = h2[slot_t]
    ys = _moe_experts(blk_e, n_used, xs, layer, w_gu, b_gu, w_down, b_down)
    return ys[pos_km].reshape(TOP_K, n, D_MODEL), wts


def _resid_kernel(x_ref, y_ref, w_ref, mod_ref, o_ref):
    w = w_ref[...]
    f = y_ref[0].astype(f32) * w[:, 0:1]
    for kk in range(1, TOP_K):
        f = f + y_ref[kk].astype(f32) * w[:, kk:kk + 1]
    o_ref[...] = x_ref[...] + mod_ref[1:2, :] * (_rms(f) * mod_ref[0:1, :])


def _resid(n_lat_blocks, xnew, picked, wts, mod):
    n = xnew.shape[0]
    tm = ROW_BLOCK
    row = lambda i: (i, 0)
    grp = lambda i: (jnp.where(i >= n_lat_blocks, 1, 0), 0, 0)
    return pl.pallas_call(
        _resid_kernel, grid=(n // tm,),
        in_specs=[pl.BlockSpec((tm, D_MODEL), row), pl.BlockSpec((TOP_K, tm, D_MODEL), lambda i: (0, i, 0)),
                  pl.BlockSpec((tm, TOP_K), row), pl.BlockSpec((None, 2, D_MODEL), grp)],
        out_specs=pl.BlockSpec((tm, D_MODEL), row),
        out_shape=jax.ShapeDtypeStruct((n, D_MODEL), f32),
        compiler_params=_cparams(("parallel",)), name="resid",
    )(xnew, picked, wts, mod)


def _rope_tables(n_lat, n_ctx, dim):
    rows = n_lat // GRID_W
    row = jnp.repeat(jnp.arange(rows, dtype=f32), GRID_W)
    col = jnp.tile(jnp.arange(GRID_W, dtype=f32), rows)
    quarter = dim // 4
    inv = ROPE_BASE ** (-jnp.arange(quarter, dtype=f32) / quarter)
    ang = jnp.concatenate([row[:, None] * inv, col[:, None] * inv], axis=-1)
    cos = jnp.concatenate([jnp.cos(ang), jnp.ones((n_ctx, dim // 2), f32)], axis=0).T
    sin = jnp.concatenate([jnp.sin(ang), jnp.zeros((n_ctx, dim // 2), f32)], axis=0).T
    return cos, sin


def kernel(x, c, ctx, c_ctx, ada_w, ada_b, norm_g, w_in, m_gate_b, m_norm_g, q_norm_g, w_uq, kv_norm_g, w_ukv,
           diff_lam, diff_norm_g, w_branch, w_out, router_w, router_b, w_gu, b_gu, w_down, b_down):
    n_lat = x.shape[1]
    n_ctx = ctx.shape[1]
    n = n_lat + n_ctx
    assert x.shape[0] == 1 and n_lat % ROW_BLOCK == 0 and n_ctx == MLSTM_T and n_lat % GRID_W == 0
    nlb = n_lat // ROW_BLOCK
    x_all = jnp.concatenate([x[0], ctx[0]], axis=0)

    cond = jnp.zeros((8, D_MODEL), f32).at[0].set(c[0]).at[1].set(c_ctx)
    mods = _adaln(cond, ada_w, ada_b)[:, :2].reshape(DEPTH, 2, 6, D_MODEL)

    ca, sa = _rope_tables(n_lat, n_ctx, A_ROPE)
    cd, sd = _rope_tables(n_lat, n_ctx, DF_DK)

    for l in range(DEPTH):
        need_ctx = l < DEPTH - 1
        lam_init = 0.8 - 0.6 * math.exp(-0.3 * l)
        sh1, sc1, ga1, sh2, sc2, ga2 = [mods[l, :, j] for j in range(6)]
        ab1 = jnp.stack([norm_g[l, 0][None] * (1.0 + sc1), sh1], axis=1)
        w_tok, w_tr, wt_uq, wt_ukv = _pack_inproj_weights(w_in[l], w_uq[l], w_ukv[l])
        (mq, mv, mo, mg, gl, mkt, mgt, qta, k_a, vta, qtd, k_d, vtd) = _inproj(
            x_all, ab1, nlb, w_tok, w_tr, wt_uq, wt_ukv,
            q_norm_g[l].reshape(Q_LORA, 1), kv_norm_g[l].reshape(KV_LORA, 1), ca, sa, cd, sd)

        hf, hb = _mlstm(mq, mkt, mv, mg, mgt, m_gate_b[l])

        qtd4 = qtd.reshape(DF_HEADS, 2 * DF_DK, n)
        dg = diff_norm_g[l].reshape(DF_HEADS, DF_DV, 1)
        n_rows = n if need_ctx else n_lat
        attn_a = functools.partial(_attention, n_maps=1, dk=A_DKP, dv=A_V, n_out=n_rows)
        attn_d = functools.partial(_attention, n_maps=2, dk=DF_DK, dv=DF_DV, n_out=n_rows,
                                   lam=diff_lam[l], g=dg, lam_init=lam_init)
        ya = attn_a(qta, k_a, vta, tq=ATTN_TQ_MLA, lq=n_lat)
        yd = attn_d(qtd4, k_d, vtd, tq=ATTN_TQ_DIFF, lq=n_lat)
        if need_ctx:
            ctx_tile = lambda v: v[:, -1:, :, v.shape[3] - n_ctx:]
            ya = attn_a(qta, k_a[:, n_lat:], ctx_tile(vta), tq=n_ctx, lq=n_ctx, q_start=n_lat, out=ya)
            yd = attn_d(qtd4, k_d[:, n_lat:], ctx_tile(vtd), tq=n_ctx, lq=n_ctx, q_start=n_lat, out=yd)

        mod_m = jnp.stack([jnp.broadcast_to(norm_g[l, 1][None], (2, D_MODEL)), ga1,
                           norm_g[l, 2][None] * (1.0 + sc2), sh2], axis=1)
        rw = jnp.pad(router_w[l], ((0, 0), (0, LANE - N_EXPERTS)))
        rb = jnp.pad(router_b[l].reshape(1, N_EXPERTS), ((0, 0), (0, LANE - N_EXPERTS)))
        xnew, h2, logits = _merge(n_rows, nlb, x_all, hf, hb, mo, ya, yd, gl,
                                  m_norm_g[l].reshape(1, -1), w_branch[l].astype(bf16), w_out[l].astype(bf16),
                                  mod_m, rw, rb)
        picked, wts = _moe(h2, logits, l, w_gu, b_gu, w_down, b_down)
        mod_r = jnp.stack([jnp.broadcast_to(norm_g[l, 3][None], (2, D_MODEL)), ga2], axis=1)
        x_all = _resid(nlb, xnew, picked, wts, mod_r)
    return x_all[:n_lat][None]
```

```python
import functools
import math

import jax
import jax.numpy as jnp
from jax import lax
from jax.experimental import pallas as pl
from jax.experimental.pallas import tpu as pltpu

f32 = jnp.float32
bf16 = jnp.bfloat16

D_MODEL = 1024
DEPTH = 2
GRID_W = 64
EPS = 1e-6
ROPE_BASE = 10000.0
M_HEADS = 4
M_DK = 128
M_DV = 128
GATE_SOFTCAP = 15.0
A_HEADS = 8
A_NOPE = 64
A_ROPE = 32
A_V = 64
Q_LORA = 256
KV_LORA = 128
DF_HEADS = 4
DF_DK = 64
DF_DV = 128
N_BRANCH = 3
BRANCH_W = 512
N_EXPERTS = 32
TOP_K = 4
D_EXPERT = 1024
SWIGLU_LIMIT = 7.0
SWIGLU_ALPHA = 1.702

IN_SIZES = (M_HEADS * M_DK, M_HEADS * M_DK, M_HEADS * M_DV, M_HEADS * M_DV, 4 * M_HEADS,
            Q_LORA, KV_LORA, A_ROPE,
            2 * DF_HEADS * DF_DK, 2 * DF_HEADS * DF_DK, DF_HEADS * DF_DV,
            N_BRANCH * D_MODEL)

LOG2E = 1.4426950408889634
LANE = 128
A_DKP = 128
A_DVP = A_V + 16
DF_DVP = DF_DV + 16
VMEM_LIMIT = 56 * 1024 * 1024

ROW_BLOCK = 256
MXU_DEPTH = 256
ATTN_TQ_MLA = 1024
ATTN_TQ_DIFF = 512
ATTN_QSUB = 256
ATTN_TK = 1280
MLSTM_T = 256
MOE_TM = 512
H2_ROWS = 32768

NT_DIMS = (((1,), (1,)), ((), ()))


def _cparams(sem):
    return pltpu.CompilerParams(dimension_semantics=sem, vmem_limit_bytes=VMEM_LIMIT)


def _const_spec(shape):
    nd = len(shape)
    return pl.BlockSpec(shape, lambda *_: (0,) * nd)


def _adaln_kernel(cond_ref, w_ref, b_ref, o_ref):
    cnd = cond_ref[...]
    a = cnd * jax.nn.sigmoid(cnd)
    o_ref[...] = jnp.dot(a, w_ref[...], preferred_element_type=f32,
                         precision=lax.Precision.HIGHEST) + b_ref[...]


def _adaln(cond, ada_w, ada_b):
    tn = 1536
    nj = (6 * D_MODEL) // tn
    return pl.pallas_call(
        _adaln_kernel,
        grid=(DEPTH, nj),
        in_specs=[pl.BlockSpec((8, D_MODEL), lambda l, j: (0, 0)),
                  pl.BlockSpec((None, D_MODEL, tn), lambda l, j: (l, 0, j)),
                  pl.BlockSpec((None, 1, tn), lambda l, j: (l, 0, j))],
        out_specs=pl.BlockSpec((None, 8, tn), lambda l, j: (l, 0, j)),
        out_shape=jax.ShapeDtypeStruct((DEPTH, 8, 6 * D_MODEL), f32),
        compiler_params=_cparams(("parallel", "parallel")),
        name="adaln",
    )(cond, ada_w, ada_b.reshape(DEPTH, 1, 6 * D_MODEL))


TOK_W = 3 * 512 + LANE + N_BRANCH * D_MODEL
TR_W = 512 + 16 + Q_LORA + KV_LORA + A_ROPE + 3 * 512


def _ones_row_block(tm):
    r = lax.broadcasted_iota(jnp.int32, (16, tm), 0)
    return jnp.where(r == 0, 1.0, 0.0).astype(bf16)


def _inproj_kernel(x_ref, ab_ref, wtok_ref, wtr_ref, wuq_ref, wukv_ref, gq_ref, gkv_ref,
                   ca_ref, sa_ref, cd_ref, sd_ref,
                   mq_ref, mv_ref, mo_ref, mg_ref, gl_ref, mkt_ref, mgt_ref,
                   qta_ref, kta_ref, vta_ref, qtd_ref, ktd_ref, vtd_ref):
    tm = x_ref.shape[0]
    x = x_ref[...]
    ms = jnp.mean(x * x, axis=-1, keepdims=True)
    h = (x * lax.rsqrt(ms + EPS)) * ab_ref[0:1, :] + ab_ref[1:2, :]
    hb = h.astype(bf16)

    tok = jnp.dot(hb, wtok_ref[...], preferred_element_type=f32)
    mq_ref[...] = (tok[:, 0:512] * (M_DK ** -0.5)).astype(bf16)
    mv_ref[...] = tok[:, 512:1024].astype(bf16)
    mo_ref[...] = tok[:, 1024:1536].astype(bf16)
    mg_ref[...] = tok[:, 1536:1664]
    gl_ref[...] = tok[:, 1664:TOK_W].astype(bf16)

    tr = lax.dot_general(wtr_ref[...], hb, NT_DIMS, preferred_element_type=f32)
    mkt_ref[...] = tr[0:512].astype(bf16)
    mgt_ref[...] = tr[512:528]
    cq = tr[528:784]
    ckv = tr[784:912]
    kpe = tr[912:944]
    dq = tr[944:1456]
    dk = tr[1456:1968]
    dv = tr[1968:2480]

    ca = ca_ref[...]
    sa = sa_ref[...]
    cd = cd_ref[...]
    sd = sd_ref[...]
    ones_blk = _ones_row_block(tm)
    zeros32 = jnp.zeros((32, tm), bf16)

    cqn = (cq * lax.rsqrt(jnp.mean(cq * cq, axis=0, keepdims=True) + EPS) * gq_ref[...]).astype(bf16)
    qscale = ((A_NOPE + A_ROPE) ** -0.5) * LOG2E
    qt = jnp.dot(wuq_ref[...], cqn, preferred_element_type=f32) * qscale
    for hh in range(A_HEADS):
        a = qt[512 + 16 * hh:528 + 16 * hh]
        b = qt[640 + 16 * hh:656 + 16 * hh]
        qta_ref[hh, 0:64, :] = qt[64 * hh:64 * hh + 64].astype(bf16)
        qta_ref[hh, 64:80, :] = (a * ca - b * sa).astype(bf16)
        qta_ref[hh, 80:96, :] = (a * sa + b * ca).astype(bf16)
        qta_ref[hh, 96:128, :] = zeros32

    ckvn = (ckv * lax.rsqrt(jnp.mean(ckv * ckv, axis=0, keepdims=True) + EPS) * gkv_ref[...]).astype(bf16)
    kvt = jnp.dot(wukv_ref[...], ckvn, preferred_element_type=f32)
    ka = kpe[0:16]
    kb = kpe[16:32]
    kpe_rot = jnp.concatenate([ka * ca - kb * sa, ka * sa + kb * ca, jnp.zeros((32, tm), f32)], axis=0)
    for hh in range(A_HEADS):
        kt_h = jnp.concatenate([kvt[64 * hh:64 * hh + 64], kpe_rot], axis=0)
        kta_ref[hh] = kt_h.T.astype(bf16)
        vta_ref[hh, 0:64, :] = kvt[512 + 64 * hh:576 + 64 * hh].astype(bf16)
        vta_ref[hh, 64:80, :] = ones_blk

    dscale = (DF_DK ** -0.5) * LOG2E
    for hh in range(2 * DF_HEADS):
        a = dq[32 * hh:32 * hh + 32]
        b = dq[256 + 32 * hh:288 + 32 * hh]
        qtd_ref[hh, 0:32, :] = ((a * cd - b * sd) * dscale).astype(bf16)
        qtd_ref[hh, 32:64, :] = ((a * sd + b * cd) * dscale).astype(bf16)
        a = dk[32 * hh:32 * hh + 32]
        b = dk[256 + 32 * hh:288 + 32 * hh]
        ktd_ref[hh] = jnp.concatenate([a * cd - b * sd, a * sd + b * cd], axis=0).T.astype(bf16)
    for hh in range(DF_HEADS):
        vtd_ref[hh, 0:128, :] = dv[128 * hh:128 * hh + 128].astype(bf16)
        vtd_ref[hh, 128:144, :] = ones_blk


def _pack_inproj_weights(w, w_uq, w_ukv):
    o = [0]
    for s in IN_SIZES:
        o.append(o[-1] + s)
    mq, mk, mv, mo, mg, cq, ckv, kpe, dq, dk, dv, gl = [w[:, o[i]:o[i + 1]] for i in range(12)]
    mg_pad = jnp.pad(mg, ((0, 0), (0, LANE - mg.shape[1])))
    w_tok = jnp.concatenate([mq, mv, mo, mg_pad, gl], axis=1).astype(bf16)

    def split_halves(t, n_heads, d):
        t3 = t.reshape(t.shape[0], n_heads, d)
        return jnp.concatenate([t3[:, :, :d // 2].reshape(t.shape[0], -1),
                                t3[:, :, d // 2:].reshape(t.shape[0], -1)], axis=1)

    w_tr = jnp.concatenate([mk, mg, cq, ckv, kpe, split_halves(dq, 2 * DF_HEADS, DF_DK),
                            split_halves(dk, 2 * DF_HEADS, DF_DK), dv], axis=1).T.astype(bf16)
    uq = w_uq.reshape(Q_LORA, A_HEADS, A_NOPE + A_ROPE)
    half = A_ROPE // 2
    wt_uq = jnp.concatenate([uq[:, :, :A_NOPE].reshape(Q_LORA, -1),
                             uq[:, :, A_NOPE:A_NOPE + half].reshape(Q_LORA, -1),
                             uq[:, :, A_NOPE + half:].reshape(Q_LORA, -1)], axis=1).T.astype(bf16)
    ukv = w_ukv.reshape(KV_LORA, A_HEADS, A_NOPE + A_V)
    wt_ukv = jnp.concatenate([ukv[:, :, :A_NOPE].reshape(KV_LORA, -1),
                              ukv[:, :, A_NOPE:].reshape(KV_LORA, -1)], axis=1).T.astype(bf16)
    return w_tok, w_tr, wt_uq, wt_ukv


def _inproj(x_all, ab, n_lat_blocks, w_tok, w_tr, wt_uq, wt_ukv, gq, gkv, ca, sa, cd, sd):
    n = x_all.shape[0]
    tm = ROW_BLOCK
    tk = _key_tile(n)
    per = tk // tm
    vtile = lambda i: (0, i // per, 0, i % per)
    nb = n // tm
    row = lambda i: (i, 0)
    colb = lambda i: (0, i)
    col3 = lambda i: (0, 0, i)
    grp = lambda i: (jnp.where(i >= n_lat_blocks, 1, 0), 0, 0)
    out_shapes = [
        jax.ShapeDtypeStruct((n, 512), bf16), jax.ShapeDtypeStruct((n, 512), bf16),
        jax.ShapeDtypeStruct((n, 512), bf16), jax.ShapeDtypeStruct((n, LANE), f32),
        jax.ShapeDtypeStruct((n, N_BRANCH * D_MODEL), bf16),
        jax.ShapeDtypeStruct((512, n), bf16), jax.ShapeDtypeStruct((16, n), f32),
        jax.ShapeDtypeStruct((A_HEADS, A_DKP, n), bf16), jax.ShapeDtypeStruct((A_HEADS, n, A_DKP), bf16),
        jax.ShapeDtypeStruct((A_HEADS, n // tk, A_DVP, tk), bf16),
        jax.ShapeDtypeStruct((2 * DF_HEADS, DF_DK, n), bf16), jax.ShapeDtypeStruct((2 * DF_HEADS, n, DF_DK), bf16),
        jax.ShapeDtypeStruct((DF_HEADS, n // tk, DF_DVP, tk), bf16),
    ]
    out_specs = [
        pl.BlockSpec((tm, 512), row), pl.BlockSpec((tm, 512), row), pl.BlockSpec((tm, 512), row),
        pl.BlockSpec((tm, LANE), row), pl.BlockSpec((tm, N_BRANCH * D_MODEL), row),
        pl.BlockSpec((512, tm), colb), pl.BlockSpec((16, tm), colb),
        pl.BlockSpec((A_HEADS, A_DKP, tm), col3), pl.BlockSpec((A_HEADS, tm, A_DKP), lambda i: (0, i, 0)),
        pl.BlockSpec((A_HEADS, None, A_DVP, tm), vtile),
        pl.BlockSpec((2 * DF_HEADS, DF_DK, tm), col3), pl.BlockSpec((2 * DF_HEADS, tm, DF_DK), lambda i: (0, i, 0)),
        pl.BlockSpec((DF_HEADS, None, DF_DVP, tm), vtile),
    ]
    in_specs = [
        pl.BlockSpec((tm, D_MODEL), row),
        pl.BlockSpec((None, 2, D_MODEL), grp),
        _const_spec(w_tok.shape), _const_spec(w_tr.shape), _const_spec(wt_uq.shape), _const_spec(wt_ukv.shape),
        _const_spec(gq.shape), _const_spec(gkv.shape),
        pl.BlockSpec((A_ROPE // 2, tm), colb), pl.BlockSpec((A_ROPE // 2, tm), colb),
        pl.BlockSpec((DF_DK // 2, tm), colb), pl.BlockSpec((DF_DK // 2, tm), colb),
    ]
    return pl.pallas_call(
        _inproj_kernel, grid=(nb,), in_specs=in_specs, out_specs=out_specs, out_shape=out_shapes,
        compiler_params=_cparams(("parallel",)), name="inproj",
    )(x_all, ab, w_tok, w_tr, wt_uq, wt_ukv, gq, gkv, ca, sa, cd, sd)


def _attn_kernel(*refs, n_maps, dk, dv, tk, n_kt, qsub, lam_init, n_in):
    o_ref = refs[n_in]
    if n_maps == 2:
        lam_ref, g_ref, qt_ref, k_ref, vt_ref = refs[:5]
    else:
        qt_ref, k_ref, vt_ref = refs[:3]
    tq = qt_ref.shape[-1]
    dvp = vt_ref.shape[-2]
    chains = [(m, c) for m in range(n_maps) for c in range(tq // qsub)]

    def qk(j, m, c):
        qt = qt_ref[m * dk:(m + 1) * dk, c * qsub:(c + 1) * qsub]
        return jnp.dot(k_ref[m, j * tk:(j + 1) * tk, :], qt, preferred_element_type=f32)

    state = [(jnp.full((1, qsub), -jnp.inf, f32), jnp.zeros((dvp, qsub), f32)) for _ in chains]
    s_cur = [qk(0, m, c) for (m, c) in chains]
    for j in range(n_kt):
        vtj = vt_ref[j]
        for i, (m, c) in enumerate(chains):
            s_nxt = qk(j + 1, m, c) if j + 1 < n_kt else None
            m_run, acc = state[i]
            m_new = jnp.maximum(m_run, jnp.max(s_cur[i], axis=0, keepdims=True))
            alpha = jnp.exp2(m_run - m_new)
            p = jnp.exp2(s_cur[i] - m_new).astype(bf16)
            pv = jnp.dot(vtj, p, preferred_element_type=f32)
            state[i] = (m_new, alpha * acc + pv)
            s_cur[i] = s_nxt
    outs = []
    for m in range(n_maps):
        accs = [state[i][1] for i, (mm, _) in enumerate(chains) if mm == m]
        acc = accs[0] if len(accs) == 1 else jnp.concatenate(accs, axis=1)
        outs.append(acc[0:dv] / acc[dv:dv + 1])
    if n_maps == 1:
        o_ref[...] = outs[0].astype(o_ref.dtype)
    else:
        lf = lam_ref[...]
        lam = (jnp.exp(jnp.sum(lf[0:1] * lf[1:2], axis=1, keepdims=True))
               - jnp.exp(jnp.sum(lf[2:3] * lf[3:4], axis=1, keepdims=True)) + lam_init)
        o = outs[0] - lam * outs[1]
        o = o * lax.rsqrt(jnp.mean(o * o, axis=0, keepdims=True) + EPS) * g_ref[...]
        o_ref[...] = (o * (1.0 - lam_init)).astype(o_ref.dtype)


def _key_tile(lk):
    return max(t for t in range(MXU_DEPTH, min(ATTN_TK, lk) + 1, MXU_DEPTH) if lk % t == 0)


def _attention(qt, k, vt4, *, n_maps, dk, dv, tq, lq, n_out, q_start=0, out=None, lam=None, g=None, lam_init=0.0):
    n_heads = qt.shape[0]
    lk = k.shape[1]
    _, n_kt, dvp, tk = vt4.shape
    assert n_kt * tk == lk
    tq = min(tq, lq)
    qsub = min(ATTN_QSUB, tq)
    qb0 = q_start // tq
    in_specs = [
        pl.BlockSpec((None, n_maps * dk, tq), lambda h, i: (h, 0, qb0 + i)),
        pl.BlockSpec((n_maps, lk, dk), lambda h, i: (h, 0, 0)),
        pl.BlockSpec((None, n_kt, dvp, tk), lambda h, i: (h, 0, 0, 0)),
    ]
    args = [qt, k, vt4]
    if n_maps == 2:
        in_specs = [_const_spec(lam.shape), pl.BlockSpec((None, dv, 1), lambda h, i: (h, 0, 0))] + in_specs
        args = [lam, g] + args
    aliases = {}
    if out is not None:
        aliases = {len(args): 0}
        in_specs.append(pl.BlockSpec(memory_space=pl.ANY))
        args.append(out)
    kern = functools.partial(_attn_kernel, n_maps=n_maps, dk=dk, dv=dv, tk=tk, n_kt=n_kt, qsub=qsub,
                             lam_init=lam_init, n_in=len(args))
    return pl.pallas_call(
        kern, grid=(n_heads, lq // tq), in_specs=in_specs,
        out_specs=pl.BlockSpec((None, dv, tq), lambda h, i: (h, 0, qb0 + i)),
        out_shape=jax.ShapeDtypeStruct((n_heads, dv, n_out), bf16),
        input_output_aliases=aliases,
        compiler_params=_cparams(("parallel", "parallel")),
        name="attn_diff" if n_maps == 2 else "attn_mla",
    )(*args)


def _split3(x):
    hi = x.astype(bf16)
    r = x - hi.astype(f32)
    mid = r.astype(bf16)
    lo = (r - mid.astype(f32)).astype(bf16)
    return hi, mid, lo


def _log_sigmoid(x):
    return jnp.minimum(x, 0.0) - jnp.log(1.0 + jnp.exp(-jnp.abs(x)))


def _softcap(x):
    return GATE_SOFTCAP * jnp.tanh(x * (1.0 / GATE_SOFTCAP))


def _mlstm_kernel(gb_ref, gbt_ref,
                  qf_ref, ktf_ref, vf_ref, gf_ref, gtf_ref,
                  qb_ref, ktb_ref, vb_ref, gb2_ref, gtb_ref,
                  hf_ref, hb_ref, c_ref, m_ref):
    t = qf_ref.shape[0]

    @pl.when(pl.program_id(0) == 0)
    def _():
        c_ref[...] = jnp.zeros_like(c_ref)
        m_ref[...] = jnp.zeros_like(m_ref)

    row = lax.broadcasted_iota(jnp.int32, (t, t), 0)
    col = lax.broadcasted_iota(jnp.int32, (t, t), 1)
    lane = lax.broadcasted_iota(jnp.int32, (t, LANE), 1)
    ones_col = jnp.where(lane == 0, 1.0, 0.0).astype(bf16)
    dirs = ((qf_ref, ktf_ref, vf_ref, gf_ref, gtf_ref, hf_ref),
            (qb_ref, ktb_ref, vb_ref, gb2_ref, gtb_ref, hb_ref))
    for d, (q_ref, kt_ref, v_ref, g_ref, gt_ref, o_ref) in enumerate(dirs):
        mask = (col <= row) if d == 0 else (col >= row)
        maskb = mask.astype(bf16)
        last = t - 1 if d == 0 else 0
        g_tok = _softcap(g_ref[...] + gb_ref[...])
        g_t = _softcap(gt_ref[...] + gbt_ref[...])
        b_tok = sum(jnp.dot(maskb, part, preferred_element_type=f32) for part in _split3(_log_sigmoid(g_tok)))
        b_t = sum(lax.dot_general(part, maskb, NT_DIMS, preferred_element_type=f32)
                  for part in _split3(_log_sigmoid(g_t)))
        for hh in range(M_HEADS):
            ri = 4 * (2 * d) + hh
            rf = 4 * (2 * d + 1) + hh
            idx = d * M_HEADS + hh
            b_col = b_tok[:, rf:rf + 1]
            b_row = b_t[rf:rf + 1, :]
            a_row = g_t[ri:ri + 1, :] - b_row
            m_prev = m_ref[idx, 0:1, 0:1]
            ld = jnp.where(mask, b_col + a_row, -jnp.inf)
            inter = b_col + m_prev
            mj = jnp.maximum(inter, jnp.max(ld, axis=1, keepdims=True))
            dmat = jnp.exp(ld - mj)
            q = q_ref[:, hh * M_DK:(hh + 1) * M_DK]
            kt = kt_ref[hh * M_DK:(hh + 1) * M_DK, :]
            vp = jnp.concatenate([v_ref[:, hh * M_DV:(hh + 1) * M_DV], ones_col], axis=1)
            sc = jnp.dot(q, kt, preferred_element_type=f32) * dmat
            cp = c_ref[idx]
            num = (jnp.dot(sc.astype(bf16), vp, preferred_element_type=f32)
                   + jnp.exp(inter - mj) * jnp.dot(q, cp.astype(bf16), preferred_element_type=f32))
            den = num[:, M_DV:M_DV + 1]
            hj = num[:, 0:M_DV] / jnp.maximum(jnp.abs(den), jnp.exp(-mj))
            o_ref[:, hh * M_DV:(hh + 1) * M_DV] = hj.astype(o_ref.dtype)
            b_tot = b_t[rf:rf + 1, last:last + 1]
            lw = b_tot + a_row
            m_new = jnp.maximum(b_tot + m_prev, jnp.max(lw, axis=1, keepdims=True))
            kw = (kt.astype(f32) * jnp.exp(lw - m_new)).astype(bf16)
            c_ref[idx] = jnp.exp(b_tot + m_prev - m_new) * cp + jnp.dot(kw, vp, preferred_element_type=f32)
            m_ref[idx] = jnp.broadcast_to(m_new, (8, LANE))


def _mlstm(mq, mkt, mv, mg, mgt, gate_b):
    n = mq.shape[0]
    t = MLSTM_T
    nb = n // t
    gb = jnp.pad(gate_b.reshape(1, 16), ((0, 0), (0, LANE - 16)))
    gbt = gate_b.reshape(16, 1)
    fwd = lambda s: jnp.where(s == 0, nb - 1, s - 1)
    bwd = lambda s: nb - 1 - s
    specs = []
    for im in (fwd, bwd):
        specs += [pl.BlockSpec((t, 512), lambda s, im=im: (im(s), 0)),
                  pl.BlockSpec((512, t), lambda s, im=im: (0, im(s))),
                  pl.BlockSpec((t, 512), lambda s, im=im: (im(s), 0)),
                  pl.BlockSpec((t, LANE), lambda s, im=im: (im(s), 0)),
                  pl.BlockSpec((16, t), lambda s, im=im: (0, im(s)))]
    return pl.pallas_call(
        _mlstm_kernel, grid=(nb,),
        in_specs=[_const_spec(gb.shape), _const_spec(gbt.shape)] + specs,
        out_specs=[pl.BlockSpec((t, 512), lambda s: (fwd(s), 0)), pl.BlockSpec((t, 512), lambda s: (bwd(s), 0))],
        out_shape=[jax.ShapeDtypeStruct((n, 512), bf16), jax.ShapeDtypeStruct((n, 512), bf16)],
        scratch_shapes=[pltpu.VMEM((2 * M_HEADS, M_DK, 2 * M_DV), f32), pltpu.VMEM((2 * M_HEADS, 8, LANE), f32)],
        compiler_params=_cparams(("arbitrary",)), name="mlstm",
    )(gb, gbt, mq, mkt, mv, mg, mgt, mq, mkt, mv, mg, mgt)


def _rms(x, axis=-1):
    return x * lax.rsqrt(jnp.mean(x * x, axis=axis, keepdims=True) + EPS)


def _merge_kernel(x_ref, hf_ref, hb_ref, mo_ref, ya_ref, yd_ref, gl_ref, mng_ref, wb_ref, wout_ref,
                  mod_ref, rw_ref, rb_ref, xnew_ref, h2_ref, logit_ref):
    hs = hf_ref[...].astype(f32) + hb_ref[...].astype(f32)
    parts = [_rms(hs[:, hh * M_DV:(hh + 1) * M_DV]) for hh in range(M_HEADS)]
    ym = jnp.concatenate(parts, axis=1) * mng_ref[...] * jax.nn.sigmoid(mo_ref[...].astype(f32))
    yb0 = jnp.dot(ym.astype(bf16), wb_ref[0], preferred_element_type=f32)
    tm = x_ref.shape[0]
    tn_dims = (((0,), (0,)), ((), ()))
    yb1 = lax.dot_general(ya_ref[...].reshape(BRANCH_W, tm), wb_ref[1], tn_dims, preferred_element_type=f32)
    yb2 = lax.dot_general(yd_ref[...].reshape(BRANCH_W, tm), wb_ref[2], tn_dims, preferred_element_type=f32)
    d = D_MODEL
    z = (jax.nn.sigmoid(gl_ref[:, 0:d].astype(f32)) * yb0
         + jax.nn.sigmoid(gl_ref[:, d:2 * d].astype(f32)) * yb1
         + jax.nn.sigmoid(gl_ref[:, 2 * d:3 * d].astype(f32)) * yb2)
    y = jnp.dot(z.astype(bf16), wout_ref[...], preferred_element_type=f32)
    xnew = x_ref[...] + mod_ref[1:2, :] * (_rms(y) * mod_ref[0:1, :])
    h2 = _rms(xnew) * mod_ref[2:3, :] + mod_ref[3:4, :]
    xnew_ref[...] = xnew
    h2_ref[...] = h2.astype(bf16)
    h2_hi = h2.astype(bf16)
    h2_lo = (h2 - h2_hi.astype(f32)).astype(bf16)
    rw = rw_ref[...]
    rw_hi = rw.astype(bf16)
    rw_lo = (rw - rw_hi.astype(f32)).astype(bf16)
    logit_ref[...] = (jnp.dot(h2_hi, rw_hi, preferred_element_type=f32)
                      + jnp.dot(h2_lo, rw_hi, preferred_element_type=f32)
                      + jnp.dot(h2_hi, rw_lo, preferred_element_type=f32) + rb_ref[...])


def _merge(n_rows, n_lat_blocks, x_all, hf, hb, mo, ya, yd, gl, mng, wb, wout, mod, rw, rb):
    tm = ROW_BLOCK
    nb = n_rows // tm
    row = lambda i: (i, 0)
    grp = lambda i: (jnp.where(i >= n_lat_blocks, 1, 0), 0, 0)
    in_specs = [pl.BlockSpec((tm, D_MODEL), row), pl.BlockSpec((tm, 512), row), pl.BlockSpec((tm, 512), row),
                pl.BlockSpec((tm, 512), row),
                pl.BlockSpec((A_HEADS, A_V, tm), lambda i: (0, 0, i)),
                pl.BlockSpec((DF_HEADS, DF_DV, tm), lambda i: (0, 0, i)),
                pl.BlockSpec((tm, N_BRANCH * D_MODEL), row), _const_spec(mng.shape), _const_spec(wb.shape),
                _const_spec(wout.shape), pl.BlockSpec((None, 4, D_MODEL), grp),
                _const_spec(rw.shape), _const_spec(rb.shape)]
    return pl.pallas_call(
        _merge_kernel, grid=(nb,), in_specs=in_specs,
        out_specs=[pl.BlockSpec((tm, D_MODEL), row), pl.BlockSpec((tm, D_MODEL), row), pl.BlockSpec((tm, LANE), row)],
        out_shape=[jax.ShapeDtypeStruct((n_rows, D_MODEL), f32),
                   jax.ShapeDtypeStruct((max(n_rows, H2_ROWS), D_MODEL), bf16),
                   jax.ShapeDtypeStruct((n_rows, LANE), f32)],
        compiler_params=_cparams(("parallel",)), name="merge",
    )(x_all, hf, hb, mo, ya, yd, gl, mng, wb, wout, mod, rw, rb)


def _moe_kernel(be_ref, nu_ref, xs_ref, wgu_ref, bgu_ref, wd_ref, bd_ref, o_ref, wgu_s, wd_s):
    i = pl.program_id(0)
    prev = be_ref[jnp.maximum(i - 1, 0)]
    used = i < nu_ref[0]

    @pl.when(jnp.logical_and(used, jnp.logical_or(i == 0, be_ref[i] != prev)))
    def _():
        wgu_s[...] = wgu_ref[...].astype(bf16)
        wd_s[...] = wd_ref[...].astype(bf16)

    @pl.when(used)
    def _():
        gu = jnp.dot(xs_ref[...], wgu_s[...], preferred_element_type=f32) + bgu_ref[...]
        gate = jnp.minimum(gu[:, :D_EXPERT], SWIGLU_LIMIT)
        up = jnp.clip(gu[:, D_EXPERT:], -SWIGLU_LIMIT, SWIGLU_LIMIT)
        act = (up + 1.0) * (gate * jax.nn.sigmoid(SWIGLU_ALPHA * gate))
        y = jnp.dot(act.astype(bf16), wd_s[...], preferred_element_type=f32) + bd_ref[...]
        o_ref[...] = y.astype(o_ref.dtype)

    @pl.when(jnp.logical_not(used))
    def _():
        o_ref[...] = jnp.zeros_like(o_ref)


def _moe_experts(blk_e, n_used, xs, layer, w_gu, b_gu, w_down, b_down):
    cap = xs.shape[0]
    tm = MOE_TM
    nblk = cap // tm
    grid_spec = pltpu.PrefetchScalarGridSpec(
        num_scalar_prefetch=2, grid=(nblk,),
        in_specs=[pl.BlockSpec((tm, D_MODEL), lambda i, be, nu: (i, 0)),
                  pl.BlockSpec((None, None, D_MODEL, 2 * D_EXPERT), lambda i, be, nu: (layer, be[i], 0, 0)),
                  pl.BlockSpec((None, None, 1, 2 * D_EXPERT), lambda i, be, nu: (layer, be[i], 0, 0)),
                  pl.BlockSpec((None, None, D_EXPERT, D_MODEL), lambda i, be, nu: (layer, be[i], 0, 0)),
                  pl.BlockSpec((None, None, 1, D_MODEL), lambda i, be, nu: (layer, be[i], 0, 0))],
        out_specs=pl.BlockSpec((tm, D_MODEL), lambda i, be, nu: (i, 0)),
        scratch_shapes=[pltpu.VMEM((D_MODEL, 2 * D_EXPERT), bf16), pltpu.VMEM((D_EXPERT, D_MODEL), bf16)])
    return pl.pallas_call(
        _moe_kernel, grid_spec=grid_spec,
        out_shape=jax.ShapeDtypeStruct((cap, D_MODEL), bf16),
        compiler_params=_cparams(("arbitrary",)), name="moe_experts",
    )(blk_e, n_used, xs, w_gu, b_gu.reshape(DEPTH, N_EXPERTS, 1, -1), w_down, b_down.reshape(DEPTH, N_EXPERTS, 1, -1))


def _moe(h2, logits, layer, w_gu, b_gu, w_down, b_down):
    n = logits.shape[0]
    tm = MOE_TM
    i32 = jnp.int32
    top_v, top_i = lax.top_k(logits[:, :N_EXPERTS], TOP_K)
    wts = jax.nn.softmax(top_v, axis=-1)
    nk = n * TOP_K
    flat_e = top_i.reshape(nk).astype(i32)
    iota = jnp.arange(nk, dtype=i32)
    se, order = lax.sort((flat_e, iota), num_keys=1, is_stable=True)
    e_ids = jnp.arange(N_EXPERTS, dtype=i32)
    grp_start = jnp.sum((se[:, None] < e_ids[None, :]).astype(i32), axis=0)
    counts = jnp.sum((se[:, None] == e_ids[None, :]).astype(i32), axis=0)
    padded = (counts + tm - 1) // tm * tm
    pad_end = jnp.cumsum(padded)
    pad_start = pad_end - padded
    off = pad_start - grp_start
    step = off - jnp.concatenate([jnp.zeros((1,), i32), off[:-1]])
    dest_sorted = iota + jnp.sum(jnp.where(iota[:, None] >= grp_start[None, :], step[None, :], 0), axis=1)
    _, pos_km = lax.sort(((order % TOP_K) * n + order // TOP_K, dest_sorted), num_keys=1)
    cap = -(-(nk + N_EXPERTS * (tm - 1)) // tm) * tm
    nblk = cap // tm
    bstart = jnp.arange(nblk, dtype=i32) * tm
    blk_e = jnp.minimum(jnp.sum((pad_end[None, :] <= bstart[:, None]).astype(i32), axis=1), N_EXPERTS - 1)
    n_used = (pad_end[-1] // tm).astype(i32).reshape(1)
    sel = blk_e[:, None] == e_ids[None, :]
    off_b = jnp.sum(jnp.where(sel, off[None, :], 0), axis=1)
    end_b = jnp.sum(jnp.where(sel, (pad_start + counts)[None, :], 0), axis=1)
    p2 = bstart[:, None] + jnp.arange(tm, dtype=i32)[None, :]
    src = jnp.clip(p2 - off_b[:, None], 0, nk - 1)
    slot_t = jnp.where(p2 < end_b[:, None], (order // TOP_K)[src], p2 % n).reshape(cap)
    xs = h2[slot_t]
    ys = _moe_experts(blk_e, n_used, xs, layer, w_gu, b_gu, w_down, b_down)
    return ys[pos_km].reshape(TOP_K, n, D_MODEL), wts


def _resid_kernel(x_ref, y_ref, w_ref, mod_ref, o_ref):
    w = w_ref[...]
    f = y_ref[0].astype(f32) * w[:, 0:1]
    for kk in range(1, TOP_K):
        f = f + y_ref[kk].astype(f32) * w[:, kk:kk + 1]
    o_ref[...] = x_ref[...] + mod_ref[1:2, :] * (_rms(f) * mod_ref[0:1, :])


def _resid(n_lat_blocks, xnew, picked, wts, mod):
    n = xnew.shape[0]
    tm = ROW_BLOCK
    row = lambda i: (i, 0)
    grp = lambda i: (jnp.where(i >= n_lat_blocks, 1, 0), 0, 0)
    return pl.pallas_call(
        _resid_kernel, grid=(n // tm,),
        in_specs=[pl.BlockSpec((tm, D_MODEL), row), pl.BlockSpec((TOP_K, tm, D_MODEL), lambda i: (0, i, 0)),
                  pl.BlockSpec((tm, TOP_K), row), pl.BlockSpec((None, 2, D_MODEL), grp)],
        out_specs=pl.BlockSpec((tm, D_MODEL), row),
        out_shape=jax.ShapeDtypeStruct((n, D_MODEL), f32),
        compiler_params=_cparams(("parallel",)), name="resid",
    )(xnew, picked, wts, mod)


def _rope_tables(n_lat, n_ctx, dim):
    rows = n_lat // GRID_W
    row = jnp.repeat(jnp.arange(rows, dtype=f32), GRID_W)
    col = jnp.tile(jnp.arange(GRID_W, dtype=f32), rows)
    quarter = dim // 4
    inv = ROPE_BASE ** (-jnp.arange(quarter, dtype=f32) / quarter)
    ang = jnp.concatenate([row[:, None] * inv, col[:, None] * inv], axis=-1)
    cos = jnp.concatenate([jnp.cos(ang), jnp.ones((n_ctx, dim // 2), f32)], axis=0).T
    sin = jnp.concatenate([jnp.sin(ang), jnp.zeros((n_ctx, dim // 2), f32)], axis=0).T
    return cos, sin


def kernel(x, c, ctx, c_ctx, ada_w, ada_b, norm_g, w_in, m_gate_b, m_norm_g, q_norm_g, w_uq, kv_norm_g, w_ukv,
           diff_lam, diff_norm_g, w_branch, w_out, router_w, router_b, w_gu, b_gu, w_down, b_down):
    n_lat = x.shape[1]
    n_ctx = ctx.shape[1]
    n = n_lat + n_ctx
    assert x.shape[0] == 1 and n_lat % ROW_BLOCK == 0 and n_ctx == MLSTM_T and n_lat % GRID_W == 0
    nlb = n_lat // ROW_BLOCK
    x_all = jnp.concatenate([x[0], ctx[0]], axis=0)

    cond = jnp.zeros((8, D_MODEL), f32).at[0].set(c[0]).at[1].set(c_ctx)
    mods = _adaln(cond, ada_w, ada_b)[:, :2].reshape(DEPTH, 2, 6, D_MODEL)

    ca, sa = _rope_tables(n_lat, n_ctx, A_ROPE)
    cd, sd = _rope_tables(n_lat, n_ctx, DF_DK)

    for l in range(DEPTH):
        need_ctx = l < DEPTH - 1
        lam_init = 0.8 - 0.6 * math.exp(-0.3 * l)
        sh1, sc1, ga1, sh2, sc2, ga2 = [mods[l, :, j] for j in range(6)]
        ab1 = jnp.stack([norm_g[l, 0][None] * (1.0 + sc1), sh1], axis=1)
        w_tok, w_tr, wt_uq, wt_ukv = _pack_inproj_weights(w_in[l], w_uq[l], w_ukv[l])
        (mq, mv, mo, mg, gl, mkt, mgt, qta, k_a, vta, qtd, k_d, vtd) = _inproj(
            x_all, ab1, nlb, w_tok, w_tr, wt_uq, wt_ukv,
            q_norm_g[l].reshape(Q_LORA, 1), kv_norm_g[l].reshape(KV_LORA, 1), ca, sa, cd, sd)

        hf, hb = _mlstm(mq, mkt, mv, mg, mgt, m_gate_b[l])

        qtd4 = qtd.reshape(DF_HEADS, 2 * DF_DK, n)
        dg = diff_norm_g[l].reshape(DF_HEADS, DF_DV, 1)
        n_rows = n if need_ctx else n_lat
        attn_a = functools.partial(_attention, n_maps=1, dk=A_DKP, dv=A_V, n_out=n_rows)
        attn_d = functools.partial(_attention, n_maps=2, dk=DF_DK, dv=DF_DV, n_out=n_rows,
                                   lam=diff_lam[l], g=dg, lam_init=lam_init)
        ya = attn_a(qta, k_a, vta, tq=ATTN_TQ_MLA, lq=n_lat)
        yd = attn_d(qtd4, k_d, vtd, tq=ATTN_TQ_DIFF, lq=n_lat)
        if need_ctx:
            ctx_tile = lambda v: v[:, -1:, :, v.shape[3] - n_ctx:]
            ya = attn_a(qta, k_a[:, n_lat:], ctx_tile(vta), tq=n_ctx, lq=n_ctx, q_start=n_lat, out=ya)
            yd = attn_d(qtd4, k_d[:, n_lat:], ctx_tile(vtd), tq=n_ctx, lq=n_ctx, q_start=n_lat, out=yd)

        mod_m = jnp.stack([jnp.broadcast_to(norm_g[l, 1][None], (2, D_MODEL)), ga1,
                           norm_g[l, 2][None] * (1.0 + sc2), sh2], axis=1)
        rw = jnp.pad(router_w[l], ((0, 0), (0, LANE - N_EXPERTS)))
        rb = jnp.pad(router_b[l].reshape(1, N_EXPERTS), ((0, 0), (0, LANE - N_EXPERTS)))
        xnew, h2, logits = _merge(n_rows, nlb, x_all, hf, hb, mo, ya, yd, gl,
                                  m_norm_g[l].reshape(1, -1), w_branch[l].astype(bf16), w_out[l].astype(bf16),
                                  mod_m, rw, rb)
        picked, wts = _moe(h2, logits, l, w_gu, b_gu, w_down, b_down)
        mod_r = jnp.stack([jnp.broadcast_to(norm_g[l, 3][None], (2, D_MODEL)), ga2], axis=1)
        x_all = _resid(nlb, xnew, picked, wts, mod_r)
    return x_all[:n_lat][None]
```

```python
import functools
import math

import jax
import jax.numpy as jnp
from jax import lax
from jax.experimental import pallas as pl
from jax.experimental.pallas import tpu as pltpu

f32 = jnp.float32
bf16 = jnp.bfloat16

D_MODEL = 1024
DEPTH = 2
GRID_W = 64
EPS = 1e-6
ROPE_BASE = 10000.0
M_HEADS = 4
M_DK = 128
M_DV = 128
GATE_SOFTCAP = 15.0
A_HEADS = 8
A_NOPE = 64
A_ROPE = 32
A_V = 64
Q_LORA = 256
KV_LORA = 128
DF_HEADS = 4
DF_DK = 64
DF_DV = 128
N_BRANCH = 3
BRANCH_W = 512
N_EXPERTS = 32
TOP_K = 4
D_EXPERT = 1024
SWIGLU_LIMIT = 7.0
SWIGLU_ALPHA = 1.702

IN_SIZES = (M_HEADS * M_DK, M_HEADS * M_DK, M_HEADS * M_DV, M_HEADS * M_DV, 4 * M_HEADS,
            Q_LORA, KV_LORA, A_ROPE,
            2 * DF_HEADS * DF_DK, 2 * DF_HEADS * DF_DK, DF_HEADS * DF_DV,
            N_BRANCH * D_MODEL)

LOG2E = 1.4426950408889634
LANE = 128
A_DKP = 128
A_DVP = A_V + 16
DF_DVP = DF_DV + 16
VMEM_LIMIT = 56 * 1024 * 1024

ROW_BLOCK = 256
MXU_DEPTH = 256
ATTN_TQ_MLA = 2048
ATTN_TQ_DIFF = 1024
ATTN_QSUB = 256
ATTN_TK = 1280
MLSTM_T = 256
MOE_TM = 512
H2_ROWS = 32768

NT_DIMS = (((1,), (1,)), ((), ()))


def _cparams(sem):
    return pltpu.CompilerParams(dimension_semantics=sem, vmem_limit_bytes=VMEM_LIMIT)


def _const_spec(shape):
    nd = len(shape)
    return pl.BlockSpec(shape, lambda *_: (0,) * nd)


def _adaln_kernel(cond_ref, w_ref, b_ref, o_ref):
    cnd = cond_ref[...]
    a = cnd * jax.nn.sigmoid(cnd)
    o_ref[...] = jnp.dot(a, w_ref[...], preferred_element_type=f32,
                         precision=lax.Precision.HIGHEST) + b_ref[...]


def _adaln(cond, ada_w, ada_b):
    tn = 1536
    nj = (6 * D_MODEL) // tn
    return pl.pallas_call(
        _adaln_kernel,
        grid=(DEPTH, nj),
        in_specs=[pl.BlockSpec((8, D_MODEL), lambda l, j: (0, 0)),
                  pl.BlockSpec((None, D_MODEL, tn), lambda l, j: (l, 0, j)),
                  pl.BlockSpec((None, 1, tn), lambda l, j: (l, 0, j))],
        out_specs=pl.BlockSpec((None, 8, tn), lambda l, j: (l, 0, j)),
        out_shape=jax.ShapeDtypeStruct((DEPTH, 8, 6 * D_MODEL), f32),
        compiler_params=_cparams(("parallel", "parallel")),
        name="adaln",
    )(cond, ada_w, ada_b.reshape(DEPTH, 1, 6 * D_MODEL))


TOK_W = 3 * 512 + LANE + N_BRANCH * D_MODEL
TR_W = 512 + 16 + Q_LORA + KV_LORA + A_ROPE + 3 * 512


def _ones_row_block(tm):
    r = lax.broadcasted_iota(jnp.int32, (16, tm), 0)
    return jnp.where(r == 0, 1.0, 0.0).astype(bf16)


def _inproj_kernel(x_ref, ab_ref, wtok_ref, wtr_ref, wuq_ref, wukv_ref, gq_ref, gkv_ref,
                   ca_ref, sa_ref, cd_ref, sd_ref,
                   mq_ref, mv_ref, mo_ref, mg_ref, gl_ref, mkt_ref, mgt_ref,
                   qta_ref, kta_ref, vta_ref, qtd_ref, ktd_ref, vtd_ref):
    tm = x_ref.shape[0]
    x = x_ref[...]
    ms = jnp.mean(x * x, axis=-1, keepdims=True)
    h = (x * lax.rsqrt(ms + EPS)) * ab_ref[0:1, :] + ab_ref[1:2, :]
    hb = h.astype(bf16)

    tok = jnp.dot(hb, wtok_ref[...], preferred_element_type=f32)
    mq_ref[...] = (tok[:, 0:512] * (M_DK ** -0.5)).astype(bf16)
    mv_ref[...] = tok[:, 512:1024].astype(bf16)
    mo_ref[...] = tok[:, 1024:1536].astype(bf16)
    mg_ref[...] = tok[:, 1536:1664]
    gl_ref[...] = tok[:, 1664:TOK_W].astype(bf16)

    tr = lax.dot_general(wtr_ref[...], hb, NT_DIMS, preferred_element_type=f32)
    mkt_ref[...] = tr[0:512].astype(bf16)
    mgt_ref[...] = tr[512:528]
    cq = tr[528:784]
    ckv = tr[784:912]
    kpe = tr[912:944]
    dq = tr[944:1456]
    dk = tr[1456:1968]
    dv = tr[1968:2480]

    ca = ca_ref[...]
    sa = sa_ref[...]
    cd = cd_ref[...]
    sd = sd_ref[...]
    ones_blk = _ones_row_block(tm)
    zeros32 = jnp.zeros((32, tm), bf16)

    cqn = (cq * lax.rsqrt(jnp.mean(cq * cq, axis=0, keepdims=True) + EPS) * gq_ref[...]).astype(bf16)
    qscale = ((A_NOPE + A_ROPE) ** -0.5) * LOG2E
    qt = jnp.dot(wuq_ref[...], cqn, preferred_element_type=f32) * qscale
    for hh in range(A_HEADS):
        a = qt[512 + 16 * hh:528 + 16 * hh]
        b = qt[640 + 16 * hh:656 + 16 * hh]
        qta_ref[hh, 0:64, :] = qt[64 * hh:64 * hh + 64].astype(bf16)
        qta_ref[hh, 64:80, :] = (a * ca - b * sa).astype(bf16)
        qta_ref[hh, 80:96, :] = (a * sa + b * ca).astype(bf16)
        qta_ref[hh, 96:128, :] = zeros32

    ckvn = (ckv * lax.rsqrt(jnp.mean(ckv * ckv, axis=0, keepdims=True) + EPS) * gkv_ref[...]).astype(bf16)
    kvt = jnp.dot(wukv_ref[...], ckvn, preferred_element_type=f32)
    ka = kpe[0:16]
    kb = kpe[16:32]
    kpe_rot = jnp.concatenate([ka * ca - kb * sa, ka * sa + kb * ca, jnp.zeros((32, tm), f32)], axis=0)
    for hh in range(A_HEADS):
        kt_h = jnp.concatenate([kvt[64 * hh:64 * hh + 64], kpe_rot], axis=0)
        kta_ref[hh] = kt_h.T.astype(bf16)
        vta_ref[hh, 0:64, :] = kvt[512 + 64 * hh:576 + 64 * hh].astype(bf16)
        vta_ref[hh, 64:80, :] = ones_blk

    dscale = (DF_DK ** -0.5) * LOG2E
    for hh in range(2 * DF_HEADS):
        a = dq[32 * hh:32 * hh + 32]
        b = dq[256 + 32 * hh:288 + 32 * hh]
        qtd_ref[hh, 0:32, :] = ((a * cd - b * sd) * dscale).astype(bf16)
        qtd_ref[hh, 32:64, :] = ((a * sd + b * cd) * dscale).astype(bf16)
        a = dk[32 * hh:32 * hh + 32]
        b = dk[256 + 32 * hh:288 + 32 * hh]
        ktd_ref[hh] = jnp.concatenate([a * cd - b * sd, a * sd + b * cd], axis=0).T.astype(bf16)
    for hh in range(DF_HEADS):
        vtd_ref[hh, 0:128, :] = dv[128 * hh:128 * hh + 128].astype(bf16)
        vtd_ref[hh, 128:144, :] = ones_blk


def _pack_inproj_weights(w, w_uq, w_ukv):
    o = [0]
    for s in IN_SIZES:
        o.append(o[-1] + s)
    mq, mk, mv, mo, mg, cq, ckv, kpe, dq, dk, dv, gl = [w[:, o[i]:o[i + 1]] for i in range(12)]
    mg_pad = jnp.pad(mg, ((0, 0), (0, LANE - mg.shape[1])))
    w_tok = jnp.concatenate([mq, mv, mo, mg_pad, gl], axis=1).astype(bf16)

    def split_halves(t, n_heads, d):
        t3 = t.reshape(t.shape[0], n_heads, d)
        return jnp.concatenate([t3[:, :, :d // 2].reshape(t.shape[0], -1),
                                t3[:, :, d // 2:].reshape(t.shape[0], -1)], axis=1)

    w_tr = jnp.concatenate([mk, mg, cq, ckv, kpe, split_halves(dq, 2 * DF_HEADS, DF_DK),
                            split_halves(dk, 2 * DF_HEADS, DF_DK), dv], axis=1).T.astype(bf16)
    uq = w_uq.reshape(Q_LORA, A_HEADS, A_NOPE + A_ROPE)
    half = A_ROPE // 2
    wt_uq = jnp.concatenate([uq[:, :, :A_NOPE].reshape(Q_LORA, -1),
                             uq[:, :, A_NOPE:A_NOPE + half].reshape(Q_LORA, -1),
                             uq[:, :, A_NOPE + half:].reshape(Q_LORA, -1)], axis=1).T.astype(bf16)
    ukv = w_ukv.reshape(KV_LORA, A_HEADS, A_NOPE + A_V)
    wt_ukv = jnp.concatenate([ukv[:, :, :A_NOPE].reshape(KV_LORA, -1),
                              ukv[:, :, A_NOPE:].reshape(KV_LORA, -1)], axis=1).T.astype(bf16)
    return w_tok, w_tr, wt_uq, wt_ukv


def _inproj(x_all, ab, n_lat_blocks, w_tok, w_tr, wt_uq, wt_ukv, gq, gkv, ca, sa, cd, sd):
    n = x_all.shape[0]
    tm = ROW_BLOCK
    tk = _key_tile(n)
    per = tk // tm
    vtile = lambda i: (0, i // per, 0, i % per)
    nb = n // tm
    row = lambda i: (i, 0)
    colb = lambda i: (0, i)
    col3 = lambda i: (0, 0, i)
    grp = lambda i: (jnp.where(i >= n_lat_blocks, 1, 0), 0, 0)
    out_shapes = [
        jax.ShapeDtypeStruct((n, 512), bf16), jax.ShapeDtypeStruct((n, 512), bf16),
        jax.ShapeDtypeStruct((n, 512), bf16), jax.ShapeDtypeStruct((n, LANE), f32),
        jax.ShapeDtypeStruct((n, N_BRANCH * D_MODEL), bf16),
        jax.ShapeDtypeStruct((512, n), bf16), jax.ShapeDtypeStruct((16, n), f32),
        jax.ShapeDtypeStruct((A_HEADS, A_DKP, n), bf16), jax.ShapeDtypeStruct((A_HEADS, n, A_DKP), bf16),
        jax.ShapeDtypeStruct((A_HEADS, n // tk, A_DVP, tk), bf16),
        jax.ShapeDtypeStruct((2 * DF_HEADS, DF_DK, n), bf16), jax.ShapeDtypeStruct((2 * DF_HEADS, n, DF_DK), bf16),
        jax.ShapeDtypeStruct((DF_HEADS, n // tk, DF_DVP, tk), bf16),
    ]
    out_specs = [
        pl.BlockSpec((tm, 512), row), pl.BlockSpec((tm, 512), row), pl.BlockSpec((tm, 512), row),
        pl.BlockSpec((tm, LANE), row), pl.BlockSpec((tm, N_BRANCH * D_MODEL), row),
        pl.BlockSpec((512, tm), colb), pl.BlockSpec((16, tm), colb),
        pl.BlockSpec((A_HEADS, A_DKP, tm), col3), pl.BlockSpec((A_HEADS, tm, A_DKP), lambda i: (0, i, 0)),
        pl.BlockSpec((A_HEADS, None, A_DVP, tm), vtile),
        pl.BlockSpec((2 * DF_HEADS, DF_DK, tm), col3), pl.BlockSpec((2 * DF_HEADS, tm, DF_DK), lambda i: (0, i, 0)),
        pl.BlockSpec((DF_HEADS, None, DF_DVP, tm), vtile),
    ]
    in_specs = [
        pl.BlockSpec((tm, D_MODEL), row),
        pl.BlockSpec((None, 2, D_MODEL), grp),
        _const_spec(w_tok.shape), _const_spec(w_tr.shape), _const_spec(wt_uq.shape), _const_spec(wt_ukv.shape),
        _const_spec(gq.shape), _const_spec(gkv.shape),
        pl.BlockSpec((A_ROPE // 2, tm), colb), pl.BlockSpec((A_ROPE // 2, tm), colb),
        pl.BlockSpec((DF_DK // 2, tm), colb), pl.BlockSpec((DF_DK // 2, tm), colb),
    ]
    return pl.pallas_call(
        _inproj_kernel, grid=(nb,), in_specs=in_specs, out_specs=out_specs, out_shape=out_shapes,
        compiler_params=_cparams(("parallel",)), name="inproj",
    )(x_all, ab, w_tok, w_tr, wt_uq, wt_ukv, gq, gkv, ca, sa, cd, sd)


def _attn_kernel(*refs, n_maps, dk, dv, tk, n_kt, qsub, lam_init, n_in):
    o_ref = refs[n_in]
    if n_maps == 2:
        lam_ref, g_ref, qt_ref, k_ref, vt_ref = refs[:5]
    else:
        qt_ref, k_ref, vt_ref = refs[:3]
    tq = qt_ref.shape[-1]
    dvp = vt_ref.shape[-2]
    chains = [(m, c) for m in range(n_maps) for c in range(tq // qsub)]

    def qk(j, m, c):
        qt = qt_ref[m * dk:(m + 1) * dk, c * qsub:(c + 1) * qsub]
        return jnp.dot(k_ref[m, j * tk:(j + 1) * tk, :], qt, preferred_element_type=f32)

    state = [(jnp.full((1, qsub), -jnp.inf, f32), jnp.zeros((dvp, qsub), f32)) for _ in chains]
    s_cur = [qk(0, m, c) for (m, c) in chains]
    for j in range(n_kt):
        vtj = vt_ref[j]
        for i, (m, c) in enumerate(chains):
            s_nxt = qk(j + 1, m, c) if j + 1 < n_kt else None
            m_run, acc = state[i]
            m_new = jnp.maximum(m_run, jnp.max(s_cur[i], axis=0, keepdims=True))
            alpha = jnp.exp2(m_run - m_new)
            p = jnp.exp2(s_cur[i] - m_new).astype(bf16)
            pv = jnp.dot(vtj, p, preferred_element_type=f32)
            state[i] = (m_new, alpha * acc + pv)
            s_cur[i] = s_nxt
    outs = []
    for m in range(n_maps):
        accs = [state[i][1] for i, (mm, _) in enumerate(chains) if mm == m]
        acc = accs[0] if len(accs) == 1 else jnp.concatenate(accs, axis=1)
        outs.append(acc[0:dv] / acc[dv:dv + 1])
    if n_maps == 1:
        o_ref[...] = outs[0].astype(o_ref.dtype)
    else:
        lf = lam_ref[...]
        lam = (jnp.exp(jnp.sum(lf[0:1] * lf[1:2], axis=1, keepdims=True))
               - jnp.exp(jnp.sum(lf[2:3] * lf[3:4], axis=1, keepdims=True)) + lam_init)
        o = outs[0] - lam * outs[1]
        o = o * lax.rsqrt(jnp.mean(o * o, axis=0, keepdims=True) + EPS) * g_ref[...]
        o_ref[...] = (o * (1.0 - lam_init)).astype(o_ref.dtype)


def _key_tile(lk):
    return max(t for t in range(MXU_DEPTH, min(ATTN_TK, lk) + 1, MXU_DEPTH) if lk % t == 0)


def _attention(qt, k, vt4, *, n_maps, dk, dv, tq, lq, n_out, q_start=0, out=None, lam=None, g=None, lam_init=0.0):
    n_heads = qt.shape[0]
    lk = k.shape[1]
    _, n_kt, dvp, tk = vt4.shape
    assert n_kt * tk == lk
    tq = min(tq, lq)
    qsub = min(ATTN_QSUB, tq)
    qb0 = q_start // tq
    in_specs = [
        pl.BlockSpec((None, n_maps * dk, tq), lambda h, i: (h, 0, qb0 + i)),
        pl.BlockSpec((n_maps, lk, dk), lambda h, i: (h, 0, 0)),
        pl.BlockSpec((None, n_kt, dvp, tk), lambda h, i: (h, 0, 0, 0)),
    ]
    args = [qt, k, vt4]
    if n_maps == 2:
        in_specs = [_const_spec(lam.shape), pl.BlockSpec((None, dv, 1), lambda h, i: (h, 0, 0))] + in_specs
        args = [lam, g] + args
    aliases = {}
    if out is not None:
        aliases = {len(args): 0}
        in_specs.append(pl.BlockSpec(memory_space=pl.ANY))
        args.append(out)
    kern = functools.partial(_attn_kernel, n_maps=n_maps, dk=dk, dv=dv, tk=tk, n_kt=n_kt, qsub=qsub,
                             lam_init=lam_init, n_in=len(args))
    return pl.pallas_call(
        kern, grid=(n_heads, lq // tq), in_specs=in_specs,
        out_specs=pl.BlockSpec((None, dv, tq), lambda h, i: (h, 0, qb0 + i)),
        out_shape=jax.ShapeDtypeStruct((n_heads, dv, n_out), bf16),
        input_output_aliases=aliases,
        compiler_params=_cparams(("parallel", "parallel")),
        name="attn_diff" if n_maps == 2 else "attn_mla",
    )(*args)


def _split3(x):
    hi = x.astype(bf16)
    r = x - hi.astype(f32)
    mid = r.astype(bf16)
    lo = (r - mid.astype(f32)).astype(bf16)
    return hi, mid, lo


def _log_sigmoid(x):
    return jnp.minimum(x, 0.0) - jnp.log(1.0 + jnp.exp(-jnp.abs(x)))


def _softcap(x):
    return GATE_SOFTCAP * jnp.tanh(x * (1.0 / GATE_SOFTCAP))


def _mlstm_kernel(gb_ref, gbt_ref,
                  qf_ref, ktf_ref, vf_ref, gf_ref, gtf_ref,
                  qb_ref, ktb_ref, vb_ref, gb2_ref, gtb_ref,
                  hf_ref, hb_ref, c_ref, m_ref):
    t = qf_ref.shape[0]

    @pl.when(pl.program_id(0) == 0)
    def _():
        c_ref[...] = jnp.zeros_like(c_ref)
        m_ref[...] = jnp.zeros_like(m_ref)

    row = lax.broadcasted_iota(jnp.int32, (t, t), 0)
    col = lax.broadcasted_iota(jnp.int32, (t, t), 1)
    lane = lax.broadcasted_iota(jnp.int32, (t, LANE), 1)
    ones_col = jnp.where(lane == 0, 1.0, 0.0).astype(bf16)
    dirs = ((qf_ref, ktf_ref, vf_ref, gf_ref, gtf_ref, hf_ref),
            (qb_ref, ktb_ref, vb_ref, gb2_ref, gtb_ref, hb_ref))
    for d, (q_ref, kt_ref, v_ref, g_ref, gt_ref, o_ref) in enumerate(dirs):
        mask = (col <= row) if d == 0 else (col >= row)
        maskb = mask.astype(bf16)
        last = t - 1 if d == 0 else 0
        g_tok = _softcap(g_ref[...] + gb_ref[...])
        g_t = _softcap(gt_ref[...] + gbt_ref[...])
        b_tok = sum(jnp.dot(maskb, part, preferred_element_type=f32) for part in _split3(_log_sigmoid(g_tok)))
        b_t = sum(lax.dot_general(part, maskb, NT_DIMS, preferred_element_type=f32)
                  for part in _split3(_log_sigmoid(g_t)))
        for hh in range(M_HEADS):
            ri = 4 * (2 * d) + hh
            rf = 4 * (2 * d + 1) + hh
            idx = d * M_HEADS + hh
            b_col = b_tok[:, rf:rf + 1]
            b_row = b_t[rf:rf + 1, :]
            a_row = g_t[ri:ri + 1, :] - b_row
            m_prev = m_ref[idx, 0:1, 0:1]
            ld = jnp.where(mask, b_col + a_row, -jnp.inf)
            inter = b_col + m_prev
            mj = jnp.maximum(inter, jnp.max(ld, axis=1, keepdims=True))
            dmat = jnp.exp(ld - mj)
            q = q_ref[:, hh * M_DK:(hh + 1) * M_DK]
            kt = kt_ref[hh * M_DK:(hh + 1) * M_DK, :]
            vp = jnp.concatenate([v_ref[:, hh * M_DV:(hh + 1) * M_DV], ones_col], axis=1)
            sc = jnp.dot(q, kt, preferred_element_type=f32) * dmat
            cp = c_ref[idx]
            num = (jnp.dot(sc.astype(bf16), vp, preferred_element_type=f32)
                   + jnp.exp(inter - mj) * jnp.dot(q, cp.astype(bf16), preferred_element_type=f32))
            den = num[:, M_DV:M_DV + 1]
            hj = num[:, 0:M_DV] / jnp.maximum(jnp.abs(den), jnp.exp(-mj))
            o_ref[:, hh * M_DV:(hh + 1) * M_DV] = hj.astype(o_ref.dtype)
            b_tot = b_t[rf:rf + 1, last:last + 1]
            lw = b_tot + a_row
            m_new = jnp.maximum(b_tot + m_prev, jnp.max(lw, axis=1, keepdims=True))
            kw = (kt.astype(f32) * jnp.exp(lw - m_new)).astype(bf16)
            c_ref[idx] = jnp.exp(b_tot + m_prev - m_new) * cp + jnp.dot(kw, vp, preferred_element_type=f32)
            m_ref[idx] = jnp.broadcast_to(m_new, (8, LANE))


def _mlstm(mq, mkt, mv, mg, mgt, gate_b):
    n = mq.shape[0]
    t = MLSTM_T
    nb = n // t
    gb = jnp.pad(gate_b.reshape(1, 16), ((0, 0), (0, LANE - 16)))
    gbt = gate_b.reshape(16, 1)
    fwd = lambda s: jnp.where(s == 0, nb - 1, s - 1)
    bwd = lambda s: nb - 1 - s
    specs = []
    for im in (fwd, bwd):
        specs += [pl.BlockSpec((t, 512), lambda s, im=im: (im(s), 0)),
                  pl.BlockSpec((512, t), lambda s, im=im: (0, im(s))),
                  pl.BlockSpec((t, 512), lambda s, im=im: (im(s), 0)),
                  pl.BlockSpec((t, LANE), lambda s, im=im: (im(s), 0)),
                  pl.BlockSpec((16, t), lambda s, im=im: (0, im(s)))]
    return pl.pallas_call(
        _mlstm_kernel, grid=(nb,),
        in_specs=[_const_spec(gb.shape), _const_spec(gbt.shape)] + specs,
        out_specs=[pl.BlockSpec((t, 512), lambda s: (fwd(s), 0)), pl.BlockSpec((t, 512), lambda s: (bwd(s), 0))],
        out_shape=[jax.ShapeDtypeStruct((n, 512), bf16), jax.ShapeDtypeStruct((n, 512), bf16)],
        scratch_shapes=[pltpu.VMEM((2 * M_HEADS, M_DK, 2 * M_DV), f32), pltpu.VMEM((2 * M_HEADS, 8, LANE), f32)],
        compiler_params=_cparams(("arbitrary",)), name="mlstm",
    )(gb, gbt, mq, mkt, mv, mg, mgt, mq, mkt, mv, mg, mgt)


def _rms(x, axis=-1):
    return x * lax.rsqrt(jnp.mean(x * x, axis=axis, keepdims=True) + EPS)


def _merge_kernel(x_ref, hf_ref, hb_ref, mo_ref, ya_ref, yd_ref, gl_ref, mng_ref, wb_ref, wout_ref,
                  mod_ref, rw_ref, rb_ref, xnew_ref, h2_ref, logit_ref):
    hs = hf_ref[...].astype(f32) + hb_ref[...].astype(f32)
    parts = [_rms(hs[:, hh * M_DV:(hh + 1) * M_DV]) for hh in range(M_HEADS)]
    ym = jnp.concatenate(parts, axis=1) * mng_ref[...] * jax.nn.sigmoid(mo_ref[...].astype(f32))
    yb0 = jnp.dot(ym.astype(bf16), wb_ref[0], preferred_element_type=f32)
    tm = x_ref.shape[0]
    tn_dims = (((0,), (0,)), ((), ()))
    yb1 = lax.dot_general(ya_ref[...].reshape(BRANCH_W, tm), wb_ref[1], tn_dims, preferred_element_type=f32)
    yb2 = lax.dot_general(yd_ref[...].reshape(BRANCH_W, tm), wb_ref[2], tn_dims, preferred_element_type=f32)
    d = D_MODEL
    z = (jax.nn.sigmoid(gl_ref[:, 0:d].astype(f32)) * yb0
         + jax.nn.sigmoid(gl_ref[:, d:2 * d].astype(f32)) * yb1
         + jax.nn.sigmoid(gl_ref[:, 2 * d:3 * d].astype(f32)) * yb2)
    y = jnp.dot(z.astype(bf16), wout_ref[...], preferred_element_type=f32)
    xnew = x_ref[...] + mod_ref[1:2, :] * (_rms(y) * mod_ref[0:1, :])
    h2 = _rms(xnew) * mod_ref[2:3, :] + mod_ref[3:4, :]
    xnew_ref[...] = xnew
    h2_ref[...] = h2.astype(bf16)
    h2_hi = h2.astype(bf16)
    h2_lo = (h2 - h2_hi.astype(f32)).astype(bf16)
    rw = rw_ref[...]
    rw_hi = rw.astype(bf16)
    rw_lo = (rw - rw_hi.astype(f32)).astype(bf16)
    logit_ref[...] = (jnp.dot(h2_hi, rw_hi, preferred_element_type=f32)
                      + jnp.dot(h2_lo, rw_hi, preferred_element_type=f32)
                      + jnp.dot(h2_hi, rw_lo, preferred_element_type=f32) + rb_ref[...])


def _merge(n_rows, n_lat_blocks, x_all, hf, hb, mo, ya, yd, gl, mng, wb, wout, mod, rw, rb):
    tm = ROW_BLOCK
    nb = n_rows // tm
    row = lambda i: (i, 0)
    grp = lambda i: (jnp.where(i >= n_lat_blocks, 1, 0), 0, 0)
    in_specs = [pl.BlockSpec((tm, D_MODEL), row), pl.BlockSpec((tm, 512), row), pl.BlockSpec((tm, 512), row),
                pl.BlockSpec((tm, 512), row),
                pl.BlockSpec((A_HEADS, A_V, tm), lambda i: (0, 0, i)),
                pl.BlockSpec((DF_HEADS, DF_DV, tm), lambda i: (0, 0, i)),
                pl.BlockSpec((tm, N_BRANCH * D_MODEL), row), _const_spec(mng.shape), _const_spec(wb.shape),
                _const_spec(wout.shape), pl.BlockSpec((None, 4, D_MODEL), grp),
                _const_spec(rw.shape), _const_spec(rb.shape)]
    return pl.pallas_call(
        _merge_kernel, grid=(nb,), in_specs=in_specs,
        out_specs=[pl.BlockSpec((tm, D_MODEL), row), pl.BlockSpec((tm, D_MODEL), row), pl.BlockSpec((tm, LANE), row)],
        out_shape=[jax.ShapeDtypeStruct((n_rows, D_MODEL), f32),
                   jax.ShapeDtypeStruct((max(n_rows, H2_ROWS), D_MODEL), bf16),
                   jax.ShapeDtypeStruct((n_rows, LANE), f32)],
        compiler_params=_cparams(("parallel",)), name="merge",
    )(x_all, hf, hb, mo, ya, yd, gl, mng, wb, wout, mod, rw, rb)


def _moe_kernel(be_ref, nu_ref, xs_ref, wgu_ref, bgu_ref, wd_ref, bd_ref, o_ref, wgu_s, wd_s):
    i = pl.program_id(0)
    prev = be_ref[jnp.maximum(i - 1, 0)]
    used = i < nu_ref[0]

    @pl.when(jnp.logical_and(used, jnp.logical_or(i == 0, be_ref[i] != prev)))
    def _():
        wgu_s[...] = wgu_ref[...].astype(bf16)
        wd_s[...] = wd_ref[...].astype(bf16)

    @pl.when(used)
    def _():
        gu = jnp.dot(xs_ref[...], wgu_s[...], preferred_element_type=f32) + bgu_ref[...]
        gate = jnp.minimum(gu[:, :D_EXPERT], SWIGLU_LIMIT)
        up = jnp.clip(gu[:, D_EXPERT:], -SWIGLU_LIMIT, SWIGLU_LIMIT)
        act = (up + 1.0) * (gate * jax.nn.sigmoid(SWIGLU_ALPHA * gate))
        y = jnp.dot(act.astype(bf16), wd_s[...], preferred_element_type=f32) + bd_ref[...]
        o_ref[...] = y.astype(o_ref.dtype)

    @pl.when(jnp.logical_not(used))
    def _():
        o_ref[...] = jnp.zeros_like(o_ref)


def _moe_experts(blk_e, n_used, xs, layer, w_gu, b_gu, w_down, b_down):
    cap = xs.shape[0]
    tm = MOE_TM
    nblk = cap // tm
    grid_spec = pltpu.PrefetchScalarGridSpec(
        num_scalar_prefetch=2, grid=(nblk,),
        in_specs=[pl.BlockSpec((tm, D_MODEL), lambda i, be, nu: (i, 0)),
                  pl.BlockSpec((None, None, D_MODEL, 2 * D_EXPERT), lambda i, be, nu: (layer, be[i], 0, 0)),
                  pl.BlockSpec((None, None, 1, 2 * D_EXPERT), lambda i, be, nu: (layer, be[i], 0, 0)),
                  pl.BlockSpec((None, None, D_EXPERT, D_MODEL), lambda i, be, nu: (layer, be[i], 0, 0)),
                  pl.BlockSpec((None, None, 1, D_MODEL), lambda i, be, nu: (layer, be[i], 0, 0))],
        out_specs=pl.BlockSpec((tm, D_MODEL), lambda i, be, nu: (i, 0)),
        scratch_shapes=[pltpu.VMEM((D_MODEL, 2 * D_EXPERT), bf16), pltpu.VMEM((D_EXPERT, D_MODEL), bf16)])
    return pl.pallas_call(
        _moe_kernel, grid_spec=grid_spec,
        out_shape=jax.ShapeDtypeStruct((cap, D_MODEL), bf16),
        compiler_params=_cparams(("arbitrary",)), name="moe_experts",
    )(blk_e, n_used, xs, w_gu, b_gu.reshape(DEPTH, N_EXPERTS, 1, -1), w_down, b_down.reshape(DEPTH, N_EXPERTS, 1, -1))


def _moe(h2, logits, layer, w_gu, b_gu, w_down, b_down):
    n = logits.shape[0]
    tm = MOE_TM
    i32 = jnp.int32
    top_v, top_i = lax.top_k(logits[:, :N_EXPERTS], TOP_K)
    wts = jax.nn.softmax(top_v, axis=-1)
    nk = n * TOP_K
    flat_e = top_i.reshape(nk).astype(i32)
    iota = jnp.arange(nk, dtype=i32)
    se, order = lax.sort((flat_e, iota), num_keys=1, is_stable=True)
    e_ids = jnp.arange(N_EXPERTS, dtype=i32)
    grp_start = jnp.sum((se[:, None] < e_ids[None, :]).astype(i32), axis=0)
    counts = jnp.sum((se[:, None] == e_ids[None, :]).astype(i32), axis=0)
    padded = (counts + tm - 1) // tm * tm
    pad_end = jnp.cumsum(padded)
    pad_start = pad_end - padded
    off = pad_start - grp_start
    step = off - jnp.concatenate([jnp.zeros((1,), i32), off[:-1]])
    dest_sorted = iota + jnp.sum(jnp.where(iota[:, None] >= grp_start[None, :], step[None, :], 0), axis=1)
    _, pos_km = lax.sort(((order % TOP_K) * n + order // TOP_K, dest_sorted), num_keys=1)
    cap = -(-(nk + N_EXPERTS * (tm - 1)) // tm) * tm
    nblk = cap // tm
    bstart = jnp.arange(nblk, dtype=i32) * tm
    blk_e = jnp.minimum(jnp.sum((pad_end[None, :] <= bstart[:, None]).astype(i32), axis=1), N_EXPERTS - 1)
    n_used = (pad_end[-1] // tm).astype(i32).reshape(1)
    sel = blk_e[:, None] == e_ids[None, :]
    off_b = jnp.sum(jnp.where(sel, off[None, :], 0), axis=1)
    end_b = jnp.sum(jnp.where(sel, (pad_start + counts)[None, :], 0), axis=1)
    p2 = bstart[:, None] + jnp.arange(tm, dtype=i32)[None, :]
    src = jnp.clip(p2 - off_b[:, None], 0, nk - 1)
    slot_t = jnp.where(p2 < end_b[:, None], (order // TOP_K)[src], p2 % n).reshape(cap)
    xs = h2[slot_t]
    ys = _moe_experts(blk_e, n_used, xs, layer, w_gu, b_gu, w_down, b_down)
    return ys[pos_km].reshape(TOP_K, n, D_MODEL), wts


def _resid_kernel(x_ref, y_ref, w_ref, mod_ref, o_ref):
    w = w_ref[...]
    f = y_ref[0].astype(f32) * w[:, 0:1]
    for kk in range(1, TOP_K):
        f = f + y_ref[kk].astype(f32) * w[:, kk:kk + 1]
    o_ref[...] = x_ref[...] + mod_ref[1:2, :] * (_rms(f) * mod_ref[0:1, :])


def _resid(n_lat_blocks, xnew, picked, wts, mod):
    n = xnew.shape[0]
    tm = ROW_BLOCK
    row = lambda i: (i, 0)
    grp = lambda i: (jnp.where(i >= n_lat_blocks, 1, 0), 0, 0)
    return pl.pallas_call(
        _resid_kernel, grid=(n // tm,),
        in_specs=[pl.BlockSpec((tm, D_MODEL), row), pl.BlockSpec((TOP_K, tm, D_MODEL), lambda i: (0, i, 0)),
                  pl.BlockSpec((tm, TOP_K), row), pl.BlockSpec((None, 2, D_MODEL), grp)],
        out_specs=pl.BlockSpec((tm, D_MODEL), row),
        out_shape=jax.ShapeDtypeStruct((n, D_MODEL), f32),
        compiler_params=_cparams(("parallel",)), name="resid",
    )(xnew, picked, wts, mod)


def _rope_tables(n_lat, n_ctx, dim):
    rows = n_lat // GRID_W
    row = jnp.repeat(jnp.arange(rows, dtype=f32), GRID_W)
    col = jnp.tile(jnp.arange(GRID_W, dtype=f32), rows)
    quarter = dim // 4
    inv = ROPE_BASE ** (-jnp.arange(quarter, dtype=f32) / quarter)
    ang = jnp.concatenate([row[:, None] * inv, col[:, None] * inv], axis=-1)
    cos = jnp.concatenate([jnp.cos(ang), jnp.ones((n_ctx, dim // 2), f32)], axis=0).T
    sin = jnp.concatenate([jnp.sin(ang), jnp.zeros((n_ctx, dim // 2), f32)], axis=0).T
    return cos, sin


def kernel(x, c, ctx, c_ctx, ada_w, ada_b, norm_g, w_in, m_gate_b, m_norm_g, q_norm_g, w_uq, kv_norm_g, w_ukv,
           diff_lam, diff_norm_g, w_branch, w_out, router_w, router_b, w_gu, b_gu, w_down, b_down):
    n_lat = x.shape[1]
    n_ctx = ctx.shape[1]
    n = n_lat + n_ctx
    assert x.shape[0] == 1 and n_lat % ROW_BLOCK == 0 and n_ctx == MLSTM_T and n_lat % GRID_W == 0
    nlb = n_lat // ROW_BLOCK
    x_all = jnp.concatenate([x[0], ctx[0]], axis=0)

    cond = jnp.zeros((8, D_MODEL), f32).at[0].set(c[0]).at[1].set(c_ctx)
    mods = _adaln(cond, ada_w, ada_b)[:, :2].reshape(DEPTH, 2, 6, D_MODEL)

    ca, sa = _rope_tables(n_lat, n_ctx, A_ROPE)
    cd, sd = _rope_tables(n_lat, n_ctx, DF_DK)

    for l in range(DEPTH):
        need_ctx = l < DEPTH - 1
        lam_init = 0.8 - 0.6 * math.exp(-0.3 * l)
        sh1, sc1, ga1, sh2, sc2, ga2 = [mods[l, :, j] for j in range(6)]
        ab1 = jnp.stack([norm_g[l, 0][None] * (1.0 + sc1), sh1], axis=1)
        w_tok, w_tr, wt_uq, wt_ukv = _pack_inproj_weights(w_in[l], w_uq[l], w_ukv[l])
        (mq, mv, mo, mg, gl, mkt, mgt, qta, k_a, vta, qtd, k_d, vtd) = _inproj(
            x_all, ab1, nlb, w_tok, w_tr, wt_uq, wt_ukv,
            q_norm_g[l].reshape(Q_LORA, 1), kv_norm_g[l].reshape(KV_LORA, 1), ca, sa, cd, sd)

        hf, hb = _mlstm(mq, mkt, mv, mg, mgt, m_gate_b[l])

        qtd4 = qtd.reshape(DF_HEADS, 2 * DF_DK, n)
        dg = diff_norm_g[l].reshape(DF_HEADS, DF_DV, 1)
        n_rows = n if need_ctx else n_lat
        attn_a = functools.partial(_attention, n_maps=1, dk=A_DKP, dv=A_V, n_out=n_rows)
        attn_d = functools.partial(_attention, n_maps=2, dk=DF_DK, dv=DF_DV, n_out=n_rows,
                                   lam=diff_lam[l], g=dg, lam_init=lam_init)
        ya = attn_a(qta, k_a, vta, tq=ATTN_TQ_MLA, lq=n_lat)
        yd = attn_d(qtd4, k_d, vtd, tq=ATTN_TQ_DIFF, lq=n_lat)
        if need_ctx:
            ctx_tile = lambda v: v[:, -1:, :, v.shape[3] - n_ctx:]
            ya = attn_a(qta, k_a[:, n_lat:], ctx_tile(vta), tq=n_ctx, lq=n_ctx, q_start=n_lat, out=ya)
            yd = attn_d(qtd4, k_d[:, n_lat:], ctx_tile(vtd), tq=n_ctx, lq=n_ctx, q_start=n_lat, out=yd)

        mod_m = jnp.stack([jnp.broadcast_to(norm_g[l, 1][None], (2, D_MODEL)), ga1,
                           norm_g[l, 2][None] * (1.0 + sc2), sh2], axis=1)
        rw = jnp.pad(router_w[l], ((0, 0), (0, LANE - N_EXPERTS)))
        rb = jnp.pad(router_b[l].reshape(1, N_EXPERTS), ((0, 0), (0, LANE - N_EXPERTS)))
        xnew, h2, logits = _merge(n_rows, nlb, x_all, hf, hb, mo, ya, yd, gl,
                                  m_norm_g[l].reshape(1, -1), w_branch[l].astype(bf16), w_out[l].astype(bf16),
                                  mod_m, rw, rb)
        picked, wts = _moe(h2, logits, l, w_gu, b_gu, w_down, b_down)
        mod_r = jnp.stack([jnp.broadcast_to(norm_g[l, 3][None], (2, D_MODEL)), ga2], axis=1)
        x_all = _resid(nlb, xnew, picked, wts, mod_r)
    return x_all[:n_lat][None]
```

```python
import functools
import math

import jax
import jax.numpy as jnp
from jax import lax
from jax.experimental import pallas as pl
from jax.experimental.pallas import tpu as pltpu

f32 = jnp.float32
bf16 = jnp.bfloat16

D_MODEL = 1024
DEPTH = 2
GRID_W = 64
EPS = 1e-6
ROPE_BASE = 10000.0
M_HEADS = 4
M_DK = 128
M_DV = 128
GATE_SOFTCAP = 15.0
A_HEADS = 8
A_NOPE = 64
A_ROPE = 32
A_V = 64
Q_LORA = 256
KV_LORA = 128
DF_HEADS = 4
DF_DK = 64
DF_DV = 128
N_BRANCH = 3
BRANCH_W = 512
N_EXPERTS = 32
TOP_K = 4
D_EXPERT = 1024
SWIGLU_LIMIT = 7.0
SWIGLU_ALPHA = 1.702

IN_SIZES = (M_HEADS * M_DK, M_HEADS * M_DK, M_HEADS * M_DV, M_HEADS * M_DV, 4 * M_HEADS,
            Q_LORA, KV_LORA, A_ROPE,
            2 * DF_HEADS * DF_DK, 2 * DF_HEADS * DF_DK, DF_HEADS * DF_DV,
            N_BRANCH * D_MODEL)

LOG2E = 1.4426950408889634
LANE = 128
A_DKP = 128
A_DVP = A_V + 16
DF_DVP = DF_DV + 16
VMEM_LIMIT = 56 * 1024 * 1024

ROW_BLOCK = 256
MXU_DEPTH = 256
ATTN_TQ_MLA = 4096
ATTN_TQ_DIFF = 2048
ATTN_QSUB = 256
ATTN_TK = 1280
MLSTM_T = 256
MOE_TM = 512
H2_ROWS = 32768

NT_DIMS = (((1,), (1,)), ((), ()))


def _cparams(sem):
    return pltpu.CompilerParams(dimension_semantics=sem, vmem_limit_bytes=VMEM_LIMIT)


def _const_spec(shape):
    nd = len(shape)
    return pl.BlockSpec(shape, lambda *_: (0,) * nd)


def _adaln_kernel(cond_ref, w_ref, b_ref, o_ref):
    cnd = cond_ref[...]
    a = cnd * jax.nn.sigmoid(cnd)
    o_ref[...] = jnp.dot(a, w_ref[...], preferred_element_type=f32,
                         precision=lax.Precision.HIGHEST) + b_ref[...]


def _adaln(cond, ada_w, ada_b):
    tn = 1536
    nj = (6 * D_MODEL) // tn
    return pl.pallas_call(
        _adaln_kernel,
        grid=(DEPTH, nj),
        in_specs=[pl.BlockSpec((8, D_MODEL), lambda l, j: (0, 0)),
                  pl.BlockSpec((None, D_MODEL, tn), lambda l, j: (l, 0, j)),
                  pl.BlockSpec((None, 1, tn), lambda l, j: (l, 0, j))],
        out_specs=pl.BlockSpec((None, 8, tn), lambda l, j: (l, 0, j)),
        out_shape=jax.ShapeDtypeStruct((DEPTH, 8, 6 * D_MODEL), f32),
        compiler_params=_cparams(("parallel", "parallel")),
        name="adaln",
    )(cond, ada_w, ada_b.reshape(DEPTH, 1, 6 * D_MODEL))


TOK_W = 3 * 512 + LANE + N_BRANCH * D_MODEL
TR_W = 512 + 16 + Q_LORA + KV_LORA + A_ROPE + 3 * 512


def _ones_row_block(tm):
    r = lax.broadcasted_iota(jnp.int32, (16, tm), 0)
    return jnp.where(r == 0, 1.0, 0.0).astype(bf16)


def _inproj_kernel(x_ref, ab_ref, wtok_ref, wtr_ref, wuq_ref, wukv_ref, gq_ref, gkv_ref,
                   ca_ref, sa_ref, cd_ref, sd_ref,
                   mq_ref, mv_ref, mo_ref, mg_ref, gl_ref, mkt_ref, mgt_ref,
                   qta_ref, kta_ref, vta_ref, qtd_ref, ktd_ref, vtd_ref):
    tm = x_ref.shape[0]
    x = x_ref[...]
    ms = jnp.mean(x * x, axis=-1, keepdims=True)
    h = (x * lax.rsqrt(ms + EPS)) * ab_ref[0:1, :] + ab_ref[1:2, :]
    hb = h.astype(bf16)

    tok = jnp.dot(hb, wtok_ref[...], preferred_element_type=f32)
    mq_ref[...] = (tok[:, 0:512] * (M_DK ** -0.5)).astype(bf16)
    mv_ref[...] = tok[:, 512:1024].astype(bf16)
    mo_ref[...] = tok[:, 1024:1536].astype(bf16)
    mg_ref[...] = tok[:, 1536:1664]
    gl_ref[...] = tok[:, 1664:TOK_W].astype(bf16)

    tr = lax.dot_general(wtr_ref[...], hb, NT_DIMS, preferred_element_type=f32)
    mkt_ref[...] = tr[0:512].astype(bf16)
    mgt_ref[...] = tr[512:528]
    cq = tr[528:784]
    ckv = tr[784:912]
    kpe = tr[912:944]
    dq = tr[944:1456]
    dk = tr[1456:1968]
    dv = tr[1968:2480]

    ca = ca_ref[...]
    sa = sa_ref[...]
    cd = cd_ref[...]
    sd = sd_ref[...]
    ones_blk = _ones_row_block(tm)
    zeros32 = jnp.zeros((32, tm), bf16)

    cqn = (cq * lax.rsqrt(jnp.mean(cq * cq, axis=0, keepdims=True) + EPS) * gq_ref[...]).astype(bf16)
    qscale = ((A_NOPE + A_ROPE) ** -0.5) * LOG2E
    qt = jnp.dot(wuq_ref[...], cqn, preferred_element_type=f32) * qscale
    for hh in range(A_HEADS):
        a = qt[512 + 16 * hh:528 + 16 * hh]
        b = qt[640 + 16 * hh:656 + 16 * hh]
        qta_ref[hh, 0:64, :] = qt[64 * hh:64 * hh + 64].astype(bf16)
        qta_ref[hh, 64:80, :] = (a * ca - b * sa).astype(bf16)
        qta_ref[hh, 80:96, :] = (a * sa + b * ca).astype(bf16)
        qta_ref[hh, 96:128, :] = zeros32

    ckvn = (ckv * lax.rsqrt(jnp.mean(ckv * ckv, axis=0, keepdims=True) + EPS) * gkv_ref[...]).astype(bf16)
    kvt = jnp.dot(wukv_ref[...], ckvn, preferred_element_type=f32)
    ka = kpe[0:16]
    kb = kpe[16:32]
    kpe_rot = jnp.concatenate([ka * ca - kb * sa, ka * sa + kb * ca, jnp.zeros((32, tm), f32)], axis=0)
    for hh in range(A_HEADS):
        kt_h = jnp.concatenate([kvt[64 * hh:64 * hh + 64], kpe_rot], axis=0)
        kta_ref[hh] = kt_h.T.astype(bf16)
        vta_ref[hh, 0:64, :] = kvt[512 + 64 * hh:576 + 64 * hh].astype(bf16)
        vta_ref[hh, 64:80, :] = ones_blk

    dscale = (DF_DK ** -0.5) * LOG2E
    for hh in range(2 * DF_HEADS):
        a = dq[32 * hh:32 * hh + 32]
        b = dq[256 + 32 * hh:288 + 32 * hh]
        qtd_ref[hh, 0:32, :] = ((a * cd - b * sd) * dscale).astype(bf16)
        qtd_ref[hh, 32:64, :] = ((a * sd + b * cd) * dscale).astype(bf16)
        a = dk[32 * hh:32 * hh + 32]
        b = dk[256 + 32 * hh:288 + 32 * hh]
        ktd_ref[hh] = jnp.concatenate([a * cd - b * sd, a * sd + b * cd], axis=0).T.astype(bf16)
    for hh in range(DF_HEADS):
        vtd_ref[hh, 0:128, :] = dv[128 * hh:128 * hh + 128].astype(bf16)
        vtd_ref[hh, 128:144, :] = ones_blk


def _pack_inproj_weights(w, w_uq, w_ukv):
    o = [0]
    for s in IN_SIZES:
        o.append(o[-1] + s)
    mq, mk, mv, mo, mg, cq, ckv, kpe, dq, dk, dv, gl = [w[:, o[i]:o[i + 1]] for i in range(12)]
    mg_pad = jnp.pad(mg, ((0, 0), (0, LANE - mg.shape[1])))
    w_tok = jnp.concatenate([mq, mv, mo, mg_pad, gl], axis=1).astype(bf16)

    def split_halves(t, n_heads, d):
        t3 = t.reshape(t.shape[0], n_heads, d)
        return jnp.concatenate([t3[:, :, :d // 2].reshape(t.shape[0], -1),
                                t3[:, :, d // 2:].reshape(t.shape[0], -1)], axis=1)

    w_tr = jnp.concatenate([mk, mg, cq, ckv, kpe, split_halves(dq, 2 * DF_HEADS, DF_DK),
                            split_halves(dk, 2 * DF_HEADS, DF_DK), dv], axis=1).T.astype(bf16)
    uq = w_uq.reshape(Q_LORA, A_HEADS, A_NOPE + A_ROPE)
    half = A_ROPE // 2
    wt_uq = jnp.concatenate([uq[:, :, :A_NOPE].reshape(Q_LORA, -1),
                             uq[:, :, A_NOPE:A_NOPE + half].reshape(Q_LORA, -1),
                             uq[:, :, A_NOPE + half:].reshape(Q_LORA, -1)], axis=1).T.astype(bf16)
    ukv = w_ukv.reshape(KV_LORA, A_HEADS, A_NOPE + A_V)
    wt_ukv = jnp.concatenate([ukv[:, :, :A_NOPE].reshape(KV_LORA, -1),
                              ukv[:, :, A_NOPE:].reshape(KV_LORA, -1)], axis=1).T.astype(bf16)
    return w_tok, w_tr, wt_uq, wt_ukv


def _inproj(x_all, ab, n_lat_blocks, w_tok, w_tr, wt_uq, wt_ukv, gq, gkv, ca, sa, cd, sd):
    n = x_all.shape[0]
    tm = ROW_BLOCK
    tk = _key_tile(n)
    per = tk // tm
    vtile = lambda i: (0, i // per, 0, i % per)
    nb = n // tm
    row = lambda i: (i, 0)
    colb = lambda i: (0, i)
    col3 = lambda i: (0, 0, i)
    grp = lambda i: (jnp.where(i >= n_lat_blocks, 1, 0), 0, 0)
    out_shapes = [
        jax.ShapeDtypeStruct((n, 512), bf16), jax.ShapeDtypeStruct((n, 512), bf16),
        jax.ShapeDtypeStruct((n, 512), bf16), jax.ShapeDtypeStruct((n, LANE), f32),
        jax.ShapeDtypeStruct((n, N_BRANCH * D_MODEL), bf16),
        jax.ShapeDtypeStruct((512, n), bf16), jax.ShapeDtypeStruct((16, n), f32),
        jax.ShapeDtypeStruct((A_HEADS, A_DKP, n), bf16), jax.ShapeDtypeStruct((A_HEADS, n, A_DKP), bf16),
        jax.ShapeDtypeStruct((A_HEADS, n // tk, A_DVP, tk), bf16),
        jax.ShapeDtypeStruct((2 * DF_HEADS, DF_DK, n), bf16), jax.ShapeDtypeStruct((2 * DF_HEADS, n, DF_DK), bf16),
        jax.ShapeDtypeStruct((DF_HEADS, n // tk, DF_DVP, tk), bf16),
    ]
    out_specs = [
        pl.BlockSpec((tm, 512), row), pl.BlockSpec((tm, 512), row), pl.BlockSpec((tm, 512), row),
        pl.BlockSpec((tm, LANE), row), pl.BlockSpec((tm, N_BRANCH * D_MODEL), row),
        pl.BlockSpec((512, tm), colb), pl.BlockSpec((16, tm), colb),
        pl.BlockSpec((A_HEADS, A_DKP, tm), col3), pl.BlockSpec((A_HEADS, tm, A_DKP), lambda i: (0, i, 0)),
        pl.BlockSpec((A_HEADS, None, A_DVP, tm), vtile),
        pl.BlockSpec((2 * DF_HEADS, DF_DK, tm), col3), pl.BlockSpec((2 * DF_HEADS, tm, DF_DK), lambda i: (0, i, 0)),
        pl.BlockSpec((DF_HEADS, None, DF_DVP, tm), vtile),
    ]
    in_specs = [
        pl.BlockSpec((tm, D_MODEL), row),
        pl.BlockSpec((None, 2, D_MODEL), grp),
        _const_spec(w_tok.shape), _const_spec(w_tr.shape), _const_spec(wt_uq.shape), _const_spec(wt_ukv.shape),
        _const_spec(gq.shape), _const_spec(gkv.shape),
        pl.BlockSpec((A_ROPE // 2, tm), colb), pl.BlockSpec((A_ROPE // 2, tm), colb),
        pl.BlockSpec((DF_DK // 2, tm), colb), pl.BlockSpec((DF_DK // 2, tm), colb),
    ]
    return pl.pallas_call(
        _inproj_kernel, grid=(nb,), in_specs=in_specs, out_specs=out_specs, out_shape=out_shapes,
        compiler_params=_cparams(("parallel",)), name="inproj",
    )(x_all, ab, w_tok, w_tr, wt_uq, wt_ukv, gq, gkv, ca, sa, cd, sd)


def _attn_kernel(*refs, n_maps, dk, dv, tk, n_kt, qsub, lam_init, n_in):
    o_ref = refs[n_in]
    if n_maps == 2:
        lam_ref, g_ref, qt_ref, k_ref, vt_ref = refs[:5]
    else:
        qt_ref, k_ref, vt_ref = refs[:3]
    tq = qt_ref.shape[-1]
    dvp = vt_ref.shape[-2]
    chains = [(m, c) for m in range(n_maps) for c in range(tq // qsub)]

    def qk(j, m, c):
        qt = qt_ref[m * dk:(m + 1) * dk, c * qsub:(c + 1) * qsub]
        return jnp.dot(k_ref[m, j * tk:(j + 1) * tk, :], qt, preferred_element_type=f32)

    state = [(jnp.full((1, qsub), -jnp.inf, f32), jnp.zeros((dvp, qsub), f32)) for _ in chains]
    s_cur = [qk(0, m, c) for (m, c) in chains]
    for j in range(n_kt):
        vtj = vt_ref[j]
        for i, (m, c) in enumerate(chains):
            s_nxt = qk(j + 1, m, c) if j + 1 < n_kt else None
            m_run, acc = state[i]
            m_new = jnp.maximum(m_run, jnp.max(s_cur[i], axis=0, keepdims=True))
            alpha = jnp.exp2(m_run - m_new)
            p = jnp.exp2(s_cur[i] - m_new).astype(bf16)
            pv = jnp.dot(vtj, p, preferred_element_type=f32)
            state[i] = (m_new, alpha * acc + pv)
            s_cur[i] = s_nxt
    outs = []
    for m in range(n_maps):
        accs = [state[i][1] for i, (mm, _) in enumerate(chains) if mm == m]
        acc = accs[0] if len(accs) == 1 else jnp.concatenate(accs, axis=1)
        outs.append(acc[0:dv] / acc[dv:dv + 1])
    if n_maps == 1:
        o_ref[...] = outs[0].astype(o_ref.dtype)
    else:
        lf = lam_ref[...]
        lam = (jnp.exp(jnp.sum(lf[0:1] * lf[1:2], axis=1, keepdims=True))
               - jnp.exp(jnp.sum(lf[2:3] * lf[3:4], axis=1, keepdims=True)) + lam_init)
        o = outs[0] - lam * outs[1]
        o = o * lax.rsqrt(jnp.mean(o * o, axis=0, keepdims=True) + EPS) * g_ref[...]
        o_ref[...] = (o * (1.0 - lam_init)).astype(o_ref.dtype)


def _key_tile(lk):
    return max(t for t in range(MXU_DEPTH, min(ATTN_TK, lk) + 1, MXU_DEPTH) if lk % t == 0)


def _attention(qt, k, vt4, *, n_maps, dk, dv, tq, lq, n_out, q_start=0, out=None, lam=None, g=None, lam_init=0.0):
    n_heads = qt.shape[0]
    lk = k.shape[1]
    _, n_kt, dvp, tk = vt4.shape
    assert n_kt * tk == lk
    tq = min(tq, lq)
    qsub = min(ATTN_QSUB, tq)
    qb0 = q_start // tq
    in_specs = [
        pl.BlockSpec((None, n_maps * dk, tq), lambda h, i: (h, 0, qb0 + i)),
        pl.BlockSpec((n_maps, lk, dk), lambda h, i: (h, 0, 0), pipeline_mode=pl.Buffered(1)),
        pl.BlockSpec((None, n_kt, dvp, tk), lambda h, i: (h, 0, 0, 0), pipeline_mode=pl.Buffered(1)),
    ]
    args = [qt, k, vt4]
    if n_maps == 2:
        in_specs = [_const_spec(lam.shape), pl.BlockSpec((None, dv, 1), lambda h, i: (h, 0, 0))] + in_specs
        args = [lam, g] + args
    aliases = {}
    if out is not None:
        aliases = {len(args): 0}
        in_specs.append(pl.BlockSpec(memory_space=pl.ANY))
        args.append(out)
    kern = functools.partial(_attn_kernel, n_maps=n_maps, dk=dk, dv=dv, tk=tk, n_kt=n_kt, qsub=qsub,
                             lam_init=lam_init, n_in=len(args))
    return pl.pallas_call(
        kern, grid=(n_heads, lq // tq), in_specs=in_specs,
        out_specs=pl.BlockSpec((None, dv, tq), lambda h, i: (h, 0, qb0 + i)),
        out_shape=jax.ShapeDtypeStruct((n_heads, dv, n_out), bf16),
        input_output_aliases=aliases,
        compiler_params=_cparams(("parallel", "parallel")),
        name="attn_diff" if n_maps == 2 else "attn_mla",
    )(*args)


def _split3(x):
    hi = x.astype(bf16)
    r = x - hi.astype(f32)
    mid = r.astype(bf16)
    lo = (r - mid.astype(f32)).astype(bf16)
    return hi, mid, lo


def _log_sigmoid(x):
    return jnp.minimum(x, 0.0) - jnp.log(1.0 + jnp.exp(-jnp.abs(x)))


def _softcap(x):
    return GATE_SOFTCAP * jnp.tanh(x * (1.0 / GATE_SOFTCAP))


def _mlstm_kernel(gb_ref, gbt_ref,
                  qf_ref, ktf_ref, vf_ref, gf_ref, gtf_ref,
                  qb_ref, ktb_ref, vb_ref, gb2_ref, gtb_ref,
                  hf_ref, hb_ref, c_ref, m_ref):
    t = qf_ref.shape[0]

    @pl.when(pl.program_id(0) == 0)
    def _():
        c_ref[...] = jnp.zeros_like(c_ref)
        m_ref[...] = jnp.zeros_like(m_ref)

    row = lax.broadcasted_iota(jnp.int32, (t, t), 0)
    col = lax.broadcasted_iota(jnp.int32, (t, t), 1)
    lane = lax.broadcasted_iota(jnp.int32, (t, LANE), 1)
    ones_col = jnp.where(lane == 0, 1.0, 0.0).astype(bf16)
    dirs = ((qf_ref, ktf_ref, vf_ref, gf_ref, gtf_ref, hf_ref),
            (qb_ref, ktb_ref, vb_ref, gb2_ref, gtb_ref, hb_ref))
    for d, (q_ref, kt_ref, v_ref, g_ref, gt_ref, o_ref) in enumerate(dirs):
        mask = (col <= row) if d == 0 else (col >= row)
        maskb = mask.astype(bf16)
        last = t - 1 if d == 0 else 0
        g_tok = _softcap(g_ref[...] + gb_ref[...])
        g_t = _softcap(gt_ref[...] + gbt_ref[...])
        b_tok = sum(jnp.dot(maskb, part, preferred_element_type=f32) for part in _split3(_log_sigmoid(g_tok)))
        b_t = sum(lax.dot_general(part, maskb, NT_DIMS, preferred_element_type=f32)
                  for part in _split3(_log_sigmoid(g_t)))
        for hh in range(M_HEADS):
            ri = 4 * (2 * d) + hh
            rf = 4 * (2 * d + 1) + hh
            idx = d * M_HEADS + hh
            b_col = b_tok[:, rf:rf + 1]
            b_row = b_t[rf:rf + 1, :]
            a_row = g_t[ri:ri + 1, :] - b_row
            m_prev = m_ref[idx, 0:1, 0:1]
            ld = jnp.where(mask, b_col + a_row, -jnp.inf)
            inter = b_col + m_prev
            mj = jnp.maximum(inter, jnp.max(ld, axis=1, keepdims=True))
            dmat = jnp.exp(ld - mj)
            q = q_ref[:, hh * M_DK:(hh + 1) * M_DK]
            kt = kt_ref[hh * M_DK:(hh + 1) * M_DK, :]
            vp = jnp.concatenate([v_ref[:, hh * M_DV:(hh + 1) * M_DV], ones_col], axis=1)
            sc = jnp.dot(q, kt, preferred_element_type=f32) * dmat
            cp = c_ref[idx]
            num = (jnp.dot(sc.astype(bf16), vp, preferred_element_type=f32)
                   + jnp.exp(inter - mj) * jnp.dot(q, cp.astype(bf16), preferred_element_type=f32))
            den = num[:, M_DV:M_DV + 1]
            hj = num[:, 0:M_DV] / jnp.maximum(jnp.abs(den), jnp.exp(-mj))
            o_ref[:, hh * M_DV:(hh + 1) * M_DV] = hj.astype(o_ref.dtype)
            b_tot = b_t[rf:rf + 1, last:last + 1]
            lw = b_tot + a_row
            m_new = jnp.maximum(b_tot + m_prev, jnp.max(lw, axis=1, keepdims=True))
            kw = (kt.astype(f32) * jnp.exp(lw - m_new)).astype(bf16)
            c_ref[idx] = jnp.exp(b_tot + m_prev - m_new) * cp + jnp.dot(kw, vp, preferred_element_type=f32)
            m_ref[idx] = jnp.broadcast_to(m_new, (8, LANE))


def _mlstm(mq, mkt, mv, mg, mgt, gate_b):
    n = mq.shape[0]
    t = MLSTM_T
    nb = n // t
    gb = jnp.pad(gate_b.reshape(1, 16), ((0, 0), (0, LANE - 16)))
    gbt = gate_b.reshape(16, 1)
    fwd = lambda s: jnp.where(s == 0, nb - 1, s - 1)
    bwd = lambda s: nb - 1 - s
    specs = []
    for im in (fwd, bwd):
        specs += [pl.BlockSpec((t, 512), lambda s, im=im: (im(s), 0)),
                  pl.BlockSpec((512, t), lambda s, im=im: (0, im(s))),
                  pl.BlockSpec((t, 512), lambda s, im=im: (im(s), 0)),
                  pl.BlockSpec((t, LANE), lambda s, im=im: (im(s), 0)),
                  pl.BlockSpec((16, t), lambda s, im=im: (0, im(s)))]
    return pl.pallas_call(
        _mlstm_kernel, grid=(nb,),
        in_specs=[_const_spec(gb.shape), _const_spec(gbt.shape)] + specs,
        out_specs=[pl.BlockSpec((t, 512), lambda s: (fwd(s), 0)), pl.BlockSpec((t, 512), lambda s: (bwd(s), 0))],
        out_shape=[jax.ShapeDtypeStruct((n, 512), bf16), jax.ShapeDtypeStruct((n, 512), bf16)],
        scratch_shapes=[pltpu.VMEM((2 * M_HEADS, M_DK, 2 * M_DV), f32), pltpu.VMEM((2 * M_HEADS, 8, LANE), f32)],
        compiler_params=_cparams(("arbitrary",)), name="mlstm",
    )(gb, gbt, mq, mkt, mv, mg, mgt, mq, mkt, mv, mg, mgt)


def _rms(x, axis=-1):
    return x * lax.rsqrt(jnp.mean(x * x, axis=axis, keepdims=True) + EPS)


def _merge_kernel(x_ref, hf_ref, hb_ref, mo_ref, ya_ref, yd_ref, gl_ref, mng_ref, wb_ref, wout_ref,
                  mod_ref, rw_ref, rb_ref, xnew_ref, h2_ref, logit_ref):
    hs = hf_ref[...].astype(f32) + hb_ref[...].astype(f32)
    parts = [_rms(hs[:, hh * M_DV:(hh + 1) * M_DV]) for hh in range(M_HEADS)]
    ym = jnp.concatenate(parts, axis=1) * mng_ref[...] * jax.nn.sigmoid(mo_ref[...].astype(f32))
    yb0 = jnp.dot(ym.astype(bf16), wb_ref[0], preferred_element_type=f32)
    tm = x_ref.shape[0]
    tn_dims = (((0,), (0,)), ((), ()))
    yb1 = lax.dot_general(ya_ref[...].reshape(BRANCH_W, tm), wb_ref[1], tn_dims, preferred_element_type=f32)
    yb2 = lax.dot_general(yd_ref[...].reshape(BRANCH_W, tm), wb_ref[2], tn_dims, preferred_element_type=f32)
    d = D_MODEL
    z = (jax.nn.sigmoid(gl_ref[:, 0:d].astype(f32)) * yb0
         + jax.nn.sigmoid(gl_ref[:, d:2 * d].astype(f32)) * yb1
         + jax.nn.sigmoid(gl_ref[:, 2 * d:3 * d].astype(f32)) * yb2)
    y = jnp.dot(z.astype(bf16), wout_ref[...], preferred_element_type=f32)
    xnew = x_ref[...] + mod_ref[1:2, :] * (_rms(y) * mod_ref[0:1, :])
    h2 = _rms(xnew) * mod_ref[2:3, :] + mod_ref[3:4, :]
    xnew_ref[...] = xnew
    h2_ref[...] = h2.astype(bf16)
    h2_hi = h2.astype(bf16)
    h2_lo = (h2 - h2_hi.astype(f32)).astype(bf16)
    rw = rw_ref[...]
    rw_hi = rw.astype(bf16)
    rw_lo = (rw - rw_hi.astype(f32)).astype(bf16)
    logit_ref[...] = (jnp.dot(h2_hi, rw_hi, preferred_element_type=f32)
                      + jnp.dot(h2_lo, rw_hi, preferred_element_type=f32)
                      + jnp.dot(h2_hi, rw_lo, preferred_element_type=f32) + rb_ref[...])


def _merge(n_rows, n_lat_blocks, x_all, hf, hb, mo, ya, yd, gl, mng, wb, wout, mod, rw, rb):
    tm = ROW_BLOCK
    nb = n_rows // tm
    row = lambda i: (i, 0)
    grp = lambda i: (jnp.where(i >= n_lat_blocks, 1, 0), 0, 0)
    in_specs = [pl.BlockSpec((tm, D_MODEL), row), pl.BlockSpec((tm, 512), row), pl.BlockSpec((tm, 512), row),
                pl.BlockSpec((tm, 512), row),
                pl.BlockSpec((A_HEADS, A_V, tm), lambda i: (0, 0, i)),
                pl.BlockSpec((DF_HEADS, DF_DV, tm), lambda i: (0, 0, i)),
                pl.BlockSpec((tm, N_BRANCH * D_MODEL), row), _const_spec(mng.shape), _const_spec(wb.shape),
                _const_spec(wout.shape), pl.BlockSpec((None, 4, D_MODEL), grp),
                _const_spec(rw.shape), _const_spec(rb.shape)]
    return pl.pallas_call(
        _merge_kernel, grid=(nb,), in_specs=in_specs,
        out_specs=[pl.BlockSpec((tm, D_MODEL), row), pl.BlockSpec((tm, D_MODEL), row), pl.BlockSpec((tm, LANE), row)],
        out_shape=[jax.ShapeDtypeStruct((n_rows, D_MODEL), f32),
                   jax.ShapeDtypeStruct((max(n_rows, H2_ROWS), D_MODEL), bf16),
                   jax.ShapeDtypeStruct((n_rows, LANE), f32)],
        compiler_params=_cparams(("parallel",)), name="merge",
    )(x_all, hf, hb, mo, ya, yd, gl, mng, wb, wout, mod, rw, rb)


def _moe_kernel(be_ref, nu_ref, xs_ref, wgu_ref, bgu_ref, wd_ref, bd_ref, o_ref, wgu_s, wd_s):
    i = pl.program_id(0)
    prev = be_ref[jnp.maximum(i - 1, 0)]
    used = i < nu_ref[0]

    @pl.when(jnp.logical_and(used, jnp.logical_or(i == 0, be_ref[i] != prev)))
    def _():
        wgu_s[...] = wgu_ref[...].astype(bf16)
        wd_s[...] = wd_ref[...].astype(bf16)

    @pl.when(used)
    def _():
        gu = jnp.dot(xs_ref[...], wgu_s[...], preferred_element_type=f32) + bgu_ref[...]
        gate = jnp.minimum(gu[:, :D_EXPERT], SWIGLU_LIMIT)
        up = jnp.clip(gu[:, D_EXPERT:], -SWIGLU_LIMIT, SWIGLU_LIMIT)
        act = (up + 1.0) * (gate * jax.nn.sigmoid(SWIGLU_ALPHA * gate))
        y = jnp.dot(act.astype(bf16), wd_s[...], preferred_element_type=f32) + bd_ref[...]
        o_ref[...] = y.astype(o_ref.dtype)

    @pl.when(jnp.logical_not(used))
    def _():
        o_ref[...] = jnp.zeros_like(o_ref)


def _moe_experts(blk_e, n_used, xs, layer, w_gu, b_gu, w_down, b_down):
    cap = xs.shape[0]
    tm = MOE_TM
    nblk = cap // tm
    grid_spec = pltpu.PrefetchScalarGridSpec(
        num_scalar_prefetch=2, grid=(nblk,),
        in_specs=[pl.BlockSpec((tm, D_MODEL), lambda i, be, nu: (i, 0)),
                  pl.BlockSpec((None, None, D_MODEL, 2 * D_EXPERT), lambda i, be, nu: (layer, be[i], 0, 0)),
                  pl.BlockSpec((None, None, 1, 2 * D_EXPERT), lambda i, be, nu: (layer, be[i], 0, 0)),
                  pl.BlockSpec((None, None, D_EXPERT, D_MODEL), lambda i, be, nu: (layer, be[i], 0, 0)),
                  pl.BlockSpec((None, None, 1, D_MODEL), lambda i, be, nu: (layer, be[i], 0, 0))],
        out_specs=pl.BlockSpec((tm, D_MODEL), lambda i, be, nu: (i, 0)),
        scratch_shapes=[pltpu.VMEM((D_MODEL, 2 * D_EXPERT), bf16), pltpu.VMEM((D_EXPERT, D_MODEL), bf16)])
    return pl.pallas_call(
        _moe_kernel, grid_spec=grid_spec,
        out_shape=jax.ShapeDtypeStruct((cap, D_MODEL), bf16),
        compiler_params=_cparams(("arbitrary",)), name="moe_experts",
    )(blk_e, n_used, xs, w_gu, b_gu.reshape(DEPTH, N_EXPERTS, 1, -1), w_down, b_down.reshape(DEPTH, N_EXPERTS, 1, -1))


def _moe(h2, logits, layer, w_gu, b_gu, w_down, b_down):
    n = logits.shape[0]
    tm = MOE_TM
    i32 = jnp.int32
    top_v, top_i = lax.top_k(logits[:, :N_EXPERTS], TOP_K)
    wts = jax.nn.softmax(top_v, axis=-1)
    nk = n * TOP_K
    flat_e = top_i.reshape(nk).astype(i32)
    iota = jnp.arange(nk, dtype=i32)
    se, order = lax.sort((flat_e, iota), num_keys=1, is_stable=True)
    e_ids = jnp.arange(N_EXPERTS, dtype=i32)
    grp_start = jnp.sum((se[:, None] < e_ids[None, :]).astype(i32), axis=0)
    counts = jnp.sum((se[:, None] == e_ids[None, :]).astype(i32), axis=0)
    padded = (counts + tm - 1) // tm * tm
    pad_end = jnp.cumsum(padded)
    pad_start = pad_end - padded
    off = pad_start - grp_start
    step = off - jnp.concatenate([jnp.zeros((1,), i32), off[:-1]])
    dest_sorted = iota + jnp.sum(jnp.where(iota[:, None] >= grp_start[None, :], step[None, :], 0), axis=1)
    _, pos_km = lax.sort(((order % TOP_K) * n + order // TOP_K, dest_sorted), num_keys=1)
    cap = -(-(nk + N_EXPERTS * (tm - 1)) // tm) * tm
    nblk = cap // tm
    bstart = jnp.arange(nblk, dtype=i32) * tm
    blk_e = jnp.minimum(jnp.sum((pad_end[None, :] <= bstart[:, None]).astype(i32), axis=1), N_EXPERTS - 1)
    n_used = (pad_end[-1] // tm).astype(i32).reshape(1)
    sel = blk_e[:, None] == e_ids[None, :]
    off_b = jnp.sum(jnp.where(sel, off[None, :], 0), axis=1)
    end_b = jnp.sum(jnp.where(sel, (pad_start + counts)[None, :], 0), axis=1)
    p2 = bstart[:, None] + jnp.arange(tm, dtype=i32)[None, :]
    src = jnp.clip(p2 - off_b[:, None], 0, nk - 1)
    slot_t = jnp.where(p2 < end_b[:, None], (order // TOP_K)[src], p2 % n).reshape(cap)
    xs = h2[slot_t]
    ys = _moe_experts(blk_e, n_used, xs, layer, w_gu, b_gu, w_down, b_down)
    return ys[pos_km].reshape(TOP_K, n, D_MODEL), wts


def _resid_kernel(x_ref, y_ref, w_ref, mod_ref, o_ref):
    w = w_ref[...]
    f = y_ref[0].astype(f32) * w[:, 0:1]
    for kk in range(1, TOP_K):
        f = f + y_ref[kk].astype(f32) * w[:, kk:kk + 1]
    o_ref[...] = x_ref[...] + mod_ref[1:2, :] * (_rms(f) * mod_ref[0:1, :])


def _resid(n_lat_blocks, xnew, picked, wts, mod):
    n = xnew.shape[0]
    tm = ROW_BLOCK
    row = lambda i: (i, 0)
    grp = lambda i: (jnp.where(i >= n_lat_blocks, 1, 0), 0, 0)
    return pl.pallas_call(
        _resid_kernel, grid=(n // tm,),
        in_specs=[pl.BlockSpec((tm, D_MODEL), row), pl.BlockSpec((TOP_K, tm, D_MODEL), lambda i: (0, i, 0)),
                  pl.BlockSpec((tm, TOP_K), row), pl.BlockSpec((None, 2, D_MODEL), grp)],
        out_specs=pl.BlockSpec((tm, D_MODEL), row),
        out_shape=jax.ShapeDtypeStruct((n, D_MODEL), f32),
        compiler_params=_cparams(("parallel",)), name="resid",
    )(xnew, picked, wts, mod)


def _rope_tables(n_lat, n_ctx, dim):
    rows = n_lat // GRID_W
    row = jnp.repeat(jnp.arange(rows, dtype=f32), GRID_W)
    col = jnp.tile(jnp.arange(GRID_W, dtype=f32), rows)
    quarter = dim // 4
    inv = ROPE_BASE ** (-jnp.arange(quarter, dtype=f32) / quarter)
    ang = jnp.concatenate([row[:, None] * inv, col[:, None] * inv], axis=-1)
    cos = jnp.concatenate([jnp.cos(ang), jnp.ones((n_ctx, dim // 2), f32)], axis=0).T
    sin = jnp.concatenate([jnp.sin(ang), jnp.zeros((n_ctx, dim // 2), f32)], axis=0).T
    return cos, sin


def kernel(x, c, ctx, c_ctx, ada_w, ada_b, norm_g, w_in, m_gate_b, m_norm_g, q_norm_g, w_uq, kv_norm_g, w_ukv,
           diff_lam, diff_norm_g, w_branch, w_out, router_w, router_b, w_gu, b_gu, w_down, b_down):
    n_lat = x.shape[1]
    n_ctx = ctx.shape[1]
    n = n_lat + n_ctx
    assert x.shape[0] == 1 and n_lat % ROW_BLOCK == 0 and n_ctx == MLSTM_T and n_lat % GRID_W == 0
    nlb = n_lat // ROW_BLOCK
    x_all = jnp.concatenate([x[0], ctx[0]], axis=0)

    cond = jnp.zeros((8, D_MODEL), f32).at[0].set(c[0]).at[1].set(c_ctx)
    mods = _adaln(cond, ada_w, ada_b)[:, :2].reshape(DEPTH, 2, 6, D_MODEL)

    ca, sa = _rope_tables(n_lat, n_ctx, A_ROPE)
    cd, sd = _rope_tables(n_lat, n_ctx, DF_DK)

    for l in range(DEPTH):
        need_ctx = l < DEPTH - 1
        lam_init = 0.8 - 0.6 * math.exp(-0.3 * l)
        sh1, sc1, ga1, sh2, sc2, ga2 = [mods[l, :, j] for j in range(6)]
        ab1 = jnp.stack([norm_g[l, 0][None] * (1.0 + sc1), sh1], axis=1)
        w_tok, w_tr, wt_uq, wt_ukv = _pack_inproj_weights(w_in[l], w_uq[l], w_ukv[l])
        (mq, mv, mo, mg, gl, mkt, mgt, qta, k_a, vta, qtd, k_d, vtd) = _inproj(
            x_all, ab1, nlb, w_tok, w_tr, wt_uq, wt_ukv,
            q_norm_g[l].reshape(Q_LORA, 1), kv_norm_g[l].reshape(KV_LORA, 1), ca, sa, cd, sd)

        hf, hb = _mlstm(mq, mkt, mv, mg, mgt, m_gate_b[l])

        qtd4 = qtd.reshape(DF_HEADS, 2 * DF_DK, n)
        dg = diff_norm_g[l].reshape(DF_HEADS, DF_DV, 1)
        n_rows = n if need_ctx else n_lat
        attn_a = functools.partial(_attention, n_maps=1, dk=A_DKP, dv=A_V, n_out=n_rows)
        attn_d = functools.partial(_attention, n_maps=2, dk=DF_DK, dv=DF_DV, n_out=n_rows,
                                   lam=diff_lam[l], g=dg, lam_init=lam_init)
        ya = attn_a(qta, k_a, vta, tq=ATTN_TQ_MLA, lq=n_lat)
        yd = attn_d(qtd4, k_d, vtd, tq=ATTN_TQ_DIFF, lq=n_lat)
        if need_ctx:
            ctx_tile = lambda v: v[:, -1:, :, v.shape[3] - n_ctx:]
            ya = attn_a(qta, k_a[:, n_lat:], ctx_tile(vta), tq=n_ctx, lq=n_ctx, q_start=n_lat, out=ya)
            yd = attn_d(qtd4, k_d[:, n_lat:], ctx_tile(vtd), tq=n_ctx, lq=n_ctx, q_start=n_lat, out=yd)

        mod_m = jnp.stack([jnp.broadcast_to(norm_g[l, 1][None], (2, D_MODEL)), ga1,
                           norm_g[l, 2][None] * (1.0 + sc2), sh2], axis=1)
        rw = jnp.pad(router_w[l], ((0, 0), (0, LANE - N_EXPERTS)))
        rb = jnp.pad(router_b[l].reshape(1, N_EXPERTS), ((0, 0), (0, LANE - N_EXPERTS)))
        xnew, h2, logits = _merge(n_rows, nlb, x_all, hf, hb, mo, ya, yd, gl,
                                  m_norm_g[l].reshape(1, -1), w_branch[l].astype(bf16), w_out[l].astype(bf16),
                                  mod_m, rw, rb)
        picked, wts = _moe(h2, logits, l, w_gu, b_gu, w_down, b_down)
        mod_r = jnp.stack([jnp.broadcast_to(norm_g[l, 3][None], (2, D_MODEL)), ga2], axis=1)
        x_all = _resid(nlb, xnew, picked, wts, mod_r)
    return x_all[:n_lat][None]
```

```python
import functools
import math

import jax
import jax.numpy as jnp
from jax import lax
from jax.experimental import pallas as pl
from jax.experimental.pallas import tpu as pltpu

f32 = jnp.float32
bf16 = jnp.bfloat16

D_MODEL = 1024
DEPTH = 2
GRID_W = 64
EPS = 1e-6
ROPE_BASE = 10000.0
M_HEADS = 4
M_DK = 128
M_DV = 128
GATE_SOFTCAP = 15.0
A_HEADS = 8
A_NOPE = 64
A_ROPE = 32
A_V = 64
Q_LORA = 256
KV_LORA = 128
DF_HEADS = 4
DF_DK = 64
DF_DV = 128
N_BRANCH = 3
BRANCH_W = 512
N_EXPERTS = 32
TOP_K = 4
D_EXPERT = 1024
SWIGLU_LIMIT = 7.0
SWIGLU_ALPHA = 1.702

IN_SIZES = (M_HEADS * M_DK, M_HEADS * M_DK, M_HEADS * M_DV, M_HEADS * M_DV, 4 * M_HEADS,
            Q_LORA, KV_LORA, A_ROPE,
            2 * DF_HEADS * DF_DK, 2 * DF_HEADS * DF_DK, DF_HEADS * DF_DV,
            N_BRANCH * D_MODEL)

LOG2E = 1.4426950408889634
LANE = 128
A_DKP = 128
A_DVP = A_V + 16
DF_DVP = DF_DV + 16
VMEM_LIMIT = 56 * 1024 * 1024

ROW_BLOCK = 256
MXU_DEPTH = 256
ATTN_TQ_MLA = 2048
ATTN_TQ_DIFF = 1024
ATTN_QSUB = 256
ATTN_TK = 1280
MLSTM_T = 256
MOE_TM = 512
H2_ROWS = 32768

NT_DIMS = (((1,), (1,)), ((), ()))


def _cparams(sem):
    return pltpu.CompilerParams(dimension_semantics=sem, vmem_limit_bytes=VMEM_LIMIT)


def _const_spec(shape):
    nd = len(shape)
    return pl.BlockSpec(shape, lambda *_: (0,) * nd)


def _adaln_kernel(cond_ref, w_ref, b_ref, o_ref):
    cnd = cond_ref[...]
    a = cnd * jax.nn.sigmoid(cnd)
    o_ref[...] = jnp.dot(a, w_ref[...], preferred_element_type=f32,
                         precision=lax.Precision.HIGHEST) + b_ref[...]


def _adaln(cond, ada_w, ada_b):
    tn = 1536
    nj = (6 * D_MODEL) // tn
    return pl.pallas_call(
        _adaln_kernel,
        grid=(DEPTH, nj),
        in_specs=[pl.BlockSpec((8, D_MODEL), lambda l, j: (0, 0)),
                  pl.BlockSpec((None, D_MODEL, tn), lambda l, j: (l, 0, j)),
                  pl.BlockSpec((None, 1, tn), lambda l, j: (l, 0, j))],
        out_specs=pl.BlockSpec((None, 8, tn), lambda l, j: (l, 0, j)),
        out_shape=jax.ShapeDtypeStruct((DEPTH, 8, 6 * D_MODEL), f32),
        compiler_params=_cparams(("parallel", "parallel")),
        name="adaln",
    )(cond, ada_w, ada_b.reshape(DEPTH, 1, 6 * D_MODEL))


TOK_W = 3 * 512 + LANE + N_BRANCH * D_MODEL
TR_W = 512 + 16 + Q_LORA + KV_LORA + A_ROPE + 3 * 512


def _ones_row_block(tm):
    r = lax.broadcasted_iota(jnp.int32, (16, tm), 0)
    return jnp.where(r == 0, 1.0, 0.0).astype(bf16)


def _inproj_kernel(x_ref, ab_ref, wtok_ref, wtr_ref, wuq_ref, wukv_ref, gq_ref, gkv_ref,
                   ca_ref, sa_ref, cd_ref, sd_ref,
                   mq_ref, mv_ref, mo_ref, mg_ref, gl_ref, mkt_ref, mgt_ref,
                   qta_ref, kta_ref, vta_ref, qtd_ref, ktd_ref, vtd_ref):
    tm = x_ref.shape[0]
    x = x_ref[...]
    ms = jnp.mean(x * x, axis=-1, keepdims=True)
    h = (x * lax.rsqrt(ms + EPS)) * ab_ref[0:1, :] + ab_ref[1:2, :]
    hb = h.astype(bf16)

    tok = jnp.dot(hb, wtok_ref[...], preferred_element_type=f32)
    mq_ref[...] = (tok[:, 0:512] * (M_DK ** -0.5)).astype(bf16)
    mv_ref[...] = tok[:, 512:1024].astype(bf16)
    mo_ref[...] = tok[:, 1024:1536].astype(bf16)
    mg_ref[...] = tok[:, 1536:1664]
    gl_ref[...] = tok[:, 1664:TOK_W].astype(bf16)

    tr = lax.dot_general(wtr_ref[...], hb, NT_DIMS, preferred_element_type=f32)
    mkt_ref[...] = tr[0:512].astype(bf16)
    mgt_ref[...] = tr[512:528]
    cq = tr[528:784]
    ckv = tr[784:912]
    kpe = tr[912:944]
    dq = tr[944:1456]
    dk = tr[1456:1968]
    dv = tr[1968:2480]

    ca = ca_ref[...]
    sa = sa_ref[...]
    cd = cd_ref[...]
    sd = sd_ref[...]
    ones_blk = _ones_row_block(tm)
    zeros32 = jnp.zeros((32, tm), bf16)

    cqn = (cq * lax.rsqrt(jnp.mean(cq * cq, axis=0, keepdims=True) + EPS) * gq_ref[...]).astype(bf16)
    qscale = ((A_NOPE + A_ROPE) ** -0.5) * LOG2E
    qt = jnp.dot(wuq_ref[...], cqn, preferred_element_type=f32) * qscale
    for hh in range(A_HEADS):
        a = qt[512 + 16 * hh:528 + 16 * hh]
        b = qt[640 + 16 * hh:656 + 16 * hh]
        qta_ref[hh, 0:64, :] = qt[64 * hh:64 * hh + 64].astype(bf16)
        qta_ref[hh, 64:80, :] = (a * ca - b * sa).astype(bf16)
        qta_ref[hh, 80:96, :] = (a * sa + b * ca).astype(bf16)
        qta_ref[hh, 96:128, :] = zeros32

    ckvn = (ckv * lax.rsqrt(jnp.mean(ckv * ckv, axis=0, keepdims=True) + EPS) * gkv_ref[...]).astype(bf16)
    kvt = jnp.dot(wukv_ref[...], ckvn, preferred_element_type=f32)
    ka = kpe[0:16]
    kb = kpe[16:32]
    kpe_rot = jnp.concatenate([ka * ca - kb * sa, ka * sa + kb * ca, jnp.zeros((32, tm), f32)], axis=0)
    for hh in range(A_HEADS):
        kt_h = jnp.concatenate([kvt[64 * hh:64 * hh + 64], kpe_rot], axis=0)
        kta_ref[hh] = kt_h.T.astype(bf16)
        vta_ref[hh, 0:64, :] = kvt[512 + 64 * hh:576 + 64 * hh].astype(bf16)
        vta_ref[hh, 64:80, :] = ones_blk

    dscale = (DF_DK ** -0.5) * LOG2E
    for hh in range(2 * DF_HEADS):
        a = dq[32 * hh:32 * hh + 32]
        b = dq[256 + 32 * hh:288 + 32 * hh]
        qtd_ref[hh, 0:32, :] = ((a * cd - b * sd) * dscale).astype(bf16)
        qtd_ref[hh, 32:64, :] = ((a * sd + b * cd) * dscale).astype(bf16)
        a = dk[32 * hh:32 * hh + 32]
        b = dk[256 + 32 * hh:288 + 32 * hh]
        ktd_ref[hh] = jnp.concatenate([a * cd - b * sd, a * sd + b * cd], axis=0).T.astype(bf16)
    for hh in range(DF_HEADS):
        vtd_ref[hh, 0:128, :] = dv[128 * hh:128 * hh + 128].astype(bf16)
        vtd_ref[hh, 128:144, :] = ones_blk


def _pack_inproj_weights(w, w_uq, w_ukv):
    o = [0]
    for s in IN_SIZES:
        o.append(o[-1] + s)
    mq, mk, mv, mo, mg, cq, ckv, kpe, dq, dk, dv, gl = [w[:, o[i]:o[i + 1]] for i in range(12)]
    mg_pad = jnp.pad(mg, ((0, 0), (0, LANE - mg.shape[1])))
    w_tok = jnp.concatenate([mq, mv, mo, mg_pad, gl], axis=1).astype(bf16)

    def split_halves(t, n_heads, d):
        t3 = t.reshape(t.shape[0], n_heads, d)
        return jnp.concatenate([t3[:, :, :d // 2].reshape(t.shape[0], -1),
                                t3[:, :, d // 2:].reshape(t.shape[0], -1)], axis=1)

    w_tr = jnp.concatenate([mk, mg, cq, ckv, kpe, split_halves(dq, 2 * DF_HEADS, DF_DK),
                            split_halves(dk, 2 * DF_HEADS, DF_DK), dv], axis=1).T.astype(bf16)
    uq = w_uq.reshape(Q_LORA, A_HEADS, A_NOPE + A_ROPE)
    half = A_ROPE // 2
    wt_uq = jnp.concatenate([uq[:, :, :A_NOPE].reshape(Q_LORA, -1),
                             uq[:, :, A_NOPE:A_NOPE + half].reshape(Q_LORA, -1),
                             uq[:, :, A_NOPE + half:].reshape(Q_LORA, -1)], axis=1).T.astype(bf16)
    ukv = w_ukv.reshape(KV_LORA, A_HEADS, A_NOPE + A_V)
    wt_ukv = jnp.concatenate([ukv[:, :, :A_NOPE].reshape(KV_LORA, -1),
                              ukv[:, :, A_NOPE:].reshape(KV_LORA, -1)], axis=1).T.astype(bf16)
    return w_tok, w_tr, wt_uq, wt_ukv


def _inproj(x_all, ab, n_lat_blocks, w_tok, w_tr, wt_uq, wt_ukv, gq, gkv, ca, sa, cd, sd):
    n = x_all.shape[0]
    tm = ROW_BLOCK
    tk = _key_tile(n)
    per = tk // tm
    vtile = lambda i: (0, i // per, 0, i % per)
    nb = n // tm
    row = lambda i: (i, 0)
    colb = lambda i: (0, i)
    col3 = lambda i: (0, 0, i)
    grp = lambda i: (jnp.where(i >= n_lat_blocks, 1, 0), 0, 0)
    out_shapes = [
        jax.ShapeDtypeStruct((n, 512), bf16), jax.ShapeDtypeStruct((n, 512), bf16),
        jax.ShapeDtypeStruct((n, 512), bf16), jax.ShapeDtypeStruct((n, LANE), f32),
        jax.ShapeDtypeStruct((n, N_BRANCH * D_MODEL), bf16),
        jax.ShapeDtypeStruct((512, n), bf16), jax.ShapeDtypeStruct((16, n), f32),
        jax.ShapeDtypeStruct((A_HEADS, A_DKP, n), bf16), jax.ShapeDtypeStruct((A_HEADS, n, A_DKP), bf16),
        jax.ShapeDtypeStruct((A_HEADS, n // tk, A_DVP, tk), bf16),
        jax.ShapeDtypeStruct((2 * DF_HEADS, DF_DK, n), bf16), jax.ShapeDtypeStruct((2 * DF_HEADS, n, DF_DK), bf16),
        jax.ShapeDtypeStruct((DF_HEADS, n // tk, DF_DVP, tk), bf16),
    ]
    out_specs = [
        pl.BlockSpec((tm, 512), row), pl.BlockSpec((tm, 512), row), pl.BlockSpec((tm, 512), row),
        pl.BlockSpec((tm, LANE), row), pl.BlockSpec((tm, N_BRANCH * D_MODEL), row),
        pl.BlockSpec((512, tm), colb), pl.BlockSpec((16, tm), colb),
        pl.BlockSpec((A_HEADS, A_DKP, tm), col3), pl.BlockSpec((A_HEADS, tm, A_DKP), lambda i: (0, i, 0)),
        pl.BlockSpec((A_HEADS, None, A_DVP, tm), vtile),
        pl.BlockSpec((2 * DF_HEADS, DF_DK, tm), col3), pl.BlockSpec((2 * DF_HEADS, tm, DF_DK), lambda i: (0, i, 0)),
        pl.BlockSpec((DF_HEADS, None, DF_DVP, tm), vtile),
    ]
    in_specs = [
        pl.BlockSpec((tm, D_MODEL), row),
        pl.BlockSpec((None, 2, D_MODEL), grp),
        _const_spec(w_tok.shape), _const_spec(w_tr.shape), _const_spec(wt_uq.shape), _const_spec(wt_ukv.shape),
        _const_spec(gq.shape), _const_spec(gkv.shape),
        pl.BlockSpec((A_ROPE // 2, tm), colb), pl.BlockSpec((A_ROPE // 2, tm), colb),
        pl.BlockSpec((DF_DK // 2, tm), colb), pl.BlockSpec((DF_DK // 2, tm), colb),
    ]
    return pl.pallas_call(
        _inproj_kernel, grid=(nb,), in_specs=in_specs, out_specs=out_specs, out_shape=out_shapes,
        compiler_params=_cparams(("parallel",)), name="inproj",
    )(x_all, ab, w_tok, w_tr, wt_uq, wt_ukv, gq, gkv, ca, sa, cd, sd)


def _attn_kernel(*refs, n_maps, dk, dv, tk, n_kt, qsub, lam_init, n_in):
    o_ref = refs[n_in]
    if n_maps == 2:
        lam_ref, g_ref, qt_ref, k_ref, vt_ref = refs[:5]
    else:
        qt_ref, k_ref, vt_ref = refs[:3]
    tq = qt_ref.shape[-1]
    dvp = vt_ref.shape[-2]
    chains = [(m, c) for m in range(n_maps) for c in range(tq // qsub)]

    def qk(j, m, c):
        qt = qt_ref[m * dk:(m + 1) * dk, c * qsub:(c + 1) * qsub]
        return jnp.dot(k_ref[m, j * tk:(j + 1) * tk, :], qt, preferred_element_type=f32)

    state = [(jnp.full((1, qsub), -jnp.inf, f32), jnp.zeros((dvp, qsub), f32)) for _ in chains]
    s_cur = [qk(0, m, c) for (m, c) in chains]
    for j in range(n_kt):
        vtj = vt_ref[j]
        for i, (m, c) in enumerate(chains):
            s_nxt = qk(j + 1, m, c) if j + 1 < n_kt else None
            m_run, acc = state[i]
            m_new = jnp.maximum(m_run, jnp.max(s_cur[i], axis=0, keepdims=True))
            alpha = jnp.exp2(m_run - m_new)
            p = jnp.exp2(s_cur[i] - m_new).astype(bf16)
            pv = jnp.dot(vtj, p, preferred_element_type=f32)
            state[i] = (m_new, alpha * acc + pv)
            s_cur[i] = s_nxt
    outs = []
    for m in range(n_maps):
        accs = [state[i][1] for i, (mm, _) in enumerate(chains) if mm == m]
        acc = accs[0] if len(accs) == 1 else jnp.concatenate(accs, axis=1)
        outs.append(acc[0:dv] / acc[dv:dv + 1])
    if n_maps == 1:
        o_ref[...] = outs[0].astype(o_ref.dtype)
    else:
        lf = lam_ref[...]
        lam = (jnp.exp(jnp.sum(lf[0:1] * lf[1:2], axis=1, keepdims=True))
               - jnp.exp(jnp.sum(lf[2:3] * lf[3:4], axis=1, keepdims=True)) + lam_init)
        o = outs[0] - lam * outs[1]
        o = o * lax.rsqrt(jnp.mean(o * o, axis=0, keepdims=True) + EPS) * g_ref[...]
        o_ref[...] = (o * (1.0 - lam_init)).astype(o_ref.dtype)


def _key_tile(lk):
    return max(t for t in range(MXU_DEPTH, min(ATTN_TK, lk) + 1, MXU_DEPTH) if lk % t == 0)


def _attention(qt, k, vt4, *, n_maps, dk, dv, tq, lq, n_out, q_start=0, out=None, lam=None, g=None, lam_init=0.0):
    n_heads = qt.shape[0]
    lk = k.shape[1]
    _, n_kt, dvp, tk = vt4.shape
    assert n_kt * tk == lk
    tq = min(tq, lq)
    qsub = min(ATTN_QSUB, tq)
    qb0 = q_start // tq
    in_specs = [
        pl.BlockSpec((None, n_maps * dk, tq), lambda h, i: (h, 0, qb0 + i)),
        pl.BlockSpec((n_maps, lk, dk), lambda h, i: (h, 0, 0)),
        pl.BlockSpec((None, n_kt, dvp, tk), lambda h, i: (h, 0, 0, 0)),
    ]
    args = [qt, k, vt4]
    if n_maps == 2:
        in_specs = [_const_spec(lam.shape), pl.BlockSpec((None, dv, 1), lambda h, i: (h, 0, 0))] + in_specs
        args = [lam, g] + args
    aliases = {}
    if out is not None:
        aliases = {len(args): 0}
        in_specs.append(pl.BlockSpec(memory_space=pl.ANY))
        args.append(out)
    kern = functools.partial(_attn_kernel, n_maps=n_maps, dk=dk, dv=dv, tk=tk, n_kt=n_kt, qsub=qsub,
                             lam_init=lam_init, n_in=len(args))
    return pl.pallas_call(
        kern, grid=(n_heads, lq // tq), in_specs=in_specs,
        out_specs=pl.BlockSpec((None, dv, tq), lambda h, i: (h, 0, qb0 + i)),
        out_shape=jax.ShapeDtypeStruct((n_heads, dv, n_out), bf16),
        input_output_aliases=aliases,
        compiler_params=_cparams(("parallel", "parallel")),
        name="attn_diff" if n_maps == 2 else "attn_mla",
    )(*args)


def _split3(x):
    hi = x.astype(bf16)
    r = x - hi.astype(f32)
    mid = r.astype(bf16)
    lo = (r - mid.astype(f32)).astype(bf16)
    return hi, mid, lo


def _log_sigmoid(x):
    return jnp.minimum(x, 0.0) - jnp.log(1.0 + jnp.exp(-jnp.abs(x)))


def _softcap(x):
    return GATE_SOFTCAP * jnp.tanh(x * (1.0 / GATE_SOFTCAP))


def _mlstm_kernel(gb_ref, gbt_ref,
                  qf_ref, ktf_ref, vf_ref, gf_ref, gtf_ref,
                  qb_ref, ktb_ref, vb_ref, gb2_ref, gtb_ref,
                  hf_ref, hb_ref, c_ref, m_ref):
    t = qf_ref.shape[0]

    @pl.when(pl.program_id(0) == 0)
    def _():
        c_ref[...] = jnp.zeros_like(c_ref)
        m_ref[...] = jnp.zeros_like(m_ref)

    row = lax.broadcasted_iota(jnp.int32, (t, t), 0)
    col = lax.broadcasted_iota(jnp.int32, (t, t), 1)
    lane = lax.broadcasted_iota(jnp.int32, (t, LANE), 1)
    ones_col = jnp.where(lane == 0, 1.0, 0.0).astype(bf16)
    dirs = ((qf_ref, ktf_ref, vf_ref, gf_ref, gtf_ref, hf_ref),
            (qb_ref, ktb_ref, vb_ref, gb2_ref, gtb_ref, hb_ref))
    for d, (q_ref, kt_ref, v_ref, g_ref, gt_ref, o_ref) in enumerate(dirs):
        mask = (col <= row) if d == 0 else (col >= row)
        maskb = mask.astype(bf16)
        last = t - 1 if d == 0 else 0
        g_tok = _softcap(g_ref[...] + gb_ref[...])
        g_t = _softcap(gt_ref[...] + gbt_ref[...])
        b_tok = sum(jnp.dot(maskb, part, preferred_element_type=f32) for part in _split3(_log_sigmoid(g_tok)))
        b_t = sum(lax.dot_general(part, maskb, NT_DIMS, preferred_element_type=f32)
                  for part in _split3(_log_sigmoid(g_t)))
        for hh in range(M_HEADS):
            ri = 4 * (2 * d) + hh
            rf = 4 * (2 * d + 1) + hh
            idx = d * M_HEADS + hh
            b_col = b_tok[:, rf:rf + 1]
            b_row = b_t[rf:rf + 1, :]
            a_row = g_t[ri:ri + 1, :] - b_row
            m_prev = m_ref[idx, 0:1, 0:1]
            ld = jnp.where(mask, b_col + a_row, -jnp.inf)
            inter = b_col + m_prev
            mj = jnp.maximum(inter, jnp.max(ld, axis=1, keepdims=True))
            dmat = jnp.exp(ld - mj)
            q = q_ref[:, hh * M_DK:(hh + 1) * M_DK]
            kt = kt_ref[hh * M_DK:(hh + 1) * M_DK, :]
            vp = jnp.concatenate([v_ref[:, hh * M_DV:(hh + 1) * M_DV], ones_col], axis=1)
            sc = jnp.dot(q, kt, preferred_element_type=f32) * dmat
            cp = c_ref[idx]
            num = (jnp.dot(sc.astype(bf16), vp, preferred_element_type=f32)
                   + jnp.exp(inter - mj) * jnp.dot(q, cp.astype(bf16), preferred_element_type=f32))
            den = num[:, M_DV:M_DV + 1]
            hj = num[:, 0:M_DV] / jnp.maximum(jnp.abs(den), jnp.exp(-mj))
            o_ref[:, hh * M_DV:(hh + 1) * M_DV] = hj.astype(o_ref.dtype)
            b_tot = b_t[rf:rf + 1, last:last + 1]
            lw = b_tot + a_row
            m_new = jnp.maximum(b_tot + m_prev, jnp.max(lw, axis=1, keepdims=True))
            kw = (kt.astype(f32) * jnp.exp(lw - m_new)).astype(bf16)
            c_ref[idx] = jnp.exp(b_tot + m_prev - m_new) * cp + jnp.dot(kw, vp, preferred_element_type=f32)
            m_ref[idx] = jnp.broadcast_to(m_new, (8, LANE))


def _mlstm(mq, mkt, mv, mg, mgt, gate_b):
    n = mq.shape[0]
    t = MLSTM_T
    nb = n // t
    gb = jnp.pad(gate_b.reshape(1, 16), ((0, 0), (0, LANE - 16)))
    gbt = gate_b.reshape(16, 1)
    fwd = lambda s: jnp.where(s == 0, nb - 1, s - 1)
    bwd = lambda s: nb - 1 - s
    specs = []
    for im in (fwd, bwd):
        specs += [pl.BlockSpec((t, 512), lambda s, im=im: (im(s), 0)),
                  pl.BlockSpec((512, t), lambda s, im=im: (0, im(s))),
                  pl.BlockSpec((t, 512), lambda s, im=im: (im(s), 0)),
                  pl.BlockSpec((t, LANE), lambda s, im=im: (im(s), 0)),
                  pl.BlockSpec((16, t), lambda s, im=im: (0, im(s)))]
    return pl.pallas_call(
        _mlstm_kernel, grid=(nb,),
        in_specs=[_const_spec(gb.shape), _const_spec(gbt.shape)] + specs,
        out_specs=[pl.BlockSpec((t, 512), lambda s: (fwd(s), 0)), pl.BlockSpec((t, 512), lambda s: (bwd(s), 0))],
        out_shape=[jax.ShapeDtypeStruct((n, 512), bf16), jax.ShapeDtypeStruct((n, 512), bf16)],
        scratch_shapes=[pltpu.VMEM((2 * M_HEADS, M_DK, 2 * M_DV), f32), pltpu.VMEM((2 * M_HEADS, 8, LANE), f32)],
        compiler_params=_cparams(("arbitrary",)), name="mlstm",
    )(gb, gbt, mq, mkt, mv, mg, mgt, mq, mkt, mv, mg, mgt)


def _rms(x, axis=-1):
    return x * lax.rsqrt(jnp.mean(x * x, axis=axis, keepdims=True) + EPS)


def _merge_kernel(x_ref, hf_ref, hb_ref, mo_ref, ya_ref, yd_ref, gl_ref, mng_ref, wb_ref, wout_ref,
                  mod_ref, rw_ref, rb_ref, xnew_ref, h2_ref, logit_ref):
    hs = hf_ref[...].astype(f32) + hb_ref[...].astype(f32)
    parts = [_rms(hs[:, hh * M_DV:(hh + 1) * M_DV]) for hh in range(M_HEADS)]
    ym = jnp.concatenate(parts, axis=1) * mng_ref[...] * jax.nn.sigmoid(mo_ref[...].astype(f32))
    yb0 = jnp.dot(ym.astype(bf16), wb_ref[0], preferred_element_type=f32)
    tm = x_ref.shape[0]
    tn_dims = (((0,), (0,)), ((), ()))
    yb1 = lax.dot_general(ya_ref[...].reshape(BRANCH_W, tm), wb_ref[1], tn_dims, preferred_element_type=f32)
    yb2 = lax.dot_general(yd_ref[...].reshape(BRANCH_W, tm), wb_ref[2], tn_dims, preferred_element_type=f32)
    d = D_MODEL
    z = (jax.nn.sigmoid(gl_ref[:, 0:d].astype(f32)) * yb0
         + jax.nn.sigmoid(gl_ref[:, d:2 * d].astype(f32)) * yb1
         + jax.nn.sigmoid(gl_ref[:, 2 * d:3 * d].astype(f32)) * yb2)
    y = jnp.dot(z.astype(bf16), wout_ref[...], preferred_element_type=f32)
    xnew = x_ref[...] + mod_ref[1:2, :] * (_rms(y) * mod_ref[0:1, :])
    h2 = _rms(xnew) * mod_ref[2:3, :] + mod_ref[3:4, :]
    xnew_ref[...] = xnew
    h2_ref[...] = h2.astype(bf16)
    h2_hi = h2.astype(bf16)
    h2_lo = (h2 - h2_hi.astype(f32)).astype(bf16)
    rw = rw_ref[...]
    rw_hi = rw.astype(bf16)
    rw_lo = (rw - rw_hi.astype(f32)).astype(bf16)
    logit_ref[...] = (jnp.dot(h2_hi, rw_hi, preferred_element_type=f32)
                      + jnp.dot(h2_lo, rw_hi, preferred_element_type=f32)
                      + jnp.dot(h2_hi, rw_lo, preferred_element_type=f32) + rb_ref[...])


def _merge(n_rows, n_lat_blocks, x_all, hf, hb, mo, ya, yd, gl, mng, wb, wout, mod, rw, rb):
    tm = ROW_BLOCK
    nb = n_rows // tm
    row = lambda i: (i, 0)
    grp = lambda i: (jnp.where(i >= n_lat_blocks, 1, 0), 0, 0)
    in_specs = [pl.BlockSpec((tm, D_MODEL), row), pl.BlockSpec((tm, 512), row), pl.BlockSpec((tm, 512), row),
                pl.BlockSpec((tm, 512), row),
                pl.BlockSpec((A_HEADS, A_V, tm), lambda i: (0, 0, i)),
                pl.BlockSpec((DF_HEADS, DF_DV, tm), lambda i: (0, 0, i)),
                pl.BlockSpec((tm, N_BRANCH * D_MODEL), row), _const_spec(mng.shape), _const_spec(wb.shape),
                _const_spec(wout.shape), pl.BlockSpec((None, 4, D_MODEL), grp),
                _const_spec(rw.shape), _const_spec(rb.shape)]
    return pl.pallas_call(
        _merge_kernel, grid=(nb,), in_specs=in_specs,
        out_specs=[pl.BlockSpec((tm, D_MODEL), row), pl.BlockSpec((tm, D_MODEL), row), pl.BlockSpec((tm, LANE), row)],
        out_shape=[jax.ShapeDtypeStruct((n_rows, D_MODEL), f32),
                   jax.ShapeDtypeStruct((max(n_rows, H2_ROWS), D_MODEL), bf16),
                   jax.ShapeDtypeStruct((n_rows, LANE), f32)],
        compiler_params=_cparams(("parallel",)), name="merge",
    )(x_all, hf, hb, mo, ya, yd, gl, mng, wb, wout, mod, rw, rb)


def _moe_kernel(be_ref, nu_ref, xs_ref, wgu_ref, bgu_ref, wd_ref, bd_ref, o_ref, wgu_s, wd_s):
    i = pl.program_id(0)
    prev = be_ref[jnp.maximum(i - 1, 0)]
    used = i < nu_ref[0]

    @pl.when(jnp.logical_and(used, jnp.logical_or(i == 0, be_ref[i] != prev)))
    def _():
        wgu_s[...] = wgu_ref[...].astype(bf16)
        wd_s[...] = wd_ref[...].astype(bf16)

    @pl.when(used)
    def _():
        gu = jnp.dot(xs_ref[...], wgu_s[...], preferred_element_type=f32) + bgu_ref[...]
        gate = jnp.minimum(gu[:, :D_EXPERT], SWIGLU_LIMIT)
        up = jnp.clip(gu[:, D_EXPERT:], -SWIGLU_LIMIT, SWIGLU_LIMIT)
        act = (up + 1.0) * (gate * jax.nn.sigmoid(SWIGLU_ALPHA * gate))
        y = jnp.dot(act.astype(bf16), wd_s[...], preferred_element_type=f32) + bd_ref[...]
        o_ref[...] = y.astype(o_ref.dtype)

    @pl.when(jnp.logical_not(used))
    def _():
        o_ref[...] = jnp.zeros_like(o_ref)


def _moe_experts(blk_e, n_used, xs, layer, w_gu, b_gu, w_down, b_down):
    cap = xs.shape[0]
    tm = MOE_TM
    nblk = cap // tm
    grid_spec = pltpu.PrefetchScalarGridSpec(
        num_scalar_prefetch=2, grid=(nblk,),
        in_specs=[pl.BlockSpec((tm, D_MODEL), lambda i, be, nu: (i, 0)),
                  pl.BlockSpec((None, None, D_MODEL, 2 * D_EXPERT), lambda i, be, nu: (layer, be[i], 0, 0)),
                  pl.BlockSpec((None, None, 1, 2 * D_EXPERT), lambda i, be, nu: (layer, be[i], 0, 0)),
                  pl.BlockSpec((None, None, D_EXPERT, D_MODEL), lambda i, be, nu: (layer, be[i], 0, 0)),
                  pl.BlockSpec((None, None, 1, D_MODEL), lambda i, be, nu: (layer, be[i], 0, 0))],
        out_specs=pl.BlockSpec((tm, D_MODEL), lambda i, be, nu: (i, 0)),
        scratch_shapes=[pltpu.VMEM((D_MODEL, 2 * D_EXPERT), bf16), pltpu.VMEM((D_EXPERT, D_MODEL), bf16)])
    return pl.pallas_call(
        _moe_kernel, grid_spec=grid_spec,
        out_shape=jax.ShapeDtypeStruct((cap, D_MODEL), bf16),
        compiler_params=_cparams(("arbitrary",)), name="moe_experts",
    )(blk_e, n_used, xs, w_gu, b_gu.reshape(DEPTH, N_EXPERTS, 1, -1), w_down, b_down.reshape(DEPTH, N_EXPERTS, 1, -1))


def _moe(h2, logits, layer, w_gu, b_gu, w_down, b_down):
    n = logits.shape[0]
    tm = MOE_TM
    i32 = jnp.int32
    top_v, top_i = lax.top_k(logits[:, :N_EXPERTS], TOP_K)
    wts = jax.nn.softmax(top_v, axis=-1)
    nk = n * TOP_K
    flat_e = top_i.reshape(nk).astype(i32)
    iota = jnp.arange(nk, dtype=i32)
    skey = lax.sort(flat_e * nk + iota)
    se = skey // nk
    order = skey - se * nk
    e_ids = jnp.arange(N_EXPERTS, dtype=i32)
    grp_start = jnp.sum((se[:, None] < e_ids[None, :]).astype(i32), axis=0)
    counts = jnp.sum((se[:, None] == e_ids[None, :]).astype(i32), axis=0)
    padded = (counts + tm - 1) // tm * tm
    pad_end = jnp.cumsum(padded)
    pad_start = pad_end - padded
    off = pad_start - grp_start
    step = off - jnp.concatenate([jnp.zeros((1,), i32), off[:-1]])
    dest_sorted = iota + jnp.sum(jnp.where(iota[:, None] >= grp_start[None, :], step[None, :], 0), axis=1)
    _, pos_km = lax.sort(((order % TOP_K) * n + order // TOP_K, dest_sorted), num_keys=1)
    cap = -(-(nk + N_EXPERTS * (tm - 1)) // tm) * tm
    nblk = cap // tm
    bstart = jnp.arange(nblk, dtype=i32) * tm
    blk_e = jnp.minimum(jnp.sum((pad_end[None, :] <= bstart[:, None]).astype(i32), axis=1), N_EXPERTS - 1)
    n_used = (pad_end[-1] // tm).astype(i32).reshape(1)
    sel = blk_e[:, None] == e_ids[None, :]
    off_b = jnp.sum(jnp.where(sel, off[None, :], 0), axis=1)
    end_b = jnp.sum(jnp.where(sel, (pad_start + counts)[None, :], 0), axis=1)
    p2 = bstart[:, None] + jnp.arange(tm, dtype=i32)[None, :]
    src = jnp.clip(p2 - off_b[:, None], 0, nk - 1)
    slot_t = jnp.where(p2 < end_b[:, None], (order // TOP_K)[src], p2 % n).reshape(cap)
    xs = h2[slot_t]
    ys = _moe_experts(blk_e, n_used, xs, layer, w_gu, b_gu, w_down, b_down)
    return ys[pos_km].reshape(TOP_K, n, D_MODEL), wts


def _resid_kernel(x_ref, y_ref, w_ref, mod_ref, o_ref):
    w = w_ref[...]
    f = y_ref[0].astype(f32) * w[:, 0:1]
    for kk in range(1, TOP_K):
        f = f + y_ref[kk].astype(f32) * w[:, kk:kk + 1]
    o_ref[...] = x_ref[...] + mod_ref[1:2, :] * (_rms(f) * mod_ref[0:1, :])


def _resid(n_lat_blocks, xnew, picked, wts, mod):
    n = xnew.shape[0]
    tm = ROW_BLOCK
    row = lambda i: (i, 0)
    grp = lambda i: (jnp.where(i >= n_lat_blocks, 1, 0), 0, 0)
    return pl.pallas_call(
        _resid_kernel, grid=(n // tm,),
        in_specs=[pl.BlockSpec((tm, D_MODEL), row), pl.BlockSpec((TOP_K, tm, D_MODEL), lambda i: (0, i, 0)),
                  pl.BlockSpec((tm, TOP_K), row), pl.BlockSpec((None, 2, D_MODEL), grp)],
        out_specs=pl.BlockSpec((tm, D_MODEL), row),
        out_shape=jax.ShapeDtypeStruct((n, D_MODEL), f32),
        compiler_params=_cparams(("parallel",)), name="resid",
    )(xnew, picked, wts, mod)


def _rope_tables(n_lat, n_ctx, dim):
    rows = n_lat // GRID_W
    row = jnp.repeat(jnp.arange(rows, dtype=f32), GRID_W)
    col = jnp.tile(jnp.arange(GRID_W, dtype=f32), rows)
    quarter = dim // 4
    inv = ROPE_BASE ** (-jnp.arange(quarter, dtype=f32) / quarter)
    ang = jnp.concatenate([row[:, None] * inv, col[:, None] * inv], axis=-1)
    cos = jnp.concatenate([jnp.cos(ang), jnp.ones((n_ctx, dim // 2), f32)], axis=0).T
    sin = jnp.concatenate([jnp.sin(ang), jnp.zeros((n_ctx, dim // 2), f32)], axis=0).T
    return cos, sin


def kernel(x, c, ctx, c_ctx, ada_w, ada_b, norm_g, w_in, m_gate_b, m_norm_g, q_norm_g, w_uq, kv_norm_g, w_ukv,
           diff_lam, diff_norm_g, w_branch, w_out, router_w, router_b, w_gu, b_gu, w_down, b_down):
    n_lat = x.shape[1]
    n_ctx = ctx.shape[1]
    n = n_lat + n_ctx
    assert x.shape[0] == 1 and n_lat % ROW_BLOCK == 0 and n_ctx == MLSTM_T and n_lat % GRID_W == 0
    nlb = n_lat // ROW_BLOCK
    x_all = jnp.concatenate([x[0], ctx[0]], axis=0)

    cond = jnp.zeros((8, D_MODEL), f32).at[0].set(c[0]).at[1].set(c_ctx)
    mods = _adaln(cond, ada_w, ada_b)[:, :2].reshape(DEPTH, 2, 6, D_MODEL)

    ca, sa = _rope_tables(n_lat, n_ctx, A_ROPE)
    cd, sd = _rope_tables(n_lat, n_ctx, DF_DK)

    for l in range(DEPTH):
        need_ctx = l < DEPTH - 1
        lam_init = 0.8 - 0.6 * math.exp(-0.3 * l)
        sh1, sc1, ga1, sh2, sc2, ga2 = [mods[l, :, j] for j in range(6)]
        ab1 = jnp.stack([norm_g[l, 0][None] * (1.0 + sc1), sh1], axis=1)
        w_tok, w_tr, wt_uq, wt_ukv = _pack_inproj_weights(w_in[l], w_uq[l], w_ukv[l])
        (mq, mv, mo, mg, gl, mkt, mgt, qta, k_a, vta, qtd, k_d, vtd) = _inproj(
            x_all, ab1, nlb, w_tok, w_tr, wt_uq, wt_ukv,
            q_norm_g[l].reshape(Q_LORA, 1), kv_norm_g[l].reshape(KV_LORA, 1), ca, sa, cd, sd)

        hf, hb = _mlstm(mq, mkt, mv, mg, mgt, m_gate_b[l])

        qtd4 = qtd.reshape(DF_HEADS, 2 * DF_DK, n)
        dg = diff_norm_g[l].reshape(DF_HEADS, DF_DV, 1)
        n_rows = n if need_ctx else n_lat
        attn_a = functools.partial(_attention, n_maps=1, dk=A_DKP, dv=A_V, n_out=n_rows)
        attn_d = functools.partial(_attention, n_maps=2, dk=DF_DK, dv=DF_DV, n_out=n_rows,
                                   lam=diff_lam[l], g=dg, lam_init=lam_init)
        ya = attn_a(qta, k_a, vta, tq=ATTN_TQ_MLA, lq=n_lat)
        yd = attn_d(qtd4, k_d, vtd, tq=ATTN_TQ_DIFF, lq=n_lat)
        if need_ctx:
            ctx_tile = lambda v: v[:, -1:, :, v.shape[3] - n_ctx:]
            ya = attn_a(qta, k_a[:, n_lat:], ctx_tile(vta), tq=n_ctx, lq=n_ctx, q_start=n_lat, out=ya)
            yd = attn_d(qtd4, k_d[:, n_lat:], ctx_tile(vtd), tq=n_ctx, lq=n_ctx, q_start=n_lat, out=yd)

        mod_m = jnp.stack([jnp.broadcast_to(norm_g[l, 1][None], (2, D_MODEL)), ga1,
                           norm_g[l, 2][None] * (1.0 + sc2), sh2], axis=1)
        rw = jnp.pad(router_w[l], ((0, 0), (0, LANE - N_EXPERTS)))
        rb = jnp.pad(router_b[l].reshape(1, N_EXPERTS), ((0, 0), (0, LANE - N_EXPERTS)))
        xnew, h2, logits = _merge(n_rows, nlb, x_all, hf, hb, mo, ya, yd, gl,
                                  m_norm_g[l].reshape(1, -1), w_branch[l].astype(bf16), w_out[l].astype(bf16),
                                  mod_m, rw, rb)
        picked, wts = _moe(h2, logits, l, w_gu, b_gu, w_down, b_down)
        mod_r = jnp.stack([jnp.broadcast_to(norm_g[l, 3][None], (2, D_MODEL)), ga2], axis=1)
        x_all = _resid(nlb, xnew, picked, wts, mod_r)
    return x_all[:n_lat][None]
```

```python
import functools
import math

import jax
import jax.numpy as jnp
from jax import lax
from jax.experimental import pallas as pl
from jax.experimental.pallas import tpu as pltpu

f32 = jnp.float32
bf16 = jnp.bfloat16

D_MODEL = 1024
DEPTH = 2
GRID_W = 64
EPS = 1e-6
ROPE_BASE = 10000.0
M_HEADS = 4
M_DK = 128
M_DV = 128
GATE_SOFTCAP = 15.0
A_HEADS = 8
A_NOPE = 64
A_ROPE = 32
A_V = 64
Q_LORA = 256
KV_LORA = 128
DF_HEADS = 4
DF_DK = 64
DF_DV = 128
N_BRANCH = 3
BRANCH_W = 512
N_EXPERTS = 32
TOP_K = 4
D_EXPERT = 1024
SWIGLU_LIMIT = 7.0
SWIGLU_ALPHA = 1.702

IN_SIZES = (M_HEADS * M_DK, M_HEADS * M_DK, M_HEADS * M_DV, M_HEADS * M_DV, 4 * M_HEADS,
            Q_LORA, KV_LORA, A_ROPE,
            2 * DF_HEADS * DF_DK, 2 * DF_HEADS * DF_DK, DF_HEADS * DF_DV,
            N_BRANCH * D_MODEL)

LOG2E = 1.4426950408889634
LANE = 128
A_DKP = 128
A_DVP = A_V + 16
DF_DVP = DF_DV + 16
VMEM_LIMIT = 56 * 1024 * 1024

ROW_BLOCK = 256
MXU_DEPTH = 256
ATTN_TQ_MLA = 2048
ATTN_TQ_DIFF = 1024
ATTN_QSUB = 256
ATTN_TK = 1280
MLSTM_T = 256
MOE_TM = 512
H2_ROWS = 32768

NT_DIMS = (((1,), (1,)), ((), ()))


def _cparams(sem):
    return pltpu.CompilerParams(dimension_semantics=sem, vmem_limit_bytes=VMEM_LIMIT)


def _const_spec(shape):
    nd = len(shape)
    return pl.BlockSpec(shape, lambda *_: (0,) * nd)


def _adaln_kernel(cond_ref, w_ref, b_ref, o_ref):
    cnd = cond_ref[...]
    a = cnd * jax.nn.sigmoid(cnd)
    o_ref[...] = jnp.dot(a, w_ref[...], preferred_element_type=f32,
                         precision=lax.Precision.HIGHEST) + b_ref[...]


def _adaln(cond, ada_w, ada_b):
    tn = 1536
    nj = (6 * D_MODEL) // tn
    return pl.pallas_call(
        _adaln_kernel,
        grid=(DEPTH, nj),
        in_specs=[pl.BlockSpec((8, D_MODEL), lambda l, j: (0, 0)),
                  pl.BlockSpec((None, D_MODEL, tn), lambda l, j: (l, 0, j)),
                  pl.BlockSpec((None, 1, tn), lambda l, j: (l, 0, j))],
        out_specs=pl.BlockSpec((None, 8, tn), lambda l, j: (l, 0, j)),
        out_shape=jax.ShapeDtypeStruct((DEPTH, 8, 6 * D_MODEL), f32),
        compiler_params=_cparams(("parallel", "parallel")),
        name="adaln",
    )(cond, ada_w, ada_b.reshape(DEPTH, 1, 6 * D_MODEL))


TOK_W = 3 * 512 + LANE + N_BRANCH * D_MODEL
TR_W = 512 + 16 + Q_LORA + KV_LORA + A_ROPE + 3 * 512


def _ones_row_block(tm):
    r = lax.broadcasted_iota(jnp.int32, (16, tm), 0)
    return jnp.where(r == 0, 1.0, 0.0).astype(bf16)


def _inproj_kernel(x_ref, ab_ref, wtok_ref, wtr_ref, wuq_ref, wukv_ref, gq_ref, gkv_ref,
                   ca_ref, sa_ref, cd_ref, sd_ref,
                   mq_ref, mv_ref, mo_ref, mg_ref, gl_ref, mkt_ref, mgt_ref,
                   qta_ref, kta_ref, vta_ref, qtd_ref, ktd_ref, vtd_ref):
    tm = x_ref.shape[0]
    x = x_ref[...]
    ms = jnp.mean(x * x, axis=-1, keepdims=True)
    h = (x * lax.rsqrt(ms + EPS)) * ab_ref[0:1, :] + ab_ref[1:2, :]
    hb = h.astype(bf16)

    tok = jnp.dot(hb, wtok_ref[...], preferred_element_type=f32)
    mq_ref[...] = (tok[:, 0:512] * (M_DK ** -0.5)).astype(bf16)
    mv_ref[...] = tok[:, 512:1024].astype(bf16)
    mo_ref[...] = tok[:, 1024:1536].astype(bf16)
    mg_ref[...] = tok[:, 1536:1664]
    gl_ref[...] = tok[:, 1664:TOK_W].astype(bf16)

    tr = lax.dot_general(wtr_ref[...], hb, NT_DIMS, preferred_element_type=f32)
    mkt_ref[...] = tr[0:512].astype(bf16)
    mgt_ref[...] = tr[512:528]
    cq = tr[528:784]
    ckv = tr[784:912]
    kpe = tr[912:944]
    dq = tr[944:1456]
    dk = tr[1456:1968]
    dv = tr[1968:2480]

    ca = ca_ref[...]
    sa = sa_ref[...]
    cd = cd_ref[...]
    sd = sd_ref[...]
    ones_blk = _ones_row_block(tm)
    zeros32 = jnp.zeros((32, tm), bf16)

    cqn = (cq * lax.rsqrt(jnp.mean(cq * cq, axis=0, keepdims=True) + EPS) * gq_ref[...]).astype(bf16)
    qscale = ((A_NOPE + A_ROPE) ** -0.5) * LOG2E
    qt = jnp.dot(wuq_ref[...], cqn, preferred_element_type=f32) * qscale
    for hh in range(A_HEADS):
        a = qt[512 + 16 * hh:528 + 16 * hh]
        b = qt[640 + 16 * hh:656 + 16 * hh]
        qta_ref[hh, 0:64, :] = qt[64 * hh:64 * hh + 64].astype(bf16)
        qta_ref[hh, 64:80, :] = (a * ca - b * sa).astype(bf16)
        qta_ref[hh, 80:96, :] = (a * sa + b * ca).astype(bf16)
        qta_ref[hh, 96:128, :] = zeros32

    ckvn = (ckv * lax.rsqrt(jnp.mean(ckv * ckv, axis=0, keepdims=True) + EPS) * gkv_ref[...]).astype(bf16)
    kvt = jnp.dot(wukv_ref[...], ckvn, preferred_element_type=f32)
    ka = kpe[0:16]
    kb = kpe[16:32]
    kpe_rot = jnp.concatenate([ka * ca - kb * sa, ka * sa + kb * ca, jnp.zeros((32, tm), f32)], axis=0)
    for hh in range(A_HEADS):
        kt_h = jnp.concatenate([kvt[64 * hh:64 * hh + 64], kpe_rot], axis=0)
        kta_ref[hh] = kt_h.T.astype(bf16)
        vta_ref[hh, 0:64, :] = kvt[512 + 64 * hh:576 + 64 * hh].astype(bf16)
        vta_ref[hh, 64:80, :] = ones_blk

    dscale = (DF_DK ** -0.5) * LOG2E
    for hh in range(2 * DF_HEADS):
        a = dq[32 * hh:32 * hh + 32]
        b = dq[256 + 32 * hh:288 + 32 * hh]
        qtd_ref[hh, 0:32, :] = ((a * cd - b * sd) * dscale).astype(bf16)
        qtd_ref[hh, 32:64, :] = ((a * sd + b * cd) * dscale).astype(bf16)
        a = dk[32 * hh:32 * hh + 32]
        b = dk[256 + 32 * hh:288 + 32 * hh]
        ktd_ref[hh] = jnp.concatenate([a * cd - b * sd, a * sd + b * cd], axis=0).T.astype(bf16)
    for hh in range(DF_HEADS):
        vtd_ref[hh, 0:128, :] = dv[128 * hh:128 * hh + 128].astype(bf16)
        vtd_ref[hh, 128:144, :] = ones_blk


def _pack_inproj_weights(w, w_uq, w_ukv):
    o = [0]
    for s in IN_SIZES:
        o.append(o[-1] + s)
    mq, mk, mv, mo, mg, cq, ckv, kpe, dq, dk, dv, gl = [w[:, o[i]:o[i + 1]] for i in range(12)]
    mg_pad = jnp.pad(mg, ((0, 0), (0, LANE - mg.shape[1])))
    w_tok = jnp.concatenate([mq, mv, mo, mg_pad, gl], axis=1).astype(bf16)

    def split_halves(t, n_heads, d):
        t3 = t.reshape(t.shape[0], n_heads, d)
        return jnp.concatenate([t3[:, :, :d // 2].reshape(t.shape[0], -1),
                                t3[:, :, d // 2:].reshape(t.shape[0], -1)], axis=1)

    w_tr = jnp.concatenate([mk, mg, cq, ckv, kpe, split_halves(dq, 2 * DF_HEADS, DF_DK),
                            split_halves(dk, 2 * DF_HEADS, DF_DK), dv], axis=1).T.astype(bf16)
    uq = w_uq.reshape(Q_LORA, A_HEADS, A_NOPE + A_ROPE)
    half = A_ROPE // 2
    wt_uq = jnp.concatenate([uq[:, :, :A_NOPE].reshape(Q_LORA, -1),
                             uq[:, :, A_NOPE:A_NOPE + half].reshape(Q_LORA, -1),
                             uq[:, :, A_NOPE + half:].reshape(Q_LORA, -1)], axis=1).T.astype(bf16)
    ukv = w_ukv.reshape(KV_LORA, A_HEADS, A_NOPE + A_V)
    wt_ukv = jnp.concatenate([ukv[:, :, :A_NOPE].reshape(KV_LORA, -1),
                              ukv[:, :, A_NOPE:].reshape(KV_LORA, -1)], axis=1).T.astype(bf16)
    return w_tok, w_tr, wt_uq, wt_ukv


def _inproj(x_all, ab, n_lat_blocks, w_tok, w_tr, wt_uq, wt_ukv, gq, gkv, ca, sa, cd, sd):
    n = x_all.shape[0]
    tm = ROW_BLOCK
    tk = _key_tile(n)
    per = tk // tm
    vtile = lambda i: (0, i // per, 0, i % per)
    nb = n // tm
    row = lambda i: (i, 0)
    colb = lambda i: (0, i)
    col3 = lambda i: (0, 0, i)
    grp = lambda i: (jnp.where(i >= n_lat_blocks, 1, 0), 0, 0)
    out_shapes = [
        jax.ShapeDtypeStruct((n, 512), bf16), jax.ShapeDtypeStruct((n, 512), bf16),
        jax.ShapeDtypeStruct((n, 512), bf16), jax.ShapeDtypeStruct((n, LANE), f32),
        jax.ShapeDtypeStruct((n, N_BRANCH * D_MODEL), bf16),
        jax.ShapeDtypeStruct((512, n), bf16), jax.ShapeDtypeStruct((16, n), f32),
        jax.ShapeDtypeStruct((A_HEADS, A_DKP, n), bf16), jax.ShapeDtypeStruct((A_HEADS, n, A_DKP), bf16),
        jax.ShapeDtypeStruct((A_HEADS, n // tk, A_DVP, tk), bf16),
        jax.ShapeDtypeStruct((2 * DF_HEADS, DF_DK, n), bf16), jax.ShapeDtypeStruct((2 * DF_HEADS, n, DF_DK), bf16),
        jax.ShapeDtypeStruct((DF_HEADS, n // tk, DF_DVP, tk), bf16),
    ]
    out_specs = [
        pl.BlockSpec((tm, 512), row), pl.BlockSpec((tm, 512), row), pl.BlockSpec((tm, 512), row),
        pl.BlockSpec((tm, LANE), row), pl.BlockSpec((tm, N_BRANCH * D_MODEL), row),
        pl.BlockSpec((512, tm), colb), pl.BlockSpec((16, tm), colb),
        pl.BlockSpec((A_HEADS, A_DKP, tm), col3), pl.BlockSpec((A_HEADS, tm, A_DKP), lambda i: (0, i, 0)),
        pl.BlockSpec((A_HEADS, None, A_DVP, tm), vtile),
        pl.BlockSpec((2 * DF_HEADS, DF_DK, tm), col3), pl.BlockSpec((2 * DF_HEADS, tm, DF_DK), lambda i: (0, i, 0)),
        pl.BlockSpec((DF_HEADS, None, DF_DVP, tm), vtile),
    ]
    in_specs = [
        pl.BlockSpec((tm, D_MODEL), row),
        pl.BlockSpec((None, 2, D_MODEL), grp),
        _const_spec(w_tok.shape), _const_spec(w_tr.shape), _const_spec(wt_uq.shape), _const_spec(wt_ukv.shape),
        _const_spec(gq.shape), _const_spec(gkv.shape),
        pl.BlockSpec((A_ROPE // 2, tm), colb), pl.BlockSpec((A_ROPE // 2, tm), colb),
        pl.BlockSpec((DF_DK // 2, tm), colb), pl.BlockSpec((DF_DK // 2, tm), colb),
    ]
    return pl.pallas_call(
        _inproj_kernel, grid=(nb,), in_specs=in_specs, out_specs=out_specs, out_shape=out_shapes,
        compiler_params=_cparams(("parallel",)), name="inproj",
    )(x_all, ab, w_tok, w_tr, wt_uq, wt_ukv, gq, gkv, ca, sa, cd, sd)


def _attn_kernel(*refs, n_maps, dk, dv, tk, n_kt, qsub, lam_init, n_in):
    o_ref = refs[n_in]
    if n_maps == 2:
        lam_ref, g_ref, qt_ref, k_ref, vt_ref = refs[:5]
    else:
        qt_ref, k_ref, vt_ref = refs[:3]
    tq = qt_ref.shape[-1]
    dvp = vt_ref.shape[-2]
    chains = [(m, c) for m in range(n_maps) for c in range(tq // qsub)]

    def qk(j, m, c):
        qt = qt_ref[m * dk:(m + 1) * dk, c * qsub:(c + 1) * qsub]
        return jnp.dot(k_ref[m, j * tk:(j + 1) * tk, :], qt, preferred_element_type=f32)

    state = [(jnp.full((1, qsub), -jnp.inf, f32), jnp.zeros((dvp, qsub), f32)) for _ in chains]
    s_cur = [qk(0, m, c) for (m, c) in chains]
    for j in range(n_kt):
        vtj = vt_ref[j]
        for i, (m, c) in enumerate(chains):
            s_nxt = qk(j + 1, m, c) if j + 1 < n_kt else None
            m_run, acc = state[i]
            m_new = jnp.maximum(m_run, jnp.max(s_cur[i], axis=0, keepdims=True))
            alpha = jnp.exp2(m_run - m_new)
            p = jnp.exp2(s_cur[i] - m_new).astype(bf16)
            pv = jnp.dot(vtj, p, preferred_element_type=f32)
            state[i] = (m_new, alpha * acc + pv)
            s_cur[i] = s_nxt
    outs = []
    for m in range(n_maps):
        accs = [state[i][1] for i, (mm, _) in enumerate(chains) if mm == m]
        acc = accs[0] if len(accs) == 1 else jnp.concatenate(accs, axis=1)
        outs.append(acc[0:dv] / acc[dv:dv + 1])
    if n_maps == 1:
        o_ref[...] = outs[0].astype(o_ref.dtype)
    else:
        lf = lam_ref[...]
        lam = (jnp.exp(jnp.sum(lf[0:1] * lf[1:2], axis=1, keepdims=True))
               - jnp.exp(jnp.sum(lf[2:3] * lf[3:4], axis=1, keepdims=True)) + lam_init)
        o = outs[0] - lam * outs[1]
        o = o * lax.rsqrt(jnp.mean(o * o, axis=0, keepdims=True) + EPS) * g_ref[...]
        o_ref[...] = (o * (1.0 - lam_init)).astype(o_ref.dtype)


def _key_tile(lk):
    return max(t for t in range(MXU_DEPTH, min(ATTN_TK, lk) + 1, MXU_DEPTH) if lk % t == 0)


def _attention(qt, k, vt4, *, n_maps, dk, dv, tq, lq, n_out, q_start=0, out=None, lam=None, g=None, lam_init=0.0):
    n_heads = qt.shape[0]
    lk = k.shape[1]
    _, n_kt, dvp, tk = vt4.shape
    assert n_kt * tk == lk
    tq = min(tq, lq)
    qsub = min(ATTN_QSUB, tq)
    qb0 = q_start // tq
    in_specs = [
        pl.BlockSpec((None, n_maps * dk, tq), lambda h, i: (h, 0, qb0 + i)),
        pl.BlockSpec((n_maps, lk, dk), lambda h, i: (h, 0, 0)),
        pl.BlockSpec((None, n_kt, dvp, tk), lambda h, i: (h, 0, 0, 0)),
    ]
    args = [qt, k, vt4]
    if n_maps == 2:
        in_specs = [_const_spec(lam.shape), pl.BlockSpec((None, dv, 1), lambda h, i: (h, 0, 0))] + in_specs
        args = [lam, g] + args
    aliases = {}
    if out is not None:
        aliases = {len(args): 0}
        in_specs.append(pl.BlockSpec(memory_space=pl.ANY))
        args.append(out)
    kern = functools.partial(_attn_kernel, n_maps=n_maps, dk=dk, dv=dv, tk=tk, n_kt=n_kt, qsub=qsub,
                             lam_init=lam_init, n_in=len(args))
    return pl.pallas_call(
        kern, grid=(n_heads, lq // tq), in_specs=in_specs,
        out_specs=pl.BlockSpec((None, dv, tq), lambda h, i: (h, 0, qb0 + i)),
        out_shape=jax.ShapeDtypeStruct((n_heads, dv, n_out), bf16),
        input_output_aliases=aliases,
        compiler_params=_cparams(("parallel", "parallel")),
        name="attn_diff" if n_maps == 2 else "attn_mla",
    )(*args)


def _split3(x):
    hi = x.astype(bf16)
    r = x - hi.astype(f32)
    mid = r.astype(bf16)
    lo = (r - mid.astype(f32)).astype(bf16)
    return hi, mid, lo


def _log_sigmoid(x):
    return jnp.minimum(x, 0.0) - jnp.log(1.0 + jnp.exp(-jnp.abs(x)))


def _softcap(x):
    return GATE_SOFTCAP * jnp.tanh(x * (1.0 / GATE_SOFTCAP))


def _mlstm_kernel(gb_ref, gbt_ref,
                  qf_ref, ktf_ref, vf_ref, gf_ref, gtf_ref,
                  qb_ref, ktb_ref, vb_ref, gb2_ref, gtb_ref,
                  hf_ref, hb_ref, c_ref, m_ref):
    t = qf_ref.shape[0]

    @pl.when(pl.program_id(0) == 0)
    def _():
        c_ref[...] = jnp.zeros_like(c_ref)
        m_ref[...] = jnp.zeros_like(m_ref)

    row = lax.broadcasted_iota(jnp.int32, (t, t), 0)
    col = lax.broadcasted_iota(jnp.int32, (t, t), 1)
    lane = lax.broadcasted_iota(jnp.int32, (t, LANE), 1)
    ones_col = jnp.where(lane == 0, 1.0, 0.0).astype(bf16)
    dirs = ((qf_ref, ktf_ref, vf_ref, gf_ref, gtf_ref, hf_ref),
            (qb_ref, ktb_ref, vb_ref, gb2_ref, gtb_ref, hb_ref))
    for d, (q_ref, kt_ref, v_ref, g_ref, gt_ref, o_ref) in enumerate(dirs):
        mask = (col <= row) if d == 0 else (col >= row)
        maskb = mask.astype(bf16)
        last = t - 1 if d == 0 else 0
        g_tok = _softcap(g_ref[...] + gb_ref[...])
        g_t = _softcap(gt_ref[...] + gbt_ref[...])
        b_tok = sum(jnp.dot(maskb, part, preferred_element_type=f32) for part in _split3(_log_sigmoid(g_tok)))
        b_t = sum(lax.dot_general(part, maskb, NT_DIMS, preferred_element_type=f32)
                  for part in _split3(_log_sigmoid(g_t)))
        for hh in range(M_HEADS):
            ri = 4 * (2 * d) + hh
            rf = 4 * (2 * d + 1) + hh
            idx = d * M_HEADS + hh
            b_col = b_tok[:, rf:rf + 1]
            b_row = b_t[rf:rf + 1, :]
            a_row = g_t[ri:ri + 1, :] - b_row
            m_prev = m_ref[idx, 0:1, 0:1]
            ld = jnp.where(mask, b_col + a_row, -jnp.inf)
            inter = b_col + m_prev
            mj = jnp.maximum(inter, jnp.max(ld, axis=1, keepdims=True))
            dmat = jnp.exp(ld - mj)
            q = q_ref[:, hh * M_DK:(hh + 1) * M_DK]
            kt = kt_ref[hh * M_DK:(hh + 1) * M_DK, :]
            vp = jnp.concatenate([v_ref[:, hh * M_DV:(hh + 1) * M_DV], ones_col], axis=1)
            sc = jnp.dot(q, kt, preferred_element_type=f32) * dmat
            cp = c_ref[idx]
            num = (jnp.dot(sc.astype(bf16), vp, preferred_element_type=f32)
                   + jnp.exp(inter - mj) * jnp.dot(q, cp.astype(bf16), preferred_element_type=f32))
            den = num[:, M_DV:M_DV + 1]
            hj = num[:, 0:M_DV] / jnp.maximum(jnp.abs(den), jnp.exp(-mj))
            o_ref[:, hh * M_DV:(hh + 1) * M_DV] = hj.astype(o_ref.dtype)
            b_tot = b_t[rf:rf + 1, last:last + 1]
            lw = b_tot + a_row
            m_new = jnp.maximum(b_tot + m_prev, jnp.max(lw, axis=1, keepdims=True))
            kw = (kt.astype(f32) * jnp.exp(lw - m_new)).astype(bf16)
            c_ref[idx] = jnp.exp(b_tot + m_prev - m_new) * cp + jnp.dot(kw, vp, preferred_element_type=f32)
            m_ref[idx] = jnp.broadcast_to(m_new, (8, LANE))


def _mlstm(mq, mkt, mv, mg, mgt, gate_b):
    n = mq.shape[0]
    t = MLSTM_T
    nb = n // t
    gb = jnp.pad(gate_b.reshape(1, 16), ((0, 0), (0, LANE - 16)))
    gbt = gate_b.reshape(16, 1)
    fwd = lambda s: jnp.where(s == 0, nb - 1, s - 1)
    bwd = lambda s: nb - 1 - s
    specs = []
    for im in (fwd, bwd):
        specs += [pl.BlockSpec((t, 512), lambda s, im=im: (im(s), 0)),
                  pl.BlockSpec((512, t), lambda s, im=im: (0, im(s))),
                  pl.BlockSpec((t, 512), lambda s, im=im: (im(s), 0)),
                  pl.BlockSpec((t, LANE), lambda s, im=im: (im(s), 0)),
                  pl.BlockSpec((16, t), lambda s, im=im: (0, im(s)))]
    return pl.pallas_call(
        _mlstm_kernel, grid=(nb,),
        in_specs=[_const_spec(gb.shape), _const_spec(gbt.shape)] + specs,
        out_specs=[pl.BlockSpec((t, 512), lambda s: (fwd(s), 0)), pl.BlockSpec((t, 512), lambda s: (bwd(s), 0))],
        out_shape=[jax.ShapeDtypeStruct((n, 512), bf16), jax.ShapeDtypeStruct((n, 512), bf16)],
        scratch_shapes=[pltpu.VMEM((2 * M_HEADS, M_DK, 2 * M_DV), f32), pltpu.VMEM((2 * M_HEADS, 8, LANE), f32)],
        compiler_params=_cparams(("arbitrary",)), name="mlstm",
    )(gb, gbt, mq, mkt, mv, mg, mgt, mq, mkt, mv, mg, mgt)


def _rms(x, axis=-1):
    return x * lax.rsqrt(jnp.mean(x * x, axis=axis, keepdims=True) + EPS)


def _merge_kernel(x_ref, hf_ref, hb_ref, mo_ref, ya_ref, yd_ref, gl_ref, mng_ref, wb_ref, wout_ref,
                  mod_ref, rw_ref, rb_ref, xnew_ref, h2_ref, logit_ref):
    hs = hf_ref[...].astype(f32) + hb_ref[...].astype(f32)
    parts = [_rms(hs[:, hh * M_DV:(hh + 1) * M_DV]) for hh in range(M_HEADS)]
    ym = jnp.concatenate(parts, axis=1) * mng_ref[...] * jax.nn.sigmoid(mo_ref[...].astype(f32))
    yb0 = jnp.dot(ym.astype(bf16), wb_ref[0], preferred_element_type=f32)
    tm = x_ref.shape[0]
    tn_dims = (((0,), (0,)), ((), ()))
    yb1 = lax.dot_general(ya_ref[...].reshape(BRANCH_W, tm), wb_ref[1], tn_dims, preferred_element_type=f32)
    yb2 = lax.dot_general(yd_ref[...].reshape(BRANCH_W, tm), wb_ref[2], tn_dims, preferred_element_type=f32)
    d = D_MODEL
    z = (jax.nn.sigmoid(gl_ref[:, 0:d].astype(f32)) * yb0
         + jax.nn.sigmoid(gl_ref[:, d:2 * d].astype(f32)) * yb1
         + jax.nn.sigmoid(gl_ref[:, 2 * d:3 * d].astype(f32)) * yb2)
    y = jnp.dot(z.astype(bf16), wout_ref[...], preferred_element_type=f32)
    xnew = x_ref[...] + mod_ref[1:2, :] * (_rms(y) * mod_ref[0:1, :])
    h2 = _rms(xnew) * mod_ref[2:3, :] + mod_ref[3:4, :]
    xnew_ref[...] = xnew
    h2_ref[...] = h2.astype(bf16)
    h2_hi = h2.astype(bf16)
    h2_lo = (h2 - h2_hi.astype(f32)).astype(bf16)
    rw = rw_ref[...]
    rw_hi = rw.astype(bf16)
    rw_lo = (rw - rw_hi.astype(f32)).astype(bf16)
    logit_ref[...] = (jnp.dot(h2_hi, rw_hi, preferred_element_type=f32)
                      + jnp.dot(h2_lo, rw_hi, preferred_element_type=f32)
                      + jnp.dot(h2_hi, rw_lo, preferred_element_type=f32) + rb_ref[...])


def _merge(n_rows, n_lat_blocks, x_all, hf, hb, mo, ya, yd, gl, mng, wb, wout, mod, rw, rb):
    tm = ROW_BLOCK
    nb = n_rows // tm
    row = lambda i: (i, 0)
    grp = lambda i: (jnp.where(i >= n_lat_blocks, 1, 0), 0, 0)
    in_specs = [pl.BlockSpec((tm, D_MODEL), row), pl.BlockSpec((tm, 512), row), pl.BlockSpec((tm, 512), row),
                pl.BlockSpec((tm, 512), row),
                pl.BlockSpec((A_HEADS, A_V, tm), lambda i: (0, 0, i)),
                pl.BlockSpec((DF_HEADS, DF_DV, tm), lambda i: (0, 0, i)),
                pl.BlockSpec((tm, N_BRANCH * D_MODEL), row), _const_spec(mng.shape), _const_spec(wb.shape),
                _const_spec(wout.shape), pl.BlockSpec((None, 4, D_MODEL), grp),
                _const_spec(rw.shape), _const_spec(rb.shape)]
    return pl.pallas_call(
        _merge_kernel, grid=(nb,), in_specs=in_specs,
        out_specs=[pl.BlockSpec((tm, D_MODEL), row), pl.BlockSpec((tm, D_MODEL), row), pl.BlockSpec((tm, LANE), row)],
        out_shape=[jax.ShapeDtypeStruct((n_rows, D_MODEL), f32),
                   jax.ShapeDtypeStruct((max(n_rows, H2_ROWS), D_MODEL), bf16),
                   jax.ShapeDtypeStruct((n_rows, LANE), f32)],
        compiler_params=_cparams(("parallel",)), name="merge",
    )(x_all, hf, hb, mo, ya, yd, gl, mng, wb, wout, mod, rw, rb)


def _moe_kernel(be_ref, nu_ref, xs_ref, wgu_ref, bgu_ref, wd_ref, bd_ref, o_ref, wgu_s, wd_s):
    i = pl.program_id(0)
    prev = be_ref[jnp.maximum(i - 1, 0)]
    used = i < nu_ref[0]

    @pl.when(jnp.logical_and(used, jnp.logical_or(i == 0, be_ref[i] != prev)))
    def _():
        wgu_s[...] = wgu_ref[...].astype(bf16)
        wd_s[...] = wd_ref[...].astype(bf16)

    @pl.when(used)
    def _():
        gu = jnp.dot(xs_ref[...], wgu_s[...], preferred_element_type=f32) + bgu_ref[...]
        gate = jnp.minimum(gu[:, :D_EXPERT], SWIGLU_LIMIT)
        up = jnp.clip(gu[:, D_EXPERT:], -SWIGLU_LIMIT, SWIGLU_LIMIT)
        act = (up + 1.0) * (gate * jax.nn.sigmoid(SWIGLU_ALPHA * gate))
        y = jnp.dot(act.astype(bf16), wd_s[...], preferred_element_type=f32) + bd_ref[...]
        o_ref[...] = y.astype(o_ref.dtype)

    @pl.when(jnp.logical_not(used))
    def _():
        o_ref[...] = jnp.zeros_like(o_ref)


def _moe_experts(blk_e, n_used, xs, layer, w_gu, b_gu, w_down, b_down):
    cap = xs.shape[0]
    tm = MOE_TM
    nblk = cap // tm
    grid_spec = pltpu.PrefetchScalarGridSpec(
        num_scalar_prefetch=2, grid=(nblk,),
        in_specs=[pl.BlockSpec((tm, D_MODEL), lambda i, be, nu: (i, 0)),
                  pl.BlockSpec((None, None, D_MODEL, 2 * D_EXPERT), lambda i, be, nu: (layer, be[i], 0, 0)),
                  pl.BlockSpec((None, None, 1, 2 * D_EXPERT), lambda i, be, nu: (layer, be[i], 0, 0)),
                  pl.BlockSpec((None, None, D_EXPERT, D_MODEL), lambda i, be, nu: (layer, be[i], 0, 0)),
                  pl.BlockSpec((None, None, 1, D_MODEL), lambda i, be, nu: (layer, be[i], 0, 0))],
        out_specs=pl.BlockSpec((tm, D_MODEL), lambda i, be, nu: (i, 0)),
        scratch_shapes=[pltpu.VMEM((D_MODEL, 2 * D_EXPERT), bf16), pltpu.VMEM((D_EXPERT, D_MODEL), bf16)])
    return pl.pallas_call(
        _moe_kernel, grid_spec=grid_spec,
        out_shape=jax.ShapeDtypeStruct((cap, D_MODEL), bf16),
        compiler_params=_cparams(("arbitrary",)), name="moe_experts",
    )(blk_e, n_used, xs, w_gu, b_gu.reshape(DEPTH, N_EXPERTS, 1, -1), w_down, b_down.reshape(DEPTH, N_EXPERTS, 1, -1))


def _moe(h2, logits, layer, w_gu, b_gu, w_down, b_down):
    n = logits.shape[0]
    tm = MOE_TM
    i32 = jnp.int32
    top_v, top_i = lax.top_k(logits[:, :N_EXPERTS], TOP_K)
    wts = jax.nn.softmax(top_v, axis=-1)
    nk = n * TOP_K
    flat_e = top_i.reshape(nk).astype(i32)
    iota = jnp.arange(nk, dtype=i32)
    skey = lax.sort(flat_e * nk + iota)
    se = skey // nk
    order = skey - se * nk
    e_ids = jnp.arange(N_EXPERTS, dtype=i32)
    grp_start = jnp.sum((se[:, None] < e_ids[None, :]).astype(i32), axis=0)
    counts = jnp.sum((se[:, None] == e_ids[None, :]).astype(i32), axis=0)
    padded = (counts + tm - 1) // tm * tm
    pad_end = jnp.cumsum(padded)
    pad_start = pad_end - padded
    off = pad_start - grp_start
    hit = top_i[:, :, None] == e_ids[None, None, :]
    per_tok = jnp.sum(hit.astype(i32), axis=1)
    rank = jnp.cumsum(per_tok, axis=0) - per_tok
    pos = jnp.sum(jnp.where(hit, (pad_start[None, :] + rank)[:, None, :], 0), axis=2)
    pos_km = pos.T.reshape(nk)
    cap = -(-(nk + N_EXPERTS * (tm - 1)) // tm) * tm
    nblk = cap // tm
    bstart = jnp.arange(nblk, dtype=i32) * tm
    blk_e = jnp.minimum(jnp.sum((pad_end[None, :] <= bstart[:, None]).astype(i32), axis=1), N_EXPERTS - 1)
    n_used = (pad_end[-1] // tm).astype(i32).reshape(1)
    sel = blk_e[:, None] == e_ids[None, :]
    off_b = jnp.sum(jnp.where(sel, off[None, :], 0), axis=1)
    end_b = jnp.sum(jnp.where(sel, (pad_start + counts)[None, :], 0), axis=1)
    p2 = bstart[:, None] + jnp.arange(tm, dtype=i32)[None, :]
    src = jnp.clip(p2 - off_b[:, None], 0, nk - 1)
    slot_t = jnp.where(p2 < end_b[:, None], (order // TOP_K)[src], p2 % n).reshape(cap)
    xs = h2[slot_t]
    ys = _moe_experts(blk_e, n_used, xs, layer, w_gu, b_gu, w_down, b_down)
    return ys[pos_km].reshape(TOP_K, n, D_MODEL), wts


def _resid_kernel(x_ref, y_ref, w_ref, mod_ref, o_ref):
    w = w_ref[...]
    f = y_ref[0].astype(f32) * w[:, 0:1]
    for kk in range(1, TOP_K):
        f = f + y_ref[kk].astype(f32) * w[:, kk:kk + 1]
    o_ref[...] = x_ref[...] + mod_ref[1:2, :] * (_rms(f) * mod_ref[0:1, :])


def _resid(n_lat_blocks, xnew, picked, wts, mod):
    n = xnew.shape[0]
    tm = ROW_BLOCK
    row = lambda i: (i, 0)
    grp = lambda i: (jnp.where(i >= n_lat_blocks, 1, 0), 0, 0)
    return pl.pallas_call(
        _resid_kernel, grid=(n // tm,),
        in_specs=[pl.BlockSpec((tm, D_MODEL), row), pl.BlockSpec((TOP_K, tm, D_MODEL), lambda i: (0, i, 0)),
                  pl.BlockSpec((tm, TOP_K), row), pl.BlockSpec((None, 2, D_MODEL), grp)],
        out_specs=pl.BlockSpec((tm, D_MODEL), row),
        out_shape=jax.ShapeDtypeStruct((n, D_MODEL), f32),
        compiler_params=_cparams(("parallel",)), name="resid",
    )(xnew, picked, wts, mod)


def _rope_tables(n_lat, n_ctx, dim):
    rows = n_lat // GRID_W
    row = jnp.repeat(jnp.arange(rows, dtype=f32), GRID_W)
    col = jnp.tile(jnp.arange(GRID_W, dtype=f32), rows)
    quarter = dim // 4
    inv = ROPE_BASE ** (-jnp.arange(quarter, dtype=f32) / quarter)
    ang = jnp.concatenate([row[:, None] * inv, col[:, None] * inv], axis=-1)
    cos = jnp.concatenate([jnp.cos(ang), jnp.ones((n_ctx, dim // 2), f32)], axis=0).T
    sin = jnp.concatenate([jnp.sin(ang), jnp.zeros((n_ctx, dim // 2), f32)], axis=0).T
    return cos, sin


def kernel(x, c, ctx, c_ctx, ada_w, ada_b, norm_g, w_in, m_gate_b, m_norm_g, q_norm_g, w_uq, kv_norm_g, w_ukv,
           diff_lam, diff_norm_g, w_branch, w_out, router_w, router_b, w_gu, b_gu, w_down, b_down):
    n_lat = x.shape[1]
    n_ctx = ctx.shape[1]
    n = n_lat + n_ctx
    assert x.shape[0] == 1 and n_lat % ROW_BLOCK == 0 and n_ctx == MLSTM_T and n_lat % GRID_W == 0
    nlb = n_lat // ROW_BLOCK
    x_all = jnp.concatenate([x[0], ctx[0]], axis=0)

    cond = jnp.zeros((8, D_MODEL), f32).at[0].set(c[0]).at[1].set(c_ctx)
    mods = _adaln(cond, ada_w, ada_b)[:, :2].reshape(DEPTH, 2, 6, D_MODEL)

    ca, sa = _rope_tables(n_lat, n_ctx, A_ROPE)
    cd, sd = _rope_tables(n_lat, n_ctx, DF_DK)

    for l in range(DEPTH):
        need_ctx = l < DEPTH - 1
        lam_init = 0.8 - 0.6 * math.exp(-0.3 * l)
        sh1, sc1, ga1, sh2, sc2, ga2 = [mods[l, :, j] for j in range(6)]
        ab1 = jnp.stack([norm_g[l, 0][None] * (1.0 + sc1), sh1], axis=1)
        w_tok, w_tr, wt_uq, wt_ukv = _pack_inproj_weights(w_in[l], w_uq[l], w_ukv[l])
        (mq, mv, mo, mg, gl, mkt, mgt, qta, k_a, vta, qtd, k_d, vtd) = _inproj(
            x_all, ab1, nlb, w_tok, w_tr, wt_uq, wt_ukv,
            q_norm_g[l].reshape(Q_LORA, 1), kv_norm_g[l].reshape(KV_LORA, 1), ca, sa, cd, sd)

        hf, hb = _mlstm(mq, mkt, mv, mg, mgt, m_gate_b[l])

        qtd4 = qtd.reshape(DF_HEADS, 2 * DF_DK, n)
        dg = diff_norm_g[l].reshape(DF_HEADS, DF_DV, 1)
        n_rows = n if need_ctx else n_lat
        attn_a = functools.partial(_attention, n_maps=1, dk=A_DKP, dv=A_V, n_out=n_rows)
        attn_d = functools.partial(_attention, n_maps=2, dk=DF_DK, dv=DF_DV, n_out=n_rows,
                                   lam=diff_lam[l], g=dg, lam_init=lam_init)
        ya = attn_a(qta, k_a, vta, tq=ATTN_TQ_MLA, lq=n_lat)
        yd = attn_d(qtd4, k_d, vtd, tq=ATTN_TQ_DIFF, lq=n_lat)
        if need_ctx:
            ctx_tile = lambda v: v[:, -1:, :, v.shape[3] - n_ctx:]
            ya = attn_a(qta, k_a[:, n_lat:], ctx_tile(vta), tq=n_ctx, lq=n_ctx, q_start=n_lat, out=ya)
            yd = attn_d(qtd4, k_d[:, n_lat:], ctx_tile(vtd), tq=n_ctx, lq=n_ctx, q_start=n_lat, out=yd)

        mod_m = jnp.stack([jnp.broadcast_to(norm_g[l, 1][None], (2, D_MODEL)), ga1,
                           norm_g[l, 2][None] * (1.0 + sc2), sh2], axis=1)
        rw = jnp.pad(router_w[l], ((0, 0), (0, LANE - N_EXPERTS)))
        rb = jnp.pad(router_b[l].reshape(1, N_EXPERTS), ((0, 0), (0, LANE - N_EXPERTS)))
        xnew, h2, logits = _merge(n_rows, nlb, x_all, hf, hb, mo, ya, yd, gl,
                                  m_norm_g[l].reshape(1, -1), w_branch[l].astype(bf16), w_out[l].astype(bf16),
                                  mod_m, rw, rb)
        picked, wts = _moe(h2, logits, l, w_gu, b_gu, w_down, b_down)
        mod_r = jnp.stack([jnp.broadcast_to(norm_g[l, 3][None], (2, D_MODEL)), ga2], axis=1)
        x_all = _resid(nlb, xnew, picked, wts, mod_r)
    return x_all[:n_lat][None]
```

```python
import functools
import math

import jax
import jax.numpy as jnp
from jax import lax
from jax.experimental import pallas as pl
from jax.experimental.pallas import tpu as pltpu

f32 = jnp.float32
bf16 = jnp.bfloat16

D_MODEL = 1024
DEPTH = 2
GRID_W = 64
EPS = 1e-6
ROPE_BASE = 10000.0
M_HEADS = 4
M_DK = 128
M_DV = 128
GATE_SOFTCAP = 15.0
A_HEADS = 8
A_NOPE = 64
A_ROPE = 32
A_V = 64
Q_LORA = 256
KV_LORA = 128
DF_HEADS = 4
DF_DK = 64
DF_DV = 128
N_BRANCH = 3
BRANCH_W = 512
N_EXPERTS = 32
TOP_K = 4
D_EXPERT = 1024
SWIGLU_LIMIT = 7.0
SWIGLU_ALPHA = 1.702

IN_SIZES = (M_HEADS * M_DK, M_HEADS * M_DK, M_HEADS * M_DV, M_HEADS * M_DV, 4 * M_HEADS,
            Q_LORA, KV_LORA, A_ROPE,
            2 * DF_HEADS * DF_DK, 2 * DF_HEADS * DF_DK, DF_HEADS * DF_DV,
            N_BRANCH * D_MODEL)

LOG2E = 1.4426950408889634
LANE = 128
A_DKP = 128
A_DVP = A_V + 16
DF_DVP = DF_DV + 16
VMEM_LIMIT = 56 * 1024 * 1024

ROW_BLOCK = 256
MXU_DEPTH = 256
ATTN_TQ_MLA = 2048
ATTN_TQ_DIFF = 1024
ATTN_QSUB = 256
ATTN_TK = 1280
MLSTM_T = 256
MOE_TM = 512
H2_ROWS = 32768

NT_DIMS = (((1,), (1,)), ((), ()))


def _cparams(sem):
    return pltpu.CompilerParams(dimension_semantics=sem, vmem_limit_bytes=VMEM_LIMIT)


def _const_spec(shape):
    nd = len(shape)
    return pl.BlockSpec(shape, lambda *_: (0,) * nd)


def _adaln_kernel(cond_ref, w_ref, b_ref, o_ref):
    cnd = cond_ref[...]
    a = cnd * jax.nn.sigmoid(cnd)
    o_ref[...] = jnp.dot(a, w_ref[...], preferred_element_type=f32,
                         precision=lax.Precision.HIGHEST) + b_ref[...]


def _adaln(cond, ada_w, ada_b):
    tn = 1536
    nj = (6 * D_MODEL) // tn
    return pl.pallas_call(
        _adaln_kernel,
        grid=(DEPTH, nj),
        in_specs=[pl.BlockSpec((8, D_MODEL), lambda l, j: (0, 0)),
                  pl.BlockSpec((None, D_MODEL, tn), lambda l, j: (l, 0, j)),
                  pl.BlockSpec((None, 1, tn), lambda l, j: (l, 0, j))],
        out_specs=pl.BlockSpec((None, 8, tn), lambda l, j: (l, 0, j)),
        out_shape=jax.ShapeDtypeStruct((DEPTH, 8, 6 * D_MODEL), f32),
        compiler_params=_cparams(("parallel", "parallel")),
        name="adaln",
    )(cond, ada_w, ada_b.reshape(DEPTH, 1, 6 * D_MODEL))


M_W = M_HEADS * M_DK
DF_W = 2 * DF_HEADS * DF_DK
TOK_SIZES = (M_W, M_W, M_W, LANE, N_BRANCH * D_MODEL)
TR_SIZES = (M_W, 4 * M_HEADS, Q_LORA, KV_LORA, A_ROPE, DF_W, DF_W, DF_HEADS * DF_DV)


def _bounds(sizes):
    offs = [0]
    for sz in sizes:
        offs.append(offs[-1] + sz)
    return tuple(zip(offs[:-1], offs[1:]))


TOK_B = _bounds(TOK_SIZES)
TR_B = _bounds(TR_SIZES)
TOK_W = TOK_B[-1][1]
TR_W = TR_B[-1][1]


def _ones_row_block(tm):
    r = lax.broadcasted_iota(jnp.int32, (16, tm), 0)
    return jnp.where(r == 0, 1.0, 0.0).astype(bf16)


def _inproj_kernel(x_ref, ab_ref, wtok_ref, wtr_ref, wuq_ref, wukv_ref, gq_ref, gkv_ref,
                   ca_ref, sa_ref, cd_ref, sd_ref,
                   mq_ref, mv_ref, mo_ref, mg_ref, gl_ref, mkt_ref, mgt_ref,
                   qta_ref, kta_ref, vta_ref, qtd_ref, ktd_ref, vtd_ref):
    tm = x_ref.shape[0]
    x = x_ref[...]
    ms = jnp.mean(x * x, axis=-1, keepdims=True)
    h = (x * lax.rsqrt(ms + EPS)) * ab_ref[0:1, :] + ab_ref[1:2, :]
    hb = h.astype(bf16)

    tok = jnp.dot(hb, wtok_ref[...], preferred_element_type=f32)
    mq, mv, mo, mg, gl = [tok[:, a:b] for a, b in TOK_B]
    mq_ref[...] = (mq * (M_DK ** -0.5)).astype(bf16)
    mv_ref[...] = mv.astype(bf16)
    mo_ref[...] = mo.astype(bf16)
    mg_ref[...] = mg
    gl_ref[...] = gl.astype(bf16)

    tr = lax.dot_general(wtr_ref[...], hb, NT_DIMS, preferred_element_type=f32)
    mk, mgt, cq, ckv, kpe, dq, dk, dv = [tr[a:b] for a, b in TR_B]
    mkt_ref[...] = mk.astype(bf16)
    mgt_ref[...] = mgt

    ca = ca_ref[...]
    sa = sa_ref[...]
    cd = cd_ref[...]
    sd = sd_ref[...]
    ones_blk = _ones_row_block(tm)
    zero_pad = jnp.zeros((A_DKP - A_NOPE - A_ROPE, tm), bf16)

    cqn = (cq * lax.rsqrt(jnp.mean(cq * cq, axis=0, keepdims=True) + EPS) * gq_ref[...]).astype(bf16)
    qscale = ((A_NOPE + A_ROPE) ** -0.5) * LOG2E
    qt = jnp.dot(wuq_ref[...], cqn, preferred_element_type=f32) * qscale
    nope_w = A_HEADS * A_NOPE
    rh = A_ROPE // 2
    for hh in range(A_HEADS):
        a = qt[nope_w + rh * hh:nope_w + rh * (hh + 1)]
        b = qt[nope_w + rh * (A_HEADS + hh):nope_w + rh * (A_HEADS + hh + 1)]
        qta_ref[hh, 0:A_NOPE, :] = qt[A_NOPE * hh:A_NOPE * (hh + 1)].astype(bf16)
        qta_ref[hh, A_NOPE:A_NOPE + rh, :] = (a * ca - b * sa).astype(bf16)
        qta_ref[hh, A_NOPE + rh:A_NOPE + A_ROPE, :] = (a * sa + b * ca).astype(bf16)
        qta_ref[hh, A_NOPE + A_ROPE:A_DKP, :] = zero_pad

    ckvn = (ckv * lax.rsqrt(jnp.mean(ckv * ckv, axis=0, keepdims=True) + EPS) * gkv_ref[...]).astype(bf16)
    kvt = jnp.dot(wukv_ref[...], ckvn, preferred_element_type=f32)
    ka = kpe[0:rh]
    kb = kpe[rh:A_ROPE]
    kpe_rot = jnp.concatenate([ka * ca - kb * sa, ka * sa + kb * ca,
                               jnp.zeros((A_DKP - A_NOPE - A_ROPE, tm), f32)], axis=0)
    for hh in range(A_HEADS):
        kt_h = jnp.concatenate([kvt[A_NOPE * hh:A_NOPE * (hh + 1)], kpe_rot], axis=0)
        kta_ref[hh] = kt_h.T.astype(bf16)
        vta_ref[hh, 0:A_V, :] = kvt[nope_w + A_V * hh:nope_w + A_V * (hh + 1)].astype(bf16)
        vta_ref[hh, A_V:A_DVP, :] = ones_blk

    dscale = (DF_DK ** -0.5) * LOG2E
    dh = DF_DK // 2
    for hh in range(2 * DF_HEADS):
        a = dq[dh * hh:dh * (hh + 1)]
        b = dq[DF_W // 2 + dh * hh:DF_W // 2 + dh * (hh + 1)]
        qtd_ref[hh, 0:dh, :] = ((a * cd - b * sd) * dscale).astype(bf16)
        qtd_ref[hh, dh:DF_DK, :] = ((a * sd + b * cd) * dscale).astype(bf16)
        a = dk[dh * hh:dh * (hh + 1)]
        b = dk[DF_W // 2 + dh * hh:DF_W // 2 + dh * (hh + 1)]
        ktd_ref[hh] = jnp.concatenate([a * cd - b * sd, a * sd + b * cd], axis=0).T.astype(bf16)
    for hh in range(DF_HEADS):
        vtd_ref[hh, 0:DF_DV, :] = dv[DF_DV * hh:DF_DV * (hh + 1)].astype(bf16)
        vtd_ref[hh, DF_DV:DF_DVP, :] = ones_blk


def _pack_inproj_weights(w, w_uq, w_ukv):
    o = [0]
    for s in IN_SIZES:
        o.append(o[-1] + s)
    mq, mk, mv, mo, mg, cq, ckv, kpe, dq, dk, dv, gl = [w[:, o[i]:o[i + 1]] for i in range(12)]
    mg_pad = jnp.pad(mg, ((0, 0), (0, LANE - mg.shape[1])))
    w_tok = jnp.concatenate([mq, mv, mo, mg_pad, gl], axis=1).astype(bf16)

    def split_halves(t, n_heads, d):
        t3 = t.reshape(t.shape[0], n_heads, d)
        return jnp.concatenate([t3[:, :, :d // 2].reshape(t.shape[0], -1),
                                t3[:, :, d // 2:].reshape(t.shape[0], -1)], axis=1)

    w_tr = jnp.concatenate([mk, mg, cq, ckv, kpe, split_halves(dq, 2 * DF_HEADS, DF_DK),
                            split_halves(dk, 2 * DF_HEADS, DF_DK), dv], axis=1).T.astype(bf16)
    uq = w_uq.reshape(Q_LORA, A_HEADS, A_NOPE + A_ROPE)
    half = A_ROPE // 2
    wt_uq = jnp.concatenate([uq[:, :, :A_NOPE].reshape(Q_LORA, -1),
                             uq[:, :, A_NOPE:A_NOPE + half].reshape(Q_LORA, -1),
                             uq[:, :, A_NOPE + half:].reshape(Q_LORA, -1)], axis=1).T.astype(bf16)
    ukv = w_ukv.reshape(KV_LORA, A_HEADS, A_NOPE + A_V)
    wt_ukv = jnp.concatenate([ukv[:, :, :A_NOPE].reshape(KV_LORA, -1),
                              ukv[:, :, A_NOPE:].reshape(KV_LORA, -1)], axis=1).T.astype(bf16)
    return w_tok, w_tr, wt_uq, wt_ukv


def _inproj(x_all, ab, n_lat_blocks, w_tok, w_tr, wt_uq, wt_ukv, gq, gkv, ca, sa, cd, sd):
    n = x_all.shape[0]
    tm = ROW_BLOCK
    tk = _key_tile(n)
    per = tk // tm
    vtile = lambda i: (0, i // per, 0, i % per)
    nb = n // tm
    row = lambda i: (i, 0)
    colb = lambda i: (0, i)
    col3 = lambda i: (0, 0, i)
    grp = lambda i: (jnp.where(i >= n_lat_blocks, 1, 0), 0, 0)
    out_shapes = [
        jax.ShapeDtypeStruct((n, M_W), bf16), jax.ShapeDtypeStruct((n, M_W), bf16),
        jax.ShapeDtypeStruct((n, M_W), bf16), jax.ShapeDtypeStruct((n, LANE), f32),
        jax.ShapeDtypeStruct((n, N_BRANCH * D_MODEL), bf16),
        jax.ShapeDtypeStruct((M_W, n), bf16), jax.ShapeDtypeStruct((4 * M_HEADS, n), f32),
        jax.ShapeDtypeStruct((A_HEADS, A_DKP, n), bf16), jax.ShapeDtypeStruct((A_HEADS, n, A_DKP), bf16),
        jax.ShapeDtypeStruct((A_HEADS, n // tk, A_DVP, tk), bf16),
        jax.ShapeDtypeStruct((2 * DF_HEADS, DF_DK, n), bf16), jax.ShapeDtypeStruct((2 * DF_HEADS, n, DF_DK), bf16),
        jax.ShapeDtypeStruct((DF_HEADS, n // tk, DF_DVP, tk), bf16),
    ]
    out_specs = [
        pl.BlockSpec((tm, M_W), row), pl.BlockSpec((tm, M_W), row), pl.BlockSpec((tm, M_W), row),
        pl.BlockSpec((tm, LANE), row), pl.BlockSpec((tm, N_BRANCH * D_MODEL), row),
        pl.BlockSpec((M_W, tm), colb), pl.BlockSpec((4 * M_HEADS, tm), colb),
        pl.BlockSpec((A_HEADS, A_DKP, tm), col3), pl.BlockSpec((A_HEADS, tm, A_DKP), lambda i: (0, i, 0)),
        pl.BlockSpec((A_HEADS, None, A_DVP, tm), vtile),
        pl.BlockSpec((2 * DF_HEADS, DF_DK, tm), col3), pl.BlockSpec((2 * DF_HEADS, tm, DF_DK), lambda i: (0, i, 0)),
        pl.BlockSpec((DF_HEADS, None, DF_DVP, tm), vtile),
    ]
    in_specs = [
        pl.BlockSpec((tm, D_MODEL), row),
        pl.BlockSpec((None, 2, D_MODEL), grp),
        _const_spec(w_tok.shape), _const_spec(w_tr.shape), _const_spec(wt_uq.shape), _const_spec(wt_ukv.shape),
        _const_spec(gq.shape), _const_spec(gkv.shape),
        pl.BlockSpec((A_ROPE // 2, tm), colb), pl.BlockSpec((A_ROPE // 2, tm), colb),
        pl.BlockSpec((DF_DK // 2, tm), colb), pl.BlockSpec((DF_DK // 2, tm), colb),
    ]
    return pl.pallas_call(
        _inproj_kernel, grid=(nb,), in_specs=in_specs, out_specs=out_specs, out_shape=out_shapes,
        compiler_params=_cparams(("parallel",)), name="inproj",
    )(x_all, ab, w_tok, w_tr, wt_uq, wt_ukv, gq, gkv, ca, sa, cd, sd)


def _attn_kernel(*refs, n_maps, dk, dv, tk, n_kt, qsub, lam_init, n_in):
    o_ref = refs[n_in]
    if n_maps == 2:
        lam_ref, g_ref, qt_ref, k_ref, vt_ref = refs[:5]
    else:
        qt_ref, k_ref, vt_ref = refs[:3]
    tq = qt_ref.shape[-1]
    dvp = vt_ref.shape[-2]
    chains = [(m, c) for m in range(n_maps) for c in range(tq // qsub)]

    def qk(j, m, c):
        qt = qt_ref[m * dk:(m + 1) * dk, c * qsub:(c + 1) * qsub]
        return jnp.dot(k_ref[m, j * tk:(j + 1) * tk, :], qt, preferred_element_type=f32)

    state = [(jnp.full((1, qsub), -jnp.inf, f32), jnp.zeros((dvp, qsub), f32)) for _ in chains]
    s_cur = [qk(0, m, c) for (m, c) in chains]
    for j in range(n_kt):
        vtj = vt_ref[j]
        for i, (m, c) in enumerate(chains):
            s_nxt = qk(j + 1, m, c) if j + 1 < n_kt else None
            m_run, acc = state[i]
            m_new = jnp.maximum(m_run, jnp.max(s_cur[i], axis=0, keepdims=True))
            alpha = jnp.exp2(m_run - m_new)
            p = jnp.exp2(s_cur[i] - m_new).astype(bf16)
            pv = jnp.dot(vtj, p, preferred_element_type=f32)
            state[i] = (m_new, alpha * acc + pv)
            s_cur[i] = s_nxt
    outs = []
    for m in range(n_maps):
        accs = [state[i][1] for i, (mm, _) in enumerate(chains) if mm == m]
        acc = accs[0] if len(accs) == 1 else jnp.concatenate(accs, axis=1)
        outs.append(acc[0:dv] / acc[dv:dv + 1])
    if n_maps == 1:
        o_ref[...] = outs[0].astype(o_ref.dtype)
    else:
        lf = lam_ref[...]
        lam = (jnp.exp(jnp.sum(lf[0:1] * lf[1:2], axis=1, keepdims=True))
               - jnp.exp(jnp.sum(lf[2:3] * lf[3:4], axis=1, keepdims=True)) + lam_init)
        o = outs[0] - lam * outs[1]
        o = o * lax.rsqrt(jnp.mean(o * o, axis=0, keepdims=True) + EPS) * g_ref[...]
        o_ref[...] = (o * (1.0 - lam_init)).astype(o_ref.dtype)


def _key_tile(lk):
    return max(t for t in range(MXU_DEPTH, min(ATTN_TK, lk) + 1, MXU_DEPTH) if lk % t == 0)


def _attention(qt, k, vt4, *, n_maps, dk, dv, tq, lq, n_out, q_start=0, out=None, lam=None, g=None, lam_init=0.0):
    n_heads = qt.shape[0]
    lk = k.shape[1]
    _, n_kt, dvp, tk = vt4.shape
    assert n_kt * tk == lk
    tq = min(tq, lq)
    qsub = min(ATTN_QSUB, tq)
    qb0 = q_start // tq
    in_specs = [
        pl.BlockSpec((None, n_maps * dk, tq), lambda h, i: (h, 0, qb0 + i)),
        pl.BlockSpec((n_maps, lk, dk), lambda h, i: (h, 0, 0)),
        pl.BlockSpec((None, n_kt, dvp, tk), lambda h, i: (h, 0, 0, 0)),
    ]
    args = [qt, k, vt4]
    if n_maps == 2:
        in_specs = [_const_spec(lam.shape), pl.BlockSpec((None, dv, 1), lambda h, i: (h, 0, 0))] + in_specs
        args = [lam, g] + args
    aliases = {}
    if out is not None:
        aliases = {len(args): 0}
        in_specs.append(pl.BlockSpec(memory_space=pl.ANY))
        args.append(out)
    kern = functools.partial(_attn_kernel, n_maps=n_maps, dk=dk, dv=dv, tk=tk, n_kt=n_kt, qsub=qsub,
                             lam_init=lam_init, n_in=len(args))
    return pl.pallas_call(
        kern, grid=(n_heads, lq // tq), in_specs=in_specs,
        out_specs=pl.BlockSpec((None, dv, tq), lambda h, i: (h, 0, qb0 + i)),
        out_shape=jax.ShapeDtypeStruct((n_heads, dv, n_out), bf16),
        input_output_aliases=aliases,
        compiler_params=_cparams(("parallel", "parallel")),
        name="attn_diff" if n_maps == 2 else "attn_mla",
    )(*args)


def _split3(x):
    hi = x.astype(bf16)
    r = x - hi.astype(f32)
    mid = r.astype(bf16)
    lo = (r - mid.astype(f32)).astype(bf16)
    return hi, mid, lo


def _log_sigmoid(x):
    return jnp.minimum(x, 0.0) - jnp.log(1.0 + jnp.exp(-jnp.abs(x)))


def _softcap(x):
    return GATE_SOFTCAP * jnp.tanh(x * (1.0 / GATE_SOFTCAP))


def _mlstm_kernel(gb_ref, gbt_ref,
                  qf_ref, ktf_ref, vf_ref, gf_ref, gtf_ref,
                  qb_ref, ktb_ref, vb_ref, gb2_ref, gtb_ref,
                  hf_ref, hb_ref, c_ref, m_ref):
    t = qf_ref.shape[0]

    @pl.when(pl.program_id(0) == 0)
    def _():
        c_ref[...] = jnp.zeros_like(c_ref)
        m_ref[...] = jnp.zeros_like(m_ref)

    row = lax.broadcasted_iota(jnp.int32, (t, t), 0)
    col = lax.broadcasted_iota(jnp.int32, (t, t), 1)
    lane = lax.broadcasted_iota(jnp.int32, (t, LANE), 1)
    ones_col = jnp.where(lane == 0, 1.0, 0.0).astype(bf16)
    dirs = ((qf_ref, ktf_ref, vf_ref, gf_ref, gtf_ref, hf_ref),
            (qb_ref, ktb_ref, vb_ref, gb2_ref, gtb_ref, hb_ref))
    for d, (q_ref, kt_ref, v_ref, g_ref, gt_ref, o_ref) in enumerate(dirs):
        mask = (col <= row) if d == 0 else (col >= row)
        maskb = mask.astype(bf16)
        last = t - 1 if d == 0 else 0
        g_tok = _softcap(g_ref[...] + gb_ref[...])
        g_t = _softcap(gt_ref[...] + gbt_ref[...])
        b_tok = sum(jnp.dot(maskb, part, preferred_element_type=f32) for part in _split3(_log_sigmoid(g_tok)))
        b_t = sum(lax.dot_general(part, maskb, NT_DIMS, preferred_element_type=f32)
                  for part in _split3(_log_sigmoid(g_t)))
        for hh in range(M_HEADS):
            ri = 4 * (2 * d) + hh
            rf = 4 * (2 * d + 1) + hh
            idx = d * M_HEADS + hh
            b_col = b_tok[:, rf:rf + 1]
            b_row = b_t[rf:rf + 1, :]
            a_row = g_t[ri:ri + 1, :] - b_row
            m_prev = m_ref[idx, 0:1, 0:1]
            ld = jnp.where(mask, b_col + a_row, -jnp.inf)
            inter = b_col + m_prev
            mj = jnp.maximum(inter, jnp.max(ld, axis=1, keepdims=True))
            dmat = jnp.exp(ld - mj)
            q = q_ref[:, hh * M_DK:(hh + 1) * M_DK]
            kt = kt_ref[hh * M_DK:(hh + 1) * M_DK, :]
            vp = jnp.concatenate([v_ref[:, hh * M_DV:(hh + 1) * M_DV], ones_col], axis=1)
            sc = jnp.dot(q, kt, preferred_element_type=f32) * dmat
            cp = c_ref[idx]
            num = (jnp.dot(sc.astype(bf16), vp, preferred_element_type=f32)
                   + jnp.exp(inter - mj) * jnp.dot(q, cp.astype(bf16), preferred_element_type=f32))
            den = num[:, M_DV:M_DV + 1]
            hj = num[:, 0:M_DV] / jnp.maximum(jnp.abs(den), jnp.exp(-mj))
            o_ref[:, hh * M_DV:(hh + 1) * M_DV] = hj.astype(o_ref.dtype)
            b_tot = b_t[rf:rf + 1, last:last + 1]
            lw = b_tot + a_row
            m_new = jnp.maximum(b_tot + m_prev, jnp.max(lw, axis=1, keepdims=True))
            kw = (kt.astype(f32) * jnp.exp(lw - m_new)).astype(bf16)
            c_ref[idx] = jnp.exp(b_tot + m_prev - m_new) * cp + jnp.dot(kw, vp, preferred_element_type=f32)
            m_ref[idx] = jnp.broadcast_to(m_new, (8, LANE))


def _mlstm(mq, mkt, mv, mg, mgt, gate_b):
    n = mq.shape[0]
    t = MLSTM_T
    nb = n // t
    gb = jnp.pad(gate_b.reshape(1, 4 * M_HEADS), ((0, 0), (0, LANE - 4 * M_HEADS)))
    gbt = gate_b.reshape(4 * M_HEADS, 1)
    fwd = lambda s: jnp.where(s == 0, nb - 1, s - 1)
    bwd = lambda s: nb - 1 - s
    specs = []
    for im in (fwd, bwd):
        specs += [pl.BlockSpec((t, M_W), lambda s, im=im: (im(s), 0)),
                  pl.BlockSpec((M_W, t), lambda s, im=im: (0, im(s))),
                  pl.BlockSpec((t, M_W), lambda s, im=im: (im(s), 0)),
                  pl.BlockSpec((t, LANE), lambda s, im=im: (im(s), 0)),
                  pl.BlockSpec((4 * M_HEADS, t), lambda s, im=im: (0, im(s)))]
    return pl.pallas_call(
        _mlstm_kernel, grid=(nb,),
        in_specs=[_const_spec(gb.shape), _const_spec(gbt.shape)] + specs,
        out_specs=[pl.BlockSpec((t, M_W), lambda s: (fwd(s), 0)), pl.BlockSpec((t, M_W), lambda s: (bwd(s), 0))],
        out_shape=[jax.ShapeDtypeStruct((n, M_W), bf16), jax.ShapeDtypeStruct((n, M_W), bf16)],
        scratch_shapes=[pltpu.VMEM((2 * M_HEADS, M_DK, 2 * M_DV), f32), pltpu.VMEM((2 * M_HEADS, 8, LANE), f32)],
        compiler_params=_cparams(("arbitrary",)), name="mlstm",
    )(gb, gbt, mq, mkt, mv, mg, mgt, mq, mkt, mv, mg, mgt)


def _rms(x, axis=-1):
    return x * lax.rsqrt(jnp.mean(x * x, axis=axis, keepdims=True) + EPS)


def _merge_kernel(x_ref, hf_ref, hb_ref, mo_ref, ya_ref, yd_ref, gl_ref, mng_ref, wb_ref, wout_ref,
                  mod_ref, rw_ref, rb_ref, xnew_ref, h2_ref, logit_ref):
    hs = hf_ref[...].astype(f32) + hb_ref[...].astype(f32)
    parts = [_rms(hs[:, hh * M_DV:(hh + 1) * M_DV]) for hh in range(M_HEADS)]
    ym = jnp.concatenate(parts, axis=1) * mng_ref[...] * jax.nn.sigmoid(mo_ref[...].astype(f32))
    yb0 = jnp.dot(ym.astype(bf16), wb_ref[0], preferred_element_type=f32)
    tm = x_ref.shape[0]
    tn_dims = (((0,), (0,)), ((), ()))
    yb1 = lax.dot_general(ya_ref[...].reshape(BRANCH_W, tm), wb_ref[1], tn_dims, preferred_element_type=f32)
    yb2 = lax.dot_general(yd_ref[...].reshape(BRANCH_W, tm), wb_ref[2], tn_dims, preferred_element_type=f32)
    d = D_MODEL
    z = (jax.nn.sigmoid(gl_ref[:, 0:d].astype(f32)) * yb0
         + jax.nn.sigmoid(gl_ref[:, d:2 * d].astype(f32)) * yb1
         + jax.nn.sigmoid(gl_ref[:, 2 * d:3 * d].astype(f32)) * yb2)
    y = jnp.dot(z.astype(bf16), wout_ref[...], preferred_element_type=f32)
    xnew = x_ref[...] + mod_ref[1:2, :] * (_rms(y) * mod_ref[0:1, :])
    h2 = _rms(xnew) * mod_ref[2:3, :] + mod_ref[3:4, :]
    xnew_ref[...] = xnew
    h2_ref[...] = h2.astype(bf16)
    h2_hi = h2.astype(bf16)
    h2_lo = (h2 - h2_hi.astype(f32)).astype(bf16)
    rw = rw_ref[...]
    rw_hi = rw.astype(bf16)
    rw_lo = (rw - rw_hi.astype(f32)).astype(bf16)
    logit_ref[...] = (jnp.dot(h2_hi, rw_hi, preferred_element_type=f32)
                      + jnp.dot(h2_lo, rw_hi, preferred_element_type=f32)
                      + jnp.dot(h2_hi, rw_lo, preferred_element_type=f32) + rb_ref[...])


def _merge(n_rows, n_lat_blocks, x_all, hf, hb, mo, ya, yd, gl, mng, wb, wout, mod, rw, rb):
    tm = ROW_BLOCK
    nb = n_rows // tm
    row = lambda i: (i, 0)
    grp = lambda i: (jnp.where(i >= n_lat_blocks, 1, 0), 0, 0)
    in_specs = [pl.BlockSpec((tm, D_MODEL), row), pl.BlockSpec((tm, M_W), row), pl.BlockSpec((tm, M_W), row),
                pl.BlockSpec((tm, M_W), row),
                pl.BlockSpec((A_HEADS, A_V, tm), lambda i: (0, 0, i)),
                pl.BlockSpec((DF_HEADS, DF_DV, tm), lambda i: (0, 0, i)),
                pl.BlockSpec((tm, N_BRANCH * D_MODEL), row), _const_spec(mng.shape), _const_spec(wb.shape),
                _const_spec(wout.shape), pl.BlockSpec((None, 4, D_MODEL), grp),
                _const_spec(rw.shape), _const_spec(rb.shape)]
    return pl.pallas_call(
        _merge_kernel, grid=(nb,), in_specs=in_specs,
        out_specs=[pl.BlockSpec((tm, D_MODEL), row), pl.BlockSpec((tm, D_MODEL), row), pl.BlockSpec((tm, LANE), row)],
        out_shape=[jax.ShapeDtypeStruct((n_rows, D_MODEL), f32),
                   jax.ShapeDtypeStruct((max(n_rows, H2_ROWS), D_MODEL), bf16),
                   jax.ShapeDtypeStruct((n_rows, LANE), f32)],
        compiler_params=_cparams(("parallel",)), name="merge",
    )(x_all, hf, hb, mo, ya, yd, gl, mng, wb, wout, mod, rw, rb)


def _moe_kernel(be_ref, nu_ref, xs_ref, wgu_ref, bgu_ref, wd_ref, bd_ref, o_ref, wgu_s, wd_s):
    i = pl.program_id(0)
    prev = be_ref[jnp.maximum(i - 1, 0)]
    used = i < nu_ref[0]

    @pl.when(jnp.logical_and(used, jnp.logical_or(i == 0, be_ref[i] != prev)))
    def _():
        wgu_s[...] = wgu_ref[...].astype(bf16)
        wd_s[...] = wd_ref[...].astype(bf16)

    @pl.when(used)
    def _():
        gu = jnp.dot(xs_ref[...], wgu_s[...], preferred_element_type=f32) + bgu_ref[...]
        gate = jnp.minimum(gu[:, :D_EXPERT], SWIGLU_LIMIT)
        up = jnp.clip(gu[:, D_EXPERT:], -SWIGLU_LIMIT, SWIGLU_LIMIT)
        act = (up + 1.0) * (gate * jax.nn.sigmoid(SWIGLU_ALPHA * gate))
        y = jnp.dot(act.astype(bf16), wd_s[...], preferred_element_type=f32) + bd_ref[...]
        o_ref[...] = y.astype(o_ref.dtype)

    @pl.when(jnp.logical_not(used))
    def _():
        o_ref[...] = jnp.zeros_like(o_ref)


def _moe_experts(blk_e, n_used, xs, layer, w_gu, b_gu, w_down, b_down):
    cap = xs.shape[0]
    tm = MOE_TM
    nblk = cap // tm
    grid_spec = pltpu.PrefetchScalarGridSpec(
        num_scalar_prefetch=2, grid=(nblk,),
        in_specs=[pl.BlockSpec((tm, D_MODEL), lambda i, be, nu: (i, 0)),
                  pl.BlockSpec((None, None, D_MODEL, 2 * D_EXPERT), lambda i, be, nu: (layer, be[i], 0, 0)),
                  pl.BlockSpec((None, None, 1, 2 * D_EXPERT), lambda i, be, nu: (layer, be[i], 0, 0)),
                  pl.BlockSpec((None, None, D_EXPERT, D_MODEL), lambda i, be, nu: (layer, be[i], 0, 0)),
                  pl.BlockSpec((None, None, 1, D_MODEL), lambda i, be, nu: (layer, be[i], 0, 0))],
        out_specs=pl.BlockSpec((tm, D_MODEL), lambda i, be, nu: (i, 0)),
        scratch_shapes=[pltpu.VMEM((D_MODEL, 2 * D_EXPERT), bf16), pltpu.VMEM((D_EXPERT, D_MODEL), bf16)])
    return pl.pallas_call(
        _moe_kernel, grid_spec=grid_spec,
        out_shape=jax.ShapeDtypeStruct((cap, D_MODEL), bf16),
        compiler_params=_cparams(("arbitrary",)), name="moe_experts",
    )(blk_e, n_used, xs, w_gu, b_gu.reshape(DEPTH, N_EXPERTS, 1, -1), w_down, b_down.reshape(DEPTH, N_EXPERTS, 1, -1))


def _moe(h2, logits, layer, w_gu, b_gu, w_down, b_down):
    n = logits.shape[0]
    tm = MOE_TM
    i32 = jnp.int32
    top_v, top_i = lax.top_k(logits[:, :N_EXPERTS], TOP_K)
    wts = jax.nn.softmax(top_v, axis=-1)
    nk = n * TOP_K
    flat_e = top_i.reshape(nk).astype(i32)
    iota = jnp.arange(nk, dtype=i32)
    skey = lax.sort(flat_e * nk + iota)
    se = skey // nk
    order = skey - se * nk
    e_ids = jnp.arange(N_EXPERTS, dtype=i32)
    grp_start = jnp.sum((se[:, None] < e_ids[None, :]).astype(i32), axis=0)
    counts = jnp.sum((se[:, None] == e_ids[None, :]).astype(i32), axis=0)
    padded = (counts + tm - 1) // tm * tm
    pad_end = jnp.cumsum(padded)
    pad_start = pad_end - padded
    off = pad_start - grp_start
    hit = top_i[:, :, None] == e_ids[None, None, :]
    per_tok = jnp.sum(hit.astype(i32), axis=1)
    rank = jnp.cumsum(per_tok, axis=0) - per_tok
    pos = jnp.sum(jnp.where(hit, (pad_start[None, :] + rank)[:, None, :], 0), axis=2)
    pos_km = pos.T.reshape(nk)
    cap = -(-(nk + N_EXPERTS * (tm - 1)) // tm) * tm
    nblk = cap // tm
    bstart = jnp.arange(nblk, dtype=i32) * tm
    blk_e = jnp.minimum(jnp.sum((pad_end[None, :] <= bstart[:, None]).astype(i32), axis=1), N_EXPERTS - 1)
    n_used = (pad_end[-1] // tm).astype(i32).reshape(1)
    sel = blk_e[:, None] == e_ids[None, :]
    off_b = jnp.sum(jnp.where(sel, off[None, :], 0), axis=1)
    end_b = jnp.sum(jnp.where(sel, (pad_start + counts)[None, :], 0), axis=1)
    p2 = bstart[:, None] + jnp.arange(tm, dtype=i32)[None, :]
    src = jnp.clip(p2 - off_b[:, None], 0, nk - 1)
    slot_t = jnp.where(p2 < end_b[:, None], (order // TOP_K)[src], p2 % n).reshape(cap)
    xs = h2[slot_t]
    ys = _moe_experts(blk_e, n_used, xs, layer, w_gu, b_gu, w_down, b_down)
    return ys[pos_km].reshape(TOP_K, n, D_MODEL), wts


def _resid_kernel(x_ref, y_ref, w_ref, mod_ref, o_ref):
    w = w_ref[...]
    f = y_ref[0].astype(f32) * w[:, 0:1]
    for kk in range(1, TOP_K):
        f = f + y_ref[kk].astype(f32) * w[:, kk:kk + 1]
    o_ref[...] = x_ref[...] + mod_ref[1:2, :] * (_rms(f) * mod_ref[0:1, :])


def _resid(n_lat_blocks, xnew, picked, wts, mod):
    n = xnew.shape[0]
    tm = ROW_BLOCK
    row = lambda i: (i, 0)
    grp = lambda i: (jnp.where(i >= n_lat_blocks, 1, 0), 0, 0)
    return pl.pallas_call(
        _resid_kernel, grid=(n // tm,),
        in_specs=[pl.BlockSpec((tm, D_MODEL), row), pl.BlockSpec((TOP_K, tm, D_MODEL), lambda i: (0, i, 0)),
                  pl.BlockSpec((tm, TOP_K), row), pl.BlockSpec((None, 2, D_MODEL), grp)],
        out_specs=pl.BlockSpec((tm, D_MODEL), row),
        out_shape=jax.ShapeDtypeStruct((n, D_MODEL), f32),
        compiler_params=_cparams(("parallel",)), name="resid",
    )(xnew, picked, wts, mod)


def _rope_tables(n_lat, n_ctx, dim):
    rows = n_lat // GRID_W
    row = jnp.repeat(jnp.arange(rows, dtype=f32), GRID_W)
    col = jnp.tile(jnp.arange(GRID_W, dtype=f32), rows)
    quarter = dim // 4
    inv = ROPE_BASE ** (-jnp.arange(quarter, dtype=f32) / quarter)
    ang = jnp.concatenate([row[:, None] * inv, col[:, None] * inv], axis=-1)
    cos = jnp.concatenate([jnp.cos(ang), jnp.ones((n_ctx, dim // 2), f32)], axis=0).T
    sin = jnp.concatenate([jnp.sin(ang), jnp.zeros((n_ctx, dim // 2), f32)], axis=0).T
    return cos, sin


def kernel(x, c, ctx, c_ctx, ada_w, ada_b, norm_g, w_in, m_gate_b, m_norm_g, q_norm_g, w_uq, kv_norm_g, w_ukv,
           diff_lam, diff_norm_g, w_branch, w_out, router_w, router_b, w_gu, b_gu, w_down, b_down):
    n_lat = x.shape[1]
    n_ctx = ctx.shape[1]
    n = n_lat + n_ctx
    assert x.shape[0] == 1 and n_lat % ROW_BLOCK == 0 and n_ctx == MLSTM_T and n_lat % GRID_W == 0
    nlb = n_lat // ROW_BLOCK
    x_all = jnp.concatenate([x[0], ctx[0]], axis=0)

    cond = jnp.zeros((8, D_MODEL), f32).at[0].set(c[0]).at[1].set(c_ctx)
    mods = _adaln(cond, ada_w, ada_b)[:, :2].reshape(DEPTH, 2, 6, D_MODEL)

    ca, sa = _rope_tables(n_lat, n_ctx, A_ROPE)
    cd, sd = _rope_tables(n_lat, n_ctx, DF_DK)

    for l in range(DEPTH):
        need_ctx = l < DEPTH - 1
        lam_init = 0.8 - 0.6 * math.exp(-0.3 * l)
        sh1, sc1, ga1, sh2, sc2, ga2 = [mods[l, :, j] for j in range(6)]
        ab1 = jnp.stack([norm_g[l, 0][None] * (1.0 + sc1), sh1], axis=1)
        w_tok, w_tr, wt_uq, wt_ukv = _pack_inproj_weights(w_in[l], w_uq[l], w_ukv[l])
        (mq, mv, mo, mg, gl, mkt, mgt, qta, k_a, vta, qtd, k_d, vtd) = _inproj(
            x_all, ab1, nlb, w_tok, w_tr, wt_uq, wt_ukv,
            q_norm_g[l].reshape(Q_LORA, 1), kv_norm_g[l].reshape(KV_LORA, 1), ca, sa, cd, sd)

        hf, hb = _mlstm(mq, mkt, mv, mg, mgt, m_gate_b[l])

        qtd4 = qtd.reshape(DF_HEADS, 2 * DF_DK, n)
        dg = diff_norm_g[l].reshape(DF_HEADS, DF_DV, 1)
        n_rows = n if need_ctx else n_lat
        attn_a = functools.partial(_attention, n_maps=1, dk=A_DKP, dv=A_V, n_out=n_rows)
        attn_d = functools.partial(_attention, n_maps=2, dk=DF_DK, dv=DF_DV, n_out=n_rows,
                                   lam=diff_lam[l], g=dg, lam_init=lam_init)
        ya = attn_a(qta, k_a, vta, tq=ATTN_TQ_MLA, lq=n_lat)
        yd = attn_d(qtd4, k_d, vtd, tq=ATTN_TQ_DIFF, lq=n_lat)
        if need_ctx:
            ctx_tile = lambda v: v[:, -1:, :, v.shape[3] - n_ctx:]
            ya = attn_a(qta, k_a[:, n_lat:], ctx_tile(vta), tq=n_ctx, lq=n_ctx, q_start=n_lat, out=ya)
            yd = attn_d(qtd4, k_d[:, n_lat:], ctx_tile(vtd), tq=n_ctx, lq=n_ctx, q_start=n_lat, out=yd)

        mod_m = jnp.stack([jnp.broadcast_to(norm_g[l, 1][None], (2, D_MODEL)), ga1,
                           norm_g[l, 2][None] * (1.0 + sc2), sh2], axis=1)
        rw = jnp.pad(router_w[l], ((0, 0), (0, LANE - N_EXPERTS)))
        rb = jnp.pad(router_b[l].reshape(1, N_EXPERTS), ((0, 0), (0, LANE - N_EXPERTS)))
        xnew, h2, logits = _merge(n_rows, nlb, x_all, hf, hb, mo, ya, yd, gl,
                                  m_norm_g[l].reshape(1, -1), w_branch[l].astype(bf16), w_out[l].astype(bf16),
                                  mod_m, rw, rb)
        picked, wts = _moe(h2, logits, l, w_gu, b_gu, w_down, b_down)
        mod_r = jnp.stack([jnp.broadcast_to(norm_g[l, 3][None], (2, D_MODEL)), ga2], axis=1)
        x_all = _resid(nlb, xnew, picked, wts, mod_r)
    return x_all[:n_lat][None]
```

```python
import functools
import math

import jax
import jax.numpy as jnp
from jax import lax
from jax.experimental import pallas as pl
from jax.experimental.pallas import tpu as pltpu

f32 = jnp.float32
bf16 = jnp.bfloat16

D_MODEL = 1024
DEPTH = 2
GRID_W = 64
EPS = 1e-6
ROPE_BASE = 10000.0
M_HEADS = 4
M_DK = 128
M_DV = 128
GATE_SOFTCAP = 15.0
A_HEADS = 8
A_NOPE = 64
A_ROPE = 32
A_V = 64
Q_LORA = 256
KV_LORA = 128
DF_HEADS = 4
DF_DK = 64
DF_DV = 128
N_BRANCH = 3
BRANCH_W = 512
N_EXPERTS = 32
TOP_K = 4
D_EXPERT = 1024
SWIGLU_LIMIT = 7.0
SWIGLU_ALPHA = 1.702

IN_SIZES = (M_HEADS * M_DK, M_HEADS * M_DK, M_HEADS * M_DV, M_HEADS * M_DV, 4 * M_HEADS,
            Q_LORA, KV_LORA, A_ROPE,
            2 * DF_HEADS * DF_DK, 2 * DF_HEADS * DF_DK, DF_HEADS * DF_DV,
            N_BRANCH * D_MODEL)

LOG2E = 1.4426950408889634
LANE = 128
A_DKP = 128
A_DVP = A_V + 16
DF_DVP = DF_DV + 16
VMEM_LIMIT = 56 * 1024 * 1024

ROW_BLOCK = 256
MXU_DEPTH = 256
ATTN_TQ_MLA = 2048
ATTN_TQ_DIFF = 1024
ATTN_QSUB = 256
ATTN_TK = 1280
MLSTM_T = 256
MOE_TM = 256
H2_ROWS = 32768

NT_DIMS = (((1,), (1,)), ((), ()))


def _cparams(sem):
    return pltpu.CompilerParams(dimension_semantics=sem, vmem_limit_bytes=VMEM_LIMIT)


def _const_spec(shape):
    nd = len(shape)
    return pl.BlockSpec(shape, lambda *_: (0,) * nd)


def _adaln_kernel(cond_ref, w_ref, b_ref, o_ref):
    cnd = cond_ref[...]
    a = cnd * jax.nn.sigmoid(cnd)
    o_ref[...] = jnp.dot(a, w_ref[...], preferred_element_type=f32,
                         precision=lax.Precision.HIGHEST) + b_ref[...]


def _adaln(cond, ada_w, ada_b):
    tn = 1536
    nj = (6 * D_MODEL) // tn
    return pl.pallas_call(
        _adaln_kernel,
        grid=(DEPTH, nj),
        in_specs=[pl.BlockSpec((8, D_MODEL), lambda l, j: (0, 0)),
                  pl.BlockSpec((None, D_MODEL, tn), lambda l, j: (l, 0, j)),
                  pl.BlockSpec((None, 1, tn), lambda l, j: (l, 0, j))],
        out_specs=pl.BlockSpec((None, 8, tn), lambda l, j: (l, 0, j)),
        out_shape=jax.ShapeDtypeStruct((DEPTH, 8, 6 * D_MODEL), f32),
        compiler_params=_cparams(("parallel", "parallel")),
        name="adaln",
    )(cond, ada_w, ada_b.reshape(DEPTH, 1, 6 * D_MODEL))


M_W = M_HEADS * M_DK
DF_W = 2 * DF_HEADS * DF_DK
TOK_SIZES = (M_W, M_W, M_W, LANE, N_BRANCH * D_MODEL)
TR_SIZES = (M_W, 4 * M_HEADS, Q_LORA, KV_LORA, A_ROPE, DF_W, DF_W, DF_HEADS * DF_DV)


def _bounds(sizes):
    offs = [0]
    for sz in sizes:
        offs.append(offs[-1] + sz)
    return tuple(zip(offs[:-1], offs[1:]))


TOK_B = _bounds(TOK_SIZES)
TR_B = _bounds(TR_SIZES)
TOK_W = TOK_B[-1][1]
TR_W = TR_B[-1][1]


def _ones_row_block(tm):
    r = lax.broadcasted_iota(jnp.int32, (16, tm), 0)
    return jnp.where(r == 0, 1.0, 0.0).astype(bf16)


def _inproj_kernel(x_ref, ab_ref, wtok_ref, wtr_ref, wuq_ref, wukv_ref, gq_ref, gkv_ref,
                   ca_ref, sa_ref, cd_ref, sd_ref,
                   mq_ref, mv_ref, mo_ref, mg_ref, gl_ref, mkt_ref, mgt_ref,
                   qta_ref, kta_ref, vta_ref, qtd_ref, ktd_ref, vtd_ref):
    tm = x_ref.shape[0]
    x = x_ref[...]
    ms = jnp.mean(x * x, axis=-1, keepdims=True)
    h = (x * lax.rsqrt(ms + EPS)) * ab_ref[0:1, :] + ab_ref[1:2, :]
    hb = h.astype(bf16)

    tok = jnp.dot(hb, wtok_ref[...], preferred_element_type=f32)
    mq, mv, mo, mg, gl = [tok[:, a:b] for a, b in TOK_B]
    mq_ref[...] = (mq * (M_DK ** -0.5)).astype(bf16)
    mv_ref[...] = mv.astype(bf16)
    mo_ref[...] = mo.astype(bf16)
    mg_ref[...] = mg
    gl_ref[...] = gl.astype(bf16)

    tr = lax.dot_general(wtr_ref[...], hb, NT_DIMS, preferred_element_type=f32)
    mk, mgt, cq, ckv, kpe, dq, dk, dv = [tr[a:b] for a, b in TR_B]
    mkt_ref[...] = mk.astype(bf16)
    mgt_ref[...] = mgt

    ca = ca_ref[...]
    sa = sa_ref[...]
    cd = cd_ref[...]
    sd = sd_ref[...]
    ones_blk = _ones_row_block(tm)
    zero_pad = jnp.zeros((A_DKP - A_NOPE - A_ROPE, tm), bf16)

    cqn = (cq * lax.rsqrt(jnp.mean(cq * cq, axis=0, keepdims=True) + EPS) * gq_ref[...]).astype(bf16)
    qscale = ((A_NOPE + A_ROPE) ** -0.5) * LOG2E
    qt = jnp.dot(wuq_ref[...], cqn, preferred_element_type=f32) * qscale
    nope_w = A_HEADS * A_NOPE
    rh = A_ROPE // 2
    for hh in range(A_HEADS):
        a = qt[nope_w + rh * hh:nope_w + rh * (hh + 1)]
        b = qt[nope_w + rh * (A_HEADS + hh):nope_w + rh * (A_HEADS + hh + 1)]
        qta_ref[hh, 0:A_NOPE, :] = qt[A_NOPE * hh:A_NOPE * (hh + 1)].astype(bf16)
        qta_ref[hh, A_NOPE:A_NOPE + rh, :] = (a * ca - b * sa).astype(bf16)
        qta_ref[hh, A_NOPE + rh:A_NOPE + A_ROPE, :] = (a * sa + b * ca).astype(bf16)
        qta_ref[hh, A_NOPE + A_ROPE:A_DKP, :] = zero_pad

    ckvn = (ckv * lax.rsqrt(jnp.mean(ckv * ckv, axis=0, keepdims=True) + EPS) * gkv_ref[...]).astype(bf16)
    kvt = jnp.dot(wukv_ref[...], ckvn, preferred_element_type=f32)
    ka = kpe[0:rh]
    kb = kpe[rh:A_ROPE]
    kpe_rot = jnp.concatenate([ka * ca - kb * sa, ka * sa + kb * ca,
                               jnp.zeros((A_DKP - A_NOPE - A_ROPE, tm), f32)], axis=0)
    for hh in range(A_HEADS):
        kt_h = jnp.concatenate([kvt[A_NOPE * hh:A_NOPE * (hh + 1)], kpe_rot], axis=0)
        kta_ref[hh] = kt_h.T.astype(bf16)
        vta_ref[hh, 0:A_V, :] = kvt[nope_w + A_V * hh:nope_w + A_V * (hh + 1)].astype(bf16)
        vta_ref[hh, A_V:A_DVP, :] = ones_blk

    dscale = (DF_DK ** -0.5) * LOG2E
    dh = DF_DK // 2
    for hh in range(2 * DF_HEADS):
        a = dq[dh * hh:dh * (hh + 1)]
        b = dq[DF_W // 2 + dh * hh:DF_W // 2 + dh * (hh + 1)]
        qtd_ref[hh, 0:dh, :] = ((a * cd - b * sd) * dscale).astype(bf16)
        qtd_ref[hh, dh:DF_DK, :] = ((a * sd + b * cd) * dscale).astype(bf16)
        a = dk[dh * hh:dh * (hh + 1)]
        b = dk[DF_W // 2 + dh * hh:DF_W // 2 + dh * (hh + 1)]
        ktd_ref[hh] = jnp.concatenate([a * cd - b * sd, a * sd + b * cd], axis=0).T.astype(bf16)
    for hh in range(DF_HEADS):
        vtd_ref[hh, 0:DF_DV, :] = dv[DF_DV * hh:DF_DV * (hh + 1)].astype(bf16)
        vtd_ref[hh, DF_DV:DF_DVP, :] = ones_blk


def _pack_inproj_weights(w, w_uq, w_ukv):
    o = [0]
    for s in IN_SIZES:
        o.append(o[-1] + s)
    mq, mk, mv, mo, mg, cq, ckv, kpe, dq, dk, dv, gl = [w[:, o[i]:o[i + 1]] for i in range(12)]
    mg_pad = jnp.pad(mg, ((0, 0), (0, LANE - mg.shape[1])))
    w_tok = jnp.concatenate([mq, mv, mo, mg_pad, gl], axis=1).astype(bf16)

    def split_halves(t, n_heads, d):
        t3 = t.reshape(t.shape[0], n_heads, d)
        return jnp.concatenate([t3[:, :, :d // 2].reshape(t.shape[0], -1),
                                t3[:, :, d // 2:].reshape(t.shape[0], -1)], axis=1)

    w_tr = jnp.concatenate([mk, mg, cq, ckv, kpe, split_halves(dq, 2 * DF_HEADS, DF_DK),
                            split_halves(dk, 2 * DF_HEADS, DF_DK), dv], axis=1).T.astype(bf16)
    uq = w_uq.reshape(Q_LORA, A_HEADS, A_NOPE + A_ROPE)
    half = A_ROPE // 2
    wt_uq = jnp.concatenate([uq[:, :, :A_NOPE].reshape(Q_LORA, -1),
                             uq[:, :, A_NOPE:A_NOPE + half].reshape(Q_LORA, -1),
                             uq[:, :, A_NOPE + half:].reshape(Q_LORA, -1)], axis=1).T.astype(bf16)
    ukv = w_ukv.reshape(KV_LORA, A_HEADS, A_NOPE + A_V)
    wt_ukv = jnp.concatenate([ukv[:, :, :A_NOPE].reshape(KV_LORA, -1),
                              ukv[:, :, A_NOPE:].reshape(KV_LORA, -1)], axis=1).T.astype(bf16)
    return w_tok, w_tr, wt_uq, wt_ukv


def _inproj(x_all, ab, n_lat_blocks, w_tok, w_tr, wt_uq, wt_ukv, gq, gkv, ca, sa, cd, sd):
    n = x_all.shape[0]
    tm = ROW_BLOCK
    tk = _key_tile(n)
    per = tk // tm
    vtile = lambda i: (0, i // per, 0, i % per)
    nb = n // tm
    row = lambda i: (i, 0)
    colb = lambda i: (0, i)
    col3 = lambda i: (0, 0, i)
    grp = lambda i: (jnp.where(i >= n_lat_blocks, 1, 0), 0, 0)
    out_shapes = [
        jax.ShapeDtypeStruct((n, M_W), bf16), jax.ShapeDtypeStruct((n, M_W), bf16),
        jax.ShapeDtypeStruct((n, M_W), bf16), jax.ShapeDtypeStruct((n, LANE), f32),
        jax.ShapeDtypeStruct((n, N_BRANCH * D_MODEL), bf16),
        jax.ShapeDtypeStruct((M_W, n), bf16), jax.ShapeDtypeStruct((4 * M_HEADS, n), f32),
        jax.ShapeDtypeStruct((A_HEADS, A_DKP, n), bf16), jax.ShapeDtypeStruct((A_HEADS, n, A_DKP), bf16),
        jax.ShapeDtypeStruct((A_HEADS, n // tk, A_DVP, tk), bf16),
        jax.ShapeDtypeStruct((2 * DF_HEADS, DF_DK, n), bf16), jax.ShapeDtypeStruct((2 * DF_HEADS, n, DF_DK), bf16),
        jax.ShapeDtypeStruct((DF_HEADS, n // tk, DF_DVP, tk), bf16),
    ]
    out_specs = [
        pl.BlockSpec((tm, M_W), row), pl.BlockSpec((tm, M_W), row), pl.BlockSpec((tm, M_W), row),
        pl.BlockSpec((tm, LANE), row), pl.BlockSpec((tm, N_BRANCH * D_MODEL), row),
        pl.BlockSpec((M_W, tm), colb), pl.BlockSpec((4 * M_HEADS, tm), colb),
        pl.BlockSpec((A_HEADS, A_DKP, tm), col3), pl.BlockSpec((A_HEADS, tm, A_DKP), lambda i: (0, i, 0)),
        pl.BlockSpec((A_HEADS, None, A_DVP, tm), vtile),
        pl.BlockSpec((2 * DF_HEADS, DF_DK, tm), col3), pl.BlockSpec((2 * DF_HEADS, tm, DF_DK), lambda i: (0, i, 0)),
        pl.BlockSpec((DF_HEADS, None, DF_DVP, tm), vtile),
    ]
    in_specs = [
        pl.BlockSpec((tm, D_MODEL), row),
        pl.BlockSpec((None, 2, D_MODEL), grp),
        _const_spec(w_tok.shape), _const_spec(w_tr.shape), _const_spec(wt_uq.shape), _const_spec(wt_ukv.shape),
        _const_spec(gq.shape), _const_spec(gkv.shape),
        pl.BlockSpec((A_ROPE // 2, tm), colb), pl.BlockSpec((A_ROPE // 2, tm), colb),
        pl.BlockSpec((DF_DK // 2, tm), colb), pl.BlockSpec((DF_DK // 2, tm), colb),
    ]
    return pl.pallas_call(
        _inproj_kernel, grid=(nb,), in_specs=in_specs, out_specs=out_specs, out_shape=out_shapes,
        compiler_params=_cparams(("parallel",)), name="inproj",
    )(x_all, ab, w_tok, w_tr, wt_uq, wt_ukv, gq, gkv, ca, sa, cd, sd)


def _attn_kernel(*refs, n_maps, dk, dv, tk, n_kt, qsub, lam_init, n_in):
    o_ref = refs[n_in]
    if n_maps == 2:
        lam_ref, g_ref, qt_ref, k_ref, vt_ref = refs[:5]
    else:
        qt_ref, k_ref, vt_ref = refs[:3]
    tq = qt_ref.shape[-1]
    dvp = vt_ref.shape[-2]
    chains = [(m, c) for m in range(n_maps) for c in range(tq // qsub)]

    def qk(j, m, c):
        qt = qt_ref[m * dk:(m + 1) * dk, c * qsub:(c + 1) * qsub]
        return jnp.dot(k_ref[m, j * tk:(j + 1) * tk, :], qt, preferred_element_type=f32)

    state = [(jnp.full((1, qsub), -jnp.inf, f32), jnp.zeros((dvp, qsub), f32)) for _ in chains]
    s_cur = [qk(0, m, c) for (m, c) in chains]
    for j in range(n_kt):
        vtj = vt_ref[j]
        for i, (m, c) in enumerate(chains):
            s_nxt = qk(j + 1, m, c) if j + 1 < n_kt else None
            m_run, acc = state[i]
            m_new = jnp.maximum(m_run, jnp.max(s_cur[i], axis=0, keepdims=True))
            alpha = jnp.exp2(m_run - m_new)
            p = jnp.exp2(s_cur[i] - m_new).astype(bf16)
            pv = jnp.dot(vtj, p, preferred_element_type=f32)
            state[i] = (m_new, alpha * acc + pv)
            s_cur[i] = s_nxt
    outs = []
    for m in range(n_maps):
        accs = [state[i][1] for i, (mm, _) in enumerate(chains) if mm == m]
        acc = accs[0] if len(accs) == 1 else jnp.concatenate(accs, axis=1)
        outs.append(acc[0:dv] / acc[dv:dv + 1])
    if n_maps == 1:
        o_ref[...] = outs[0].astype(o_ref.dtype)
    else:
        lf = lam_ref[...]
        lam = (jnp.exp(jnp.sum(lf[0:1] * lf[1:2], axis=1, keepdims=True))
               - jnp.exp(jnp.sum(lf[2:3] * lf[3:4], axis=1, keepdims=True)) + lam_init)
        o = outs[0] - lam * outs[1]
        o = o * lax.rsqrt(jnp.mean(o * o, axis=0, keepdims=True) + EPS) * g_ref[...]
        o_ref[...] = (o * (1.0 - lam_init)).astype(o_ref.dtype)


def _key_tile(lk):
    return max(t for t in range(MXU_DEPTH, min(ATTN_TK, lk) + 1, MXU_DEPTH) if lk % t == 0)


def _attention(qt, k, vt4, *, n_maps, dk, dv, tq, lq, n_out, q_start=0, out=None, lam=None, g=None, lam_init=0.0):
    n_heads = qt.shape[0]
    lk = k.shape[1]
    _, n_kt, dvp, tk = vt4.shape
    assert n_kt * tk == lk
    tq = min(tq, lq)
    qsub = min(ATTN_QSUB, tq)
    qb0 = q_start // tq
    in_specs = [
        pl.BlockSpec((None, n_maps * dk, tq), lambda h, i: (h, 0, qb0 + i)),
        pl.BlockSpec((n_maps, lk, dk), lambda h, i: (h, 0, 0)),
        pl.BlockSpec((None, n_kt, dvp, tk), lambda h, i: (h, 0, 0, 0)),
    ]
    args = [qt, k, vt4]
    if n_maps == 2:
        in_specs = [_const_spec(lam.shape), pl.BlockSpec((None, dv, 1), lambda h, i: (h, 0, 0))] + in_specs
        args = [lam, g] + args
    aliases = {}
    if out is not None:
        aliases = {len(args): 0}
        in_specs.append(pl.BlockSpec(memory_space=pl.ANY))
        args.append(out)
    kern = functools.partial(_attn_kernel, n_maps=n_maps, dk=dk, dv=dv, tk=tk, n_kt=n_kt, qsub=qsub,
                             lam_init=lam_init, n_in=len(args))
    return pl.pallas_call(
        kern, grid=(n_heads, lq // tq), in_specs=in_specs,
        out_specs=pl.BlockSpec((None, dv, tq), lambda h, i: (h, 0, qb0 + i)),
        out_shape=jax.ShapeDtypeStruct((n_heads, dv, n_out), bf16),
        input_output_aliases=aliases,
        compiler_params=_cparams(("parallel", "parallel")),
        name="attn_diff" if n_maps == 2 else "attn_mla",
    )(*args)


def _split3(x):
    hi = x.astype(bf16)
    r = x - hi.astype(f32)
    mid = r.astype(bf16)
    lo = (r - mid.astype(f32)).astype(bf16)
    return hi, mid, lo


def _log_sigmoid(x):
    return jnp.minimum(x, 0.0) - jnp.log(1.0 + jnp.exp(-jnp.abs(x)))


def _softcap(x):
    return GATE_SOFTCAP * jnp.tanh(x * (1.0 / GATE_SOFTCAP))


def _mlstm_kernel(gb_ref, gbt_ref,
                  qf_ref, ktf_ref, vf_ref, gf_ref, gtf_ref,
                  qb_ref, ktb_ref, vb_ref, gb2_ref, gtb_ref,
                  hf_ref, hb_ref, c_ref, m_ref):
    t = qf_ref.shape[0]

    @pl.when(pl.program_id(0) == 0)
    def _():
        c_ref[...] = jnp.zeros_like(c_ref)
        m_ref[...] = jnp.zeros_like(m_ref)

    row = lax.broadcasted_iota(jnp.int32, (t, t), 0)
    col = lax.broadcasted_iota(jnp.int32, (t, t), 1)
    lane = lax.broadcasted_iota(jnp.int32, (t, LANE), 1)
    ones_col = jnp.where(lane == 0, 1.0, 0.0).astype(bf16)
    dirs = ((qf_ref, ktf_ref, vf_ref, gf_ref, gtf_ref, hf_ref),
            (qb_ref, ktb_ref, vb_ref, gb2_ref, gtb_ref, hb_ref))
    for d, (q_ref, kt_ref, v_ref, g_ref, gt_ref, o_ref) in enumerate(dirs):
        mask = (col <= row) if d == 0 else (col >= row)
        maskb = mask.astype(bf16)
        last = t - 1 if d == 0 else 0
        g_tok = _softcap(g_ref[...] + gb_ref[...])
        g_t = _softcap(gt_ref[...] + gbt_ref[...])
        b_tok = sum(jnp.dot(maskb, part, preferred_element_type=f32) for part in _split3(_log_sigmoid(g_tok)))
        b_t = sum(lax.dot_general(part, maskb, NT_DIMS, preferred_element_type=f32)
                  for part in _split3(_log_sigmoid(g_t)))
        for hh in range(M_HEADS):
            ri = 4 * (2 * d) + hh
            rf = 4 * (2 * d + 1) + hh
            idx = d * M_HEADS + hh
            b_col = b_tok[:, rf:rf + 1]
            b_row = b_t[rf:rf + 1, :]
            a_row = g_t[ri:ri + 1, :] - b_row
            m_prev = m_ref[idx, 0:1, 0:1]
            ld = jnp.where(mask, b_col + a_row, -jnp.inf)
            inter = b_col + m_prev
            mj = jnp.maximum(inter, jnp.max(ld, axis=1, keepdims=True))
            dmat = jnp.exp(ld - mj)
            q = q_ref[:, hh * M_DK:(hh + 1) * M_DK]
            kt = kt_ref[hh * M_DK:(hh + 1) * M_DK, :]
            vp = jnp.concatenate([v_ref[:, hh * M_DV:(hh + 1) * M_DV], ones_col], axis=1)
            sc = jnp.dot(q, kt, preferred_element_type=f32) * dmat
            cp = c_ref[idx]
            num = (jnp.dot(sc.astype(bf16), vp, preferred_element_type=f32)
                   + jnp.exp(inter - mj) * jnp.dot(q, cp.astype(bf16), preferred_element_type=f32))
            den = num[:, M_DV:M_DV + 1]
            hj = num[:, 0:M_DV] / jnp.maximum(jnp.abs(den), jnp.exp(-mj))
            o_ref[:, hh * M_DV:(hh + 1) * M_DV] = hj.astype(o_ref.dtype)
            b_tot = b_t[rf:rf + 1, last:last + 1]
            lw = b_tot + a_row
            m_new = jnp.maximum(b_tot + m_prev, jnp.max(lw, axis=1, keepdims=True))
            kw = (kt.astype(f32) * jnp.exp(lw - m_new)).astype(bf16)
            c_ref[idx] = jnp.exp(b_tot + m_prev - m_new) * cp + jnp.dot(kw, vp, preferred_element_type=f32)
            m_ref[idx] = jnp.broadcast_to(m_new, (8, LANE))


def _mlstm(mq, mkt, mv, mg, mgt, gate_b):
    n = mq.shape[0]
    t = MLSTM_T
    nb = n // t
    gb = jnp.pad(gate_b.reshape(1, 4 * M_HEADS), ((0, 0), (0, LANE - 4 * M_HEADS)))
    gbt = gate_b.reshape(4 * M_HEADS, 1)
    fwd = lambda s: jnp.where(s == 0, nb - 1, s - 1)
    bwd = lambda s: nb - 1 - s
    specs = []
    for im in (fwd, bwd):
        specs += [pl.BlockSpec((t, M_W), lambda s, im=im: (im(s), 0)),
                  pl.BlockSpec((M_W, t), lambda s, im=im: (0, im(s))),
                  pl.BlockSpec((t, M_W), lambda s, im=im: (im(s), 0)),
                  pl.BlockSpec((t, LANE), lambda s, im=im: (im(s), 0)),
                  pl.BlockSpec((4 * M_HEADS, t), lambda s, im=im: (0, im(s)))]
    return pl.pallas_call(
        _mlstm_kernel, grid=(nb,),
        in_specs=[_const_spec(gb.shape), _const_spec(gbt.shape)] + specs,
        out_specs=[pl.BlockSpec((t, M_W), lambda s: (fwd(s), 0)), pl.BlockSpec((t, M_W), lambda s: (bwd(s), 0))],
        out_shape=[jax.ShapeDtypeStruct((n, M_W), bf16), jax.ShapeDtypeStruct((n, M_W), bf16)],
        scratch_shapes=[pltpu.VMEM((2 * M_HEADS, M_DK, 2 * M_DV), f32), pltpu.VMEM((2 * M_HEADS, 8, LANE), f32)],
        compiler_params=_cparams(("arbitrary",)), name="mlstm",
    )(gb, gbt, mq, mkt, mv, mg, mgt, mq, mkt, mv, mg, mgt)


def _rms(x, axis=-1):
    return x * lax.rsqrt(jnp.mean(x * x, axis=axis, keepdims=True) + EPS)


def _merge_kernel(x_ref, hf_ref, hb_ref, mo_ref, ya_ref, yd_ref, gl_ref, mng_ref, wb_ref, wout_ref,
                  mod_ref, rw_ref, rb_ref, xnew_ref, h2_ref, logit_ref):
    hs = hf_ref[...].astype(f32) + hb_ref[...].astype(f32)
    parts = [_rms(hs[:, hh * M_DV:(hh + 1) * M_DV]) for hh in range(M_HEADS)]
    ym = jnp.concatenate(parts, axis=1) * mng_ref[...] * jax.nn.sigmoid(mo_ref[...].astype(f32))
    yb0 = jnp.dot(ym.astype(bf16), wb_ref[0], preferred_element_type=f32)
    tm = x_ref.shape[0]
    tn_dims = (((0,), (0,)), ((), ()))
    yb1 = lax.dot_general(ya_ref[...].reshape(BRANCH_W, tm), wb_ref[1], tn_dims, preferred_element_type=f32)
    yb2 = lax.dot_general(yd_ref[...].reshape(BRANCH_W, tm), wb_ref[2], tn_dims, preferred_element_type=f32)
    d = D_MODEL
    z = (jax.nn.sigmoid(gl_ref[:, 0:d].astype(f32)) * yb0
         + jax.nn.sigmoid(gl_ref[:, d:2 * d].astype(f32)) * yb1
         + jax.nn.sigmoid(gl_ref[:, 2 * d:3 * d].astype(f32)) * yb2)
    y = jnp.dot(z.astype(bf16), wout_ref[...], preferred_element_type=f32)
    xnew = x_ref[...] + mod_ref[1:2, :] * (_rms(y) * mod_ref[0:1, :])
    h2 = _rms(xnew) * mod_ref[2:3, :] + mod_ref[3:4, :]
    xnew_ref[...] = xnew
    h2_ref[...] = h2.astype(bf16)
    h2_hi = h2.astype(bf16)
    h2_lo = (h2 - h2_hi.astype(f32)).astype(bf16)
    rw = rw_ref[...]
    rw_hi = rw.astype(bf16)
    rw_lo = (rw - rw_hi.astype(f32)).astype(bf16)
    logit_ref[...] = (jnp.dot(h2_hi, rw_hi, preferred_element_type=f32)
                      + jnp.dot(h2_lo, rw_hi, preferred_element_type=f32)
                      + jnp.dot(h2_hi, rw_lo, preferred_element_type=f32) + rb_ref[...])


def _merge(n_rows, n_lat_blocks, x_all, hf, hb, mo, ya, yd, gl, mng, wb, wout, mod, rw, rb):
    tm = ROW_BLOCK
    nb = n_rows // tm
    row = lambda i: (i, 0)
    grp = lambda i: (jnp.where(i >= n_lat_blocks, 1, 0), 0, 0)
    in_specs = [pl.BlockSpec((tm, D_MODEL), row), pl.BlockSpec((tm, M_W), row), pl.BlockSpec((tm, M_W), row),
                pl.BlockSpec((tm, M_W), row),
                pl.BlockSpec((A_HEADS, A_V, tm), lambda i: (0, 0, i)),
                pl.BlockSpec((DF_HEADS, DF_DV, tm), lambda i: (0, 0, i)),
                pl.BlockSpec((tm, N_BRANCH * D_MODEL), row), _const_spec(mng.shape), _const_spec(wb.shape),
                _const_spec(wout.shape), pl.BlockSpec((None, 4, D_MODEL), grp),
                _const_spec(rw.shape), _const_spec(rb.shape)]
    return pl.pallas_call(
        _merge_kernel, grid=(nb,), in_specs=in_specs,
        out_specs=[pl.BlockSpec((tm, D_MODEL), row), pl.BlockSpec((tm, D_MODEL), row), pl.BlockSpec((tm, LANE), row)],
        out_shape=[jax.ShapeDtypeStruct((n_rows, D_MODEL), f32),
                   jax.ShapeDtypeStruct((max(n_rows, H2_ROWS), D_MODEL), bf16),
                   jax.ShapeDtypeStruct((n_rows, LANE), f32)],
        compiler_params=_cparams(("parallel",)), name="merge",
    )(x_all, hf, hb, mo, ya, yd, gl, mng, wb, wout, mod, rw, rb)


def _moe_kernel(be_ref, nu_ref, xs_ref, wgu_ref, bgu_ref, wd_ref, bd_ref, o_ref, wgu_s, wd_s):
    i = pl.program_id(0)
    prev = be_ref[jnp.maximum(i - 1, 0)]
    used = i < nu_ref[0]

    @pl.when(jnp.logical_and(used, jnp.logical_or(i == 0, be_ref[i] != prev)))
    def _():
        wgu_s[...] = wgu_ref[...].astype(bf16)
        wd_s[...] = wd_ref[...].astype(bf16)

    @pl.when(used)
    def _():
        gu = jnp.dot(xs_ref[...], wgu_s[...], preferred_element_type=f32) + bgu_ref[...]
        gate = jnp.minimum(gu[:, :D_EXPERT], SWIGLU_LIMIT)
        up = jnp.clip(gu[:, D_EXPERT:], -SWIGLU_LIMIT, SWIGLU_LIMIT)
        act = (up + 1.0) * (gate * jax.nn.sigmoid(SWIGLU_ALPHA * gate))
        y = jnp.dot(act.astype(bf16), wd_s[...], preferred_element_type=f32) + bd_ref[...]
        o_ref[...] = y.astype(o_ref.dtype)

    @pl.when(jnp.logical_not(used))
    def _():
        o_ref[...] = jnp.zeros_like(o_ref)


def _moe_experts(blk_e, n_used, xs, layer, w_gu, b_gu, w_down, b_down):
    cap = xs.shape[0]
    tm = MOE_TM
    nblk = cap // tm
    grid_spec = pltpu.PrefetchScalarGridSpec(
        num_scalar_prefetch=2, grid=(nblk,),
        in_specs=[pl.BlockSpec((tm, D_MODEL), lambda i, be, nu: (i, 0)),
                  pl.BlockSpec((None, None, D_MODEL, 2 * D_EXPERT), lambda i, be, nu: (layer, be[i], 0, 0)),
                  pl.BlockSpec((None, None, 1, 2 * D_EXPERT), lambda i, be, nu: (layer, be[i], 0, 0)),
                  pl.BlockSpec((None, None, D_EXPERT, D_MODEL), lambda i, be, nu: (layer, be[i], 0, 0)),
                  pl.BlockSpec((None, None, 1, D_MODEL), lambda i, be, nu: (layer, be[i], 0, 0))],
        out_specs=pl.BlockSpec((tm, D_MODEL), lambda i, be, nu: (i, 0)),
        scratch_shapes=[pltpu.VMEM((D_MODEL, 2 * D_EXPERT), bf16), pltpu.VMEM((D_EXPERT, D_MODEL), bf16)])
    return pl.pallas_call(
        _moe_kernel, grid_spec=grid_spec,
        out_shape=jax.ShapeDtypeStruct((cap, D_MODEL), bf16),
        compiler_params=_cparams(("arbitrary",)), name="moe_experts",
    )(blk_e, n_used, xs, w_gu, b_gu.reshape(DEPTH, N_EXPERTS, 1, -1), w_down, b_down.reshape(DEPTH, N_EXPERTS, 1, -1))


def _moe(h2, logits, layer, w_gu, b_gu, w_down, b_down):
    n = logits.shape[0]
    tm = MOE_TM
    i32 = jnp.int32
    top_v, top_i = lax.top_k(logits[:, :N_EXPERTS], TOP_K)
    wts = jax.nn.softmax(top_v, axis=-1)
    nk = n * TOP_K
    flat_e = top_i.reshape(nk).astype(i32)
    iota = jnp.arange(nk, dtype=i32)
    skey = lax.sort(flat_e * nk + iota)
    se = skey // nk
    order = skey - se * nk
    e_ids = jnp.arange(N_EXPERTS, dtype=i32)
    grp_start = jnp.sum((se[:, None] < e_ids[None, :]).astype(i32), axis=0)
    counts = jnp.sum((se[:, None] == e_ids[None, :]).astype(i32), axis=0)
    padded = (counts + tm - 1) // tm * tm
    pad_end = jnp.cumsum(padded)
    pad_start = pad_end - padded
    off = pad_start - grp_start
    hit = top_i[:, :, None] == e_ids[None, None, :]
    per_tok = jnp.sum(hit.astype(i32), axis=1)
    rank = jnp.cumsum(per_tok, axis=0) - per_tok
    pos = jnp.sum(jnp.where(hit, (pad_start[None, :] + rank)[:, None, :], 0), axis=2)
    pos_km = pos.T.reshape(nk)
    cap = -(-(nk + N_EXPERTS * (tm - 1)) // tm) * tm
    nblk = cap // tm
    bstart = jnp.arange(nblk, dtype=i32) * tm
    blk_e = jnp.minimum(jnp.sum((pad_end[None, :] <= bstart[:, None]).astype(i32), axis=1), N_EXPERTS - 1)
    n_used = (pad_end[-1] // tm).astype(i32).reshape(1)
    sel = blk_e[:, None] == e_ids[None, :]
    off_b = jnp.sum(jnp.where(sel, off[None, :], 0), axis=1)
    end_b = jnp.sum(jnp.where(sel, (pad_start + counts)[None, :], 0), axis=1)
    p2 = bstart[:, None] + jnp.arange(tm, dtype=i32)[None, :]
    src = jnp.clip(p2 - off_b[:, None], 0, nk - 1)
    slot_t = jnp.where(p2 < end_b[:, None], (order // TOP_K)[src], p2 % n).reshape(cap)
    xs = h2[slot_t]
    ys = _moe_experts(blk_e, n_used, xs, layer, w_gu, b_gu, w_down, b_down)
    return ys[pos_km].reshape(TOP_K, n, D_MODEL), wts


def _resid_kernel(x_ref, y_ref, w_ref, mod_ref, o_ref):
    w = w_ref[...]
    f = y_ref[0].astype(f32) * w[:, 0:1]
    for kk in range(1, TOP_K):
        f = f + y_ref[kk].astype(f32) * w[:, kk:kk + 1]
    o_ref[...] = x_ref[...] + mod_ref[1:2, :] * (_rms(f) * mod_ref[0:1, :])


def _resid(n_lat_blocks, xnew, picked, wts, mod):
    n = xnew.shape[0]
    tm = ROW_BLOCK
    row = lambda i: (i, 0)
    grp = lambda i: (jnp.where(i >= n_lat_blocks, 1, 0), 0, 0)
    return pl.pallas_call(
        _resid_kernel, grid=(n // tm,),
        in_specs=[pl.BlockSpec((tm, D_MODEL), row), pl.BlockSpec((TOP_K, tm, D_MODEL), lambda i: (0, i, 0)),
                  pl.BlockSpec((tm, TOP_K), row), pl.BlockSpec((None, 2, D_MODEL), grp)],
        out_specs=pl.BlockSpec((tm, D_MODEL), row),
        out_shape=jax.ShapeDtypeStruct((n, D_MODEL), f32),
        compiler_params=_cparams(("parallel",)), name="resid",
    )(xnew, picked, wts, mod)


def _rope_tables(n_lat, n_ctx, dim):
    rows = n_lat // GRID_W
    row = jnp.repeat(jnp.arange(rows, dtype=f32), GRID_W)
    col = jnp.tile(jnp.arange(GRID_W, dtype=f32), rows)
    quarter = dim // 4
    inv = ROPE_BASE ** (-jnp.arange(quarter, dtype=f32) / quarter)
    ang = jnp.concatenate([row[:, None] * inv, col[:, None] * inv], axis=-1)
    cos = jnp.concatenate([jnp.cos(ang), jnp.ones((n_ctx, dim // 2), f32)], axis=0).T
    sin = jnp.concatenate([jnp.sin(ang), jnp.zeros((n_ctx, dim // 2), f32)], axis=0).T
    return cos, sin


def kernel(x, c, ctx, c_ctx, ada_w, ada_b, norm_g, w_in, m_gate_b, m_norm_g, q_norm_g, w_uq, kv_norm_g, w_ukv,
           diff_lam, diff_norm_g, w_branch, w_out, router_w, router_b, w_gu, b_gu, w_down, b_down):
    n_lat = x.shape[1]
    n_ctx = ctx.shape[1]
    n = n_lat + n_ctx
    assert x.shape[0] == 1 and n_lat % ROW_BLOCK == 0 and n_ctx == MLSTM_T and n_lat % GRID_W == 0
    nlb = n_lat // ROW_BLOCK
    x_all = jnp.concatenate([x[0], ctx[0]], axis=0)

    cond = jnp.zeros((8, D_MODEL), f32).at[0].set(c[0]).at[1].set(c_ctx)
    mods = _adaln(cond, ada_w, ada_b)[:, :2].reshape(DEPTH, 2, 6, D_MODEL)

    ca, sa = _rope_tables(n_lat, n_ctx, A_ROPE)
    cd, sd = _rope_tables(n_lat, n_ctx, DF_DK)

    for l in range(DEPTH):
        need_ctx = l < DEPTH - 1
        lam_init = 0.8 - 0.6 * math.exp(-0.3 * l)
        sh1, sc1, ga1, sh2, sc2, ga2 = [mods[l, :, j] for j in range(6)]
        ab1 = jnp.stack([norm_g[l, 0][None] * (1.0 + sc1), sh1], axis=1)
        w_tok, w_tr, wt_uq, wt_ukv = _pack_inproj_weights(w_in[l], w_uq[l], w_ukv[l])
        (mq, mv, mo, mg, gl, mkt, mgt, qta, k_a, vta, qtd, k_d, vtd) = _inproj(
            x_all, ab1, nlb, w_tok, w_tr, wt_uq, wt_ukv,
            q_norm_g[l].reshape(Q_LORA, 1), kv_norm_g[l].reshape(KV_LORA, 1), ca, sa, cd, sd)

        hf, hb = _mlstm(mq, mkt, mv, mg, mgt, m_gate_b[l])

        qtd4 = qtd.reshape(DF_HEADS, 2 * DF_DK, n)
        dg = diff_norm_g[l].reshape(DF_HEADS, DF_DV, 1)
        n_rows = n if need_ctx else n_lat
        attn_a = functools.partial(_attention, n_maps=1, dk=A_DKP, dv=A_V, n_out=n_rows)
        attn_d = functools.partial(_attention, n_maps=2, dk=DF_DK, dv=DF_DV, n_out=n_rows,
                                   lam=diff_lam[l], g=dg, lam_init=lam_init)
        ya = attn_a(qta, k_a, vta, tq=ATTN_TQ_MLA, lq=n_lat)
        yd = attn_d(qtd4, k_d, vtd, tq=ATTN_TQ_DIFF, lq=n_lat)
        if need_ctx:
            ctx_tile = lambda v: v[:, -1:, :, v.shape[3] - n_ctx:]
            ya = attn_a(qta, k_a[:, n_lat:], ctx_tile(vta), tq=n_ctx, lq=n_ctx, q_start=n_lat, out=ya)
            yd = attn_d(qtd4, k_d[:, n_lat:], ctx_tile(vtd), tq=n_ctx, lq=n_ctx, q_start=n_lat, out=yd)

        mod_m = jnp.stack([jnp.broadcast_to(norm_g[l, 1][None], (2, D_MODEL)), ga1,
                           norm_g[l, 2][None] * (1.0 + sc2), sh2], axis=1)
        rw = jnp.pad(router_w[l], ((0, 0), (0, LANE - N_EXPERTS)))
        rb = jnp.pad(router_b[l].reshape(1, N_EXPERTS), ((0, 0), (0, LANE - N_EXPERTS)))
        xnew, h2, logits = _merge(n_rows, nlb, x_all, hf, hb, mo, ya, yd, gl,
                                  m_norm_g[l].reshape(1, -1), w_branch[l].astype(bf16), w_out[l].astype(bf16),
                                  mod_m, rw, rb)
        picked, wts = _moe(h2, logits, l, w_gu, b_gu, w_down, b_down)
        mod_r = jnp.stack([jnp.broadcast_to(norm_g[l, 3][None], (2, D_MODEL)), ga2], axis=1)
        x_all = _resid(nlb, xnew, picked, wts, mod_r)
    return x_all[:n_lat][None]
```

```python
import functools
import math

import jax
import jax.numpy as jnp
from jax import lax
from jax.experimental import pallas as pl
from jax.experimental.pallas import tpu as pltpu

f32 = jnp.float32
bf16 = jnp.bfloat16

D_MODEL = 1024
DEPTH = 2
GRID_W = 64
EPS = 1e-6
ROPE_BASE = 10000.0
M_HEADS = 4
M_DK = 128
M_DV = 128
GATE_SOFTCAP = 15.0
A_HEADS = 8
A_NOPE = 64
A_ROPE = 32
A_V = 64
Q_LORA = 256
KV_LORA = 128
DF_HEADS = 4
DF_DK = 64
DF_DV = 128
N_BRANCH = 3
BRANCH_W = 512
N_EXPERTS = 32
TOP_K = 4
D_EXPERT = 1024
SWIGLU_LIMIT = 7.0
SWIGLU_ALPHA = 1.702

IN_SIZES = (M_HEADS * M_DK, M_HEADS * M_DK, M_HEADS * M_DV, M_HEADS * M_DV, 4 * M_HEADS,
            Q_LORA, KV_LORA, A_ROPE,
            2 * DF_HEADS * DF_DK, 2 * DF_HEADS * DF_DK, DF_HEADS * DF_DV,
            N_BRANCH * D_MODEL)

LOG2E = 1.4426950408889634
LANE = 128
A_DKP = 128
A_DVP = A_V + 16
DF_DVP = DF_DV + 16
VMEM_LIMIT = 56 * 1024 * 1024

ROW_BLOCK = 256
MXU_DEPTH = 256
ATTN_TQ_MLA = 2048
ATTN_TQ_DIFF = 1024
ATTN_QSUB = 256
ATTN_TK = 1280
MLSTM_T = 256
MOE_TM = 512
H2_ROWS = 32768

NT_DIMS = (((1,), (1,)), ((), ()))


def _cparams(sem):
    return pltpu.CompilerParams(dimension_semantics=sem, vmem_limit_bytes=VMEM_LIMIT)


def _const_spec(shape):
    nd = len(shape)
    return pl.BlockSpec(shape, lambda *_: (0,) * nd)


def _adaln_kernel(cond_ref, w_ref, b_ref, o_ref):
    cnd = cond_ref[...]
    a = cnd * jax.nn.sigmoid(cnd)
    o_ref[...] = jnp.dot(a, w_ref[...], preferred_element_type=f32,
                         precision=lax.Precision.HIGHEST) + b_ref[...]


def _adaln(cond, ada_w, ada_b):
    tn = 1536
    nj = (6 * D_MODEL) // tn
    return pl.pallas_call(
        _adaln_kernel,
        grid=(DEPTH, nj),
        in_specs=[pl.BlockSpec((8, D_MODEL), lambda l, j: (0, 0)),
                  pl.BlockSpec((None, D_MODEL, tn), lambda l, j: (l, 0, j)),
                  pl.BlockSpec((None, 1, tn), lambda l, j: (l, 0, j))],
        out_specs=pl.BlockSpec((None, 8, tn), lambda l, j: (l, 0, j)),
        out_shape=jax.ShapeDtypeStruct((DEPTH, 8, 6 * D_MODEL), f32),
        compiler_params=_cparams(("parallel", "parallel")),
        name="adaln",
    )(cond, ada_w, ada_b.reshape(DEPTH, 1, 6 * D_MODEL))


M_W = M_HEADS * M_DK
DF_W = 2 * DF_HEADS * DF_DK
TOK_SIZES = (M_W, M_W, M_W, LANE, N_BRANCH * D_MODEL)
TR_SIZES = (M_W, 4 * M_HEADS, Q_LORA, KV_LORA, A_ROPE, DF_W, DF_W, DF_HEADS * DF_DV)


def _bounds(sizes):
    offs = [0]
    for sz in sizes:
        offs.append(offs[-1] + sz)
    return tuple(zip(offs[:-1], offs[1:]))


TOK_B = _bounds(TOK_SIZES)
TR_B = _bounds(TR_SIZES)
TOK_W = TOK_B[-1][1]
TR_W = TR_B[-1][1]


def _ones_row_block(tm):
    r = lax.broadcasted_iota(jnp.int32, (16, tm), 0)
    return jnp.where(r == 0, 1.0, 0.0).astype(bf16)


def _inproj_kernel(x_ref, ab_ref, wtok_ref, wtr_ref, wuq_ref, wukv_ref, gq_ref, gkv_ref,
                   ca_ref, sa_ref, cd_ref, sd_ref,
                   mq_ref, mv_ref, mo_ref, mg_ref, gl_ref, mkt_ref, mgt_ref,
                   qta_ref, kta_ref, vta_ref, qtd_ref, ktd_ref, vtd_ref):
    tm = x_ref.shape[0]
    x = x_ref[...]
    ms = jnp.mean(x * x, axis=-1, keepdims=True)
    h = (x * lax.rsqrt(ms + EPS)) * ab_ref[0:1, :] + ab_ref[1:2, :]
    hb = h.astype(bf16)

    tok = jnp.dot(hb, wtok_ref[...], preferred_element_type=f32)
    mq, mv, mo, mg, gl = [tok[:, a:b] for a, b in TOK_B]
    mq_ref[...] = (mq * (M_DK ** -0.5)).astype(bf16)
    mv_ref[...] = mv.astype(bf16)
    mo_ref[...] = mo.astype(bf16)
    mg_ref[...] = mg
    gl_ref[...] = gl.astype(bf16)

    tr = lax.dot_general(wtr_ref[...], hb, NT_DIMS, preferred_element_type=f32)
    mk, mgt, cq, ckv, kpe, dq, dk, dv = [tr[a:b] for a, b in TR_B]
    mkt_ref[...] = mk.astype(bf16)
    mgt_ref[...] = mgt

    ca = ca_ref[...]
    sa = sa_ref[...]
    cd = cd_ref[...]
    sd = sd_ref[...]
    ones_blk = _ones_row_block(tm)
    zero_pad = jnp.zeros((A_DKP - A_NOPE - A_ROPE, tm), bf16)

    cqn = (cq * lax.rsqrt(jnp.mean(cq * cq, axis=0, keepdims=True) + EPS) * gq_ref[...]).astype(bf16)
    qscale = ((A_NOPE + A_ROPE) ** -0.5) * LOG2E
    qt = jnp.dot(wuq_ref[...], cqn, preferred_element_type=f32) * qscale
    nope_w = A_HEADS * A_NOPE
    rh = A_ROPE // 2
    for hh in range(A_HEADS):
        a = qt[nope_w + rh * hh:nope_w + rh * (hh + 1)]
        b = qt[nope_w + rh * (A_HEADS + hh):nope_w + rh * (A_HEADS + hh + 1)]
        qta_ref[hh, 0:A_NOPE, :] = qt[A_NOPE * hh:A_NOPE * (hh + 1)].astype(bf16)
        qta_ref[hh, A_NOPE:A_NOPE + rh, :] = (a * ca - b * sa).astype(bf16)
        qta_ref[hh, A_NOPE + rh:A_NOPE + A_ROPE, :] = (a * sa + b * ca).astype(bf16)
        qta_ref[hh, A_NOPE + A_ROPE:A_DKP, :] = zero_pad

    ckvn = (ckv * lax.rsqrt(jnp.mean(ckv * ckv, axis=0, keepdims=True) + EPS) * gkv_ref[...]).astype(bf16)
    kvt = jnp.dot(wukv_ref[...], ckvn, preferred_element_type=f32)
    ka = kpe[0:rh]
    kb = kpe[rh:A_ROPE]
    kpe_rot = jnp.concatenate([ka * ca - kb * sa, ka * sa + kb * ca,
                               jnp.zeros((A_DKP - A_NOPE - A_ROPE, tm), f32)], axis=0)
    for hh in range(A_HEADS):
        kt_h = jnp.concatenate([kvt[A_NOPE * hh:A_NOPE * (hh + 1)], kpe_rot], axis=0)
        kta_ref[hh] = kt_h.T.astype(bf16)
        vta_ref[hh, 0:A_V, :] = kvt[nope_w + A_V * hh:nope_w + A_V * (hh + 1)].astype(bf16)
        vta_ref[hh, A_V:A_DVP, :] = ones_blk

    dscale = (DF_DK ** -0.5) * LOG2E
    dh = DF_DK // 2
    for hh in range(2 * DF_HEADS):
        a = dq[dh * hh:dh * (hh + 1)]
        b = dq[DF_W // 2 + dh * hh:DF_W // 2 + dh * (hh + 1)]
        qtd_ref[hh, 0:dh, :] = ((a * cd - b * sd) * dscale).astype(bf16)
        qtd_ref[hh, dh:DF_DK, :] = ((a * sd + b * cd) * dscale).astype(bf16)
        a = dk[dh * hh:dh * (hh + 1)]
        b = dk[DF_W // 2 + dh * hh:DF_W // 2 + dh * (hh + 1)]
        ktd_ref[hh] = jnp.concatenate([a * cd - b * sd, a * sd + b * cd], axis=0).T.astype(bf16)
    for hh in range(DF_HEADS):
        vtd_ref[hh, 0:DF_DV, :] = dv[DF_DV * hh:DF_DV * (hh + 1)].astype(bf16)
        vtd_ref[hh, DF_DV:DF_DVP, :] = ones_blk


def _pack_inproj_weights(w, w_uq, w_ukv):
    o = [0]
    for s in IN_SIZES:
        o.append(o[-1] + s)
    mq, mk, mv, mo, mg, cq, ckv, kpe, dq, dk, dv, gl = [w[:, o[i]:o[i + 1]] for i in range(12)]
    mg_pad = jnp.pad(mg, ((0, 0), (0, LANE - mg.shape[1])))
    w_tok = jnp.concatenate([mq, mv, mo, mg_pad, gl], axis=1).astype(bf16)

    def split_halves(t, n_heads, d):
        t3 = t.reshape(t.shape[0], n_heads, d)
        return jnp.concatenate([t3[:, :, :d // 2].reshape(t.shape[0], -1),
                                t3[:, :, d // 2:].reshape(t.shape[0], -1)], axis=1)

    w_tr = jnp.concatenate([mk, mg, cq, ckv, kpe, split_halves(dq, 2 * DF_HEADS, DF_DK),
                            split_halves(dk, 2 * DF_HEADS, DF_DK), dv], axis=1).T.astype(bf16)
    uq = w_uq.reshape(Q_LORA, A_HEADS, A_NOPE + A_ROPE)
    half = A_ROPE // 2
    wt_uq = jnp.concatenate([uq[:, :, :A_NOPE].reshape(Q_LORA, -1),
                             uq[:, :, A_NOPE:A_NOPE + half].reshape(Q_LORA, -1),
                             uq[:, :, A_NOPE + half:].reshape(Q_LORA, -1)], axis=1).T.astype(bf16)
    ukv = w_ukv.reshape(KV_LORA, A_HEADS, A_NOPE + A_V)
    wt_ukv = jnp.concatenate([ukv[:, :, :A_NOPE].reshape(KV_LORA, -1),
                              ukv[:, :, A_NOPE:].reshape(KV_LORA, -1)], axis=1).T.astype(bf16)
    return w_tok, w_tr, wt_uq, wt_ukv


def _inproj(x_all, ab, n_lat_blocks, w_tok, w_tr, wt_uq, wt_ukv, gq, gkv, ca, sa, cd, sd):
    n = x_all.shape[0]
    tm = ROW_BLOCK
    tk = _key_tile(n)
    per = tk // tm
    vtile = lambda i: (0, i // per, 0, i % per)
    nb = n // tm
    row = lambda i: (i, 0)
    colb = lambda i: (0, i)
    col3 = lambda i: (0, 0, i)
    grp = lambda i: (jnp.where(i >= n_lat_blocks, 1, 0), 0, 0)
    out_shapes = [
        jax.ShapeDtypeStruct((n, M_W), bf16), jax.ShapeDtypeStruct((n, M_W), bf16),
        jax.ShapeDtypeStruct((n, M_W), bf16), jax.ShapeDtypeStruct((n, LANE), f32),
        jax.ShapeDtypeStruct((n, N_BRANCH * D_MODEL), bf16),
        jax.ShapeDtypeStruct((M_W, n), bf16), jax.ShapeDtypeStruct((4 * M_HEADS, n), f32),
        jax.ShapeDtypeStruct((A_HEADS, A_DKP, n), bf16), jax.ShapeDtypeStruct((A_HEADS, n, A_DKP), bf16),
        jax.ShapeDtypeStruct((A_HEADS, n // tk, A_DVP, tk), bf16),
        jax.ShapeDtypeStruct((2 * DF_HEADS, DF_DK, n), bf16), jax.ShapeDtypeStruct((2 * DF_HEADS, n, DF_DK), bf16),
        jax.ShapeDtypeStruct((DF_HEADS, n // tk, DF_DVP, tk), bf16),
    ]
    out_specs = [
        pl.BlockSpec((tm, M_W), row), pl.BlockSpec((tm, M_W), row), pl.BlockSpec((tm, M_W), row),
        pl.BlockSpec((tm, LANE), row), pl.BlockSpec((tm, N_BRANCH * D_MODEL), row),
        pl.BlockSpec((M_W, tm), colb), pl.BlockSpec((4 * M_HEADS, tm), colb),
        pl.BlockSpec((A_HEADS, A_DKP, tm), col3), pl.BlockSpec((A_HEADS, tm, A_DKP), lambda i: (0, i, 0)),
        pl.BlockSpec((A_HEADS, None, A_DVP, tm), vtile),
        pl.BlockSpec((2 * DF_HEADS, DF_DK, tm), col3), pl.BlockSpec((2 * DF_HEADS, tm, DF_DK), lambda i: (0, i, 0)),
        pl.BlockSpec((DF_HEADS, None, DF_DVP, tm), vtile),
    ]
    in_specs = [
        pl.BlockSpec((tm, D_MODEL), row),
        pl.BlockSpec((None, 2, D_MODEL), grp),
        _const_spec(w_tok.shape), _const_spec(w_tr.shape), _const_spec(wt_uq.shape), _const_spec(wt_ukv.shape),
        _const_spec(gq.shape), _const_spec(gkv.shape),
        pl.BlockSpec((A_ROPE // 2, tm), colb), pl.BlockSpec((A_ROPE // 2, tm), colb),
        pl.BlockSpec((DF_DK // 2, tm), colb), pl.BlockSpec((DF_DK // 2, tm), colb),
    ]
    return pl.pallas_call(
        _inproj_kernel, grid=(nb,), in_specs=in_specs, out_specs=out_specs, out_shape=out_shapes,
        compiler_params=_cparams(("parallel",)), name="inproj",
    )(x_all, ab, w_tok, w_tr, wt_uq, wt_ukv, gq, gkv, ca, sa, cd, sd)


def _attn_kernel(*refs, n_maps, dk, dv, tk, n_kt, qsub, lam_init):
    if n_maps == 2:
        lam_ref, g_ref, qt_ref, k_ref, vt_ref, o_ref = refs
    else:
        qt_ref, k_ref, vt_ref, o_ref = refs
    tq = qt_ref.shape[-1]
    dvp = vt_ref.shape[-2]
    chains = [(m, c) for m in range(n_maps) for c in range(tq // qsub)]

    def qk(j, m, c):
        qt = qt_ref[m * dk:(m + 1) * dk, c * qsub:(c + 1) * qsub]
        return jnp.dot(k_ref[m, j * tk:(j + 1) * tk, :], qt, preferred_element_type=f32)

    state = [(jnp.full((1, qsub), -jnp.inf, f32), jnp.zeros((dvp, qsub), f32)) for _ in chains]
    s_cur = [qk(0, m, c) for (m, c) in chains]
    for j in range(n_kt):
        vtj = vt_ref[j]
        for i, (m, c) in enumerate(chains):
            s_nxt = qk(j + 1, m, c) if j + 1 < n_kt else None
            m_run, acc = state[i]
            m_new = jnp.maximum(m_run, jnp.max(s_cur[i], axis=0, keepdims=True))
            alpha = jnp.exp2(m_run - m_new)
            p = jnp.exp2(s_cur[i] - m_new).astype(bf16)
            pv = jnp.dot(vtj, p, preferred_element_type=f32)
            state[i] = (m_new, alpha * acc + pv)
            s_cur[i] = s_nxt
    outs = []
    for m in range(n_maps):
        accs = [state[i][1] for i, (mm, _) in enumerate(chains) if mm == m]
        acc = accs[0] if len(accs) == 1 else jnp.concatenate(accs, axis=1)
        outs.append(acc[0:dv] / acc[dv:dv + 1])
    if n_maps == 1:
        o_ref[...] = outs[0].astype(o_ref.dtype)
    else:
        lf = lam_ref[...]
        lam = (jnp.exp(jnp.sum(lf[0:1] * lf[1:2], axis=1, keepdims=True))
               - jnp.exp(jnp.sum(lf[2:3] * lf[3:4], axis=1, keepdims=True)) + lam_init)
        o = outs[0] - lam * outs[1]
        o = o * lax.rsqrt(jnp.mean(o * o, axis=0, keepdims=True) + EPS) * g_ref[...]
        o_ref[...] = (o * (1.0 - lam_init)).astype(o_ref.dtype)


def _key_tile(lk):
    return max(t for t in range(MXU_DEPTH, min(ATTN_TK, lk) + 1, MXU_DEPTH) if lk % t == 0)


def _attention(qt, k, vt4, *, n_maps, dk, dv, tq, lq, q_start=0, lam=None, g=None, lam_init=0.0):
    n_heads = qt.shape[0]
    lk = k.shape[1]
    _, n_kt, dvp, tk = vt4.shape
    assert n_kt * tk == lk
    tq = min(tq, lq)
    qsub = min(ATTN_QSUB, tq)
    qb0 = q_start // tq
    in_specs = [
        pl.BlockSpec((None, n_maps * dk, tq), lambda h, i: (h, 0, qb0 + i)),
        pl.BlockSpec((n_maps, lk, dk), lambda h, i: (h, 0, 0)),
        pl.BlockSpec((None, n_kt, dvp, tk), lambda h, i: (h, 0, 0, 0)),
    ]
    args = [qt, k, vt4]
    if n_maps == 2:
        in_specs = [_const_spec(lam.shape), pl.BlockSpec((None, dv, 1), lambda h, i: (h, 0, 0))] + in_specs
        args = [lam, g] + args
    kern = functools.partial(_attn_kernel, n_maps=n_maps, dk=dk, dv=dv, tk=tk, n_kt=n_kt, qsub=qsub,
                             lam_init=lam_init)
    return pl.pallas_call(
        kern, grid=(n_heads, lq // tq), in_specs=in_specs,
        out_specs=pl.BlockSpec((None, dv, tq), lambda h, i: (h, 0, i)),
        out_shape=jax.ShapeDtypeStruct((n_heads, dv, lq), bf16),
        compiler_params=_cparams(("parallel", "parallel")),
        name="attn_diff" if n_maps == 2 else "attn_mla",
    )(*args)


def _split3(x):
    hi = x.astype(bf16)
    r = x - hi.astype(f32)
    mid = r.astype(bf16)
    lo = (r - mid.astype(f32)).astype(bf16)
    return hi, mid, lo


def _log_sigmoid(x):
    return jnp.minimum(x, 0.0) - jnp.log(1.0 + jnp.exp(-jnp.abs(x)))


def _softcap(x):
    return GATE_SOFTCAP * jnp.tanh(x * (1.0 / GATE_SOFTCAP))


def _mlstm_kernel(gb_ref, gbt_ref,
                  qf_ref, ktf_ref, vf_ref, gf_ref, gtf_ref,
                  qb_ref, ktb_ref, vb_ref, gb2_ref, gtb_ref,
                  hf_ref, hb_ref, c_ref, m_ref):
    t = qf_ref.shape[0]

    @pl.when(pl.program_id(0) == 0)
    def _():
        c_ref[...] = jnp.zeros_like(c_ref)
        m_ref[...] = jnp.zeros_like(m_ref)

    row = lax.broadcasted_iota(jnp.int32, (t, t), 0)
    col = lax.broadcasted_iota(jnp.int32, (t, t), 1)
    lane = lax.broadcasted_iota(jnp.int32, (t, LANE), 1)
    ones_col = jnp.where(lane == 0, 1.0, 0.0).astype(bf16)
    dirs = ((qf_ref, ktf_ref, vf_ref, gf_ref, gtf_ref, hf_ref),
            (qb_ref, ktb_ref, vb_ref, gb2_ref, gtb_ref, hb_ref))
    for d, (q_ref, kt_ref, v_ref, g_ref, gt_ref, o_ref) in enumerate(dirs):
        mask = (col <= row) if d == 0 else (col >= row)
        maskb = mask.astype(bf16)
        last = t - 1 if d == 0 else 0
        g_tok = _softcap(g_ref[...] + gb_ref[...])
        g_t = _softcap(gt_ref[...] + gbt_ref[...])
        b_tok = sum(jnp.dot(maskb, part, preferred_element_type=f32) for part in _split3(_log_sigmoid(g_tok)))
        b_t = sum(lax.dot_general(part, maskb, NT_DIMS, preferred_element_type=f32)
                  for part in _split3(_log_sigmoid(g_t)))
        for hh in range(M_HEADS):
            ri = 4 * (2 * d) + hh
            rf = 4 * (2 * d + 1) + hh
            idx = d * M_HEADS + hh
            b_col = b_tok[:, rf:rf + 1]
            b_row = b_t[rf:rf + 1, :]
            a_row = g_t[ri:ri + 1, :] - b_row
            m_prev = m_ref[idx, 0:1, 0:1]
            ld = jnp.where(mask, b_col + a_row, -jnp.inf)
            inter = b_col + m_prev
            mj = jnp.maximum(inter, jnp.max(ld, axis=1, keepdims=True))
            dmat = jnp.exp(ld - mj)
            q = q_ref[:, hh * M_DK:(hh + 1) * M_DK]
            kt = kt_ref[hh * M_DK:(hh + 1) * M_DK, :]
            vp = jnp.concatenate([v_ref[:, hh * M_DV:(hh + 1) * M_DV], ones_col], axis=1)
            sc = jnp.dot(q, kt, preferred_element_type=f32) * dmat
            cp = c_ref[idx]
            num = (jnp.dot(sc.astype(bf16), vp, preferred_element_type=f32)
                   + jnp.exp(inter - mj) * jnp.dot(q, cp.astype(bf16), preferred_element_type=f32))
            den = num[:, M_DV:M_DV + 1]
            hj = num[:, 0:M_DV] / jnp.maximum(jnp.abs(den), jnp.exp(-mj))
            o_ref[:, hh * M_DV:(hh + 1) * M_DV] = hj.astype(o_ref.dtype)
            b_tot = b_t[rf:rf + 1, last:last + 1]
            lw = b_tot + a_row
            m_new = jnp.maximum(b_tot + m_prev, jnp.max(lw, axis=1, keepdims=True))
            kw = (kt.astype(f32) * jnp.exp(lw - m_new)).astype(bf16)
            c_ref[idx] = jnp.exp(b_tot + m_prev - m_new) * cp + jnp.dot(kw, vp, preferred_element_type=f32)
            m_ref[idx] = jnp.broadcast_to(m_new, (8, LANE))


def _mlstm(mq, mkt, mv, mg, mgt, gate_b):
    n = mq.shape[0]
    t = MLSTM_T
    nb = n // t
    gb = jnp.pad(gate_b.reshape(1, 4 * M_HEADS), ((0, 0), (0, LANE - 4 * M_HEADS)))
    gbt = gate_b.reshape(4 * M_HEADS, 1)
    fwd = lambda s: jnp.where(s == 0, nb - 1, s - 1)
    bwd = lambda s: nb - 1 - s
    specs = []
    for im in (fwd, bwd):
        specs += [pl.BlockSpec((t, M_W), lambda s, im=im: (im(s), 0)),
                  pl.BlockSpec((M_W, t), lambda s, im=im: (0, im(s))),
                  pl.BlockSpec((t, M_W), lambda s, im=im: (im(s), 0)),
                  pl.BlockSpec((t, LANE), lambda s, im=im: (im(s), 0)),
                  pl.BlockSpec((4 * M_HEADS, t), lambda s, im=im: (0, im(s)))]
    return pl.pallas_call(
        _mlstm_kernel, grid=(nb,),
        in_specs=[_const_spec(gb.shape), _const_spec(gbt.shape)] + specs,
        out_specs=[pl.BlockSpec((t, M_W), lambda s: (fwd(s), 0)), pl.BlockSpec((t, M_W), lambda s: (bwd(s), 0))],
        out_shape=[jax.ShapeDtypeStruct((n, M_W), bf16), jax.ShapeDtypeStruct((n, M_W), bf16)],
        scratch_shapes=[pltpu.VMEM((2 * M_HEADS, M_DK, 2 * M_DV), f32), pltpu.VMEM((2 * M_HEADS, 8, LANE), f32)],
        compiler_params=_cparams(("arbitrary",)), name="mlstm",
    )(gb, gbt, mq, mkt, mv, mg, mgt, mq, mkt, mv, mg, mgt)


def _rms(x, axis=-1):
    return x * lax.rsqrt(jnp.mean(x * x, axis=axis, keepdims=True) + EPS)


def _merge_kernel(x_ref, hf_ref, hb_ref, mo_ref, ya_ref, yd_ref, yac_ref, ydc_ref, gl_ref, mng_ref, wb_ref,
                  wout_ref, mod_ref, rw_ref, rb_ref, xnew_ref, h2_ref, logit_ref, *, n_blocks, n_lat_blocks):
    i = pl.program_id(0)

    @pl.when(i < n_blocks)
    def _():
        _merge_block(x_ref, hf_ref, hb_ref, mo_ref, ya_ref, yd_ref, yac_ref, ydc_ref, gl_ref, mng_ref, wb_ref,
                     wout_ref, mod_ref, rw_ref, rb_ref, xnew_ref, h2_ref, logit_ref, i >= n_lat_blocks)

    @pl.when(i >= n_blocks)
    def _():
        h2_ref[...] = jnp.zeros_like(h2_ref)


def _merge_block(x_ref, hf_ref, hb_ref, mo_ref, ya_ref, yd_ref, yac_ref, ydc_ref, gl_ref, mng_ref, wb_ref,
                 wout_ref, mod_ref, rw_ref, rb_ref, xnew_ref, h2_ref, logit_ref, is_ctx):
    hs = hf_ref[...].astype(f32) + hb_ref[...].astype(f32)
    parts = [_rms(hs[:, hh * M_DV:(hh + 1) * M_DV]) for hh in range(M_HEADS)]
    ym = jnp.concatenate(parts, axis=1) * mng_ref[...] * jax.nn.sigmoid(mo_ref[...].astype(f32))
    yb0 = jnp.dot(ym.astype(bf16), wb_ref[0], preferred_element_type=f32)
    tm = x_ref.shape[0]
    tn_dims = (((0,), (0,)), ((), ()))
    ya = jnp.where(is_ctx, yac_ref[...], ya_ref[...])
    yd = jnp.where(is_ctx, ydc_ref[...], yd_ref[...])
    yb1 = lax.dot_general(ya.reshape(BRANCH_W, tm), wb_ref[1], tn_dims, preferred_element_type=f32)
    yb2 = lax.dot_general(yd.reshape(BRANCH_W, tm), wb_ref[2], tn_dims, preferred_element_type=f32)
    d = D_MODEL
    z = (jax.nn.sigmoid(gl_ref[:, 0:d].astype(f32)) * yb0
         + jax.nn.sigmoid(gl_ref[:, d:2 * d].astype(f32)) * yb1
         + jax.nn.sigmoid(gl_ref[:, 2 * d:3 * d].astype(f32)) * yb2)
    y = jnp.dot(z.astype(bf16), wout_ref[...], preferred_element_type=f32)
    xnew = x_ref[...] + mod_ref[1:2, :] * (_rms(y) * mod_ref[0:1, :])
    h2 = _rms(xnew) * mod_ref[2:3, :] + mod_ref[3:4, :]
    xnew_ref[...] = xnew
    h2_ref[...] = h2.astype(bf16)
    h2_hi = h2.astype(bf16)
    h2_lo = (h2 - h2_hi.astype(f32)).astype(bf16)
    rw = rw_ref[...]
    rw_hi = rw.astype(bf16)
    rw_lo = (rw - rw_hi.astype(f32)).astype(bf16)
    logit_ref[...] = (jnp.dot(h2_hi, rw_hi, preferred_element_type=f32)
                      + jnp.dot(h2_lo, rw_hi, preferred_element_type=f32)
                      + jnp.dot(h2_hi, rw_lo, preferred_element_type=f32) + rb_ref[...])


def _merge(n_rows, n_lat_blocks, x_all, hf, hb, mo, ya, yd, ya_ctx, yd_ctx, gl, mng, wb, wout, mod, rw, rb):
    tm = ROW_BLOCK
    nb = n_rows // tm
    n_steps = max(n_rows, H2_ROWS) // tm
    last = nb - 1
    row = lambda i: (jnp.minimum(i, last), 0)
    lat3 = lambda i: (0, 0, jnp.minimum(i, n_lat_blocks - 1))
    grp = lambda i: (jnp.where(i >= n_lat_blocks, 1, 0), 0, 0)
    in_specs = [pl.BlockSpec((tm, D_MODEL), row), pl.BlockSpec((tm, M_W), row), pl.BlockSpec((tm, M_W), row),
                pl.BlockSpec((tm, M_W), row),
                pl.BlockSpec((A_HEADS, A_V, tm), lat3), pl.BlockSpec((DF_HEADS, DF_DV, tm), lat3),
                pl.BlockSpec((A_HEADS, A_V, tm), lambda i: (0, 0, 0)),
                pl.BlockSpec((DF_HEADS, DF_DV, tm), lambda i: (0, 0, 0)),
                pl.BlockSpec((tm, N_BRANCH * D_MODEL), row), _const_spec(mng.shape), _const_spec(wb.shape),
                _const_spec(wout.shape), pl.BlockSpec((None, 4, D_MODEL), grp),
                _const_spec(rw.shape), _const_spec(rb.shape)]
    kern = functools.partial(_merge_kernel, n_blocks=nb, n_lat_blocks=n_lat_blocks)
    return pl.pallas_call(
        kern, grid=(n_steps,), in_specs=in_specs,
        out_specs=[pl.BlockSpec((tm, D_MODEL), row), pl.BlockSpec((tm, D_MODEL), lambda i: (i, 0)),
                   pl.BlockSpec((tm, LANE), row)],
        out_shape=[jax.ShapeDtypeStruct((n_rows, D_MODEL), f32),
                   jax.ShapeDtypeStruct((max(n_rows, H2_ROWS), D_MODEL), bf16),
                   jax.ShapeDtypeStruct((n_rows, LANE), f32)],
        compiler_params=_cparams(("arbitrary",)), name="merge",
    )(x_all, hf, hb, mo, ya, yd, ya_ctx, yd_ctx, gl, mng, wb, wout, mod, rw, rb)


def _moe_kernel(be_ref, nu_ref, xs_ref, wgu_ref, bgu_ref, wd_ref, bd_ref, o_ref, wgu_s, wd_s):
    i = pl.program_id(0)
    prev = be_ref[jnp.maximum(i - 1, 0)]
    used = i < nu_ref[0]

    @pl.when(jnp.logical_and(used, jnp.logical_or(i == 0, be_ref[i] != prev)))
    def _():
        wgu_s[...] = wgu_ref[...].astype(bf16)
        wd_s[...] = wd_ref[...].astype(bf16)

    @pl.when(used)
    def _():
        gu = jnp.dot(xs_ref[...], wgu_s[...], preferred_element_type=f32) + bgu_ref[...]
        gate = jnp.minimum(gu[:, :D_EXPERT], SWIGLU_LIMIT)
        up = jnp.clip(gu[:, D_EXPERT:], -SWIGLU_LIMIT, SWIGLU_LIMIT)
        act = (up + 1.0) * (gate * jax.nn.sigmoid(SWIGLU_ALPHA * gate))
        y = jnp.dot(act.astype(bf16), wd_s[...], preferred_element_type=f32) + bd_ref[...]
        o_ref[...] = y.astype(o_ref.dtype)

    @pl.when(jnp.logical_not(used))
    def _():
        o_ref[...] = jnp.zeros_like(o_ref)


def _moe_experts(blk_e, n_used, xs, layer, w_gu, b_gu, w_down, b_down):
    cap = xs.shape[0]
    tm = MOE_TM
    nblk = cap // tm
    grid_spec = pltpu.PrefetchScalarGridSpec(
        num_scalar_prefetch=2, grid=(nblk,),
        in_specs=[pl.BlockSpec((tm, D_MODEL), lambda i, be, nu: (i, 0)),
                  pl.BlockSpec((None, None, D_MODEL, 2 * D_EXPERT), lambda i, be, nu: (layer, be[i], 0, 0)),
                  pl.BlockSpec((None, None, 1, 2 * D_EXPERT), lambda i, be, nu: (layer, be[i], 0, 0)),
                  pl.BlockSpec((None, None, D_EXPERT, D_MODEL), lambda i, be, nu: (layer, be[i], 0, 0)),
                  pl.BlockSpec((None, None, 1, D_MODEL), lambda i, be, nu: (layer, be[i], 0, 0))],
        out_specs=pl.BlockSpec((tm, D_MODEL), lambda i, be, nu: (i, 0)),
        scratch_shapes=[pltpu.VMEM((D_MODEL, 2 * D_EXPERT), bf16), pltpu.VMEM((D_EXPERT, D_MODEL), bf16)])
    return pl.pallas_call(
        _moe_kernel, grid_spec=grid_spec,
        out_shape=jax.ShapeDtypeStruct((cap, D_MODEL), bf16),
        compiler_params=_cparams(("arbitrary",)), name="moe_experts",
    )(blk_e, n_used, xs, w_gu, b_gu.reshape(DEPTH, N_EXPERTS, 1, -1), w_down, b_down.reshape(DEPTH, N_EXPERTS, 1, -1))


def _moe(h2, logits, layer, w_gu, b_gu, w_down, b_down):
    n = logits.shape[0]
    tm = MOE_TM
    i32 = jnp.int32
    top_v, top_i = lax.top_k(logits[:, :N_EXPERTS], TOP_K)
    wts = jax.nn.softmax(top_v, axis=-1)
    nk = n * TOP_K
    flat_e = top_i.reshape(nk).astype(i32)
    iota = jnp.arange(nk, dtype=i32)
    skey = lax.sort(flat_e * nk + iota)
    se = skey // nk
    order = skey - se * nk
    e_ids = jnp.arange(N_EXPERTS, dtype=i32)
    grp_start = jnp.sum((se[:, None] < e_ids[None, :]).astype(i32), axis=0)
    counts = jnp.sum((se[:, None] == e_ids[None, :]).astype(i32), axis=0)
    padded = (counts + tm - 1) // tm * tm
    pad_end = jnp.cumsum(padded)
    pad_start = pad_end - padded
    off = pad_start - grp_start
    hit = top_i[:, :, None] == e_ids[None, None, :]
    per_tok = jnp.sum(hit.astype(i32), axis=1)
    rank = jnp.cumsum(per_tok, axis=0) - per_tok
    pos = jnp.sum(jnp.where(hit, (pad_start[None, :] + rank)[:, None, :], 0), axis=2)
    pos_km = pos.T.reshape(nk)
    cap = -(-(nk + N_EXPERTS * (tm - 1)) // tm) * tm
    nblk = cap // tm
    bstart = jnp.arange(nblk, dtype=i32) * tm
    blk_e = jnp.minimum(jnp.sum((pad_end[None, :] <= bstart[:, None]).astype(i32), axis=1), N_EXPERTS - 1)
    n_used = (pad_end[-1] // tm).astype(i32).reshape(1)
    sel = blk_e[:, None] == e_ids[None, :]
    off_b = jnp.sum(jnp.where(sel, off[None, :], 0), axis=1)
    end_b = jnp.sum(jnp.where(sel, (pad_start + counts)[None, :], 0), axis=1)
    p2 = bstart[:, None] + jnp.arange(tm, dtype=i32)[None, :]
    src = jnp.clip(p2 - off_b[:, None], 0, nk - 1)
    slot_t = jnp.where(p2 < end_b[:, None], (order // TOP_K)[src], p2 % n).reshape(cap)
    xs = h2[slot_t]
    ys = _moe_experts(blk_e, n_used, xs, layer, w_gu, b_gu, w_down, b_down)
    return ys[pos_km].reshape(TOP_K, n, D_MODEL), wts


def _resid_kernel(x_ref, y_ref, w_ref, mod_ref, o_ref):
    w = w_ref[...]
    f = y_ref[0].astype(f32) * w[:, 0:1]
    for kk in range(1, TOP_K):
        f = f + y_ref[kk].astype(f32) * w[:, kk:kk + 1]
    o_ref[...] = x_ref[...] + mod_ref[1:2, :] * (_rms(f) * mod_ref[0:1, :])


def _resid(n_lat_blocks, xnew, picked, wts, mod):
    n = xnew.shape[0]
    tm = ROW_BLOCK
    row = lambda i: (i, 0)
    grp = lambda i: (jnp.where(i >= n_lat_blocks, 1, 0), 0, 0)
    return pl.pallas_call(
        _resid_kernel, grid=(n // tm,),
        in_specs=[pl.BlockSpec((tm, D_MODEL), row), pl.BlockSpec((TOP_K, tm, D_MODEL), lambda i: (0, i, 0)),
                  pl.BlockSpec((tm, TOP_K), row), pl.BlockSpec((None, 2, D_MODEL), grp)],
        out_specs=pl.BlockSpec((tm, D_MODEL), row),
        out_shape=jax.ShapeDtypeStruct((n, D_MODEL), f32),
        compiler_params=_cparams(("parallel",)), name="resid",
    )(xnew, picked, wts, mod)


def _rope_tables(n_lat, n_ctx, dim):
    rows = n_lat // GRID_W
    row = jnp.repeat(jnp.arange(rows, dtype=f32), GRID_W)
    col = jnp.tile(jnp.arange(GRID_W, dtype=f32), rows)
    quarter = dim // 4
    inv = ROPE_BASE ** (-jnp.arange(quarter, dtype=f32) / quarter)
    ang = jnp.concatenate([row[:, None] * inv, col[:, None] * inv], axis=-1)
    cos = jnp.concatenate([jnp.cos(ang), jnp.ones((n_ctx, dim // 2), f32)], axis=0).T
    sin = jnp.concatenate([jnp.sin(ang), jnp.zeros((n_ctx, dim // 2), f32)], axis=0).T
    return cos, sin


def kernel(x, c, ctx, c_ctx, ada_w, ada_b, norm_g, w_in, m_gate_b, m_norm_g, q_norm_g, w_uq, kv_norm_g, w_ukv,
           diff_lam, diff_norm_g, w_branch, w_out, router_w, router_b, w_gu, b_gu, w_down, b_down):
    n_lat = x.shape[1]
    n_ctx = ctx.shape[1]
    n = n_lat + n_ctx
    assert x.shape[0] == 1 and n_lat % ROW_BLOCK == 0 and n_ctx == MLSTM_T == ROW_BLOCK and n_lat % GRID_W == 0
    nlb = n_lat // ROW_BLOCK
    x_all = jnp.concatenate([x[0], ctx[0]], axis=0)

    cond = jnp.zeros((8, D_MODEL), f32).at[0].set(c[0]).at[1].set(c_ctx)
    mods = _adaln(cond, ada_w, ada_b)[:, :2].reshape(DEPTH, 2, 6, D_MODEL)

    ca, sa = _rope_tables(n_lat, n_ctx, A_ROPE)
    cd, sd = _rope_tables(n_lat, n_ctx, DF_DK)

    for l in range(DEPTH):
        need_ctx = l < DEPTH - 1
        lam_init = 0.8 - 0.6 * math.exp(-0.3 * l)
        sh1, sc1, ga1, sh2, sc2, ga2 = [mods[l, :, j] for j in range(6)]
        ab1 = jnp.stack([norm_g[l, 0][None] * (1.0 + sc1), sh1], axis=1)
        w_tok, w_tr, wt_uq, wt_ukv = _pack_inproj_weights(w_in[l], w_uq[l], w_ukv[l])
        (mq, mv, mo, mg, gl, mkt, mgt, qta, k_a, vta, qtd, k_d, vtd) = _inproj(
            x_all, ab1, nlb, w_tok, w_tr, wt_uq, wt_ukv,
            q_norm_g[l].reshape(Q_LORA, 1), kv_norm_g[l].reshape(KV_LORA, 1), ca, sa, cd, sd)

        hf, hb = _mlstm(mq, mkt, mv, mg, mgt, m_gate_b[l])

        qtd4 = qtd.reshape(DF_HEADS, 2 * DF_DK, n)
        dg = diff_norm_g[l].reshape(DF_HEADS, DF_DV, 1)
        n_rows = n if need_ctx else n_lat
        attn_a = functools.partial(_attention, n_maps=1, dk=A_DKP, dv=A_V)
        attn_d = functools.partial(_attention, n_maps=2, dk=DF_DK, dv=DF_DV,
                                   lam=diff_lam[l], g=dg, lam_init=lam_init)
        ya = attn_a(qta, k_a, vta, tq=ATTN_TQ_MLA, lq=n_lat)
        yd = attn_d(qtd4, k_d, vtd, tq=ATTN_TQ_DIFF, lq=n_lat)
        ya_ctx, yd_ctx = ya, yd
        if need_ctx:
            ctx_tile = lambda v: v[:, -1:, :, v.shape[3] - n_ctx:]
            ya_ctx = attn_a(qta, k_a[:, n_lat:], ctx_tile(vta), tq=n_ctx, lq=n_ctx, q_start=n_lat)
            yd_ctx = attn_d(qtd4, k_d[:, n_lat:], ctx_tile(vtd), tq=n_ctx, lq=n_ctx, q_start=n_lat)

        mod_m = jnp.stack([jnp.broadcast_to(norm_g[l, 1][None], (2, D_MODEL)), ga1,
                           norm_g[l, 2][None] * (1.0 + sc2), sh2], axis=1)
        rw = jnp.pad(router_w[l], ((0, 0), (0, LANE - N_EXPERTS)))
        rb = jnp.pad(router_b[l].reshape(1, N_EXPERTS), ((0, 0), (0, LANE - N_EXPERTS)))
        xnew, h2, logits = _merge(n_rows, nlb, x_all, hf, hb, mo, ya, yd, ya_ctx, yd_ctx, gl,
                                  m_norm_g[l].reshape(1, -1), w_branch[l].astype(bf16), w_out[l].astype(bf16),
                                  mod_m, rw, rb)
        picked, wts = _moe(h2, logits, l, w_gu, b_gu, w_down, b_down)
        mod_r = jnp.stack([jnp.broadcast_to(norm_g[l, 3][None], (2, D_MODEL)), ga2], axis=1)
        x_all = _resid(nlb, xnew, picked, wts, mod_r)
    return x_all[:n_lat][None]
```

```python
import functools
import math

import jax
import jax.numpy as jnp
from jax import lax
from jax.experimental import pallas as pl
from jax.experimental.pallas import tpu as pltpu

f32 = jnp.float32
bf16 = jnp.bfloat16

D_MODEL = 1024
DEPTH = 2
GRID_W = 64
EPS = 1e-6
ROPE_BASE = 10000.0
M_HEADS = 4
M_DK = 128
M_DV = 128
GATE_SOFTCAP = 15.0
A_HEADS = 8
A_NOPE = 64
A_ROPE = 32
A_V = 64
Q_LORA = 256
KV_LORA = 128
DF_HEADS = 4
DF_DK = 64
DF_DV = 128
N_BRANCH = 3
BRANCH_W = 512
N_EXPERTS = 32
TOP_K = 4
D_EXPERT = 1024
SWIGLU_LIMIT = 7.0
SWIGLU_ALPHA = 1.702

IN_SIZES = (M_HEADS * M_DK, M_HEADS * M_DK, M_HEADS * M_DV, M_HEADS * M_DV, 4 * M_HEADS,
            Q_LORA, KV_LORA, A_ROPE,
            2 * DF_HEADS * DF_DK, 2 * DF_HEADS * DF_DK, DF_HEADS * DF_DV,
            N_BRANCH * D_MODEL)

LOG2E = 1.4426950408889634
LANE = 128
A_DKP = 128
A_DVP = A_V + 16
DF_DVP = DF_DV + 16
VMEM_LIMIT = 56 * 1024 * 1024

ROW_BLOCK = 256
MXU_DEPTH = 256
ATTN_TQ_MLA = 2048
ATTN_TQ_DIFF = 1024
ATTN_QSUB = 256
ATTN_TK = 1280
MLSTM_T = 256
MOE_TM = 512
H2_ROWS = 32768

NT_DIMS = (((1,), (1,)), ((), ()))


def _cparams(sem):
    return pltpu.CompilerParams(dimension_semantics=sem, vmem_limit_bytes=VMEM_LIMIT)


def _const_spec(shape):
    nd = len(shape)
    return pl.BlockSpec(shape, lambda *_: (0,) * nd)


def _adaln_kernel(cond_ref, w_ref, b_ref, o_ref):
    cnd = cond_ref[...]
    a = cnd * jax.nn.sigmoid(cnd)
    o_ref[...] = jnp.dot(a, w_ref[...], preferred_element_type=f32,
                         precision=lax.Precision.HIGHEST) + b_ref[...]


def _adaln(cond, ada_w, ada_b):
    tn = 1536
    nj = (6 * D_MODEL) // tn
    return pl.pallas_call(
        _adaln_kernel,
        grid=(DEPTH, nj),
        in_specs=[pl.BlockSpec((8, D_MODEL), lambda l, j: (0, 0)),
                  pl.BlockSpec((None, D_MODEL, tn), lambda l, j: (l, 0, j)),
                  pl.BlockSpec((None, 1, tn), lambda l, j: (l, 0, j))],
        out_specs=pl.BlockSpec((None, 8, tn), lambda l, j: (l, 0, j)),
        out_shape=jax.ShapeDtypeStruct((DEPTH, 8, 6 * D_MODEL), f32),
        compiler_params=_cparams(("parallel", "parallel")),
        name="adaln",
    )(cond, ada_w, ada_b.reshape(DEPTH, 1, 6 * D_MODEL))


M_W = M_HEADS * M_DK
DF_W = 2 * DF_HEADS * DF_DK
TOK_SIZES = (M_W, M_W, M_W, LANE, N_BRANCH * D_MODEL)
TR_SIZES = (M_W, 4 * M_HEADS, Q_LORA, KV_LORA, A_ROPE, DF_W, DF_W, DF_HEADS * DF_DV)


def _bounds(sizes):
    offs = [0]
    for sz in sizes:
        offs.append(offs[-1] + sz)
    return tuple(zip(offs[:-1], offs[1:]))


TOK_B = _bounds(TOK_SIZES)
TR_B = _bounds(TR_SIZES)
TOK_W = TOK_B[-1][1]
TR_W = TR_B[-1][1]


def _ones_row_block(tm):
    r = lax.broadcasted_iota(jnp.int32, (16, tm), 0)
    return jnp.where(r == 0, 1.0, 0.0).astype(bf16)


def _inproj_kernel(x_ref, *refs):
    _inproj_body(x_ref[...], *refs)


def _inproj_after_moe_kernel(xn_ref, y_ref, w_ref, modr_ref, *refs):
    x = _combine_residual(xn_ref, y_ref, w_ref, modr_ref)
    refs[-1][...] = x
    _inproj_body(x, *refs[:-1])


def _inproj_body(x, ab_ref, wtok_ref, wtr_ref, wuq_ref, wukv_ref, gq_ref, gkv_ref,
                 ca_ref, sa_ref, cd_ref, sd_ref,
                 mq_ref, mv_ref, mo_ref, mg_ref, gl_ref, mkt_ref, mgt_ref,
                 qta_ref, kta_ref, vta_ref, qtd_ref, ktd_ref, vtd_ref):
    tm = x.shape[0]
    ms = jnp.mean(x * x, axis=-1, keepdims=True)
    h = (x * lax.rsqrt(ms + EPS)) * ab_ref[0:1, :] + ab_ref[1:2, :]
    hb = h.astype(bf16)

    tok = jnp.dot(hb, wtok_ref[...], preferred_element_type=f32)
    mq, mv, mo, mg, gl = [tok[:, a:b] for a, b in TOK_B]
    mq_ref[...] = (mq * (M_DK ** -0.5)).astype(bf16)
    mv_ref[...] = mv.astype(bf16)
    mo_ref[...] = mo.astype(bf16)
    mg_ref[...] = mg
    gl_ref[...] = gl.astype(bf16)

    tr = lax.dot_general(wtr_ref[...], hb, NT_DIMS, preferred_element_type=f32)
    mk, mgt, cq, ckv, kpe, dq, dk, dv = [tr[a:b] for a, b in TR_B]
    mkt_ref[...] = mk.astype(bf16)
    mgt_ref[...] = mgt

    ca = ca_ref[...]
    sa = sa_ref[...]
    cd = cd_ref[...]
    sd = sd_ref[...]
    ones_blk = _ones_row_block(tm)
    zero_pad = jnp.zeros((A_DKP - A_NOPE - A_ROPE, tm), bf16)

    cqn = (cq * lax.rsqrt(jnp.mean(cq * cq, axis=0, keepdims=True) + EPS) * gq_ref[...]).astype(bf16)
    qscale = ((A_NOPE + A_ROPE) ** -0.5) * LOG2E
    qt = jnp.dot(wuq_ref[...], cqn, preferred_element_type=f32) * qscale
    nope_w = A_HEADS * A_NOPE
    rh = A_ROPE // 2
    for hh in range(A_HEADS):
        a = qt[nope_w + rh * hh:nope_w + rh * (hh + 1)]
        b = qt[nope_w + rh * (A_HEADS + hh):nope_w + rh * (A_HEADS + hh + 1)]
        qta_ref[hh, 0:A_NOPE, :] = qt[A_NOPE * hh:A_NOPE * (hh + 1)].astype(bf16)
        qta_ref[hh, A_NOPE:A_NOPE + rh, :] = (a * ca - b * sa).astype(bf16)
        qta_ref[hh, A_NOPE + rh:A_NOPE + A_ROPE, :] = (a * sa + b * ca).astype(bf16)
        qta_ref[hh, A_NOPE + A_ROPE:A_DKP, :] = zero_pad

    ckvn = (ckv * lax.rsqrt(jnp.mean(ckv * ckv, axis=0, keepdims=True) + EPS) * gkv_ref[...]).astype(bf16)
    kvt = jnp.dot(wukv_ref[...], ckvn, preferred_element_type=f32)
    ka = kpe[0:rh]
    kb = kpe[rh:A_ROPE]
    kpe_rot = jnp.concatenate([ka * ca - kb * sa, ka * sa + kb * ca,
                               jnp.zeros((A_DKP - A_NOPE - A_ROPE, tm), f32)], axis=0)
    for hh in range(A_HEADS):
        kt_h = jnp.concatenate([kvt[A_NOPE * hh:A_NOPE * (hh + 1)], kpe_rot], axis=0)
        kta_ref[hh] = kt_h.T.astype(bf16)
        vta_ref[hh, 0:A_V, :] = kvt[nope_w + A_V * hh:nope_w + A_V * (hh + 1)].astype(bf16)
        vta_ref[hh, A_V:A_DVP, :] = ones_blk

    dscale = (DF_DK ** -0.5) * LOG2E
    dh = DF_DK // 2
    for hh in range(2 * DF_HEADS):
        a = dq[dh * hh:dh * (hh + 1)]
        b = dq[DF_W // 2 + dh * hh:DF_W // 2 + dh * (hh + 1)]
        qtd_ref[hh, 0:dh, :] = ((a * cd - b * sd) * dscale).astype(bf16)
        qtd_ref[hh, dh:DF_DK, :] = ((a * sd + b * cd) * dscale).astype(bf16)
        a = dk[dh * hh:dh * (hh + 1)]
        b = dk[DF_W // 2 + dh * hh:DF_W // 2 + dh * (hh + 1)]
        ktd_ref[hh] = jnp.concatenate([a * cd - b * sd, a * sd + b * cd], axis=0).T.astype(bf16)
    for hh in range(DF_HEADS):
        vtd_ref[hh, 0:DF_DV, :] = dv[DF_DV * hh:DF_DV * (hh + 1)].astype(bf16)
        vtd_ref[hh, DF_DV:DF_DVP, :] = ones_blk


def _pack_inproj_weights(w, w_uq, w_ukv):
    o = [0]
    for s in IN_SIZES:
        o.append(o[-1] + s)
    mq, mk, mv, mo, mg, cq, ckv, kpe, dq, dk, dv, gl = [w[:, o[i]:o[i + 1]] for i in range(12)]
    mg_pad = jnp.pad(mg, ((0, 0), (0, LANE - mg.shape[1])))
    w_tok = jnp.concatenate([mq, mv, mo, mg_pad, gl], axis=1).astype(bf16)

    def split_halves(t, n_heads, d):
        t3 = t.reshape(t.shape[0], n_heads, d)
        return jnp.concatenate([t3[:, :, :d // 2].reshape(t.shape[0], -1),
                                t3[:, :, d // 2:].reshape(t.shape[0], -1)], axis=1)

    w_tr = jnp.concatenate([mk, mg, cq, ckv, kpe, split_halves(dq, 2 * DF_HEADS, DF_DK),
                            split_halves(dk, 2 * DF_HEADS, DF_DK), dv], axis=1).T.astype(bf16)
    uq = w_uq.reshape(Q_LORA, A_HEADS, A_NOPE + A_ROPE)
    half = A_ROPE // 2
    wt_uq = jnp.concatenate([uq[:, :, :A_NOPE].reshape(Q_LORA, -1),
                             uq[:, :, A_NOPE:A_NOPE + half].reshape(Q_LORA, -1),
                             uq[:, :, A_NOPE + half:].reshape(Q_LORA, -1)], axis=1).T.astype(bf16)
    ukv = w_ukv.reshape(KV_LORA, A_HEADS, A_NOPE + A_V)
    wt_ukv = jnp.concatenate([ukv[:, :, :A_NOPE].reshape(KV_LORA, -1),
                              ukv[:, :, A_NOPE:].reshape(KV_LORA, -1)], axis=1).T.astype(bf16)
    return w_tok, w_tr, wt_uq, wt_ukv


def _inproj(x_all, ab, n_lat_blocks, w_tok, w_tr, wt_uq, wt_ukv, gq, gkv, ca, sa, cd, sd, after_moe=None):
    n = x_all.shape[0]
    tm = ROW_BLOCK
    tk = _key_tile(n)
    per = tk // tm
    vtile = lambda i: (0, i // per, 0, i % per)
    nb = n // tm
    row = lambda i: (i, 0)
    colb = lambda i: (0, i)
    col3 = lambda i: (0, 0, i)
    grp = lambda i: (jnp.where(i >= n_lat_blocks, 1, 0), 0, 0)
    out_shapes = [
        jax.ShapeDtypeStruct((n, M_W), bf16), jax.ShapeDtypeStruct((n, M_W), bf16),
        jax.ShapeDtypeStruct((n, M_W), bf16), jax.ShapeDtypeStruct((n, LANE), f32),
        jax.ShapeDtypeStruct((n, N_BRANCH * D_MODEL), bf16),
        jax.ShapeDtypeStruct((M_W, n), bf16), jax.ShapeDtypeStruct((4 * M_HEADS, n), f32),
        jax.ShapeDtypeStruct((A_HEADS, A_DKP, n), bf16), jax.ShapeDtypeStruct((A_HEADS, n, A_DKP), bf16),
        jax.ShapeDtypeStruct((A_HEADS, n // tk, A_DVP, tk), bf16),
        jax.ShapeDtypeStruct((2 * DF_HEADS, DF_DK, n), bf16), jax.ShapeDtypeStruct((2 * DF_HEADS, n, DF_DK), bf16),
        jax.ShapeDtypeStruct((DF_HEADS, n // tk, DF_DVP, tk), bf16),
    ]
    out_specs = [
        pl.BlockSpec((tm, M_W), row), pl.BlockSpec((tm, M_W), row), pl.BlockSpec((tm, M_W), row),
        pl.BlockSpec((tm, LANE), row), pl.BlockSpec((tm, N_BRANCH * D_MODEL), row),
        pl.BlockSpec((M_W, tm), colb), pl.BlockSpec((4 * M_HEADS, tm), colb),
        pl.BlockSpec((A_HEADS, A_DKP, tm), col3), pl.BlockSpec((A_HEADS, tm, A_DKP), lambda i: (0, i, 0)),
        pl.BlockSpec((A_HEADS, None, A_DVP, tm), vtile),
        pl.BlockSpec((2 * DF_HEADS, DF_DK, tm), col3), pl.BlockSpec((2 * DF_HEADS, tm, DF_DK), lambda i: (0, i, 0)),
        pl.BlockSpec((DF_HEADS, None, DF_DVP, tm), vtile),
    ]
    in_specs = [
        pl.BlockSpec((tm, D_MODEL), row),
        pl.BlockSpec((None, 2, D_MODEL), grp),
        _const_spec(w_tok.shape), _const_spec(w_tr.shape), _const_spec(wt_uq.shape), _const_spec(wt_ukv.shape),
        _const_spec(gq.shape), _const_spec(gkv.shape),
        pl.BlockSpec((A_ROPE // 2, tm), colb), pl.BlockSpec((A_ROPE // 2, tm), colb),
        pl.BlockSpec((DF_DK // 2, tm), colb), pl.BlockSpec((DF_DK // 2, tm), colb),
    ]
    args = [x_all, ab, w_tok, w_tr, wt_uq, wt_ukv, gq, gkv, ca, sa, cd, sd]
    kern = _inproj_kernel
    if after_moe is not None:
        picked, wts, mod = after_moe
        kern = _inproj_after_moe_kernel
        in_specs = in_specs[:1] + [pl.BlockSpec((TOP_K, tm, D_MODEL), lambda i: (0, i, 0)),
                                   pl.BlockSpec((tm, TOP_K), row), pl.BlockSpec((None, 2, D_MODEL), grp)] + in_specs[1:]
        args = args[:1] + [picked, wts, mod] + args[1:]
        out_specs = out_specs + [pl.BlockSpec((tm, D_MODEL), row)]
        out_shapes = out_shapes + [jax.ShapeDtypeStruct((n, D_MODEL), f32)]
    return pl.pallas_call(
        kern, grid=(nb,), in_specs=in_specs, out_specs=out_specs, out_shape=out_shapes,
        compiler_params=_cparams(("parallel",)), name="inproj",
    )(*args)


def _attn_kernel(*refs, n_maps, dk, dv, tk, n_kt, qsub, lam_init):
    if n_maps == 2:
        lam_ref, g_ref, qt_ref, k_ref, vt_ref, o_ref = refs
    else:
        qt_ref, k_ref, vt_ref, o_ref = refs
    tq = qt_ref.shape[-1]
    dvp = vt_ref.shape[-2]
    chains = [(m, c) for m in range(n_maps) for c in range(tq // qsub)]

    def qk(j, m, c):
        qt = qt_ref[m * dk:(m + 1) * dk, c * qsub:(c + 1) * qsub]
        return jnp.dot(k_ref[m, j * tk:(j + 1) * tk, :], qt, preferred_element_type=f32)

    state = [(jnp.full((1, qsub), -jnp.inf, f32), jnp.zeros((dvp, qsub), f32)) for _ in chains]
    s_cur = [qk(0, m, c) for (m, c) in chains]
    for j in range(n_kt):
        vtj = vt_ref[j]
        for i, (m, c) in enumerate(chains):
            s_nxt = qk(j + 1, m, c) if j + 1 < n_kt else None
            m_run, acc = state[i]
            m_new = jnp.maximum(m_run, jnp.max(s_cur[i], axis=0, keepdims=True))
            alpha = jnp.exp2(m_run - m_new)
            p = jnp.exp2(s_cur[i] - m_new).astype(bf16)
            pv = jnp.dot(vtj, p, preferred_element_type=f32)
            state[i] = (m_new, alpha * acc + pv)
            s_cur[i] = s_nxt
    outs = []
    for m in range(n_maps):
        accs = [state[i][1] for i, (mm, _) in enumerate(chains) if mm == m]
        acc = accs[0] if len(accs) == 1 else jnp.concatenate(accs, axis=1)
        outs.append(acc[0:dv] / acc[dv:dv + 1])
    if n_maps == 1:
        o_ref[...] = outs[0].astype(o_ref.dtype)
    else:
        lf = lam_ref[...]
        lam = (jnp.exp(jnp.sum(lf[0:1] * lf[1:2], axis=1, keepdims=True))
               - jnp.exp(jnp.sum(lf[2:3] * lf[3:4], axis=1, keepdims=True)) + lam_init)
        o = outs[0] - lam * outs[1]
        o = o * lax.rsqrt(jnp.mean(o * o, axis=0, keepdims=True) + EPS) * g_ref[...]
        o_ref[...] = (o * (1.0 - lam_init)).astype(o_ref.dtype)


def _key_tile(lk):
    return max(t for t in range(MXU_DEPTH, min(ATTN_TK, lk) + 1, MXU_DEPTH) if lk % t == 0)


def _attention(qt, k, vt4, *, n_maps, dk, dv, tq, lq, q_start=0, lam=None, g=None, lam_init=0.0):
    n_heads = qt.shape[0]
    lk = k.shape[1]
    _, n_kt, dvp, tk = vt4.shape
    assert n_kt * tk == lk
    tq = min(tq, lq)
    qsub = min(ATTN_QSUB, tq)
    qb0 = q_start // tq
    in_specs = [
        pl.BlockSpec((None, n_maps * dk, tq), lambda h, i: (h, 0, qb0 + i)),
        pl.BlockSpec((n_maps, lk, dk), lambda h, i: (h, 0, 0)),
        pl.BlockSpec((None, n_kt, dvp, tk), lambda h, i: (h, 0, 0, 0)),
    ]
    args = [qt, k, vt4]
    if n_maps == 2:
        in_specs = [_const_spec(lam.shape), pl.BlockSpec((None, dv, 1), lambda h, i: (h, 0, 0))] + in_specs
        args = [lam, g] + args
    kern = functools.partial(_attn_kernel, n_maps=n_maps, dk=dk, dv=dv, tk=tk, n_kt=n_kt, qsub=qsub,
                             lam_init=lam_init)
    return pl.pallas_call(
        kern, grid=(n_heads, lq // tq), in_specs=in_specs,
        out_specs=pl.BlockSpec((None, dv, tq), lambda h, i: (h, 0, i)),
        out_shape=jax.ShapeDtypeStruct((n_heads, dv, lq), bf16),
        compiler_params=_cparams(("parallel", "parallel")),
        name="attn_diff" if n_maps == 2 else "attn_mla",
    )(*args)


def _split3(x):
    hi = x.astype(bf16)
    r = x - hi.astype(f32)
    mid = r.astype(bf16)
    lo = (r - mid.astype(f32)).astype(bf16)
    return hi, mid, lo


def _log_sigmoid(x):
    return jnp.minimum(x, 0.0) - jnp.log(1.0 + jnp.exp(-jnp.abs(x)))


def _softcap(x):
    return GATE_SOFTCAP * jnp.tanh(x * (1.0 / GATE_SOFTCAP))


def _mlstm_kernel(gb_ref, gbt_ref,
                  qf_ref, ktf_ref, vf_ref, gf_ref, gtf_ref,
                  qb_ref, ktb_ref, vb_ref, gb2_ref, gtb_ref,
                  hf_ref, hb_ref, c_ref, m_ref):
    t = qf_ref.shape[0]

    @pl.when(pl.program_id(0) == 0)
    def _():
        c_ref[...] = jnp.zeros_like(c_ref)
        m_ref[...] = jnp.zeros_like(m_ref)

    row = lax.broadcasted_iota(jnp.int32, (t, t), 0)
    col = lax.broadcasted_iota(jnp.int32, (t, t), 1)
    lane = lax.broadcasted_iota(jnp.int32, (t, LANE), 1)
    ones_col = jnp.where(lane == 0, 1.0, 0.0).astype(bf16)
    dirs = ((qf_ref, ktf_ref, vf_ref, gf_ref, gtf_ref, hf_ref),
            (qb_ref, ktb_ref, vb_ref, gb2_ref, gtb_ref, hb_ref))
    for d, (q_ref, kt_ref, v_ref, g_ref, gt_ref, o_ref) in enumerate(dirs):
        mask = (col <= row) if d == 0 else (col >= row)
        maskb = mask.astype(bf16)
        last = t - 1 if d == 0 else 0
        g_tok = _softcap(g_ref[...] + gb_ref[...])
        g_t = _softcap(gt_ref[...] + gbt_ref[...])
        b_tok = sum(jnp.dot(maskb, part, preferred_element_type=f32) for part in _split3(_log_sigmoid(g_tok)))
        b_t = sum(lax.dot_general(part, maskb, NT_DIMS, preferred_element_type=f32)
                  for part in _split3(_log_sigmoid(g_t)))
        for hh in range(M_HEADS):
            ri = 4 * (2 * d) + hh
            rf = 4 * (2 * d + 1) + hh
            idx = d * M_HEADS + hh
            b_col = b_tok[:, rf:rf + 1]
            b_row = b_t[rf:rf + 1, :]
            a_row = g_t[ri:ri + 1, :] - b_row
            m_prev = m_ref[idx, 0:1, 0:1]
            ld = jnp.where(mask, b_col + a_row, -jnp.inf)
            inter = b_col + m_prev
            mj = jnp.maximum(inter, jnp.max(ld, axis=1, keepdims=True))
            dmat = jnp.exp(ld - mj)
            q = q_ref[:, hh * M_DK:(hh + 1) * M_DK]
            kt = kt_ref[hh * M_DK:(hh + 1) * M_DK, :]
            vp = jnp.concatenate([v_ref[:, hh * M_DV:(hh + 1) * M_DV], ones_col], axis=1)
            sc = jnp.dot(q, kt, preferred_element_type=f32) * dmat
            cp = c_ref[idx]
            num = (jnp.dot(sc.astype(bf16), vp, preferred_element_type=f32)
                   + jnp.exp(inter - mj) * jnp.dot(q, cp.astype(bf16), preferred_element_type=f32))
            den = num[:, M_DV:M_DV + 1]
            hj = num[:, 0:M_DV] / jnp.maximum(jnp.abs(den), jnp.exp(-mj))
            o_ref[:, hh * M_DV:(hh + 1) * M_DV] = hj.astype(o_ref.dtype)
            b_tot = b_t[rf:rf + 1, last:last + 1]
            lw = b_tot + a_row
            m_new = jnp.maximum(b_tot + m_prev, jnp.max(lw, axis=1, keepdims=True))
            kw = (kt.astype(f32) * jnp.exp(lw - m_new)).astype(bf16)
            c_ref[idx] = jnp.exp(b_tot + m_prev - m_new) * cp + jnp.dot(kw, vp, preferred_element_type=f32)
            m_ref[idx] = jnp.broadcast_to(m_new, (8, LANE))


def _mlstm(mq, mkt, mv, mg, mgt, gate_b):
    n = mq.shape[0]
    t = MLSTM_T
    nb = n // t
    gb = jnp.pad(gate_b.reshape(1, 4 * M_HEADS), ((0, 0), (0, LANE - 4 * M_HEADS)))
    gbt = gate_b.reshape(4 * M_HEADS, 1)
    fwd = lambda s: jnp.where(s == 0, nb - 1, s - 1)
    bwd = lambda s: nb - 1 - s
    specs = []
    for im in (fwd, bwd):
        specs += [pl.BlockSpec((t, M_W), lambda s, im=im: (im(s), 0)),
                  pl.BlockSpec((M_W, t), lambda s, im=im: (0, im(s))),
                  pl.BlockSpec((t, M_W), lambda s, im=im: (im(s), 0)),
                  pl.BlockSpec((t, LANE), lambda s, im=im: (im(s), 0)),
                  pl.BlockSpec((4 * M_HEADS, t), lambda s, im=im: (0, im(s)))]
    return pl.pallas_call(
        _mlstm_kernel, grid=(nb,),
        in_specs=[_const_spec(gb.shape), _const_spec(gbt.shape)] + specs,
        out_specs=[pl.BlockSpec((t, M_W), lambda s: (fwd(s), 0)), pl.BlockSpec((t, M_W), lambda s: (bwd(s), 0))],
        out_shape=[jax.ShapeDtypeStruct((n, M_W), bf16), jax.ShapeDtypeStruct((n, M_W), bf16)],
        scratch_shapes=[pltpu.VMEM((2 * M_HEADS, M_DK, 2 * M_DV), f32), pltpu.VMEM((2 * M_HEADS, 8, LANE), f32)],
        compiler_params=_cparams(("arbitrary",)), name="mlstm",
    )(gb, gbt, mq, mkt, mv, mg, mgt, mq, mkt, mv, mg, mgt)


def _rms(x, axis=-1):
    return x * lax.rsqrt(jnp.mean(x * x, axis=axis, keepdims=True) + EPS)


def _merge_kernel(x_ref, hf_ref, hb_ref, mo_ref, ya_ref, yd_ref, yac_ref, ydc_ref, gl_ref, mng_ref, wb_ref,
                  wout_ref, mod_ref, rw_ref, rb_ref, xnew_ref, h2_ref, logit_ref, *, n_blocks, n_lat_blocks):
    i = pl.program_id(0)

    @pl.when(i < n_blocks)
    def _():
        _merge_block(x_ref, hf_ref, hb_ref, mo_ref, ya_ref, yd_ref, yac_ref, ydc_ref, gl_ref, mng_ref, wb_ref,
                     wout_ref, mod_ref, rw_ref, rb_ref, xnew_ref, h2_ref, logit_ref, i >= n_lat_blocks)

    @pl.when(i >= n_blocks)
    def _():
        h2_ref[...] = jnp.zeros_like(h2_ref)


def _merge_block(x_ref, hf_ref, hb_ref, mo_ref, ya_ref, yd_ref, yac_ref, ydc_ref, gl_ref, mng_ref, wb_ref,
                 wout_ref, mod_ref, rw_ref, rb_ref, xnew_ref, h2_ref, logit_ref, is_ctx):
    hs = hf_ref[...].astype(f32) + hb_ref[...].astype(f32)
    parts = [_rms(hs[:, hh * M_DV:(hh + 1) * M_DV]) for hh in range(M_HEADS)]
    ym = jnp.concatenate(parts, axis=1) * mng_ref[...] * jax.nn.sigmoid(mo_ref[...].astype(f32))
    yb0 = jnp.dot(ym.astype(bf16), wb_ref[0], preferred_element_type=f32)
    tm = x_ref.shape[0]
    tn_dims = (((0,), (0,)), ((), ()))
    ya = jnp.where(is_ctx, yac_ref[...], ya_ref[...])
    yd = jnp.where(is_ctx, ydc_ref[...], yd_ref[...])
    yb1 = lax.dot_general(ya.reshape(BRANCH_W, tm), wb_ref[1], tn_dims, preferred_element_type=f32)
    yb2 = lax.dot_general(yd.reshape(BRANCH_W, tm), wb_ref[2], tn_dims, preferred_element_type=f32)
    d = D_MODEL
    z = (jax.nn.sigmoid(gl_ref[:, 0:d].astype(f32)) * yb0
         + jax.nn.sigmoid(gl_ref[:, d:2 * d].astype(f32)) * yb1
         + jax.nn.sigmoid(gl_ref[:, 2 * d:3 * d].astype(f32)) * yb2)
    y = jnp.dot(z.astype(bf16), wout_ref[...], preferred_element_type=f32)
    xnew = x_ref[...] + mod_ref[1:2, :] * (_rms(y) * mod_ref[0:1, :])
    h2 = _rms(xnew) * mod_ref[2:3, :] + mod_ref[3:4, :]
    xnew_ref[...] = xnew
    h2_ref[...] = h2.astype(bf16)
    h2_hi = h2.astype(bf16)
    h2_lo = (h2 - h2_hi.astype(f32)).astype(bf16)
    rw = rw_ref[...]
    rw_hi = rw.astype(bf16)
    rw_lo = (rw - rw_hi.astype(f32)).astype(bf16)
    logit_ref[...] = (jnp.dot(h2_hi, rw_hi, preferred_element_type=f32)
                      + jnp.dot(h2_lo, rw_hi, preferred_element_type=f32)
                      + jnp.dot(h2_hi, rw_lo, preferred_element_type=f32) + rb_ref[...])


def _merge(n_rows, n_lat_blocks, x_all, hf, hb, mo, ya, yd, ya_ctx, yd_ctx, gl, mng, wb, wout, mod, rw, rb):
    tm = ROW_BLOCK
    nb = n_rows // tm
    n_steps = max(n_rows, H2_ROWS) // tm
    last = nb - 1
    row = lambda i: (jnp.minimum(i, last), 0)
    lat3 = lambda i: (0, 0, jnp.minimum(i, n_lat_blocks - 1))
    grp = lambda i: (jnp.where(i >= n_lat_blocks, 1, 0), 0, 0)
    in_specs = [pl.BlockSpec((tm, D_MODEL), row), pl.BlockSpec((tm, M_W), row), pl.BlockSpec((tm, M_W), row),
                pl.BlockSpec((tm, M_W), row),
                pl.BlockSpec((A_HEADS, A_V, tm), lat3), pl.BlockSpec((DF_HEADS, DF_DV, tm), lat3),
                pl.BlockSpec((A_HEADS, A_V, tm), lambda i: (0, 0, 0)),
                pl.BlockSpec((DF_HEADS, DF_DV, tm), lambda i: (0, 0, 0)),
                pl.BlockSpec((tm, N_BRANCH * D_MODEL), row), _const_spec(mng.shape), _const_spec(wb.shape),
                _const_spec(wout.shape), pl.BlockSpec((None, 4, D_MODEL), grp),
                _const_spec(rw.shape), _const_spec(rb.shape)]
    kern = functools.partial(_merge_kernel, n_blocks=nb, n_lat_blocks=n_lat_blocks)
    return pl.pallas_call(
        kern, grid=(n_steps,), in_specs=in_specs,
        out_specs=[pl.BlockSpec((tm, D_MODEL), row), pl.BlockSpec((tm, D_MODEL), lambda i: (i, 0)),
                   pl.BlockSpec((tm, LANE), row)],
        out_shape=[jax.ShapeDtypeStruct((n_rows, D_MODEL), f32),
                   jax.ShapeDtypeStruct((max(n_rows, H2_ROWS), D_MODEL), bf16),
                   jax.ShapeDtypeStruct((n_rows, LANE), f32)],
        compiler_params=_cparams(("arbitrary",)), name="merge",
    )(x_all, hf, hb, mo, ya, yd, ya_ctx, yd_ctx, gl, mng, wb, wout, mod, rw, rb)


def _moe_kernel(be_ref, nu_ref, xs_ref, wgu_ref, bgu_ref, wd_ref, bd_ref, o_ref, wgu_s, wd_s):
    i = pl.program_id(0)
    prev = be_ref[jnp.maximum(i - 1, 0)]
    used = i < nu_ref[0]

    @pl.when(jnp.logical_and(used, jnp.logical_or(i == 0, be_ref[i] != prev)))
    def _():
        wgu_s[...] = wgu_ref[...].astype(bf16)
        wd_s[...] = wd_ref[...].astype(bf16)

    @pl.when(used)
    def _():
        gu = jnp.dot(xs_ref[...], wgu_s[...], preferred_element_type=f32) + bgu_ref[...]
        gate = jnp.minimum(gu[:, :D_EXPERT], SWIGLU_LIMIT)
        up = jnp.clip(gu[:, D_EXPERT:], -SWIGLU_LIMIT, SWIGLU_LIMIT)
        act = (up + 1.0) * (gate * jax.nn.sigmoid(SWIGLU_ALPHA * gate))
        y = jnp.dot(act.astype(bf16), wd_s[...], preferred_element_type=f32) + bd_ref[...]
        o_ref[...] = y.astype(o_ref.dtype)

    @pl.when(jnp.logical_not(used))
    def _():
        o_ref[...] = jnp.zeros_like(o_ref)


def _moe_experts(blk_e, n_used, xs, layer, w_gu, b_gu, w_down, b_down):
    cap = xs.shape[0]
    tm = MOE_TM
    nblk = cap // tm
    grid_spec = pltpu.PrefetchScalarGridSpec(
        num_scalar_prefetch=2, grid=(nblk,),
        in_specs=[pl.BlockSpec((tm, D_MODEL), lambda i, be, nu: (i, 0)),
                  pl.BlockSpec((None, None, D_MODEL, 2 * D_EXPERT), lambda i, be, nu: (layer, be[i], 0, 0)),
                  pl.BlockSpec((None, None, 1, 2 * D_EXPERT), lambda i, be, nu: (layer, be[i], 0, 0)),
                  pl.BlockSpec((None, None, D_EXPERT, D_MODEL), lambda i, be, nu: (layer, be[i], 0, 0)),
                  pl.BlockSpec((None, None, 1, D_MODEL), lambda i, be, nu: (layer, be[i], 0, 0))],
        out_specs=pl.BlockSpec((tm, D_MODEL), lambda i, be, nu: (i, 0)),
        scratch_shapes=[pltpu.VMEM((D_MODEL, 2 * D_EXPERT), bf16), pltpu.VMEM((D_EXPERT, D_MODEL), bf16)])
    return pl.pallas_call(
        _moe_kernel, grid_spec=grid_spec,
        out_shape=jax.ShapeDtypeStruct((cap, D_MODEL), bf16),
        compiler_params=_cparams(("arbitrary",)), name="moe_experts",
    )(blk_e, n_used, xs, w_gu, b_gu.reshape(DEPTH, N_EXPERTS, 1, -1), w_down, b_down.reshape(DEPTH, N_EXPERTS, 1, -1))


def _moe(h2, logits, layer, w_gu, b_gu, w_down, b_down):
    n = logits.shape[0]
    tm = MOE_TM
    i32 = jnp.int32
    top_v, top_i = lax.top_k(logits[:, :N_EXPERTS], TOP_K)
    wts = jax.nn.softmax(top_v, axis=-1)
    nk = n * TOP_K
    flat_e = top_i.reshape(nk).astype(i32)
    iota = jnp.arange(nk, dtype=i32)
    skey = lax.sort(flat_e * nk + iota)
    se = skey // nk
    order = skey - se * nk
    e_ids = jnp.arange(N_EXPERTS, dtype=i32)
    grp_start = jnp.sum((se[:, None] < e_ids[None, :]).astype(i32), axis=0)
    counts = jnp.sum((se[:, None] == e_ids[None, :]).astype(i32), axis=0)
    padded = (counts + tm - 1) // tm * tm
    pad_end = jnp.cumsum(padded)
    pad_start = pad_end - padded
    off = pad_start - grp_start
    hit = top_i[:, :, None] == e_ids[None, None, :]
    per_tok = jnp.sum(hit.astype(i32), axis=1)
    rank = jnp.cumsum(per_tok, axis=0) - per_tok
    pos = jnp.sum(jnp.where(hit, (pad_start[None, :] + rank)[:, None, :], 0), axis=2)
    pos_km = pos.T.reshape(nk)
    cap = -(-(nk + N_EXPERTS * (tm - 1)) // tm) * tm
    nblk = cap // tm
    bstart = jnp.arange(nblk, dtype=i32) * tm
    blk_e = jnp.minimum(jnp.sum((pad_end[None, :] <= bstart[:, None]).astype(i32), axis=1), N_EXPERTS - 1)
    n_used = (pad_end[-1] // tm).astype(i32).reshape(1)
    sel = blk_e[:, None] == e_ids[None, :]
    off_b = jnp.sum(jnp.where(sel, off[None, :], 0), axis=1)
    end_b = jnp.sum(jnp.where(sel, (pad_start + counts)[None, :], 0), axis=1)
    p2 = bstart[:, None] + jnp.arange(tm, dtype=i32)[None, :]
    src = jnp.clip(p2 - off_b[:, None], 0, nk - 1)
    slot_t = jnp.where(p2 < end_b[:, None], (order // TOP_K)[src], p2 % n).reshape(cap)
    xs = h2[slot_t]
    ys = _moe_experts(blk_e, n_used, xs, layer, w_gu, b_gu, w_down, b_down)
    return ys[pos_km].reshape(TOP_K, n, D_MODEL), wts


def _combine_residual(x_ref, y_ref, w_ref, mod_ref):
    w = w_ref[...]
    f = y_ref[0].astype(f32) * w[:, 0:1]
    for kk in range(1, TOP_K):
        f = f + y_ref[kk].astype(f32) * w[:, kk:kk + 1]
    return x_ref[...] + mod_ref[1:2, :] * (_rms(f) * mod_ref[0:1, :])


def _resid_kernel(x_ref, y_ref, w_ref, mod_ref, o_ref):
    o_ref[...] = _combine_residual(x_ref, y_ref, w_ref, mod_ref)


def _resid(n_lat_blocks, xnew, picked, wts, mod):
    n = xnew.shape[0]
    tm = ROW_BLOCK
    row = lambda i: (i, 0)
    grp = lambda i: (jnp.where(i >= n_lat_blocks, 1, 0), 0, 0)
    return pl.pallas_call(
        _resid_kernel, grid=(n // tm,),
        in_specs=[pl.BlockSpec((tm, D_MODEL), row), pl.BlockSpec((TOP_K, tm, D_MODEL), lambda i: (0, i, 0)),
                  pl.BlockSpec((tm, TOP_K), row), pl.BlockSpec((None, 2, D_MODEL), grp)],
        out_specs=pl.BlockSpec((tm, D_MODEL), row),
        out_shape=jax.ShapeDtypeStruct((n, D_MODEL), f32),
        compiler_params=_cparams(("parallel",)), name="resid",
    )(xnew, picked, wts, mod)


def _rope_tables(n_lat, n_ctx, dim):
    rows = n_lat // GRID_W
    row = jnp.repeat(jnp.arange(rows, dtype=f32), GRID_W)
    col = jnp.tile(jnp.arange(GRID_W, dtype=f32), rows)
    quarter = dim // 4
    inv = ROPE_BASE ** (-jnp.arange(quarter, dtype=f32) / quarter)
    ang = jnp.concatenate([row[:, None] * inv, col[:, None] * inv], axis=-1)
    cos = jnp.concatenate([jnp.cos(ang), jnp.ones((n_ctx, dim // 2), f32)], axis=0).T
    sin = jnp.concatenate([jnp.sin(ang), jnp.zeros((n_ctx, dim // 2), f32)], axis=0).T
    return cos, sin


def kernel(x, c, ctx, c_ctx, ada_w, ada_b, norm_g, w_in, m_gate_b, m_norm_g, q_norm_g, w_uq, kv_norm_g, w_ukv,
           diff_lam, diff_norm_g, w_branch, w_out, router_w, router_b, w_gu, b_gu, w_down, b_down):
    n_lat = x.shape[1]
    n_ctx = ctx.shape[1]
    n = n_lat + n_ctx
    assert x.shape[0] == 1 and n_lat % ROW_BLOCK == 0 and n_ctx == MLSTM_T == ROW_BLOCK and n_lat % GRID_W == 0
    nlb = n_lat // ROW_BLOCK
    x_all = jnp.concatenate([x[0], ctx[0]], axis=0)

    cond = jnp.zeros((8, D_MODEL), f32).at[0].set(c[0]).at[1].set(c_ctx)
    mods = _adaln(cond, ada_w, ada_b)[:, :2].reshape(DEPTH, 2, 6, D_MODEL)

    ca, sa = _rope_tables(n_lat, n_ctx, A_ROPE)
    cd, sd = _rope_tables(n_lat, n_ctx, DF_DK)

    pending = None
    for l in range(DEPTH):
        need_ctx = l < DEPTH - 1
        lam_init = 0.8 - 0.6 * math.exp(-0.3 * l)
        sh1, sc1, ga1, sh2, sc2, ga2 = [mods[l, :, j] for j in range(6)]
        ab1 = jnp.stack([norm_g[l, 0][None] * (1.0 + sc1), sh1], axis=1)
        w_tok, w_tr, wt_uq, wt_ukv = _pack_inproj_weights(w_in[l], w_uq[l], w_ukv[l])
        proj = _inproj(x_all, ab1, nlb, w_tok, w_tr, wt_uq, wt_ukv,
                       q_norm_g[l].reshape(Q_LORA, 1), kv_norm_g[l].reshape(KV_LORA, 1), ca, sa, cd, sd,
                       after_moe=pending)
        if pending is not None:
            x_all = proj[-1]
        (mq, mv, mo, mg, gl, mkt, mgt, qta, k_a, vta, qtd, k_d, vtd) = proj[:13]

        hf, hb = _mlstm(mq, mkt, mv, mg, mgt, m_gate_b[l])

        qtd4 = qtd.reshape(DF_HEADS, 2 * DF_DK, n)
        dg = diff_norm_g[l].reshape(DF_HEADS, DF_DV, 1)
        n_rows = n if need_ctx else n_lat
        attn_a = functools.partial(_attention, n_maps=1, dk=A_DKP, dv=A_V)
        attn_d = functools.partial(_attention, n_maps=2, dk=DF_DK, dv=DF_DV,
                                   lam=diff_lam[l], g=dg, lam_init=lam_init)
        ya = attn_a(qta, k_a, vta, tq=ATTN_TQ_MLA, lq=n_lat)
        yd = attn_d(qtd4, k_d, vtd, tq=ATTN_TQ_DIFF, lq=n_lat)
        ya_ctx, yd_ctx = ya, yd
        if need_ctx:
            ctx_tile = lambda v: v[:, -1:, :, v.shape[3] - n_ctx:]
            ya_ctx = attn_a(qta, k_a[:, n_lat:], ctx_tile(vta), tq=n_ctx, lq=n_ctx, q_start=n_lat)
            yd_ctx = attn_d(qtd4, k_d[:, n_lat:], ctx_tile(vtd), tq=n_ctx, lq=n_ctx, q_start=n_lat)

        mod_m = jnp.stack([jnp.broadcast_to(norm_g[l, 1][None], (2, D_MODEL)), ga1,
                           norm_g[l, 2][None] * (1.0 + sc2), sh2], axis=1)
        rw = jnp.pad(router_w[l], ((0, 0), (0, LANE - N_EXPERTS)))
        rb = jnp.pad(router_b[l].reshape(1, N_EXPERTS), ((0, 0), (0, LANE - N_EXPERTS)))
        xnew, h2, logits = _merge(n_rows, nlb, x_all, hf, hb, mo, ya, yd, ya_ctx, yd_ctx, gl,
                                  m_norm_g[l].reshape(1, -1), w_branch[l].astype(bf16), w_out[l].astype(bf16),
                                  mod_m, rw, rb)
        picked, wts = _moe(h2, logits, l, w_gu, b_gu, w_down, b_down)
        mod_r = jnp.stack([jnp.broadcast_to(norm_g[l, 3][None], (2, D_MODEL)), ga2], axis=1)
        if need_ctx:
            x_all, pending = xnew, (picked, wts, mod_r)
        else:
            x_all = _resid(nlb, xnew, picked, wts, mod_r)
    return x_all[:n_lat][None]
```

```python
import functools
import math

import jax
import jax.numpy as jnp
from jax import lax
from jax.experimental import pallas as pl
from jax.experimental.pallas import tpu as pltpu

f32 = jnp.float32
bf16 = jnp.bfloat16

D_MODEL = 1024
DEPTH = 2
GRID_W = 64
EPS = 1e-6
ROPE_BASE = 10000.0
M_HEADS = 4
M_DK = 128
M_DV = 128
GATE_SOFTCAP = 15.0
A_HEADS = 8
A_NOPE = 64
A_ROPE = 32
A_V = 64
Q_LORA = 256
KV_LORA = 128
DF_HEADS = 4
DF_DK = 64
DF_DV = 128
N_BRANCH = 3
BRANCH_W = 512
N_EXPERTS = 32
TOP_K = 4
D_EXPERT = 1024
SWIGLU_LIMIT = 7.0
SWIGLU_ALPHA = 1.702

IN_SIZES = (M_HEADS * M_DK, M_HEADS * M_DK, M_HEADS * M_DV, M_HEADS * M_DV, 4 * M_HEADS,
            Q_LORA, KV_LORA, A_ROPE,
            2 * DF_HEADS * DF_DK, 2 * DF_HEADS * DF_DK, DF_HEADS * DF_DV,
            N_BRANCH * D_MODEL)

LOG2E = 1.4426950408889634
LANE = 128
A_DKP = 128
A_DVP = A_V + 16
DF_DVP = DF_DV + 16
VMEM_LIMIT = 56 * 1024 * 1024

ROW_BLOCK = 256
MXU_DEPTH = 256
ATTN_TQ_MLA = 2048
ATTN_TQ_DIFF = 1024
ATTN_QSUB = 256
ATTN_TK = 1280
MLSTM_T = 256
MOE_TM = 512
H2_ROWS = 32768

NT_DIMS = (((1,), (1,)), ((), ()))


def _cparams(sem):
    return pltpu.CompilerParams(dimension_semantics=sem, vmem_limit_bytes=VMEM_LIMIT)


def _const_spec(shape):
    nd = len(shape)
    return pl.BlockSpec(shape, lambda *_: (0,) * nd)


def _adaln_kernel(cond_ref, w_ref, b_ref, o_ref):
    cnd = cond_ref[...]
    a = cnd * jax.nn.sigmoid(cnd)
    o_ref[...] = jnp.dot(a, w_ref[...], preferred_element_type=f32,
                         precision=lax.Precision.HIGHEST) + b_ref[...]


def _adaln(cond, ada_w, ada_b):
    tn = 1536
    nj = (6 * D_MODEL) // tn
    return pl.pallas_call(
        _adaln_kernel,
        grid=(DEPTH, nj),
        in_specs=[pl.BlockSpec((8, D_MODEL), lambda l, j: (0, 0)),
                  pl.BlockSpec((None, D_MODEL, tn), lambda l, j: (l, 0, j)),
                  pl.BlockSpec((None, 1, tn), lambda l, j: (l, 0, j))],
        out_specs=pl.BlockSpec((None, 8, tn), lambda l, j: (l, 0, j)),
        out_shape=jax.ShapeDtypeStruct((DEPTH, 8, 6 * D_MODEL), f32),
        compiler_params=_cparams(("parallel", "parallel")),
        name="adaln",
    )(cond, ada_w, ada_b.reshape(DEPTH, 1, 6 * D_MODEL))


M_W = M_HEADS * M_DK
DF_W = 2 * DF_HEADS * DF_DK
TOK_SIZES = (M_W, M_W, M_W, LANE, N_BRANCH * D_MODEL)
TR_SIZES = (M_W, 4 * M_HEADS, Q_LORA, KV_LORA, A_ROPE, DF_W, DF_W, DF_HEADS * DF_DV)


def _bounds(sizes):
    offs = [0]
    for sz in sizes:
        offs.append(offs[-1] + sz)
    return tuple(zip(offs[:-1], offs[1:]))


TOK_B = _bounds(TOK_SIZES)
TR_B = _bounds(TR_SIZES)
TOK_W = TOK_B[-1][1]
TR_W = TR_B[-1][1]


def _ones_row_block(tm):
    r = lax.broadcasted_iota(jnp.int32, (16, tm), 0)
    return jnp.where(r == 0, 1.0, 0.0).astype(bf16)


def _inproj_first_kernel(xl_ref, xc_ref, *refs, n_lat_blocks):
    x = jnp.where(pl.program_id(0) >= n_lat_blocks, xc_ref[...], xl_ref[...])
    _inproj_body(x, *refs)


def _inproj_after_moe_kernel(xn_ref, y_ref, w_ref, modr_ref, *refs):
    x = _combine_residual(xn_ref, y_ref, w_ref, modr_ref)
    refs[-1][...] = x
    _inproj_body(x, *refs[:-1])


def _inproj_body(x, ab_ref, wtok_ref, wtr_ref, wuq_ref, wukv_ref, gq_ref, gkv_ref,
                 ca_ref, sa_ref, cd_ref, sd_ref,
                 mq_ref, mv_ref, mo_ref, mg_ref, gl_ref, mkt_ref, mgt_ref,
                 qta_ref, kta_ref, vta_ref, qtd_ref, ktd_ref, vtd_ref):
    tm = x.shape[0]
    ms = jnp.mean(x * x, axis=-1, keepdims=True)
    h = (x * lax.rsqrt(ms + EPS)) * ab_ref[0:1, :] + ab_ref[1:2, :]
    hb = h.astype(bf16)

    tok = jnp.dot(hb, wtok_ref[...], preferred_element_type=f32)
    mq, mv, mo, mg, gl = [tok[:, a:b] for a, b in TOK_B]
    mq_ref[...] = (mq * (M_DK ** -0.5)).astype(bf16)
    mv_ref[...] = mv.astype(bf16)
    mo_ref[...] = mo.astype(bf16)
    mg_ref[...] = mg
    gl_ref[...] = gl.astype(bf16)

    tr = lax.dot_general(wtr_ref[...], hb, NT_DIMS, preferred_element_type=f32)
    mk, mgt, cq, ckv, kpe, dq, dk, dv = [tr[a:b] for a, b in TR_B]
    mkt_ref[...] = mk.astype(bf16)
    mgt_ref[...] = mgt

    ca = ca_ref[...]
    sa = sa_ref[...]
    cd = cd_ref[...]
    sd = sd_ref[...]
    ones_blk = _ones_row_block(tm)
    zero_pad = jnp.zeros((A_DKP - A_NOPE - A_ROPE, tm), bf16)

    cqn = (cq * lax.rsqrt(jnp.mean(cq * cq, axis=0, keepdims=True) + EPS) * gq_ref[...]).astype(bf16)
    qscale = ((A_NOPE + A_ROPE) ** -0.5) * LOG2E
    qt = jnp.dot(wuq_ref[...], cqn, preferred_element_type=f32) * qscale
    nope_w = A_HEADS * A_NOPE
    rh = A_ROPE // 2
    for hh in range(A_HEADS):
        a = qt[nope_w + rh * hh:nope_w + rh * (hh + 1)]
        b = qt[nope_w + rh * (A_HEADS + hh):nope_w + rh * (A_HEADS + hh + 1)]
        qta_ref[hh, 0:A_NOPE, :] = qt[A_NOPE * hh:A_NOPE * (hh + 1)].astype(bf16)
        qta_ref[hh, A_NOPE:A_NOPE + rh, :] = (a * ca - b * sa).astype(bf16)
        qta_ref[hh, A_NOPE + rh:A_NOPE + A_ROPE, :] = (a * sa + b * ca).astype(bf16)
        qta_ref[hh, A_NOPE + A_ROPE:A_DKP, :] = zero_pad

    ckvn = (ckv * lax.rsqrt(jnp.mean(ckv * ckv, axis=0, keepdims=True) + EPS) * gkv_ref[...]).astype(bf16)
    kvt = jnp.dot(wukv_ref[...], ckvn, preferred_element_type=f32)
    ka = kpe[0:rh]
    kb = kpe[rh:A_ROPE]
    kpe_rot = jnp.concatenate([ka * ca - kb * sa, ka * sa + kb * ca,
                               jnp.zeros((A_DKP - A_NOPE - A_ROPE, tm), f32)], axis=0)
    for hh in range(A_HEADS):
        kt_h = jnp.concatenate([kvt[A_NOPE * hh:A_NOPE * (hh + 1)], kpe_rot], axis=0)
        kta_ref[hh] = kt_h.T.astype(bf16)
        vta_ref[hh, 0:A_V, :] = kvt[nope_w + A_V * hh:nope_w + A_V * (hh + 1)].astype(bf16)
        vta_ref[hh, A_V:A_DVP, :] = ones_blk

    dscale = (DF_DK ** -0.5) * LOG2E
    dh = DF_DK // 2
    for hh in range(2 * DF_HEADS):
        a = dq[dh * hh:dh * (hh + 1)]
        b = dq[DF_W // 2 + dh * hh:DF_W // 2 + dh * (hh + 1)]
        qtd_ref[hh, 0:dh, :] = ((a * cd - b * sd) * dscale).astype(bf16)
        qtd_ref[hh, dh:DF_DK, :] = ((a * sd + b * cd) * dscale).astype(bf16)
        a = dk[dh * hh:dh * (hh + 1)]
        b = dk[DF_W // 2 + dh * hh:DF_W // 2 + dh * (hh + 1)]
        ktd_ref[hh] = jnp.concatenate([a * cd - b * sd, a * sd + b * cd], axis=0).T.astype(bf16)
    for hh in range(DF_HEADS):
        vtd_ref[hh, 0:DF_DV, :] = dv[DF_DV * hh:DF_DV * (hh + 1)].astype(bf16)
        vtd_ref[hh, DF_DV:DF_DVP, :] = ones_blk


def _pack_inproj_weights(w, w_uq, w_ukv):
    o = [0]
    for s in IN_SIZES:
        o.append(o[-1] + s)
    mq, mk, mv, mo, mg, cq, ckv, kpe, dq, dk, dv, gl = [w[:, o[i]:o[i + 1]] for i in range(12)]
    mg_pad = jnp.pad(mg, ((0, 0), (0, LANE - mg.shape[1])))
    w_tok = jnp.concatenate([mq, mv, mo, mg_pad, gl], axis=1).astype(bf16)

    def split_halves(t, n_heads, d):
        t3 = t.reshape(t.shape[0], n_heads, d)
        return jnp.concatenate([t3[:, :, :d // 2].reshape(t.shape[0], -1),
                                t3[:, :, d // 2:].reshape(t.shape[0], -1)], axis=1)

    w_tr = jnp.concatenate([mk, mg, cq, ckv, kpe, split_halves(dq, 2 * DF_HEADS, DF_DK),
                            split_halves(dk, 2 * DF_HEADS, DF_DK), dv], axis=1).T.astype(bf16)
    uq = w_uq.reshape(Q_LORA, A_HEADS, A_NOPE + A_ROPE)
    half = A_ROPE // 2
    wt_uq = jnp.concatenate([uq[:, :, :A_NOPE].reshape(Q_LORA, -1),
                             uq[:, :, A_NOPE:A_NOPE + half].reshape(Q_LORA, -1),
                             uq[:, :, A_NOPE + half:].reshape(Q_LORA, -1)], axis=1).T.astype(bf16)
    ukv = w_ukv.reshape(KV_LORA, A_HEADS, A_NOPE + A_V)
    wt_ukv = jnp.concatenate([ukv[:, :, :A_NOPE].reshape(KV_LORA, -1),
                              ukv[:, :, A_NOPE:].reshape(KV_LORA, -1)], axis=1).T.astype(bf16)
    return w_tok, w_tr, wt_uq, wt_ukv


def _inproj(x_all, ab, n_lat_blocks, w_tok, w_tr, wt_uq, wt_ukv, gq, gkv, ca, sa, cd, sd, after_moe=None):
    x_ctx = None
    if after_moe is None:
        x_all, x_ctx = x_all
    n = x_all.shape[0] + (0 if x_ctx is None else x_ctx.shape[0])
    tm = ROW_BLOCK
    tk = _key_tile(n)
    per = tk // tm
    vtile = lambda i: (0, i // per, 0, i % per)
    nb = n // tm
    row = lambda i: (i, 0)
    colb = lambda i: (0, i)
    col3 = lambda i: (0, 0, i)
    grp = lambda i: (jnp.where(i >= n_lat_blocks, 1, 0), 0, 0)
    out_shapes = [
        jax.ShapeDtypeStruct((n, M_W), bf16), jax.ShapeDtypeStruct((n, M_W), bf16),
        jax.ShapeDtypeStruct((n, M_W), bf16), jax.ShapeDtypeStruct((n, LANE), f32),
        jax.ShapeDtypeStruct((n, N_BRANCH * D_MODEL), bf16),
        jax.ShapeDtypeStruct((M_W, n), bf16), jax.ShapeDtypeStruct((4 * M_HEADS, n), f32),
        jax.ShapeDtypeStruct((A_HEADS, A_DKP, n), bf16), jax.ShapeDtypeStruct((A_HEADS, n, A_DKP), bf16),
        jax.ShapeDtypeStruct((A_HEADS, n // tk, A_DVP, tk), bf16),
        jax.ShapeDtypeStruct((2 * DF_HEADS, DF_DK, n), bf16), jax.ShapeDtypeStruct((2 * DF_HEADS, n, DF_DK), bf16),
        jax.ShapeDtypeStruct((DF_HEADS, n // tk, DF_DVP, tk), bf16),
    ]
    out_specs = [
        pl.BlockSpec((tm, M_W), row), pl.BlockSpec((tm, M_W), row), pl.BlockSpec((tm, M_W), row),
        pl.BlockSpec((tm, LANE), row), pl.BlockSpec((tm, N_BRANCH * D_MODEL), row),
        pl.BlockSpec((M_W, tm), colb), pl.BlockSpec((4 * M_HEADS, tm), colb),
        pl.BlockSpec((A_HEADS, A_DKP, tm), col3), pl.BlockSpec((A_HEADS, tm, A_DKP), lambda i: (0, i, 0)),
        pl.BlockSpec((A_HEADS, None, A_DVP, tm), vtile),
        pl.BlockSpec((2 * DF_HEADS, DF_DK, tm), col3), pl.BlockSpec((2 * DF_HEADS, tm, DF_DK), lambda i: (0, i, 0)),
        pl.BlockSpec((DF_HEADS, None, DF_DVP, tm), vtile),
    ]
    in_specs = [
        pl.BlockSpec((tm, D_MODEL), row),
        pl.BlockSpec((None, 2, D_MODEL), grp),
        _const_spec(w_tok.shape), _const_spec(w_tr.shape), _const_spec(wt_uq.shape), _const_spec(wt_ukv.shape),
        _const_spec(gq.shape), _const_spec(gkv.shape),
        pl.BlockSpec((A_ROPE // 2, tm), colb), pl.BlockSpec((A_ROPE // 2, tm), colb),
        pl.BlockSpec((DF_DK // 2, tm), colb), pl.BlockSpec((DF_DK // 2, tm), colb),
    ]
    args = [x_all, ab, w_tok, w_tr, wt_uq, wt_ukv, gq, gkv, ca, sa, cd, sd]
    if after_moe is None:
        kern = functools.partial(_inproj_first_kernel, n_lat_blocks=n_lat_blocks)
        in_specs = [pl.BlockSpec((tm, D_MODEL), lambda i: (jnp.minimum(i, n_lat_blocks - 1), 0)),
                    pl.BlockSpec((tm, D_MODEL), lambda i: (0, 0))] + in_specs[1:]
        args = [x_all, x_ctx] + args[1:]
    else:
        picked, wts, mod = after_moe
        kern = _inproj_after_moe_kernel
        in_specs = in_specs[:1] + [pl.BlockSpec((TOP_K, tm, D_MODEL), lambda i: (0, i, 0)),
                                   pl.BlockSpec((tm, TOP_K), row), pl.BlockSpec((None, 2, D_MODEL), grp)] + in_specs[1:]
        args = args[:1] + [picked, wts, mod] + args[1:]
        out_specs = out_specs + [pl.BlockSpec((tm, D_MODEL), row)]
        out_shapes = out_shapes + [jax.ShapeDtypeStruct((n, D_MODEL), f32)]
    return pl.pallas_call(
        kern, grid=(nb,), in_specs=in_specs, out_specs=out_specs, out_shape=out_shapes,
        compiler_params=_cparams(("parallel",)), name="inproj",
    )(*args)


def _attn_kernel(*refs, n_maps, dk, dv, tk, n_kt, qsub, lam_init):
    if n_maps == 2:
        lam_ref, g_ref, qt_ref, k_ref, vt_ref, o_ref = refs
    else:
        qt_ref, k_ref, vt_ref, o_ref = refs
    tq = qt_ref.shape[-1]
    dvp = vt_ref.shape[-2]
    chains = [(m, c) for m in range(n_maps) for c in range(tq // qsub)]

    def qk(j, m, c):
        qt = qt_ref[m * dk:(m + 1) * dk, c * qsub:(c + 1) * qsub]
        return jnp.dot(k_ref[m, j * tk:(j + 1) * tk, :], qt, preferred_element_type=f32)

    state = [(jnp.full((1, qsub), -jnp.inf, f32), jnp.zeros((dvp, qsub), f32)) for _ in chains]
    s_cur = [qk(0, m, c) for (m, c) in chains]
    for j in range(n_kt):
        vtj = vt_ref[j]
        for i, (m, c) in enumerate(chains):
            s_nxt = qk(j + 1, m, c) if j + 1 < n_kt else None
            m_run, acc = state[i]
            m_new = jnp.maximum(m_run, jnp.max(s_cur[i], axis=0, keepdims=True))
            alpha = jnp.exp2(m_run - m_new)
            p = jnp.exp2(s_cur[i] - m_new).astype(bf16)
            pv = jnp.dot(vtj, p, preferred_element_type=f32)
            state[i] = (m_new, alpha * acc + pv)
            s_cur[i] = s_nxt
    outs = []
    for m in range(n_maps):
        accs = [state[i][1] for i, (mm, _) in enumerate(chains) if mm == m]
        acc = accs[0] if len(accs) == 1 else jnp.concatenate(accs, axis=1)
        outs.append(acc[0:dv] / acc[dv:dv + 1])
    if n_maps == 1:
        o_ref[...] = outs[0].astype(o_ref.dtype)
    else:
        lf = lam_ref[...]
        lam = (jnp.exp(jnp.sum(lf[0:1] * lf[1:2], axis=1, keepdims=True))
               - jnp.exp(jnp.sum(lf[2:3] * lf[3:4], axis=1, keepdims=True)) + lam_init)
        o = outs[0] - lam * outs[1]
        o = o * lax.rsqrt(jnp.mean(o * o, axis=0, keepdims=True) + EPS) * g_ref[...]
        o_ref[...] = (o * (1.0 - lam_init)).astype(o_ref.dtype)


def _key_tile(lk):
    return max(t for t in range(MXU_DEPTH, min(ATTN_TK, lk) + 1, MXU_DEPTH) if lk % t == 0)


def _attention(qt, k, vt4, *, n_maps, dk, dv, tq, lq, q_start=0, lam=None, g=None, lam_init=0.0):
    n_heads = qt.shape[0]
    lk = k.shape[1]
    _, n_kt, dvp, tk = vt4.shape
    assert n_kt * tk == lk
    tq = min(tq, lq)
    qsub = min(ATTN_QSUB, tq)
    qb0 = q_start // tq
    in_specs = [
        pl.BlockSpec((None, n_maps * dk, tq), lambda h, i: (h, 0, qb0 + i)),
        pl.BlockSpec((n_maps, lk, dk), lambda h, i: (h, 0, 0)),
        pl.BlockSpec((None, n_kt, dvp, tk), lambda h, i: (h, 0, 0, 0)),
    ]
    args = [qt, k, vt4]
    if n_maps == 2:
        in_specs = [_const_spec(lam.shape), pl.BlockSpec((None, dv, 1), lambda h, i: (h, 0, 0))] + in_specs
        args = [lam, g] + args
    kern = functools.partial(_attn_kernel, n_maps=n_maps, dk=dk, dv=dv, tk=tk, n_kt=n_kt, qsub=qsub,
                             lam_init=lam_init)
    return pl.pallas_call(
        kern, grid=(n_heads, lq // tq), in_specs=in_specs,
        out_specs=pl.BlockSpec((None, dv, tq), lambda h, i: (h, 0, i)),
        out_shape=jax.ShapeDtypeStruct((n_heads, dv, lq), bf16),
        compiler_params=_cparams(("parallel", "parallel")),
        name="attn_diff" if n_maps == 2 else "attn_mla",
    )(*args)


def _split3(x):
    hi = x.astype(bf16)
    r = x - hi.astype(f32)
    mid = r.astype(bf16)
    lo = (r - mid.astype(f32)).astype(bf16)
    return hi, mid, lo


def _log_sigmoid(x):
    return jnp.minimum(x, 0.0) - jnp.log(1.0 + jnp.exp(-jnp.abs(x)))


def _softcap(x):
    return GATE_SOFTCAP * jnp.tanh(x * (1.0 / GATE_SOFTCAP))


def _mlstm_kernel(gb_ref, gbt_ref,
                  qf_ref, ktf_ref, vf_ref, gf_ref, gtf_ref,
                  qb_ref, ktb_ref, vb_ref, gb2_ref, gtb_ref,
                  hf_ref, hb_ref, c_ref, m_ref):
    t = qf_ref.shape[0]

    @pl.when(pl.program_id(0) == 0)
    def _():
        c_ref[...] = jnp.zeros_like(c_ref)
        m_ref[...] = jnp.zeros_like(m_ref)

    row = lax.broadcasted_iota(jnp.int32, (t, t), 0)
    col = lax.broadcasted_iota(jnp.int32, (t, t), 1)
    lane = lax.broadcasted_iota(jnp.int32, (t, LANE), 1)
    ones_col = jnp.where(lane == 0, 1.0, 0.0).astype(bf16)
    dirs = ((qf_ref, ktf_ref, vf_ref, gf_ref, gtf_ref, hf_ref),
            (qb_ref, ktb_ref, vb_ref, gb2_ref, gtb_ref, hb_ref))
    for d, (q_ref, kt_ref, v_ref, g_ref, gt_ref, o_ref) in enumerate(dirs):
        mask = (col <= row) if d == 0 else (col >= row)
        maskb = mask.astype(bf16)
        last = t - 1 if d == 0 else 0
        g_tok = _softcap(g_ref[...] + gb_ref[...])
        g_t = _softcap(gt_ref[...] + gbt_ref[...])
        b_tok = sum(jnp.dot(maskb, part, preferred_element_type=f32) for part in _split3(_log_sigmoid(g_tok)))
        b_t = sum(lax.dot_general(part, maskb, NT_DIMS, preferred_element_type=f32)
                  for part in _split3(_log_sigmoid(g_t)))
        for hh in range(M_HEADS):
            ri = 4 * (2 * d) + hh
            rf = 4 * (2 * d + 1) + hh
            idx = d * M_HEADS + hh
            b_col = b_tok[:, rf:rf + 1]
            b_row = b_t[rf:rf + 1, :]
            a_row = g_t[ri:ri + 1, :] - b_row
            m_prev = m_ref[idx, 0:1, 0:1]
            ld = jnp.where(mask, b_col + a_row, -jnp.inf)
            inter = b_col + m_prev
            mj = jnp.maximum(inter, jnp.max(ld, axis=1, keepdims=True))
            dmat = jnp.exp(ld - mj)
            q = q_ref[:, hh * M_DK:(hh + 1) * M_DK]
            kt = kt_ref[hh * M_DK:(hh + 1) * M_DK, :]
            vp = jnp.concatenate([v_ref[:, hh * M_DV:(hh + 1) * M_DV], ones_col], axis=1)
            sc = jnp.dot(q, kt, preferred_element_type=f32) * dmat
            cp = c_ref[idx]
            num = (jnp.dot(sc.astype(bf16), vp, preferred_element_type=f32)
                   + jnp.exp(inter - mj) * jnp.dot(q, cp.astype(bf16), preferred_element_type=f32))
            den = num[:, M_DV:M_DV + 1]
            hj = num[:, 0:M_DV] / jnp.maximum(jnp.abs(den), jnp.exp(-mj))
            o_ref[:, hh * M_DV:(hh + 1) * M_DV] = hj.astype(o_ref.dtype)
            b_tot = b_t[rf:rf + 1, last:last + 1]
            lw = b_tot + a_row
            m_new = jnp.maximum(b_tot + m_prev, jnp.max(lw, axis=1, keepdims=True))
            kw = (kt.astype(f32) * jnp.exp(lw - m_new)).astype(bf16)
            c_ref[idx] = jnp.exp(b_tot + m_prev - m_new) * cp + jnp.dot(kw, vp, preferred_element_type=f32)
            m_ref[idx] = jnp.broadcast_to(m_new, (8, LANE))


def _mlstm(mq, mkt, mv, mg, mgt, gate_b):
    n = mq.shape[0]
    t = MLSTM_T
    nb = n // t
    gb = jnp.pad(gate_b.reshape(1, 4 * M_HEADS), ((0, 0), (0, LANE - 4 * M_HEADS)))
    gbt = gate_b.reshape(4 * M_HEADS, 1)
    fwd = lambda s: jnp.where(s == 0, nb - 1, s - 1)
    bwd = lambda s: nb - 1 - s
    specs = []
    for im in (fwd, bwd):
        specs += [pl.BlockSpec((t, M_W), lambda s, im=im: (im(s), 0)),
                  pl.BlockSpec((M_W, t), lambda s, im=im: (0, im(s))),
                  pl.BlockSpec((t, M_W), lambda s, im=im: (im(s), 0)),
                  pl.BlockSpec((t, LANE), lambda s, im=im: (im(s), 0)),
                  pl.BlockSpec((4 * M_HEADS, t), lambda s, im=im: (0, im(s)))]
    return pl.pallas_call(
        _mlstm_kernel, grid=(nb,),
        in_specs=[_const_spec(gb.shape), _const_spec(gbt.shape)] + specs,
        out_specs=[pl.BlockSpec((t, M_W), lambda s: (fwd(s), 0)), pl.BlockSpec((t, M_W), lambda s: (bwd(s), 0))],
        out_shape=[jax.ShapeDtypeStruct((n, M_W), bf16), jax.ShapeDtypeStruct((n, M_W), bf16)],
        scratch_shapes=[pltpu.VMEM((2 * M_HEADS, M_DK, 2 * M_DV), f32), pltpu.VMEM((2 * M_HEADS, 8, LANE), f32)],
        compiler_params=_cparams(("arbitrary",)), name="mlstm",
    )(gb, gbt, mq, mkt, mv, mg, mgt, mq, mkt, mv, mg, mgt)


def _rms(x, axis=-1):
    return x * lax.rsqrt(jnp.mean(x * x, axis=axis, keepdims=True) + EPS)


def _merge_kernel(x_ref, xc_ref, hf_ref, hb_ref, mo_ref, ya_ref, yd_ref, yac_ref, ydc_ref, gl_ref, mng_ref, wb_ref,
                  wout_ref, mod_ref, rw_ref, rb_ref, xnew_ref, h2_ref, logit_ref, *, n_blocks, n_lat_blocks):
    i = pl.program_id(0)

    @pl.when(i < n_blocks)
    def _():
        _merge_block(x_ref, xc_ref, hf_ref, hb_ref, mo_ref, ya_ref, yd_ref, yac_ref, ydc_ref, gl_ref, mng_ref, wb_ref,
                     wout_ref, mod_ref, rw_ref, rb_ref, xnew_ref, h2_ref, logit_ref, i >= n_lat_blocks)

    @pl.when(i >= n_blocks)
    def _():
        h2_ref[...] = jnp.zeros_like(h2_ref)


def _merge_block(x_ref, xc_ref, hf_ref, hb_ref, mo_ref, ya_ref, yd_ref, yac_ref, ydc_ref, gl_ref, mng_ref, wb_ref,
                 wout_ref, mod_ref, rw_ref, rb_ref, xnew_ref, h2_ref, logit_ref, is_ctx):
    hs = hf_ref[...].astype(f32) + hb_ref[...].astype(f32)
    parts = [_rms(hs[:, hh * M_DV:(hh + 1) * M_DV]) for hh in range(M_HEADS)]
    ym = jnp.concatenate(parts, axis=1) * mng_ref[...] * jax.nn.sigmoid(mo_ref[...].astype(f32))
    yb0 = jnp.dot(ym.astype(bf16), wb_ref[0], preferred_element_type=f32)
    tm = x_ref.shape[0]
    tn_dims = (((0,), (0,)), ((), ()))
    ya = jnp.where(is_ctx, yac_ref[...], ya_ref[...])
    yd = jnp.where(is_ctx, ydc_ref[...], yd_ref[...])
    yb1 = lax.dot_general(ya.reshape(BRANCH_W, tm), wb_ref[1], tn_dims, preferred_element_type=f32)
    yb2 = lax.dot_general(yd.reshape(BRANCH_W, tm), wb_ref[2], tn_dims, preferred_element_type=f32)
    d = D_MODEL
    z = (jax.nn.sigmoid(gl_ref[:, 0:d].astype(f32)) * yb0
         + jax.nn.sigmoid(gl_ref[:, d:2 * d].astype(f32)) * yb1
         + jax.nn.sigmoid(gl_ref[:, 2 * d:3 * d].astype(f32)) * yb2)
    y = jnp.dot(z.astype(bf16), wout_ref[...], preferred_element_type=f32)
    x = jnp.where(is_ctx, xc_ref[...], x_ref[...])
    xnew = x + mod_ref[1:2, :] * (_rms(y) * mod_ref[0:1, :])
    h2 = _rms(xnew) * mod_ref[2:3, :] + mod_ref[3:4, :]
    xnew_ref[...] = xnew
    h2_ref[...] = h2.astype(bf16)
    h2_hi = h2.astype(bf16)
    h2_lo = (h2 - h2_hi.astype(f32)).astype(bf16)
    rw = rw_ref[...]
    rw_hi = rw.astype(bf16)
    rw_lo = (rw - rw_hi.astype(f32)).astype(bf16)
    logit_ref[...] = (jnp.dot(h2_hi, rw_hi, preferred_element_type=f32)
                      + jnp.dot(h2_lo, rw_hi, preferred_element_type=f32)
                      + jnp.dot(h2_hi, rw_lo, preferred_element_type=f32) + rb_ref[...])


def _merge(n_rows, n_lat_blocks, x_lat, x_ctx, hf, hb, mo, ya, yd, ya_ctx, yd_ctx, gl, mng, wb, wout, mod, rw, rb):
    tm = ROW_BLOCK
    nb = n_rows // tm
    n_steps = max(n_rows, H2_ROWS) // tm
    last = nb - 1
    row = lambda i: (jnp.minimum(i, last), 0)
    lat3 = lambda i: (0, 0, jnp.minimum(i, n_lat_blocks - 1))
    grp = lambda i: (jnp.where(i >= n_lat_blocks, 1, 0), 0, 0)
    in_specs = [pl.BlockSpec((tm, D_MODEL), lambda i: (jnp.minimum(i, n_lat_blocks - 1), 0)),
                pl.BlockSpec((tm, D_MODEL), lambda i: (0, 0)),
                pl.BlockSpec((tm, M_W), row), pl.BlockSpec((tm, M_W), row), pl.BlockSpec((tm, M_W), row),
                pl.BlockSpec((A_HEADS, A_V, tm), lat3), pl.BlockSpec((DF_HEADS, DF_DV, tm), lat3),
                pl.BlockSpec((A_HEADS, A_V, tm), lambda i: (0, 0, 0)),
                pl.BlockSpec((DF_HEADS, DF_DV, tm), lambda i: (0, 0, 0)),
                pl.BlockSpec((tm, N_BRANCH * D_MODEL), row), _const_spec(mng.shape), _const_spec(wb.shape),
                _const_spec(wout.shape), pl.BlockSpec((None, 4, D_MODEL), grp),
                _const_spec(rw.shape), _const_spec(rb.shape)]
    kern = functools.partial(_merge_kernel, n_blocks=nb, n_lat_blocks=n_lat_blocks)
    return pl.pallas_call(
        kern, grid=(n_steps,), in_specs=in_specs,
        out_specs=[pl.BlockSpec((tm, D_MODEL), row), pl.BlockSpec((tm, D_MODEL), lambda i: (i, 0)),
                   pl.BlockSpec((tm, LANE), row)],
        out_shape=[jax.ShapeDtypeStruct((n_rows, D_MODEL), f32),
                   jax.ShapeDtypeStruct((max(n_rows, H2_ROWS), D_MODEL), bf16),
                   jax.ShapeDtypeStruct((n_rows, LANE), f32)],
        compiler_params=_cparams(("arbitrary",)), name="merge",
    )(x_lat, x_ctx, hf, hb, mo, ya, yd, ya_ctx, yd_ctx, gl, mng, wb, wout, mod, rw, rb)


def _moe_kernel(be_ref, nu_ref, xs_ref, wgu_ref, bgu_ref, wd_ref, bd_ref, o_ref, wgu_s, wd_s):
    i = pl.program_id(0)
    prev = be_ref[jnp.maximum(i - 1, 0)]
    used = i < nu_ref[0]

    @pl.when(jnp.logical_and(used, jnp.logical_or(i == 0, be_ref[i] != prev)))
    def _():
        wgu_s[...] = wgu_ref[...].astype(bf16)
        wd_s[...] = wd_ref[...].astype(bf16)

    @pl.when(used)
    def _():
        gu = jnp.dot(xs_ref[...], wgu_s[...], preferred_element_type=f32) + bgu_ref[...]
        gate = jnp.minimum(gu[:, :D_EXPERT], SWIGLU_LIMIT)
        up = jnp.clip(gu[:, D_EXPERT:], -SWIGLU_LIMIT, SWIGLU_LIMIT)
        act = (up + 1.0) * (gate * jax.nn.sigmoid(SWIGLU_ALPHA * gate))
        y = jnp.dot(act.astype(bf16), wd_s[...], preferred_element_type=f32) + bd_ref[...]
        o_ref[...] = y.astype(o_ref.dtype)

    @pl.when(jnp.logical_not(used))
    def _():
        o_ref[...] = jnp.zeros_like(o_ref)


def _moe_experts(blk_e, n_used, xs, layer, w_gu, b_gu, w_down, b_down):
    cap = xs.shape[0]
    tm = MOE_TM
    nblk = cap // tm
    grid_spec = pltpu.PrefetchScalarGridSpec(
        num_scalar_prefetch=2, grid=(nblk,),
        in_specs=[pl.BlockSpec((tm, D_MODEL), lambda i, be, nu: (i, 0)),
                  pl.BlockSpec((None, None, D_MODEL, 2 * D_EXPERT), lambda i, be, nu: (layer, be[i], 0, 0)),
                  pl.BlockSpec((None, None, 1, 2 * D_EXPERT), lambda i, be, nu: (layer, be[i], 0, 0)),
                  pl.BlockSpec((None, None, D_EXPERT, D_MODEL), lambda i, be, nu: (layer, be[i], 0, 0)),
                  pl.BlockSpec((None, None, 1, D_MODEL), lambda i, be, nu: (layer, be[i], 0, 0))],
        out_specs=pl.BlockSpec((tm, D_MODEL), lambda i, be, nu: (i, 0)),
        scratch_shapes=[pltpu.VMEM((D_MODEL, 2 * D_EXPERT), bf16), pltpu.VMEM((D_EXPERT, D_MODEL), bf16)])
    return pl.pallas_call(
        _moe_kernel, grid_spec=grid_spec,
        out_shape=jax.ShapeDtypeStruct((cap, D_MODEL), bf16),
        compiler_params=_cparams(("arbitrary",)), name="moe_experts",
    )(blk_e, n_used, xs, w_gu, b_gu.reshape(DEPTH, N_EXPERTS, 1, -1), w_down, b_down.reshape(DEPTH, N_EXPERTS, 1, -1))


def _moe(h2, logits, layer, w_gu, b_gu, w_down, b_down):
    n = logits.shape[0]
    tm = MOE_TM
    i32 = jnp.int32
    top_v, top_i = lax.top_k(logits[:, :N_EXPERTS], TOP_K)
    wts = jax.nn.softmax(top_v, axis=-1)
    nk = n * TOP_K
    flat_e = top_i.reshape(nk).astype(i32)
    iota = jnp.arange(nk, dtype=i32)
    skey = lax.sort(flat_e * nk + iota)
    se = skey // nk
    order = skey - se * nk
    e_ids = jnp.arange(N_EXPERTS, dtype=i32)
    grp_start = jnp.sum((se[:, None] < e_ids[None, :]).astype(i32), axis=0)
    counts = jnp.sum((se[:, None] == e_ids[None, :]).astype(i32), axis=0)
    padded = (counts + tm - 1) // tm * tm
    pad_end = jnp.cumsum(padded)
    pad_start = pad_end - padded
    off = pad_start - grp_start
    hit = top_i[:, :, None] == e_ids[None, None, :]
    per_tok = jnp.sum(hit.astype(i32), axis=1)
    rank = jnp.cumsum(per_tok, axis=0) - per_tok
    pos = jnp.sum(jnp.where(hit, (pad_start[None, :] + rank)[:, None, :], 0), axis=2)
    pos_km = pos.T.reshape(nk)
    cap = -(-(nk + N_EXPERTS * (tm - 1)) // tm) * tm
    nblk = cap // tm
    bstart = jnp.arange(nblk, dtype=i32) * tm
    blk_e = jnp.minimum(jnp.sum((pad_end[None, :] <= bstart[:, None]).astype(i32), axis=1), N_EXPERTS - 1)
    n_used = (pad_end[-1] // tm).astype(i32).reshape(1)
    sel = blk_e[:, None] == e_ids[None, :]
    off_b = jnp.sum(jnp.where(sel, off[None, :], 0), axis=1)
    end_b = jnp.sum(jnp.where(sel, (pad_start + counts)[None, :], 0), axis=1)
    p2 = bstart[:, None] + jnp.arange(tm, dtype=i32)[None, :]
    src = jnp.clip(p2 - off_b[:, None], 0, nk - 1)
    slot_t = jnp.where(p2 < end_b[:, None], (order // TOP_K)[src], p2 % n).reshape(cap)
    xs = h2[slot_t]
    ys = _moe_experts(blk_e, n_used, xs, layer, w_gu, b_gu, w_down, b_down)
    return ys[pos_km].reshape(TOP_K, n, D_MODEL), wts


def _combine_residual(x_ref, y_ref, w_ref, mod_ref):
    w = w_ref[...]
    f = y_ref[0].astype(f32) * w[:, 0:1]
    for kk in range(1, TOP_K):
        f = f + y_ref[kk].astype(f32) * w[:, kk:kk + 1]
    return x_ref[...] + mod_ref[1:2, :] * (_rms(f) * mod_ref[0:1, :])


def _resid_kernel(x_ref, y_ref, w_ref, mod_ref, o_ref):
    o_ref[...] = _combine_residual(x_ref, y_ref, w_ref, mod_ref)


def _resid(n_lat_blocks, xnew, picked, wts, mod):
    n = xnew.shape[0]
    tm = ROW_BLOCK
    row = lambda i: (i, 0)
    grp = lambda i: (jnp.where(i >= n_lat_blocks, 1, 0), 0, 0)
    return pl.pallas_call(
        _resid_kernel, grid=(n // tm,),
        in_specs=[pl.BlockSpec((tm, D_MODEL), row), pl.BlockSpec((TOP_K, tm, D_MODEL), lambda i: (0, i, 0)),
                  pl.BlockSpec((tm, TOP_K), row), pl.BlockSpec((None, 2, D_MODEL), grp)],
        out_specs=pl.BlockSpec((tm, D_MODEL), row),
        out_shape=jax.ShapeDtypeStruct((n, D_MODEL), f32),
        compiler_params=_cparams(("parallel",)), name="resid",
    )(xnew, picked, wts, mod)


def _rope_tables(n_lat, n_ctx, dim):
    rows = n_lat // GRID_W
    row = jnp.repeat(jnp.arange(rows, dtype=f32), GRID_W)
    col = jnp.tile(jnp.arange(GRID_W, dtype=f32), rows)
    quarter = dim // 4
    inv = ROPE_BASE ** (-jnp.arange(quarter, dtype=f32) / quarter)
    ang = jnp.concatenate([row[:, None] * inv, col[:, None] * inv], axis=-1)
    cos = jnp.concatenate([jnp.cos(ang), jnp.ones((n_ctx, dim // 2), f32)], axis=0).T
    sin = jnp.concatenate([jnp.sin(ang), jnp.zeros((n_ctx, dim // 2), f32)], axis=0).T
    return cos, sin


def kernel(x, c, ctx, c_ctx, ada_w, ada_b, norm_g, w_in, m_gate_b, m_norm_g, q_norm_g, w_uq, kv_norm_g, w_ukv,
           diff_lam, diff_norm_g, w_branch, w_out, router_w, router_b, w_gu, b_gu, w_down, b_down):
    n_lat = x.shape[1]
    n_ctx = ctx.shape[1]
    n = n_lat + n_ctx
    assert x.shape[0] == 1 and n_lat % ROW_BLOCK == 0 and n_ctx == MLSTM_T == ROW_BLOCK and n_lat % GRID_W == 0
    nlb = n_lat // ROW_BLOCK
    x_all = (x[0], ctx[0])

    cond = jnp.zeros((8, D_MODEL), f32).at[0].set(c[0]).at[1].set(c_ctx)
    mods = _adaln(cond, ada_w, ada_b)[:, :2].reshape(DEPTH, 2, 6, D_MODEL)

    ca, sa = _rope_tables(n_lat, n_ctx, A_ROPE)
    cd, sd = _rope_tables(n_lat, n_ctx, DF_DK)

    pending = None
    for l in range(DEPTH):
        need_ctx = l < DEPTH - 1
        lam_init = 0.8 - 0.6 * math.exp(-0.3 * l)
        sh1, sc1, ga1, sh2, sc2, ga2 = [mods[l, :, j] for j in range(6)]
        ab1 = jnp.stack([norm_g[l, 0][None] * (1.0 + sc1), sh1], axis=1)
        w_tok, w_tr, wt_uq, wt_ukv = _pack_inproj_weights(w_in[l], w_uq[l], w_ukv[l])
        proj = _inproj(x_all, ab1, nlb, w_tok, w_tr, wt_uq, wt_ukv,
                       q_norm_g[l].reshape(Q_LORA, 1), kv_norm_g[l].reshape(KV_LORA, 1), ca, sa, cd, sd,
                       after_moe=pending)
        if pending is not None:
            x_all = proj[-1]
        (mq, mv, mo, mg, gl, mkt, mgt, qta, k_a, vta, qtd, k_d, vtd) = proj[:13]

        hf, hb = _mlstm(mq, mkt, mv, mg, mgt, m_gate_b[l])

        qtd4 = qtd.reshape(DF_HEADS, 2 * DF_DK, n)
        dg = diff_norm_g[l].reshape(DF_HEADS, DF_DV, 1)
        n_rows = n if need_ctx else n_lat
        attn_a = functools.partial(_attention, n_maps=1, dk=A_DKP, dv=A_V)
        attn_d = functools.partial(_attention, n_maps=2, dk=DF_DK, dv=DF_DV,
                                   lam=diff_lam[l], g=dg, lam_init=lam_init)
        ya = attn_a(qta, k_a, vta, tq=ATTN_TQ_MLA, lq=n_lat)
        yd = attn_d(qtd4, k_d, vtd, tq=ATTN_TQ_DIFF, lq=n_lat)
        ya_ctx, yd_ctx = ya, yd
        if need_ctx:
            ctx_tile = lambda v: v[:, -1:, :, v.shape[3] - n_ctx:]
            ya_ctx = attn_a(qta, k_a[:, n_lat:], ctx_tile(vta), tq=n_ctx, lq=n_ctx, q_start=n_lat)
            yd_ctx = attn_d(qtd4, k_d[:, n_lat:], ctx_tile(vtd), tq=n_ctx, lq=n_ctx, q_start=n_lat)

        mod_m = jnp.stack([jnp.broadcast_to(norm_g[l, 1][None], (2, D_MODEL)), ga1,
                           norm_g[l, 2][None] * (1.0 + sc2), sh2], axis=1)
        rw = jnp.pad(router_w[l], ((0, 0), (0, LANE - N_EXPERTS)))
        rb = jnp.pad(router_b[l].reshape(1, N_EXPERTS), ((0, 0), (0, LANE - N_EXPERTS)))
        x_lat, x_ctx = x_all if pending is None else (x_all, x_all[n_lat:] if need_ctx else x_all)
        xnew, h2, logits = _merge(n_rows, nlb, x_lat, x_ctx, hf, hb, mo, ya, yd, ya_ctx, yd_ctx, gl,
                                  m_norm_g[l].reshape(1, -1), w_branch[l].astype(bf16), w_out[l].astype(bf16),
                                  mod_m, rw, rb)
        picked, wts = _moe(h2, logits, l, w_gu, b_gu, w_down, b_down)
        mod_r = jnp.stack([jnp.broadcast_to(norm_g[l, 3][None], (2, D_MODEL)), ga2], axis=1)
        if need_ctx:
            x_all, pending = xnew, (picked, wts, mod_r)
        else:
            x_all = _resid(nlb, xnew, picked, wts, mod_r)
    return x_all[:n_lat][None]
```
